```python
import math
import jax, jax.numpy as jnp
from jax import lax
import numpy as np

D_MODEL = 2048
BATCH = 8
SEQ = 8192
DEPTH = 1

SSM_EXPAND = 2
D_SSM = SSM_EXPAND * D_MODEL
SSM_HEAD_DIM = 64
N_SSM_HEADS = D_SSM // SSM_HEAD_DIM
N_GROUPS = 8
HEADS_PER_GROUP = N_SSM_HEADS // N_GROUPS
D_STATE = 128
D_XBC = D_SSM + 2 * N_GROUPS * D_STATE
SSM_CONV = 5
CHUNK = 128
D_CONV = D_MODEL
CONV_WIDTH = 31
D_FF = -(-8 * D_MODEL // 768) * 256
N_MOD = 6
EPS = 1e-6
IN_SPLITS = (D_SSM,
             D_SSM + D_XBC,
             D_SSM + D_XBC + 2 * N_SSM_HEADS,
             D_SSM + D_XBC + 2 * N_SSM_HEADS + 2 * D_CONV)
IN_COLS = IN_SPLITS[-1] + 2 * D_MODEL

kernel_name = 'hybrid_bidir_ssd_conformer_gated_block'


def _rmsnorm(x, w):
    xf = x.astype(jnp.float32)
    y = xf * lax.rsqrt(jnp.mean(xf * xf, axis=-1, keepdims=True) + EPS)
    return (y * w.astype(jnp.float32)).astype(x.dtype)


def _layernorm(x, g, b):
    xf = x.astype(jnp.float32)
    xc = xf - jnp.mean(xf, axis=-1, keepdims=True)
    y = xc * lax.rsqrt(jnp.mean(xc * xc, axis=-1, keepdims=True) + EPS)
    return (y * g.astype(jnp.float32) + b.astype(jnp.float32)).astype(x.dtype)


def _dwconv(u, w, bias):
    k = w.shape[0]
    pad = (k - 1) // 2
    out = lax.conv_general_dilated(
        u, w[:, None, :].astype(u.dtype), window_strides=(1,), padding=[(pad, pad)],
        dimension_numbers=('NWC', 'WIO', 'NWC'), feature_group_count=u.shape[-1])
    return out + bias


def _flip(t):
    return jnp.flip(t, axis=1)


def _segsum(a):
    t = a.shape[-1]
    cs = jnp.cumsum(a, axis=-1)
    diff = cs[..., :, None] - cs[..., None, :]
    return jnp.where(jnp.tril(jnp.ones((t, t), dtype=bool)), diff, -jnp.inf)


def _ssd(xdt, adt, bm, cm):
    bsz, s, g, j, p = xdt.shape
    n = bm.shape[-1]
    nc = s // CHUNK
    x = xdt.reshape(bsz, nc, CHUNK, g, j, p)
    b = bm.reshape(bsz, nc, CHUNK, g, n)
    c = cm.reshape(bsz, nc, CHUNK, g, n)
    a = jnp.moveaxis(adt.reshape(bsz, nc, CHUNK, g, j), (1, 2), (3, 4))
    a_cs = jnp.cumsum(a, axis=-1)
    cb = jnp.einsum('bzlgn,bzsgn->bgzls', c, b)
    m = cb[:, :, None] * jnp.exp(_segsum(a))
    y_diag = jnp.einsum('bgjzls,bzsgjp->bzlgjp', m, x)
    decay_to_end = jnp.moveaxis(jnp.exp(a_cs[..., -1:] - a_cs), (3, 4), (1, 2))
    states = jnp.einsum('bzlgn,bzlgjp->bzgjpn', b, x * decay_to_end[..., None])
    chunk_decay = jnp.exp(a_cs[..., -1])

    def step(h, inp):
        s_z, d_z = inp
        return h * d_z[..., None, None] + s_z, h

    h0 = jnp.zeros((bsz, g, j, p, n), x.dtype)
    _, h_in = lax.scan(step, h0, (jnp.moveaxis(states, 1, 0), jnp.moveaxis(chunk_decay, -1, 0)))
    decay_in = jnp.moveaxis(jnp.exp(a_cs), (3, 4), (1, 2))
    y_off = jnp.einsum('bzlgn,zbgjpn->bzlgjp', c, h_in) * decay_in[..., None]
    return (y_diag + y_off).reshape(bsz, s, g, j, p)


def _ssd_branch(z, xbc, dt_raw, w_conv_ssm, b_conv_ssm, dt_bias_fwd, dt_bias_bwd,
                a_log_fwd, a_log_bwd, d_skip, g_ssm_norm, w_ssm_out):
    f32 = jnp.float32
    bsz, s, _ = xbc.shape
    xbc = jax.nn.silu(_dwconv(xbc, w_conv_ssm, b_conv_ssm))
    xs, bm, cm = jnp.split(xbc.astype(f32), (D_SSM, D_SSM + N_GROUPS * D_STATE), axis=-1)
    xh = xs.reshape(bsz, s, N_GROUPS, HEADS_PER_GROUP, SSM_HEAD_DIM)
    bm = bm.reshape(bsz, s, N_GROUPS, D_STATE)
    cm = cm.reshape(bsz, s, N_GROUPS, D_STATE)
    dtf_raw, dtb_raw = jnp.split(dt_raw.astype(f32), 2, axis=-1)
    hshape = (bsz, s, N_GROUPS, HEADS_PER_GROUP)
    dt_f = jax.nn.softplus(dtf_raw + dt_bias_fwd.astype(f32)).reshape(hshape)
    dt_b = jax.nn.softplus(dtb_raw + dt_bias_bwd.astype(f32)).reshape(hshape)
    a_f = -jnp.exp(a_log_fwd.astype(f32)).reshape(N_GROUPS, HEADS_PER_GROUP)
    a_b = -jnp.exp(a_log_bwd.astype(f32)).reshape(N_GROUPS, HEADS_PER_GROUP)
    y_fwd = _ssd(xh * dt_f[..., None], dt_f * a_f, bm, cm)
    y_bwd = _flip(_ssd(_flip(xh * dt_b[..., None]), _flip(dt_b * a_b), _flip(bm), _flip(cm)))
    y = y_fwd + y_bwd + d_skip.astype(f32).reshape(N_GROUPS, HEADS_PER_GROUP, 1) * xh
    gshape = (bsz, s, N_GROUPS, D_SSM // N_GROUPS)
    y = y.reshape(gshape) * jax.nn.silu(z.astype(f32)).reshape(gshape)
    y = y * lax.rsqrt(jnp.mean(y * y, axis=-1, keepdims=True) + EPS)
    y = y * g_ssm_norm.astype(f32).reshape(N_GROUPS, D_SSM // N_GROUPS)
    return y.reshape(bsz, s, D_SSM).astype(z.dtype) @ w_ssm_out


def _conformer_branch(glu_in, b_glu, w_dw, b_dw, ln_g, ln_b, w_conv_out, b_conv_out):
    u = jax.nn.glu(glu_in + b_glu, axis=-1)
    u = _dwconv(u, w_dw, b_dw)
    u = jax.nn.silu(_layernorm(u, ln_g, ln_b))
    return u @ w_conv_out + b_conv_out


def _mixer(h, w_in, w_conv_ssm, b_conv_ssm, dt_bias_fwd, dt_bias_bwd, a_log_fwd, a_log_bwd,
           d_skip, g_ssm_norm, w_ssm_out, b_glu, w_dw, b_dw, ln_g, ln_b, w_conv_out,
           b_conv_out, b_gate, w_mix_out):
    proj = h @ w_in
    z, xbc, dt_raw, glu_in, gate_logits = jnp.split(proj, IN_SPLITS, axis=-1)
    y_a = _ssd_branch(z, xbc, dt_raw, w_conv_ssm, b_conv_ssm, dt_bias_fwd, dt_bias_bwd,
                      a_log_fwd, a_log_bwd, d_skip, g_ssm_norm, w_ssm_out)
    y_b = _conformer_branch(glu_in, b_glu, w_dw, b_dw, ln_g, ln_b, w_conv_out, b_conv_out)
    g_a, g_b = jnp.split(jax.nn.sigmoid(gate_logits + b_gate), 2, axis=-1)
    return (g_a * y_a + g_b * y_b) @ w_mix_out


def _ffn(h, w_gate_up, w_down):
    gt, up = jnp.split(h @ w_gate_up, 2, axis=-1)
    return (jax.nn.silu(gt) * up) @ w_down


def _fwd_setup_inputs(seed: int = 0) -> dict:
    key = jax.random.key(seed)
    ks = jax.random.split(key, 32)
    L = DEPTH
    f32 = jnp.float32

    def nrm(i, shape, scale):
        return scale * jax.random.normal(ks[i], shape, f32)

    def gain(i, shape):
        return 1.0 + 0.1 * jax.random.normal(ks[i], shape, f32)

    dt0 = jnp.exp(jax.random.uniform(ks[9], (2, L, N_SSM_HEADS), f32, math.log(1e-3), math.log(1e-1)))
    dt_bias = dt0 + jnp.log(-jnp.expm1(-dt0))
    a_log = jnp.log(jax.random.uniform(ks[10], (2, L, N_SSM_HEADS), f32, 1.0, 16.0))
    return {
        'x': nrm(0, (BATCH, SEQ, D_MODEL), 1.0),
        'c': nrm(1, (BATCH, D_MODEL), 1.0),
        'w_ada': nrm(2, (L, D_MODEL, N_MOD * D_MODEL), 0.5 * D_MODEL ** -0.5),
        'b_ada': nrm(3, (L, N_MOD * D_MODEL), 0.02),
        'g_pre_mix': gain(4, (L, D_MODEL)),
        'g_post_mix': gain(5, (L, D_MODEL)),
        'w_in': nrm(6, (L, D_MODEL, IN_COLS), D_MODEL ** -0.5),
        'w_conv_ssm': nrm(7, (L, SSM_CONV, D_XBC), SSM_CONV ** -0.5),
        'b_conv_ssm': nrm(8, (L, D_XBC), 0.02),
        'dt_bias_fwd': dt_bias[0],
        'dt_bias_bwd': dt_bias[1],
        'a_log_fwd': a_log[0],
        'a_log_bwd': a_log[1],
        'd_skip': gain(11, (L, N_SSM_HEADS)),
        'g_ssm_norm': gain(12, (L, D_SSM)),
        'w_ssm_out': nrm(13, (L, D_SSM, D_MODEL), D_SSM ** -0.5),
        'b_glu': nrm(14, (L, 2 * D_CONV), 0.02),
        'w_dw': nrm(15, (L, CONV_WIDTH, D_CONV), CONV_WIDTH ** -0.5),
        'b_dw': nrm(16, (L, D_CONV), 0.02),
        'ln_g': gain(17, (L, D_CONV)),
        'ln_b': nrm(18, (L, D_CONV), 0.02),
        'w_conv_out': nrm(19, (L, D_CONV, D_MODEL), D_CONV ** -0.5),
        'b_conv_out': nrm(20, (L, D_MODEL), 0.02),
        'b_gate': nrm(21, (L, 2 * D_MODEL), 0.02),
        'w_mix_out': nrm(22, (L, D_MODEL, D_MODEL), D_MODEL ** -0.5),
        'g_pre_ffn': gain(23, (L, D_MODEL)),
        'g_post_ffn': gain(24, (L, D_MODEL)),
        'w_gate_up': nrm(25, (L, D_MODEL, 2 * D_FF), D_MODEL ** -0.5),
        'w_down': nrm(26, (L, D_FF, D_MODEL), D_FF ** -0.5),
    }


def _fwd_reference(x, c, w_ada, b_ada, g_pre_mix, g_post_mix, w_in, w_conv_ssm, b_conv_ssm,
              dt_bias_fwd, dt_bias_bwd, a_log_fwd, a_log_bwd, d_skip, g_ssm_norm, w_ssm_out,
              b_glu, w_dw, b_dw, ln_g, ln_b, w_conv_out, b_conv_out, b_gate, w_mix_out,
              g_pre_ffn, g_post_ffn, w_gate_up, w_down):
    c_act = jax.nn.silu(c)
    for l in range(DEPTH):
        mod = (c_act @ w_ada[l] + b_ada[l])[:, None, :]
        sh1, sc1, g1, sh2, sc2, g2 = jnp.split(mod, N_MOD, axis=-1)
        h = _rmsnorm(x, g_pre_mix[l]) * (1 + sc1) + sh1
        mix = _mixer(h, w_in[l], w_conv_ssm[l], b_conv_ssm[l], dt_bias_fwd[l], dt_bias_bwd[l],
                     a_log_fwd[l], a_log_bwd[l], d_skip[l], g_ssm_norm[l], w_ssm_out[l],
                     b_glu[l], w_dw[l], b_dw[l], ln_g[l], ln_b[l], w_conv_out[l],
                     b_conv_out[l], b_gate[l], w_mix_out[l])
        x = x + g1 * _rmsnorm(mix, g_post_mix[l])
        h = _rmsnorm(x, g_pre_ffn[l]) * (1 + sc2) + sh2
        x = x + g2 * _rmsnorm(_ffn(h, w_gate_up[l], w_down[l]), g_post_ffn[l])
    return x


import jax as _jax
import jax.numpy as _jnp

TWIN_FORMAT = 'train_step'
FWD_PARAMS = ['x', 'c', 'w_ada', 'b_ada', 'g_pre_mix', 'g_post_mix', 'w_in', 'w_conv_ssm', 'b_conv_ssm', 'dt_bias_fwd', 'dt_bias_bwd', 'a_log_fwd', 'a_log_bwd', 'd_skip', 'g_ssm_norm', 'w_ssm_out', 'b_glu', 'w_dw', 'b_dw', 'ln_g', 'ln_b', 'w_conv_out', 'b_conv_out', 'b_gate', 'w_mix_out', 'g_pre_ffn', 'g_post_ffn', 'w_gate_up', 'w_down']
TWIN_WEIGHTS = ['w_ada', 'b_ada', 'g_pre_mix', 'g_post_mix', 'w_in', 'w_conv_ssm', 'b_conv_ssm', 'dt_bias_fwd', 'dt_bias_bwd', 'a_log_fwd', 'a_log_bwd', 'd_skip', 'g_ssm_norm', 'w_ssm_out', 'b_glu', 'w_dw', 'b_dw', 'ln_g', 'ln_b', 'w_conv_out', 'b_conv_out', 'b_gate', 'w_mix_out', 'g_pre_ffn', 'g_post_ffn', 'w_gate_up', 'w_down']
TWIN_DIFF_INPUT = 'x'
TWIN_INPUTS = ['x', 'c', 'w_ada', 'b_ada', 'g_pre_mix', 'g_post_mix', 'w_in', 'w_conv_ssm', 'b_conv_ssm', 'dt_bias_fwd', 'dt_bias_bwd', 'a_log_fwd', 'a_log_bwd', 'd_skip', 'g_ssm_norm', 'w_ssm_out', 'b_glu', 'w_dw', 'b_dw', 'ln_g', 'ln_b', 'w_conv_out', 'b_conv_out', 'b_gate', 'w_mix_out', 'g_pre_ffn', 'g_post_ffn', 'w_gate_up', 'w_down', 'loss_target', 'm_w_ada', 'm_b_ada', 'm_g_pre_mix', 'm_g_post_mix', 'm_w_in', 'm_w_conv_ssm', 'm_b_conv_ssm', 'm_dt_bias_fwd', 'm_dt_bias_bwd', 'm_a_log_fwd', 'm_a_log_bwd', 'm_d_skip', 'm_g_ssm_norm', 'm_w_ssm_out', 'm_b_glu', 'm_w_dw', 'm_b_dw', 'm_ln_g', 'm_ln_b', 'm_w_conv_out', 'm_b_conv_out', 'm_b_gate', 'm_w_mix_out', 'm_g_pre_ffn', 'm_g_post_ffn', 'm_w_gate_up', 'm_w_down', 'v_w_ada', 'v_b_ada', 'v_g_pre_mix', 'v_g_post_mix', 'v_w_in', 'v_w_conv_ssm', 'v_b_conv_ssm', 'v_dt_bias_fwd', 'v_dt_bias_bwd', 'v_a_log_fwd', 'v_a_log_bwd', 'v_d_skip', 'v_g_ssm_norm', 'v_w_ssm_out', 'v_b_glu', 'v_w_dw', 'v_b_dw', 'v_ln_g', 'v_ln_b', 'v_w_conv_out', 'v_b_conv_out', 'v_b_gate', 'v_w_mix_out', 'v_g_pre_ffn', 'v_g_post_ffn', 'v_w_gate_up', 'v_w_down']
TWIN_OUTPUTS = ['loss', 'grad_x', 'grad_w_ada', 'grad_b_ada', 'grad_g_pre_mix', 'grad_g_post_mix', 'grad_w_in', 'grad_w_conv_ssm', 'grad_b_conv_ssm', 'grad_dt_bias_fwd', 'grad_dt_bias_bwd', 'grad_a_log_fwd', 'grad_a_log_bwd', 'grad_d_skip', 'grad_g_ssm_norm', 'grad_w_ssm_out', 'grad_b_glu', 'grad_w_dw', 'grad_b_dw', 'grad_ln_g', 'grad_ln_b', 'grad_w_conv_out', 'grad_b_conv_out', 'grad_b_gate', 'grad_w_mix_out', 'grad_g_pre_ffn', 'grad_g_post_ffn', 'grad_w_gate_up', 'grad_w_down', 'delta_w_ada', 'delta_b_ada', 'delta_g_pre_mix', 'delta_g_post_mix', 'delta_w_in', 'delta_w_conv_ssm', 'delta_b_conv_ssm', 'delta_dt_bias_fwd', 'delta_dt_bias_bwd', 'delta_a_log_fwd', 'delta_a_log_bwd', 'delta_d_skip', 'delta_g_ssm_norm', 'delta_w_ssm_out', 'delta_b_glu', 'delta_w_dw', 'delta_b_dw', 'delta_ln_g', 'delta_ln_b', 'delta_w_conv_out', 'delta_b_conv_out', 'delta_b_gate', 'delta_w_mix_out', 'delta_g_pre_ffn', 'delta_g_post_ffn', 'delta_w_gate_up', 'delta_w_down', 'new_m_w_ada', 'new_m_b_ada', 'new_m_g_pre_mix', 'new_m_g_post_mix', 'new_m_w_in', 'new_m_w_conv_ssm', 'new_m_b_conv_ssm', 'new_m_dt_bias_fwd', 'new_m_dt_bias_bwd', 'new_m_a_log_fwd', 'new_m_a_log_bwd', 'new_m_d_skip', 'new_m_g_ssm_norm', 'new_m_w_ssm_out', 'new_m_b_glu', 'new_m_w_dw', 'new_m_b_dw', 'new_m_ln_g', 'new_m_ln_b', 'new_m_w_conv_out', 'new_m_b_conv_out', 'new_m_b_gate', 'new_m_w_mix_out', 'new_m_g_pre_ffn', 'new_m_g_post_ffn', 'new_m_w_gate_up', 'new_m_w_down', 'new_v_w_ada', 'new_v_b_ada', 'new_v_g_pre_mix', 'new_v_g_post_mix', 'new_v_w_in', 'new_v_w_conv_ssm', 'new_v_b_conv_ssm', 'new_v_dt_bias_fwd', 'new_v_dt_bias_bwd', 'new_v_a_log_fwd', 'new_v_a_log_bwd', 'new_v_d_skip', 'new_v_g_ssm_norm', 'new_v_w_ssm_out', 'new_v_b_glu', 'new_v_w_dw', 'new_v_b_dw', 'new_v_ln_g', 'new_v_ln_b', 'new_v_w_conv_out', 'new_v_b_conv_out', 'new_v_b_gate', 'new_v_w_mix_out', 'new_v_g_pre_ffn', 'new_v_g_post_ffn', 'new_v_w_gate_up', 'new_v_w_down']
TWIN_LEAF_KINDS = {'loss': 'loss', 'grad_x': 'grad_x', 'grad_w_ada': 'grad_w', 'grad_b_ada': 'grad_w', 'grad_g_pre_mix': 'grad_w', 'grad_g_post_mix': 'grad_w', 'grad_w_in': 'grad_w', 'grad_w_conv_ssm': 'grad_w', 'grad_b_conv_ssm': 'grad_w', 'grad_dt_bias_fwd': 'grad_w', 'grad_dt_bias_bwd': 'grad_w', 'grad_a_log_fwd': 'grad_w', 'grad_a_log_bwd': 'grad_w', 'grad_d_skip': 'grad_w', 'grad_g_ssm_norm': 'grad_w', 'grad_w_ssm_out': 'grad_w', 'grad_b_glu': 'grad_w', 'grad_w_dw': 'grad_w', 'grad_b_dw': 'grad_w', 'grad_ln_g': 'grad_w', 'grad_ln_b': 'grad_w', 'grad_w_conv_out': 'grad_w', 'grad_b_conv_out': 'grad_w', 'grad_b_gate': 'grad_w', 'grad_w_mix_out': 'grad_w', 'grad_g_pre_ffn': 'grad_w', 'grad_g_post_ffn': 'grad_w', 'grad_w_gate_up': 'grad_w', 'grad_w_down': 'grad_w', 'delta_w_ada': 'delta_w', 'delta_b_ada': 'delta_w', 'delta_g_pre_mix': 'delta_w', 'delta_g_post_mix': 'delta_w', 'delta_w_in': 'delta_w', 'delta_w_conv_ssm': 'delta_w', 'delta_b_conv_ssm': 'delta_w', 'delta_dt_bias_fwd': 'delta_w', 'delta_dt_bias_bwd': 'delta_w', 'delta_a_log_fwd': 'delta_w', 'delta_a_log_bwd': 'delta_w', 'delta_d_skip': 'delta_w', 'delta_g_ssm_norm': 'delta_w', 'delta_w_ssm_out': 'delta_w', 'delta_b_glu': 'delta_w', 'delta_w_dw': 'delta_w', 'delta_b_dw': 'delta_w', 'delta_ln_g': 'delta_w', 'delta_ln_b': 'delta_w', 'delta_w_conv_out': 'delta_w', 'delta_b_conv_out': 'delta_w', 'delta_b_gate': 'delta_w', 'delta_w_mix_out': 'delta_w', 'delta_g_pre_ffn': 'delta_w', 'delta_g_post_ffn': 'delta_w', 'delta_w_gate_up': 'delta_w', 'delta_w_down': 'delta_w', 'new_m_w_ada': 'new_m', 'new_m_b_ada': 'new_m', 'new_m_g_pre_mix': 'new_m', 'new_m_g_post_mix': 'new_m', 'new_m_w_in': 'new_m', 'new_m_w_conv_ssm': 'new_m', 'new_m_b_conv_ssm': 'new_m', 'new_m_dt_bias_fwd': 'new_m', 'new_m_dt_bias_bwd': 'new_m', 'new_m_a_log_fwd': 'new_m', 'new_m_a_log_bwd': 'new_m', 'new_m_d_skip': 'new_m', 'new_m_g_ssm_norm': 'new_m', 'new_m_w_ssm_out': 'new_m', 'new_m_b_glu': 'new_m', 'new_m_w_dw': 'new_m', 'new_m_b_dw': 'new_m', 'new_m_ln_g': 'new_m', 'new_m_ln_b': 'new_m', 'new_m_w_conv_out': 'new_m', 'new_m_b_conv_out': 'new_m', 'new_m_b_gate': 'new_m', 'new_m_w_mix_out': 'new_m', 'new_m_g_pre_ffn': 'new_m', 'new_m_g_post_ffn': 'new_m', 'new_m_w_gate_up': 'new_m', 'new_m_w_down': 'new_m', 'new_v_w_ada': 'new_v', 'new_v_b_ada': 'new_v', 'new_v_g_pre_mix': 'new_v', 'new_v_g_post_mix': 'new_v', 'new_v_w_in': 'new_v', 'new_v_w_conv_ssm': 'new_v', 'new_v_b_conv_ssm': 'new_v', 'new_v_dt_bias_fwd': 'new_v', 'new_v_dt_bias_bwd': 'new_v', 'new_v_a_log_fwd': 'new_v', 'new_v_a_log_bwd': 'new_v', 'new_v_d_skip': 'new_v', 'new_v_g_ssm_norm': 'new_v', 'new_v_w_ssm_out': 'new_v', 'new_v_b_glu': 'new_v', 'new_v_w_dw': 'new_v', 'new_v_b_dw': 'new_v', 'new_v_ln_g': 'new_v', 'new_v_ln_b': 'new_v', 'new_v_w_conv_out': 'new_v', 'new_v_b_conv_out': 'new_v', 'new_v_b_gate': 'new_v', 'new_v_w_mix_out': 'new_v', 'new_v_g_pre_ffn': 'new_v', 'new_v_g_post_ffn': 'new_v', 'new_v_w_gate_up': 'new_v', 'new_v_w_down': 'new_v'}


def _forward(args):
    return _fwd_reference(*[args[k] for k in FWD_PARAMS])


def _output_shape():
    def fwd():
        inp = _fwd_setup_inputs(0)
        return _fwd_reference(*[inp[k] for k in FWD_PARAMS])
    out = _jax.eval_shape(fwd)
    return out.shape, out.dtype

N_MICROBATCH = 1
ADAM_LR = 0.001
ADAM_B1 = 0.9
ADAM_B2 = 0.999
ADAM_EPS = 1e-08
ADAM_WD = 0.01
ADAM_STEP = 10
PER_EXAMPLE_BATCH_AXIS = {'x': 0, 'c': 0, 'loss_target': 0}
SHARED_INPUTS = []
_WEIGHT_DTYPES = {'w_ada': _jnp.float32, 'b_ada': _jnp.float32, 'g_pre_mix': _jnp.float32, 'g_post_mix': _jnp.float32, 'w_in': _jnp.float32, 'w_conv_ssm': _jnp.float32, 'b_conv_ssm': _jnp.float32, 'dt_bias_fwd': _jnp.float32, 'dt_bias_bwd': _jnp.float32, 'a_log_fwd': _jnp.float32, 'a_log_bwd': _jnp.float32, 'd_skip': _jnp.float32, 'g_ssm_norm': _jnp.float32, 'w_ssm_out': _jnp.float32, 'b_glu': _jnp.float32, 'w_dw': _jnp.float32, 'b_dw': _jnp.float32, 'ln_g': _jnp.float32, 'ln_b': _jnp.float32, 'w_conv_out': _jnp.float32, 'b_conv_out': _jnp.float32, 'b_gate': _jnp.float32, 'w_mix_out': _jnp.float32, 'g_pre_ffn': _jnp.float32, 'g_post_ffn': _jnp.float32, 'w_gate_up': _jnp.float32, 'w_down': _jnp.float32}
MOMENT_SCALE = {'w_ada': 1.227675e+00, 'b_ada': 2.715291e+00, 'g_pre_mix': 8.749603e-02, 'g_post_mix': 3.304944e+00, 'w_in': 3.635631e-02, 'w_conv_ssm': 4.629472e-02, 'b_conv_ssm': 1.090104e-01, 'dt_bias_fwd': 7.496961e-02, 'dt_bias_bwd': 7.024632e-02, 'a_log_fwd': 1.916756e-01, 'a_log_bwd': 2.819532e-01, 'd_skip': 1.819334e-01, 'g_ssm_norm': 7.303451e-02, 'w_ssm_out': 9.875364e-02, 'b_glu': 1.152084e-01, 'w_dw': 4.539661e-02, 'b_dw': 2.704303e-01, 'ln_g': 1.122014e-01, 'ln_b': 1.649481e-01, 'w_conv_out': 7.131053e-02, 'b_conv_out': 3.333650e-01, 'b_gate': 3.378514e-02, 'w_mix_out': 1.212989e-01, 'g_pre_ffn': 8.690674e-02, 'g_post_ffn': 3.283165e+00, 'w_gate_up': 3.969085e-02, 'w_down': 6.985972e-02}


def _to_microbatches(a, axis):
    t = _jnp.moveaxis(a, axis, 0)
    t = t.reshape((N_MICROBATCH, t.shape[0] // N_MICROBATCH) + t.shape[1:])
    return _jnp.moveaxis(t, 1, axis + 1)


def setup_inputs(seed: int = 0) -> dict:
    inp = _fwd_setup_inputs(seed)
    key = _jax.random.fold_in(_jax.random.key(seed), 7919)
    shape, _ = _output_shape()
    out = dict(inp)
    out["loss_target"] = _jax.random.normal(_jax.random.fold_in(key, 0), shape, _jnp.float32)
    for i, name in enumerate(TWIN_WEIGHTS):
        w = inp[name].astype(_jnp.float32)
        if MOMENT_SCALE is None:
            s = _jnp.sqrt(_jnp.mean(_jnp.square(w)) + 1e-30)
        else:
            s = MOMENT_SCALE[name]
        km, kv = _jax.random.split(_jax.random.fold_in(key, i + 1))
        out[name] = w
        out["m_" + name] = s * _jax.random.normal(km, w.shape, _jnp.float32)
        out["v_" + name] = (s * s) * _jax.random.uniform(kv, w.shape, _jnp.float32, 0.5, 1.5)
    if N_MICROBATCH > 1:
        for name, axis in PER_EXAMPLE_BATCH_AXIS.items():
            out[name] = _to_microbatches(out[name], axis)
    return {'x': out['x'], 'c': out['c'], 'w_ada': out['w_ada'], 'b_ada': out['b_ada'], 'g_pre_mix': out['g_pre_mix'], 'g_post_mix': out['g_post_mix'], 'w_in': out['w_in'], 'w_conv_ssm': out['w_conv_ssm'], 'b_conv_ssm': out['b_conv_ssm'], 'dt_bias_fwd': out['dt_bias_fwd'], 'dt_bias_bwd': out['dt_bias_bwd'], 'a_log_fwd': out['a_log_fwd'], 'a_log_bwd': out['a_log_bwd'], 'd_skip': out['d_skip'], 'g_ssm_norm': out['g_ssm_norm'], 'w_ssm_out': out['w_ssm_out'], 'b_glu': out['b_glu'], 'w_dw': out['w_dw'], 'b_dw': out['b_dw'], 'ln_g': out['ln_g'], 'ln_b': out['ln_b'], 'w_conv_out': out['w_conv_out'], 'b_conv_out': out['b_conv_out'], 'b_gate': out['b_gate'], 'w_mix_out': out['w_mix_out'], 'g_pre_ffn': out['g_pre_ffn'], 'g_post_ffn': out['g_post_ffn'], 'w_gate_up': out['w_gate_up'], 'w_down': out['w_down'], 'loss_target': out['loss_target'], 'm_w_ada': out['m_w_ada'], 'm_b_ada': out['m_b_ada'], 'm_g_pre_mix': out['m_g_pre_mix'], 'm_g_post_mix': out['m_g_post_mix'], 'm_w_in': out['m_w_in'], 'm_w_conv_ssm': out['m_w_conv_ssm'], 'm_b_conv_ssm': out['m_b_conv_ssm'], 'm_dt_bias_fwd': out['m_dt_bias_fwd'], 'm_dt_bias_bwd': out['m_dt_bias_bwd'], 'm_a_log_fwd': out['m_a_log_fwd'], 'm_a_log_bwd': out['m_a_log_bwd'], 'm_d_skip': out['m_d_skip'], 'm_g_ssm_norm': out['m_g_ssm_norm'], 'm_w_ssm_out': out['m_w_ssm_out'], 'm_b_glu': out['m_b_glu'], 'm_w_dw': out['m_w_dw'], 'm_b_dw': out['m_b_dw'], 'm_ln_g': out['m_ln_g'], 'm_ln_b': out['m_ln_b'], 'm_w_conv_out': out['m_w_conv_out'], 'm_b_conv_out': out['m_b_conv_out'], 'm_b_gate': out['m_b_gate'], 'm_w_mix_out': out['m_w_mix_out'], 'm_g_pre_ffn': out['m_g_pre_ffn'], 'm_g_post_ffn': out['m_g_post_ffn'], 'm_w_gate_up': out['m_w_gate_up'], 'm_w_down': out['m_w_down'], 'v_w_ada': out['v_w_ada'], 'v_b_ada': out['v_b_ada'], 'v_g_pre_mix': out['v_g_pre_mix'], 'v_g_post_mix': out['v_g_post_mix'], 'v_w_in': out['v_w_in'], 'v_w_conv_ssm': out['v_w_conv_ssm'], 'v_b_conv_ssm': out['v_b_conv_ssm'], 'v_dt_bias_fwd': out['v_dt_bias_fwd'], 'v_dt_bias_bwd': out['v_dt_bias_bwd'], 'v_a_log_fwd': out['v_a_log_fwd'], 'v_a_log_bwd': out['v_a_log_bwd'], 'v_d_skip': out['v_d_skip'], 'v_g_ssm_norm': out['v_g_ssm_norm'], 'v_w_ssm_out': out['v_w_ssm_out'], 'v_b_glu': out['v_b_glu'], 'v_w_dw': out['v_w_dw'], 'v_b_dw': out['v_b_dw'], 'v_ln_g': out['v_ln_g'], 'v_ln_b': out['v_ln_b'], 'v_w_conv_out': out['v_w_conv_out'], 'v_b_conv_out': out['v_b_conv_out'], 'v_b_gate': out['v_b_gate'], 'v_w_mix_out': out['v_w_mix_out'], 'v_g_pre_ffn': out['v_g_pre_ffn'], 'v_g_post_ffn': out['v_g_post_ffn'], 'v_w_gate_up': out['v_w_gate_up'], 'v_w_down': out['v_w_down']}


def _loss(weights, diff, rest, loss_target):
    with _jax.named_scope("forward"):
        args = {**rest, TWIN_DIFF_INPUT: diff, **{k: w.astype(_WEIGHT_DTYPES[k]) for k, w in weights.items()}}
        y = _forward(args)
    with _jax.named_scope("loss_head"):
        err = _jnp.square(y.astype(_jnp.float32) - loss_target)
        return 0.5 * _jnp.sum(_jnp.mean(err, axis=-1)) if err.ndim else 0.5 * err


def _adamw(w, g, m, v):
    m = ADAM_B1 * m + (1.0 - ADAM_B1) * g
    v = ADAM_B2 * v + (1.0 - ADAM_B2) * _jnp.square(g)
    m_hat = m / (1.0 - ADAM_B1 ** ADAM_STEP)
    v_hat = v / (1.0 - ADAM_B2 ** ADAM_STEP)
    delta = -ADAM_LR * (m_hat / (_jnp.sqrt(v_hat) + ADAM_EPS) + ADAM_WD * w)
    return delta, m, v


def reference(x, c, w_ada, b_ada, g_pre_mix, g_post_mix, w_in, w_conv_ssm, b_conv_ssm, dt_bias_fwd, dt_bias_bwd, a_log_fwd, a_log_bwd, d_skip, g_ssm_norm, w_ssm_out, b_glu, w_dw, b_dw, ln_g, ln_b, w_conv_out, b_conv_out, b_gate, w_mix_out, g_pre_ffn, g_post_ffn, w_gate_up, w_down, loss_target, m_w_ada, m_b_ada, m_g_pre_mix, m_g_post_mix, m_w_in, m_w_conv_ssm, m_b_conv_ssm, m_dt_bias_fwd, m_dt_bias_bwd, m_a_log_fwd, m_a_log_bwd, m_d_skip, m_g_ssm_norm, m_w_ssm_out, m_b_glu, m_w_dw, m_b_dw, m_ln_g, m_ln_b, m_w_conv_out, m_b_conv_out, m_b_gate, m_w_mix_out, m_g_pre_ffn, m_g_post_ffn, m_w_gate_up, m_w_down, v_w_ada, v_b_ada, v_g_pre_mix, v_g_post_mix, v_w_in, v_w_conv_ssm, v_b_conv_ssm, v_dt_bias_fwd, v_dt_bias_bwd, v_a_log_fwd, v_a_log_bwd, v_d_skip, v_g_ssm_norm, v_w_ssm_out, v_b_glu, v_w_dw, v_b_dw, v_ln_g, v_ln_b, v_w_conv_out, v_b_conv_out, v_b_gate, v_w_mix_out, v_g_pre_ffn, v_g_post_ffn, v_w_gate_up, v_w_down):
    given = dict(x=x, c=c, w_ada=w_ada, b_ada=b_ada, g_pre_mix=g_pre_mix, g_post_mix=g_post_mix, w_in=w_in, w_conv_ssm=w_conv_ssm, b_conv_ssm=b_conv_ssm, dt_bias_fwd=dt_bias_fwd, dt_bias_bwd=dt_bias_bwd, a_log_fwd=a_log_fwd, a_log_bwd=a_log_bwd, d_skip=d_skip, g_ssm_norm=g_ssm_norm, w_ssm_out=w_ssm_out, b_glu=b_glu, w_dw=w_dw, b_dw=b_dw, ln_g=ln_g, ln_b=ln_b, w_conv_out=w_conv_out, b_conv_out=b_conv_out, b_gate=b_gate, w_mix_out=w_mix_out, g_pre_ffn=g_pre_ffn, g_post_ffn=g_post_ffn, w_gate_up=w_gate_up, w_down=w_down, loss_target=loss_target, m_w_ada=m_w_ada, m_b_ada=m_b_ada, m_g_pre_mix=m_g_pre_mix, m_g_post_mix=m_g_post_mix, m_w_in=m_w_in, m_w_conv_ssm=m_w_conv_ssm, m_b_conv_ssm=m_b_conv_ssm, m_dt_bias_fwd=m_dt_bias_fwd, m_dt_bias_bwd=m_dt_bias_bwd, m_a_log_fwd=m_a_log_fwd, m_a_log_bwd=m_a_log_bwd, m_d_skip=m_d_skip, m_g_ssm_norm=m_g_ssm_norm, m_w_ssm_out=m_w_ssm_out, m_b_glu=m_b_glu, m_w_dw=m_w_dw, m_b_dw=m_b_dw, m_ln_g=m_ln_g, m_ln_b=m_ln_b, m_w_conv_out=m_w_conv_out, m_b_conv_out=m_b_conv_out, m_b_gate=m_b_gate, m_w_mix_out=m_w_mix_out, m_g_pre_ffn=m_g_pre_ffn, m_g_post_ffn=m_g_post_ffn, m_w_gate_up=m_w_gate_up, m_w_down=m_w_down, v_w_ada=v_w_ada, v_b_ada=v_b_ada, v_g_pre_mix=v_g_pre_mix, v_g_post_mix=v_g_post_mix, v_w_in=v_w_in, v_w_conv_ssm=v_w_conv_ssm, v_b_conv_ssm=v_b_conv_ssm, v_dt_bias_fwd=v_dt_bias_fwd, v_dt_bias_bwd=v_dt_bias_bwd, v_a_log_fwd=v_a_log_fwd, v_a_log_bwd=v_a_log_bwd, v_d_skip=v_d_skip, v_g_ssm_norm=v_g_ssm_norm, v_w_ssm_out=v_w_ssm_out, v_b_glu=v_b_glu, v_w_dw=v_w_dw, v_b_dw=v_b_dw, v_ln_g=v_ln_g, v_ln_b=v_ln_b, v_w_conv_out=v_w_conv_out, v_b_conv_out=v_b_conv_out, v_b_gate=v_b_gate, v_w_mix_out=v_w_mix_out, v_g_pre_ffn=v_g_pre_ffn, v_g_post_ffn=v_g_post_ffn, v_w_gate_up=v_w_gate_up, v_w_down=v_w_down)
    weights = {n: given[n] for n in TWIN_WEIGHTS}
    shared = {n: given[n] for n in SHARED_INPUTS}
    per_example = {n: given[n] for n in ['x', 'c']}
    grad_fn = _jax.value_and_grad(_loss, argnums=(0, 1))

    def one_microbatch(ex, loss_target):
        ex = dict(ex)
        diff = ex.pop(TWIN_DIFF_INPUT)
        return grad_fn(weights, diff, {**shared, **ex}, loss_target)

    if N_MICROBATCH == 1:
        loss, (grad_w, grad_x) = one_microbatch(per_example, given["loss_target"])
    else:
        def body(carry, xs):
            loss_sum, grad_sum = carry
            l_k, (gw_k, gx_k) = one_microbatch(xs[0], xs[1])
            with _jax.named_scope("update"):
                return (loss_sum + l_k, _jax.tree.map(_jnp.add, grad_sum, gw_k)), gx_k

        init = (_jnp.zeros((), _jnp.float32), _jax.tree.map(_jnp.zeros_like, weights))
        (loss, grad_w), grad_x = _jax.lax.scan(body, init, (per_example, given["loss_target"]))
    with _jax.named_scope("update"):
        delta_w, new_m, new_v = {}, {}, {}
        for n in TWIN_WEIGHTS:
            delta_w[n], new_m[n], new_v[n] = _adamw(weights[n], grad_w[n], given["m_" + n], given["v_" + n])
    return (loss, grad_x, *[grad_w[n] for n in TWIN_WEIGHTS], *[delta_w[n] for n in TWIN_WEIGHTS],
            *[new_m[n] for n in TWIN_WEIGHTS], *[new_v[n] for n in TWIN_WEIGHTS])
```

```python
import functools

import numpy as np
import jax
import jax.numpy as jnp
from jax import lax
from jax.experimental import pallas as pl
from jax.experimental.pallas import tpu as pltpu

F32 = jnp.float32
BF16 = jnp.bfloat16
HIGHEST = lax.Precision.HIGHEST
MESH = pl.DeviceIdType.MESH
N_DEV = 8
EPS = 1e-6
CHUNK = 128
HEAD_DIM = 64
D_STATE = 128
HEADS_PER_GROUP = 8
GROUP_X = HEADS_PER_GROUP * HEAD_DIM
GROUP_W = GROUP_X + 2 * D_STATE
COL_BLOCK = 512
HALO = 16
VMEM_LIMIT = 56 * 1024 * 1024
ADAM_LR, ADAM_B1, ADAM_B2, ADAM_EPS, ADAM_WD, ADAM_STEP = 0.001, 0.9, 0.999, 1e-08, 0.01, 10

NN = (((1,), (0,)), ((), ()))
NT = (((1,), (1,)), ((), ()))
TN = (((0,), (0,)), ((), ()))


def _tile(dim, prefs):
    for p in prefs:
        if p <= dim and dim % p == 0:
            return p
    return dim


def _params(n_grid):
    return pltpu.CompilerParams(dimension_semantics=("arbitrary",) * n_grid, vmem_limit_bytes=VMEM_LIMIT)


def _f(v):
    return v.astype(F32)


def _matmul(name, a, b, mode, out_dtype, acc_in=None, highest=False):
    if mode == "nn":
        (m, k), (k2, n) = a.shape, b.shape
    elif mode == "nt":
        (m, k), (n, k2) = a.shape, b.shape
    else:
        (k, m), (k2, n) = a.shape, b.shape
    assert k == k2, (name, a.shape, b.shape, mode)
    tm = _tile(m, (1024, 512, 256, 128))
    tn = _tile(n, (1024, 512, 256, 128))
    tk = _tile(k, (1024, 512, 256, 128))
    nk = k // tk
    dims = {"nn": NN, "nt": NT, "tn": TN}[mode]
    a_spec = pl.BlockSpec((tk, tm), lambda i, j, kk: (kk, i)) if mode == "tn" else pl.BlockSpec((tm, tk), lambda i, j, kk: (i, kk))
    b_spec = pl.BlockSpec((tn, tk), lambda i, j, kk: (j, kk)) if mode == "nt" else pl.BlockSpec((tk, tn), lambda i, j, kk: (kk, j))
    o_spec = pl.BlockSpec((tm, tn), lambda i, j, kk: (i, j))
    has_acc = acc_in is not None

    def body(*refs):
        if has_acc:
            a_ref, b_ref, c_ref, o_ref, acc = refs
        else:
            a_ref, b_ref, o_ref, acc = refs
        kk = pl.program_id(2)

        @pl.when(kk == 0)
        def _():
            acc[...] = _f(c_ref[...]) if has_acc else jnp.zeros_like(acc)

        if highest:
            acc[...] += lax.dot_general(_f(a_ref[...]), _f(b_ref[...]), dims, precision=HIGHEST, preferred_element_type=F32)
        else:
            acc[...] += lax.dot_general(a_ref[...].astype(BF16), b_ref[...].astype(BF16), dims, preferred_element_type=F32)

        @pl.when(kk == nk - 1)
        def _():
            o_ref[...] = acc[...].astype(out_dtype)

    return pl.pallas_call(
        body, name=name, grid=(m // tm, n // tn, nk),
        in_specs=[a_spec, b_spec] + ([o_spec] if has_acc else []), out_specs=o_spec,
        out_shape=jax.ShapeDtypeStruct((m, n), out_dtype),
        scratch_shapes=[pltpu.VMEM((tm, tn), F32)], compiler_params=_params(3),
    )(*([a, b] + ([acc_in] if has_acc else [])))


def _rowcall(name, fn, rows, params, out_rows, out_accs, tm=256, ncol=1):
    s = rows[0][0].shape[0]
    tm = _tile(s, (tm, 128, 64, 32, 16, 8))
    args, in_specs = [], []
    for arr, off, w in rows:
        bw = w // ncol
        assert w % ncol == 0 and off % bw == 0, (name, off, w, ncol)
        in_specs.append(pl.BlockSpec((tm, bw), functools.partial(lambda j, i, ob: (i, ob + j), ob=off // bw)))
        args.append(arr)
    for p in params:
        assert p.shape[1] % ncol == 0, (name, p.shape)
        in_specs.append(pl.BlockSpec((p.shape[0], p.shape[1] // ncol), lambda j, i: (0, j)))
        args.append(p)
    out_shape, out_specs = [], []
    for w, dt in out_rows:
        out_shape.append(jax.ShapeDtypeStruct((s, w), dt))
        out_specs.append(pl.BlockSpec((tm, w // ncol), lambda j, i: (i, j)))
    for w in out_accs:
        out_shape.append(jax.ShapeDtypeStruct((1, w), F32))
        out_specs.append(pl.BlockSpec((1, w // ncol), lambda j, i: (0, j)))
    n_in, n_ro = len(args), len(out_rows)

    def body(*refs):
        i = pl.program_id(1)
        ro, ao = fn(*[r[...] for r in refs[:n_in]])
        outs = refs[n_in:]
        assert len(ro) == n_ro and len(ao) == len(out_accs), name
        for ref, v in zip(outs[:n_ro], ro):
            ref[...] = v.astype(ref.dtype)
        for ref, v in zip(outs[n_ro:], ao):
            @pl.when(i == 0)
            def _(ref=ref, v=v):
                ref[...] = v

            @pl.when(i > 0)
            def _(ref=ref, v=v):
                ref[...] += v

    res = pl.pallas_call(
        body, name=name, grid=(ncol, s // tm), in_specs=in_specs, out_specs=out_specs,
        out_shape=out_shape, compiler_params=_params(2),
    )(*args)
    return res[:n_ro], res[n_ro:]


def _fwd_fn(f):
    def fn(*vals):
        out = f(*[_f(v) for v in vals])
        return (out if isinstance(out, tuple) else (out,)), ()
    return fn


def _vjp_fn(f, n_prim, n_ct, want_rows, want_params):
    def fn(*vals):
        prims = [_f(v) for v in vals[:n_prim]]
        cts = [_f(v) for v in vals[n_prim:n_prim + n_ct]]
        prms = [_f(v) for v in vals[n_prim + n_ct:]]
        out, vjp = jax.vjp(f, *prims, *prms)
        g = vjp(tuple(cts) if isinstance(out, tuple) else cts[0])
        return tuple(g[k] for k in want_rows), tuple(g[n_prim + k] for k in want_params)
    return fn


def _rms(v):
    return v * lax.rsqrt(jnp.mean(v * v, axis=-1, keepdims=True) + EPS)


def _silu(v):
    return v * jax.nn.sigmoid(v)


def _t_prenorm(x, g, sc, sh):
    return _rms(x) * g * (1.0 + sc) + sh


def _t_gnorm(yf, yb, z, gn):
    return _rms((yf + yb) * _silu(z)) * gn


def _t_glu(a, b, ba, bb):
    return (a + ba) * jax.nn.sigmoid(b + bb)


def _t_lnsilu(u, g, b):
    uc = u - jnp.mean(u, axis=-1, keepdims=True)
    return _silu(uc * lax.rsqrt(jnp.mean(uc * uc, axis=-1, keepdims=True) + EPS) * g + b)


def _t_gatemix(ya, yb, la, lb, bco, bga, bgb):
    return jax.nn.sigmoid(la + bga) * ya + jax.nn.sigmoid(lb + bgb) * (yb + bco)


def _t_postpre(x, mix, gpost, g1, gpre, sc, sh):
    x1 = x + g1 * (_rms(mix) * gpost)
    return x1, _rms(x1) * gpre * (1.0 + sc) + sh


def _t_swiglu(gt, up):
    return _silu(gt) * up


def _t_loss(x1, f, tgt, gpost, g2):
    e = x1 + g2 * (_rms(f) * gpost) - tgt
    return 0.5 * jnp.sum(jnp.mean(e * e, axis=-1))


def _conv_specs(s, tm, cw, ob, w_rows):
    nh, last_h = tm // HALO, s // HALO - 1
    return [
        pl.BlockSpec((tm, cw), lambda j, i: (i, ob + j)),
        pl.BlockSpec((HALO, cw), lambda j, i: (jnp.maximum(i * nh - 1, 0), ob + j)),
        pl.BlockSpec((HALO, cw), lambda j, i: (jnp.minimum((i + 1) * nh, last_h), ob + j)),
        pl.BlockSpec((w_rows, cw), lambda j, i: (0, j)),
        pl.BlockSpec((1, cw), lambda j, i: (0, j)),
    ]


def _window(x_ref, p_ref, n_ref, i, nrow):
    prev = _f(p_ref[...]) * jnp.where(i > 0, 1.0, 0.0).astype(F32)
    nxt = _f(n_ref[...]) * jnp.where(i < nrow - 1, 1.0, 0.0).astype(F32)
    return jnp.concatenate([prev, _f(x_ref[...]), nxt], axis=0)


def _taps(win, w_ref, k_taps, tm):
    pad = (k_taps - 1) // 2
    acc = w_ref[0:1, :] * win[HALO - pad:HALO - pad + tm, :]
    for k in range(1, k_taps):
        acc = acc + w_ref[k:k + 1, :] * win[HALO + k - pad:HALO + k - pad + tm, :]
    return acc


def _dwconv(name, x, xoff, c, w_pad, k_taps, b, act, out_dtype, tm=256, cw=256):
    s = x.shape[0]
    tm, cw = _tile(s, (tm, 128, 64, 32, 16)), _tile(c, (cw, 128))
    assert xoff % cw == 0 and tm % HALO == 0 and (k_taps - 1) // 2 <= HALO
    nrow = s // tm

    def body(x_ref, p_ref, n_ref, w_ref, b_ref, o_ref):
        win = _window(x_ref, p_ref, n_ref, pl.program_id(1), nrow)
        acc = _taps(win, w_ref, k_taps, tm) + b_ref[...]
        o_ref[...] = (_silu(acc) if act else acc).astype(out_dtype)

    return pl.pallas_call(
        body, name=name, grid=(c // cw, nrow), in_specs=_conv_specs(s, tm, cw, xoff // cw, w_pad.shape[0]),
        out_specs=pl.BlockSpec((tm, cw), lambda j, i: (i, j)), out_shape=jax.ShapeDtypeStruct((s, c), out_dtype),
        compiler_params=_params(2),
    )(x, x, x, w_pad, b)


def _dwconv_bwd_w(name, x, xoff, c, w_pad, k_taps, b, douts, act, tm=256, cw=256):
    s = x.shape[0]
    tm, cw = _tile(s, (tm, 128, 64, 32, 16)), _tile(c, (cw, 128))
    assert xoff % cw == 0 and tm % HALO == 0
    nrow, pad, n_d = s // tm, (k_taps - 1) // 2, len(douts)

    def body(x_ref, p_ref, n_ref, w_ref, b_ref, *rest):
        d_refs, outs = rest[:n_d], rest[n_d:]
        i = pl.program_id(1)
        win = _window(x_ref, p_ref, n_ref, i, nrow)
        d = _f(d_refs[0][...])
        for r in d_refs[1:]:
            d = d + _f(r[...])
        if act:
            pre = _taps(win, w_ref, k_taps, tm) + b_ref[...]
            sg = jax.nn.sigmoid(pre)
            d = d * (sg * (1.0 + pre * (1.0 - sg)))
            outs[0][...] = d.astype(outs[0].dtype)
        dw_ref, db_ref = outs[-2], outs[-1]

        @pl.when(i == 0)
        def _():
            dw_ref[...] = jnp.zeros_like(dw_ref)
            db_ref[...] = jnp.zeros_like(db_ref)

        for k in range(k_taps):
            dw_ref[k:k + 1, :] += jnp.sum(d * win[HALO + k - pad:HALO + k - pad + tm, :], axis=0, keepdims=True)
        db_ref[...] += jnp.sum(d, axis=0, keepdims=True)

    row_spec = pl.BlockSpec((tm, cw), lambda j, i: (i, j))
    out_shape = [jax.ShapeDtypeStruct((w_pad.shape[0], c), F32), jax.ShapeDtypeStruct((1, c), F32)]
    out_specs = [pl.BlockSpec((w_pad.shape[0], cw), lambda j, i: (0, j)), pl.BlockSpec((1, cw), lambda j, i: (0, j))]
    if act:
        out_shape, out_specs = [jax.ShapeDtypeStruct((s, c), BF16)] + out_shape, [row_spec] + out_specs
    return pl.pallas_call(
        body, name=name, grid=(c // cw, nrow),
        in_specs=_conv_specs(s, tm, cw, xoff // cw, w_pad.shape[0]) + [row_spec] * n_d,
        out_specs=out_specs, out_shape=out_shape, compiler_params=_params(2),
    )(x, x, x, w_pad, b, *douts)


def _dtprep(name, dt_raw, bias, alog_col):
    s, h2 = dt_raw.shape
    ts = _tile(s, (512, 256, 128))

    def body(r_ref, b_ref, al_ref, dt_ref, a_ref):
        v = r_ref[...] + b_ref[...]
        dt = (jnp.maximum(v, 0.0) + jnp.log(1.0 + jnp.exp(-jnp.abs(v)))).T
        dt_ref[...] = dt
        a_ref[...] = dt * (-jnp.exp(al_ref[...]))

    t_spec = pl.BlockSpec((h2, ts), lambda i: (0, i))
    return pl.pallas_call(
        body, name=name, grid=(s // ts,),
        in_specs=[pl.BlockSpec((ts, h2), lambda i: (i, 0)), pl.BlockSpec((1, h2), lambda i: (0, 0)), pl.BlockSpec((h2, 1), lambda i: (0, 0))],
        out_specs=[t_spec, t_spec], out_shape=[jax.ShapeDtypeStruct((h2, s), F32)] * 2, compiler_params=_params(1),
    )(dt_raw, bias, alog_col)


def _dtprep_bwd(name, ddt_f, ddt_b, da_f, da_b, dt_t, alog_col):
    h2, s = dt_t.shape
    h = h2 // 2
    ts = _tile(s, (512, 256, 128))

    def body(ddf, ddb, daf, dab, dt_ref, al_ref, raw_ref, db_ref, dal_ref):
        i = pl.program_id(0)
        a_col = -jnp.exp(al_ref[...])
        da = jnp.concatenate([daf[...], dab[...]], axis=0)
        dt = dt_ref[...]
        d_raw = (jnp.concatenate([ddf[...], ddb[...]], axis=0) + da * a_col) * (1.0 - jnp.exp(-dt))
        raw_ref[...] = d_raw.T

        @pl.when(i == 0)
        def _():
            db_ref[...] = jnp.zeros_like(db_ref)
            dal_ref[...] = jnp.zeros_like(dal_ref)

        db_ref[...] += jnp.sum(d_raw, axis=1, keepdims=True)
        dal_ref[...] += jnp.sum(da * dt, axis=1, keepdims=True) * a_col

    half = pl.BlockSpec((h, ts), lambda i: (0, i))
    col = pl.BlockSpec((h2, 1), lambda i: (0, 0))
    return pl.pallas_call(
        body, name=name, grid=(s // ts,),
        in_specs=[half, half, half, half, pl.BlockSpec((h2, ts), lambda i: (0, i)), col],
        out_specs=[pl.BlockSpec((ts, h2), lambda i: (i, 0)), col, col],
        out_shape=[jax.ShapeDtypeStruct((s, h2), F32), jax.ShapeDtypeStruct((h2, 1), F32), jax.ShapeDtypeStruct((h2, 1), F32)],
        compiler_params=_params(1),
    )(ddt_f, ddt_b, da_f, da_b, dt_t, alog_col)


def _chunk_consts(a_r, dt_r, rev):
    ii = lax.broadcasted_iota(jnp.int32, (CHUNK, CHUNK), 0)
    jj = lax.broadcasted_iota(jnp.int32, (CHUNK, CHUNK), 1)
    w = (jj >= ii) if rev else (jj <= ii)
    wt = (ii >= jj) if rev else (ii <= jj)
    wf = w.astype(F32)
    eye = (ii == jj).astype(F32)
    cs_col = lax.dot_general(wf, a_r, NT, precision=HIGHEST, preferred_element_type=F32)
    cs_row = lax.dot_general(a_r, wf, NT, precision=HIGHEST, preferred_element_type=F32)
    dt_col = lax.dot_general(eye, dt_r, NT, precision=HIGHEST, preferred_element_type=F32)
    tot = jnp.sum(a_r, axis=1, keepdims=True)
    return w, wt, eye, cs_col, cs_row, dt_col, tot


def _ssd_fwd(name, xbc, dt_t, a_t, dsk, rev, skip, n_groups):
    s = xbc.shape[0]
    nc = s // CHUNK
    d_ssm = n_groups * GROUP_X
    dir_off = n_groups if rev else 0

    def zmap(zi):
        return nc - 1 - zi if rev else zi

    def body(*refs):
        if skip:
            x_ref, dt_ref, a_ref, dsk_ref, y_ref, hin_ref, h_scr = refs
        else:
            x_ref, dt_ref, a_ref, y_ref, hin_ref, h_scr = refs

        @pl.when(pl.program_id(1) == 0)
        def _():
            h_scr[...] = jnp.zeros_like(h_scr)

        xs = _f(x_ref[:, 0:GROUP_X])
        bm = x_ref[:, GROUP_X:GROUP_X + D_STATE]
        cm = x_ref[:, GROUP_X + D_STATE:GROUP_W]
        w, _, _, cs_col, cs_row, dt_col, tot = _chunk_consts(a_ref[...], dt_ref[...], rev)
        cb = lax.dot_general(cm, bm, NT, preferred_element_type=F32)
        hin_ref[0, 0] = h_scr[...].astype(BF16)
        ys = []
        for j in range(HEADS_PER_GROUP):
            lo = HEAD_DIM * j
            csc, csr, tj = cs_col[:, j:j + 1], cs_row[j:j + 1, :], tot[j:j + 1, 0:1]
            lm = jnp.exp(jnp.where(w, csc - csr, -jnp.inf))
            xj = xs[:, lo:lo + HEAD_DIM]
            xdt = xj * dt_col[:, j:j + 1]
            yd = jnp.dot((cb * lm).astype(BF16), xdt.astype(BF16), preferred_element_type=F32)
            hj = h_scr[lo:lo + HEAD_DIM, :]
            yo = lax.dot_general(cm, hj.astype(BF16), NT, preferred_element_type=F32) * jnp.exp(csc)
            st = lax.dot_general((xdt * jnp.exp(tj - csc)).astype(BF16), bm, TN, preferred_element_type=F32)
            h_scr[lo:lo + HEAD_DIM, :] = hj * jnp.exp(tj) + st
            yj = yd + yo
            if skip:
                yj = yj + dsk_ref[:, lo:lo + HEAD_DIM] * xj
            ys.append(yj)
        y_ref[...] = jnp.concatenate(ys, axis=1).astype(y_ref.dtype)

    t_spec = pl.BlockSpec((HEADS_PER_GROUP, CHUNK), lambda g, zi: (dir_off + g, zmap(zi)))
    in_specs = [pl.BlockSpec((CHUNK, GROUP_W), lambda g, zi: (zmap(zi), g)), t_spec, t_spec]
    args = [xbc, dt_t, a_t]
    if skip:
        in_specs.append(pl.BlockSpec((1, GROUP_X), lambda g, zi: (0, g)))
        args.append(dsk)
    return pl.pallas_call(
        body, name=name, grid=(n_groups, nc), in_specs=in_specs,
        out_specs=[pl.BlockSpec((CHUNK, GROUP_X), lambda g, zi: (zmap(zi), g)),
                   pl.BlockSpec((1, 1, GROUP_X, D_STATE), lambda g, zi: (g, zmap(zi), 0, 0))],
        out_shape=[jax.ShapeDtypeStruct((s, d_ssm), BF16), jax.ShapeDtypeStruct((n_groups, nc, GROUP_X, D_STATE), BF16)],
        scratch_shapes=[pltpu.VMEM((GROUP_X, D_STATE), F32)], compiler_params=_params(2),
    )(*args)


def _ssd_bwd(name, xbc, dt_t, a_t, dy, hin, dsk, rev, skip, n_groups):
    s = xbc.shape[0]
    nc = s // CHUNK
    n_heads = n_groups * HEADS_PER_GROUP
    dir_off = n_groups if rev else 0

    def zmap(zi):
        return zi if rev else nc - 1 - zi

    def body(*refs):
        if skip:
            x_ref, dt_ref, a_ref, dy_ref, hin_ref, dsk_ref, dx_ref, ddt_ref, da_ref, ddsk_ref, g_scr = refs
        else:
            x_ref, dt_ref, a_ref, dy_ref, hin_ref, dx_ref, ddt_ref, da_ref, g_scr = refs
        zi = pl.program_id(1)

        @pl.when(zi == 0)
        def _():
            g_scr[...] = jnp.zeros_like(g_scr)

        xs = _f(x_ref[:, 0:GROUP_X])
        bm = x_ref[:, GROUP_X:GROUP_X + D_STATE]
        cm = x_ref[:, GROUP_X + D_STATE:GROUP_W]
        dyv = _f(dy_ref[...])
        w, wt, eye, cs_col, cs_row, dt_col, tot = _chunk_consts(a_ref[...], dt_ref[...], rev)
        cb = lax.dot_general(cm, bm, NT, preferred_element_type=F32)
        cbt = lax.dot_general(bm, cm, NT, preferred_element_type=F32)
        onehot = lax.broadcasted_iota(jnp.int32, (1, HEADS_PER_GROUP), 1)
        dcs = jnp.zeros((CHUNK, HEADS_PER_GROUP), F32)
        ddt_c = jnp.zeros((CHUNK, HEADS_PER_GROUP), F32)
        dtot = jnp.zeros((1, HEADS_PER_GROUP), F32)
        dcb = jnp.zeros((CHUNK, CHUNK), F32)
        dcbt = jnp.zeros((CHUNK, CHUNK), F32)
        d_b = jnp.zeros((CHUNK, D_STATE), F32)
        d_c = jnp.zeros((CHUNK, D_STATE), F32)
        dxs = []
        for j in range(HEADS_PER_GROUP):
            lo = HEAD_DIM * j
            sel = (onehot == j).astype(F32)
            csc, csr, tj = cs_col[:, j:j + 1], cs_row[j:j + 1, :], tot[j:j + 1, 0:1]
            dtj = dt_col[:, j:j + 1]
            xj = xs[:, lo:lo + HEAD_DIM]
            dyj = dyv[:, lo:lo + HEAD_DIM]
            dyj_b = dyj.astype(BF16)
            xdt = xj * dtj
            xdt_b = xdt.astype(BF16)
            hj = hin_ref[0, 0, lo:lo + HEAD_DIM, :]
            gj = g_scr[lo:lo + HEAD_DIM, :]
            gj_b = gj.astype(BF16)
            din, dte, cd = jnp.exp(csc), jnp.exp(tj - csc), jnp.exp(tj)
            dyo = dyj * din
            dyo_b = dyo.astype(BF16)
            q = lax.dot_general(cm, hj, NT, preferred_element_type=F32)
            col = jnp.sum(dyj * q, axis=1, keepdims=True) * din
            d_c = d_c + jnp.dot(dyo_b, hj, preferred_element_type=F32)
            dhin = lax.dot_general(dyo_b, cm, TN, preferred_element_type=F32)
            d_b = d_b + jnp.dot((xdt * dte).astype(BF16), gj_b, preferred_element_type=F32)
            r = lax.dot_general(bm, gj_b, NT, preferred_element_type=F32)
            dxdt = r * dte
            t1 = jnp.sum(r * xdt, axis=1, keepdims=True) * dte
            col = col - t1
            dt_j = jnp.sum(t1, axis=0, keepdims=True)
            dcd = jnp.sum(jnp.sum(gj * _f(hj), axis=1, keepdims=True), axis=0, keepdims=True)
            dt_j = dt_j + dcd * cd
            g_scr[lo:lo + HEAD_DIM, :] = dhin + gj * cd
            lm = jnp.exp(jnp.where(w, csc - csr, -jnp.inf))
            lmt = jnp.exp(jnp.where(wt, csr - csc, -jnp.inf))
            dm = lax.dot_general(dyj_b, xdt_b, NT, preferred_element_type=F32)
            dmt = lax.dot_general(xdt_b, dyj_b, NT, preferred_element_type=F32)
            mt = cbt * lmt
            dxdt = dxdt + jnp.dot(mt.astype(BF16), dyj_b, preferred_element_type=F32)
            col = col + jnp.sum(dm * (cb * lm), axis=1, keepdims=True) - jnp.sum(dmt * mt, axis=1, keepdims=True)
            dcb = dcb + dm * lm
            dcbt = dcbt + dmt * lmt
            dxj = dxdt * dtj
            if skip:
                dxj = dxj + dsk_ref[:, lo:lo + HEAD_DIM] * dyj
            dxs.append(dxj)
            dcs = dcs + col * sel
            ddt_c = ddt_c + jnp.sum(dxdt * xj, axis=1, keepdims=True) * sel
            dtot = dtot + dt_j * sel
        d_c = d_c + jnp.dot(dcb.astype(BF16), bm, preferred_element_type=F32)
        d_b = d_b + jnp.dot(dcbt.astype(BF16), cm, preferred_element_type=F32)
        da_c = jnp.dot(wt.astype(F32), dcs, precision=HIGHEST, preferred_element_type=F32) + dtot
        ddt_ref[...] = lax.dot_general(ddt_c, eye, TN, precision=HIGHEST, preferred_element_type=F32)
        da_ref[...] = lax.dot_general(da_c, eye, TN, precision=HIGHEST, preferred_element_type=F32)
        dx_ref[...] = jnp.concatenate(dxs + [d_b, d_c], axis=1).astype(dx_ref.dtype)
        if skip:
            part = jnp.sum(dyv * xs, axis=0, keepdims=True)

            @pl.when(zi == 0)
            def _():
                ddsk_ref[...] = part

            @pl.when(zi > 0)
            def _():
                ddsk_ref[...] += part

    t_in = pl.BlockSpec((HEADS_PER_GROUP, CHUNK), lambda g, zi: (dir_off + g, zmap(zi)))
    t_out = pl.BlockSpec((HEADS_PER_GROUP, CHUNK), lambda g, zi: (g, zmap(zi)))
    x_spec = pl.BlockSpec((CHUNK, GROUP_W), lambda g, zi: (zmap(zi), g))
    in_specs = [x_spec, t_in, t_in, pl.BlockSpec((CHUNK, GROUP_X), lambda g, zi: (zmap(zi), g)),
                pl.BlockSpec((1, 1, GROUP_X, D_STATE), lambda g, zi: (g, zmap(zi), 0, 0))]
    args = [xbc, dt_t, a_t, dy, hin]
    out_specs = [x_spec, t_out, t_out]
    out_shape = [jax.ShapeDtypeStruct((s, n_groups * GROUP_W), BF16), jax.ShapeDtypeStruct((n_heads, s), F32),
                 jax.ShapeDtypeStruct((n_heads, s), F32)]
    if skip:
        in_specs.append(pl.BlockSpec((1, GROUP_X), lambda g, zi: (0, g)))
        args.append(dsk)
        out_specs.append(pl.BlockSpec((1, GROUP_X), lambda g, zi: (0, g)))
        out_shape.append(jax.ShapeDtypeStruct((1, n_groups * GROUP_X), F32))
    return pl.pallas_call(
        body, name=name, grid=(n_groups, nc), in_specs=in_specs, out_specs=out_specs, out_shape=out_shape,
        scratch_shapes=[pltpu.VMEM((GROUP_X, D_STATE), F32)], compiler_params=_params(2),
    )(*args)


def _mesh_pos():
    return lax.axis_index("x"), lax.axis_index("y"), lax.axis_index("c")


def _slot(p):
    return 4 * p[0] + 2 * p[1] + p[2]


def _all_gather(name, arrs):
    n = len(arrs)

    def body(*refs):
        ins, outs = refs[:n], refs[n:2 * n]
        send, recv, loc = refs[2 * n:]
        x, y, c = _mesh_pos()
        me, sib = (x, y, c), (x, y, 1 - c)
        chips = [(1 - x, y), (x, 1 - y), (1 - x, 1 - y)]

        def cp(a, k, block, to, src=None):
            dst = outs[a].at[_slot(block)]
            return pltpu.make_async_remote_copy(
                src_ref=dst if src is None else src, dst_ref=dst, send_sem=send.at[7 * a + k], recv_sem=recv.at[7 * a + k],
                device_id=to, device_id_type=MESH)

        mine = [pltpu.make_async_copy(ins[a], outs[a].at[_slot(me)], loc.at[a]) for a in range(n)]
        for m in mine:
            m.start()
        first = []
        for a in range(n):
            first.append(cp(a, 0, me, sib, src=ins[a]))
            first += [cp(a, 1 + j, me, (*chip, c), src=ins[a]) for j, chip in enumerate(chips)]
        for f in first:
            f.start()
        passed = []
        for a in range(n):
            for j, chip in enumerate(chips):
                cp(a, 1 + j, (*chip, c), me).wait_recv()
                p = cp(a, 4 + j, (*chip, c), sib)
                p.start()
                passed.append(p)
        for a in range(n):
            cp(a, 0, sib, me).wait_recv()
            for j, chip in enumerate(chips):
                cp(a, 4 + j, (*chip, 1 - c), me).wait_recv()
        for f in first + passed:
            f.wait_send()
        for m in mine:
            m.wait()

    any_spec = pl.BlockSpec(memory_space=pl.ANY)
    return pl.pallas_call(
        body, name=name, in_specs=[any_spec] * n, out_specs=[any_spec] * n,
        out_shape=[jax.ShapeDtypeStruct((N_DEV,) + a.shape, a.dtype) for a in arrs],
        scratch_shapes=[pltpu.SemaphoreType.DMA((7 * n,)), pltpu.SemaphoreType.DMA((7 * n,)), pltpu.SemaphoreType.DMA((n,))],
    )(*arrs)


def _all_to_all(name, arrs):
    n = len(arrs)

    def body(*refs):
        ins, outs = refs[:n], refs[n:2 * n]
        send, recv, loc = refs[2 * n:]
        x, y, c = _mesh_pos()
        me = (x, y, c)
        flips = [(fx, fy, fc) for fx in (0, 1) for fy in (0, 1) for fc in (0, 1)][1:]
        peers = [tuple(1 - p if fl else p for p, fl in zip(me, fl3)) for fl3 in flips]

        def cp(a, k, peer):
            return pltpu.make_async_remote_copy(
                src_ref=ins[a].at[_slot(peer)], dst_ref=outs[a].at[_slot(me)], send_sem=send.at[7 * a + k],
                recv_sem=recv.at[7 * a + k], device_id=peer, device_id_type=MESH)

        def landed(a, k, peer):
            return pltpu.make_async_remote_copy(
                src_ref=ins[a].at[_slot(peer)], dst_ref=outs[a].at[_slot(peer)], send_sem=send.at[7 * a + k],
                recv_sem=recv.at[7 * a + k], device_id=peer, device_id_type=MESH)

        mine = [pltpu.make_async_copy(ins[a].at[_slot(me)], outs[a].at[_slot(me)], loc.at[a]) for a in range(n)]
        for m in mine:
            m.start()
        sends = [cp(a, k, peer) for a in range(n) for k, peer in enumerate(peers)]
        for s_ in sends:
            s_.start()
        for a in range(n):
            for k, peer in enumerate(peers):
                landed(a, k, peer).wait_recv()
        for s_ in sends:
            s_.wait_send()
        for m in mine:
            m.wait()

    any_spec = pl.BlockSpec(memory_space=pl.ANY)
    return pl.pallas_call(
        body, name=name, in_specs=[any_spec] * n, out_specs=[any_spec] * n,
        out_shape=[jax.ShapeDtypeStruct(a.shape, a.dtype) for a in arrs],
        scratch_shapes=[pltpu.SemaphoreType.DMA((7 * n,)), pltpu.SemaphoreType.DMA((7 * n,)), pltpu.SemaphoreType.DMA((n,))],
    )(*arrs)


def _sum8(parts_ref):
    g = _f(parts_ref[0])
    for j in range(1, N_DEV):
        g = g + _f(parts_ref[j])
    return g


def _adam_math(w, g, m, v):
    m = ADAM_B1 * m + (1.0 - ADAM_B1) * g
    v = ADAM_B2 * v + (1.0 - ADAM_B2) * (g * g)
    m_hat = m / (1.0 - ADAM_B1 ** ADAM_STEP)
    v_hat = v / (1.0 - ADAM_B2 ** ADAM_STEP)
    return -ADAM_LR * (m_hat / (jnp.sqrt(v_hat) + ADAM_EPS) + ADAM_WD * w), m, v


def _reduce_sum(name, parts):
    _, r, c = parts.shape

    def body(p_ref, o_ref):
        o_ref[...] = _sum8(p_ref)

    return pl.pallas_call(body, name=name, out_shape=jax.ShapeDtypeStruct((r, c), F32))(parts)


def _adam(name, w, g, m, v, parts=None):
    r, c = w.shape
    tr = _tile(r, (128, 64, 32, 16)) if r * c > 2 ** 18 else r
    spec = pl.BlockSpec((tr, c), lambda i: (i, 0))
    summed = parts is not None

    def body(*refs):
        if summed:
            p_ref, w_ref, m_ref, v_ref, g_out, d_out, m_out, v_out = refs
            gv = _sum8(p_ref)
            g_out[...] = gv
        else:
            g_ref, w_ref, m_ref, v_ref, d_out, m_out, v_out = refs
            gv = g_ref[...]
        d, mn, vn = _adam_math(w_ref[...], gv, m_ref[...], v_ref[...])
        d_out[...] = d
        m_out[...] = mn
        v_out[...] = vn

    first = [pl.BlockSpec((N_DEV, tr, c), lambda i: (0, i, 0))] if summed else [spec]
    n_out = 4 if summed else 3
    return pl.pallas_call(
        body, name=name, grid=(r // tr,), in_specs=first + [spec] * 3, out_specs=[spec] * n_out,
        out_shape=[jax.ShapeDtypeStruct((r, c), F32)] * n_out, compiler_params=_params(1),
    )(parts if summed else g, w, m, v)


def _xbc_runs(d_ssm, n_groups):
    runs = []
    for g in range(n_groups):
        runs += [(GROUP_X * g, GROUP_X), (d_ssm + D_STATE * g, D_STATE), (d_ssm + n_groups * D_STATE + D_STATE * g, D_STATE)]
    return runs


def _in_runs(d, d_ssm, n_groups, n_heads):
    d_xbc = d_ssm + 2 * n_groups * D_STATE
    o_dt = d_ssm + d_xbc
    o_glu = o_dt + 2 * n_heads
    runs = [(0, d_ssm)] + [(d_ssm + s, l) for s, l in _xbc_runs(d_ssm, n_groups)]
    return runs + [(o_glu, 4 * d), (o_dt, 2 * n_heads)]


def _permute(a, runs):
    return jnp.concatenate([a[..., s:s + l] for s, l in runs], axis=-1)


def _unpermute(a, runs):
    offs = np.concatenate([[0], np.cumsum([l for _, l in runs])])
    order = sorted(range(len(runs)), key=lambda k: runs[k][0])
    return jnp.concatenate([a[..., int(offs[k]):int(offs[k]) + runs[k][1]] for k in order], axis=-1)


def _cols_from_shards(g):
    return g.transpose(1, 0, 2).reshape(g.shape[1], -1)


def _cols_to_shards(a):
    return a.reshape(a.shape[0], N_DEV, -1).transpose(1, 0, 2)


def _pad_rows(a, rows):
    return jnp.concatenate([a, jnp.zeros((rows - a.shape[0], a.shape[1]), a.dtype)], axis=0)


def _pack(vecs):
    a = jnp.concatenate(vecs, axis=1)
    return jnp.concatenate([a, jnp.zeros((1, -a.shape[1] % 1024), a.dtype)], axis=1).reshape(-1, 128)


def kernel(x, c, w_ada, b_ada, g_pre_mix, g_post_mix, w_in, w_conv_ssm, b_conv_ssm, dt_bias_fwd, dt_bias_bwd, a_log_fwd, a_log_bwd, d_skip, g_ssm_norm, w_ssm_out, b_glu, w_dw, b_dw, ln_g, ln_b, w_conv_out, b_conv_out, b_gate, w_mix_out, g_pre_ffn, g_post_ffn, w_gate_up, w_down, loss_target, m_w_ada, m_b_ada, m_g_pre_mix, m_g_post_mix, m_w_in, m_w_conv_ssm, m_b_conv_ssm, m_dt_bias_fwd, m_dt_bias_bwd, m_a_log_fwd, m_a_log_bwd, m_d_skip, m_g_ssm_norm, m_w_ssm_out, m_b_glu, m_w_dw, m_b_dw, m_ln_g, m_ln_b, m_w_conv_out, m_b_conv_out, m_b_gate, m_w_mix_out, m_g_pre_ffn, m_g_post_ffn, m_w_gate_up, m_w_down, v_w_ada, v_b_ada, v_g_pre_mix, v_g_post_mix, v_w_in, v_w_conv_ssm, v_b_conv_ssm, v_dt_bias_fwd, v_dt_bias_bwd, v_a_log_fwd, v_a_log_bwd, v_d_skip, v_g_ssm_norm, v_w_ssm_out, v_b_glu, v_w_dw, v_b_dw, v_ln_g, v_ln_b, v_w_conv_out, v_b_conv_out, v_b_gate, v_w_mix_out, v_g_pre_ffn, v_g_post_ffn, v_w_gate_up, v_w_down):
    weights = dict(w_ada=w_ada, b_ada=b_ada, g_pre_mix=g_pre_mix, g_post_mix=g_post_mix, w_in=w_in, w_conv_ssm=w_conv_ssm, b_conv_ssm=b_conv_ssm, dt_bias_fwd=dt_bias_fwd, dt_bias_bwd=dt_bias_bwd, a_log_fwd=a_log_fwd, a_log_bwd=a_log_bwd, d_skip=d_skip, g_ssm_norm=g_ssm_norm, w_ssm_out=w_ssm_out, b_glu=b_glu, w_dw=w_dw, b_dw=b_dw, ln_g=ln_g, ln_b=ln_b, w_conv_out=w_conv_out, b_conv_out=b_conv_out, b_gate=b_gate, w_mix_out=w_mix_out, g_pre_ffn=g_pre_ffn, g_post_ffn=g_post_ffn, w_gate_up=w_gate_up, w_down=w_down)
    mom_m = dict(w_ada=m_w_ada, b_ada=m_b_ada, g_pre_mix=m_g_pre_mix, g_post_mix=m_g_post_mix, w_in=m_w_in, w_conv_ssm=m_w_conv_ssm, b_conv_ssm=m_b_conv_ssm, dt_bias_fwd=m_dt_bias_fwd, dt_bias_bwd=m_dt_bias_bwd, a_log_fwd=m_a_log_fwd, a_log_bwd=m_a_log_bwd, d_skip=m_d_skip, g_ssm_norm=m_g_ssm_norm, w_ssm_out=m_w_ssm_out, b_glu=m_b_glu, w_dw=m_w_dw, b_dw=m_b_dw, ln_g=m_ln_g, ln_b=m_ln_b, w_conv_out=m_w_conv_out, b_conv_out=m_b_conv_out, b_gate=m_b_gate, w_mix_out=m_w_mix_out, g_pre_ffn=m_g_pre_ffn, g_post_ffn=m_g_post_ffn, w_gate_up=m_w_gate_up, w_down=m_w_down)
    mom_v = dict(w_ada=v_w_ada, b_ada=v_b_ada, g_pre_mix=v_g_pre_mix, g_post_mix=v_g_post_mix, w_in=v_w_in, w_conv_ssm=v_w_conv_ssm, b_conv_ssm=v_b_conv_ssm, dt_bias_fwd=v_dt_bias_fwd, dt_bias_bwd=v_dt_bias_bwd, a_log_fwd=v_a_log_fwd, a_log_bwd=v_a_log_bwd, d_skip=v_d_skip, g_ssm_norm=v_g_ssm_norm, w_ssm_out=v_w_ssm_out, b_glu=v_b_glu, w_dw=v_w_dw, b_dw=v_b_dw, ln_g=v_ln_g, ln_b=v_ln_b, w_conv_out=v_w_conv_out, b_conv_out=v_b_conv_out, b_gate=v_b_gate, w_mix_out=v_w_mix_out, g_pre_ffn=v_g_pre_ffn, g_post_ffn=v_g_post_ffn, w_gate_up=v_w_gate_up, w_down=v_w_down)
    names = list(weights)

    s, d = x.shape[1], x.shape[2]
    d_ssm, n_heads, d_xbc = g_ssm_norm.shape[-1], d_skip.shape[-1], b_conv_ssm.shape[-1]
    n_groups = n_heads // HEADS_PER_GROUP
    d_ff = w_down.shape[1] * N_DEV
    k_ssm, k_dw = w_conv_ssm.shape[1], w_dw.shape[1]
    assert d_ssm == n_groups * GROUP_X and d_xbc == n_groups * GROUP_W and s % CHUNK == 0
    assert d % COL_BLOCK == 0 and d_ff % COL_BLOCK == 0
    mc = w_ada.shape[-1]
    me = _slot(_mesh_pos())
    in_runs, xbc_runs = _in_runs(d, d_ssm, n_groups, n_heads), _xbc_runs(d_ssm, n_groups)
    wb = d_ssm + d_xbc + 4 * d
    p_x, p_ga, p_gb, p_la, p_lb = d_ssm, d_ssm + d_xbc, d_ssm + d_xbc + d, d_ssm + d_xbc + 2 * d, d_ssm + d_xbc + 3 * d
    ncb, nfb = d // COL_BLOCK, d_ff // COL_BLOCK
    x2, tgt = x[0], loss_target[0]

    big = ["w_in", "w_ssm_out", "w_conv_out", "w_mix_out", "w_gate_up", "w_down"]
    gath = dict(zip(big, _all_gather("gather_weights", [weights[k][0].astype(BF16) for k in big])))
    w_in_p = _permute(_cols_from_shards(gath["w_in"]), in_runs)
    w_big, w_dt = w_in_p[:, :wb], w_in_p[:, wb:]
    w_ssm, w_co, w_mo = gath["w_ssm_out"].reshape(d_ssm, d), gath["w_conv_out"].reshape(d, d), gath["w_mix_out"].reshape(d, d)
    w_gu, w_dn = _cols_from_shards(gath["w_gate_up"]), gath["w_down"].reshape(d_ff, d)

    n_cs, n_dw = k_ssm * (d_xbc // N_DEV), k_dw * (d // N_DEV)
    (small_all,) = _all_gather("gather_small", [_pack([c, w_conv_ssm.reshape(1, n_cs), w_dw.reshape(1, n_dw)])])
    small_all = small_all.reshape(N_DEV, -1)
    c_all = small_all[:, :d]
    wcs = _cols_from_shards(small_all[:, d:d + n_cs].reshape(N_DEV, k_ssm, -1))
    wdw = _cols_from_shards(small_all[:, d + n_cs:d + n_cs + n_dw].reshape(N_DEV, k_dw, -1))
    wcs_p = _permute(wcs, xbc_runs)
    bcs_p = _permute(b_conv_ssm, xbc_runs)
    kp_ssm, kp_dw = -(-k_ssm // 8) * 8, -(-k_dw // 8) * 8
    zero_x, zero_d = jnp.zeros((1, d_xbc), F32), jnp.zeros((1, d), F32)

    (c_act,), _ = _rowcall("c_act", _fwd_fn(_silu), [(c_all, 0, d)], [], [(d, F32)], [])
    b_my = lax.dynamic_slice(b_ada, (0, me * mc), (1, mc))
    mod_cols = _matmul("mod", c_act, w_ada[0], "nn", F32, acc_in=jnp.broadcast_to(b_my, (N_DEV, mc)))
    (mod_rows,) = _all_to_all("mod_to_owner", [mod_cols.reshape(N_DEV, 1, mc)])
    mod = mod_rows.reshape(1, N_DEV * mc)
    sh1, sc1, g1, sh2, sc2, g2 = [mod[:, k * d:(k + 1) * d] for k in range(6)]

    (h1,), _ = _rowcall("prenorm_mix", _fwd_fn(_t_prenorm), [(x2, 0, d)], [g_pre_mix, sc1, sh1], [(d, BF16)], [])
    proj = _matmul("in_proj", h1, w_big, "nn", BF16)
    dt_raw = _matmul("in_proj_dt", h1, w_dt, "nn", F32)
    xbc_c = _dwconv("conv_ssm", proj, p_x, d_xbc, _pad_rows(wcs_p, kp_ssm), k_ssm, bcs_p, True, BF16)
    alog_col = jnp.concatenate([a_log_fwd, a_log_bwd], axis=1).reshape(2 * n_heads, 1)
    dt_t, a_t = _dtprep("dt_prep", dt_raw, jnp.concatenate([dt_bias_fwd, dt_bias_bwd], axis=1), alog_col)
    dsk = jnp.repeat(d_skip, HEAD_DIM, axis=1)
    y_f, hin_f = _ssd_fwd("ssd_fwd", xbc_c, dt_t, a_t, dsk, False, True, n_groups)
    y_b, hin_b = _ssd_fwd("ssd_rev", xbc_c, dt_t, a_t, dsk, True, False, n_groups)
    gn_rows = [(y_f, 0, d_ssm), (y_b, 0, d_ssm), (proj, 0, d_ssm)]
    (yn,), _ = _rowcall("ssm_norm", _fwd_fn(_t_gnorm), gn_rows, [g_ssm_norm], [(d_ssm, BF16)], [], ncol=n_groups)
    ya = _matmul("ssm_out", yn, w_ssm, "nn", BF16)
    glu_rows, glu_prm = [(proj, p_ga, d), (proj, p_gb, d)], [b_glu[:, :d], b_glu[:, d:]]
    (u0,), _ = _rowcall("glu", _fwd_fn(_t_glu), glu_rows, glu_prm, [(d, BF16)], [], ncol=ncb)
    wdw_pad = _pad_rows(wdw, kp_dw)
    u1 = _dwconv("conv_dw", u0, 0, d, wdw_pad, k_dw, b_dw, False, BF16)
    (u2,), _ = _rowcall("ln_silu", _fwd_fn(_t_lnsilu), [(u1, 0, d)], [ln_g, ln_b], [(d, BF16)], [])
    yb = _matmul("conv_out", u2, w_co, "nn", BF16)
    gm_rows, gm_prm = [(ya, 0, d), (yb, 0, d), (proj, p_la, d), (proj, p_lb, d)], [b_conv_out, b_gate[:, :d], b_gate[:, d:]]
    (m_in,), _ = _rowcall("gate_mix", _fwd_fn(_t_gatemix), gm_rows, gm_prm, [(d, BF16)], [], ncol=ncb)
    mix = _matmul("mix_out", m_in, w_mo, "nn", BF16)
    pp_prm = [g_post_mix, g1, g_pre_ffn, sc2, sh2]
    (x1, h2), _ = _rowcall("post_mix", _fwd_fn(_t_postpre), [(x2, 0, d), (mix, 0, d)], pp_prm, [(d, F32), (d, BF16)], [])
    gu = _matmul("gate_up", h2, w_gu, "nn", BF16)
    sw_rows = [(gu, 0, d_ff), (gu, d_ff, d_ff)]
    (act,), _ = _rowcall("swiglu", _fwd_fn(_t_swiglu), sw_rows, [], [(d_ff, BF16)], [], ncol=nfb)
    f = _matmul("down", act, w_dn, "nn", BF16)

    def loss_fn(x1_t, f_t, tgt_t, gpost, g2_t):
        args = [_f(x1_t), _f(f_t), _f(tgt_t), _f(gpost), _f(g2_t)]
        val, vjp = jax.vjp(_t_loss, *args)
        gr = vjp(jnp.ones((), F32))
        return (gr[0], gr[1]), (jnp.zeros((1, 128), F32) + val, gr[3], gr[4])

    (dx1a, df), (loss_v, d_gpost2, d_g2) = _rowcall(
        "loss", loss_fn, [(x1, 0, d), (f, 0, d), (tgt, 0, d)], [g_post_ffn, g2], [(d, F32), (d, BF16)], [128, d, d])
    loss = lax.psum(loss_v[0, 0], ("x", "y", "c"))

    dact = _matmul("down_dx", df, w_dn, "nt", BF16)
    dw_dn = _matmul("down_dw", act, df, "tn", BF16)
    (dgt, dup), _ = _rowcall("swiglu_bwd", _vjp_fn(_t_swiglu, 2, 1, (0, 1), ()), sw_rows + [(dact, 0, d_ff)], [],
                             [(d_ff, BF16), (d_ff, BF16)], [], ncol=nfb)
    dgu = jnp.concatenate([dgt, dup], axis=1)
    dh2 = _matmul("gate_up_dx", dgu, w_gu, "nt", BF16)
    dw_gu = _matmul("gate_up_dw", h2, dgu, "tn", BF16)
    (dxa, dmix), (d_gpost1, d_g1, d_gpre2, d_sc2, d_sh2) = _rowcall(
        "post_mix_bwd", _vjp_fn(_t_postpre, 2, 2, (0, 1), (0, 1, 2, 3, 4)),
        [(x2, 0, d), (mix, 0, d), (dx1a, 0, d), (dh2, 0, d)], pp_prm, [(d, F32), (d, BF16)], [d] * 5)
    dm_in = _matmul("mix_out_dx", dmix, w_mo, "nt", BF16)
    dw_mo = _matmul("mix_out_dw", m_in, dmix, "tn", BF16)
    (dya, dyb, dla, dlb), (d_bco, d_bga, d_bgb) = _rowcall(
        "gate_mix_bwd", _vjp_fn(_t_gatemix, 4, 1, (0, 1, 2, 3), (0, 1, 2)), gm_rows + [(dm_in, 0, d)], gm_prm,
        [(d, BF16)] * 4, [d] * 3, ncol=ncb)
    du2 = _matmul("conv_out_dx", dyb, w_co, "nt", BF16)
    dw_co = _matmul("conv_out_dw", u2, dyb, "tn", BF16)
    (du1,), (d_lng, d_lnb) = _rowcall("ln_silu_bwd", _vjp_fn(_t_lnsilu, 1, 1, (0,), (0, 1)), [(u1, 0, d), (du2, 0, d)],
                                      [ln_g, ln_b], [(d, BF16)], [d, d])
    dwdw_p, d_bdw = _dwconv_bwd_w("conv_dw_dw", u0, 0, d, wdw_pad, k_dw, b_dw, [du1], False)
    du0 = _dwconv("conv_dw_dx", du1, 0, d, _pad_rows(wdw[::-1], kp_dw), k_dw, zero_d, False, BF16)
    (dga, dgb), (d_bglu_a, d_bglu_b) = _rowcall("glu_bwd", _vjp_fn(_t_glu, 2, 1, (0, 1), (0, 1)), glu_rows + [(du0, 0, d)],
                                                glu_prm, [(d, BF16)] * 2, [d, d], ncol=ncb)
    dyn = _matmul("ssm_out_dx", dya, w_ssm, "nt", BF16)
    dw_ssm = _matmul("ssm_out_dw", yn, dya, "tn", BF16)
    (dy_ssd, dz), (d_gn,) = _rowcall("ssm_norm_bwd", _vjp_fn(_t_gnorm, 3, 1, (0, 2), (0,)), gn_rows + [(dyn, 0, d_ssm)],
                                     [g_ssm_norm], [(d_ssm, BF16)] * 2, [d_ssm], ncol=n_groups)
    dxbc_f, ddt_f, da_f, ddsk = _ssd_bwd("ssd_fwd_bwd", xbc_c, dt_t, a_t, dy_ssd, hin_f, dsk, False, True, n_groups)
    dxbc_b, ddt_b, da_b = _ssd_bwd("ssd_rev_bwd", xbc_c, dt_t, a_t, dy_ssd, hin_b, dsk, True, False, n_groups)
    ddt_raw, d_dtb, d_alog = _dtprep_bwd("dt_prep_bwd", ddt_f, ddt_b, da_f, da_b, dt_t, alog_col)
    dpre, dwcs_p, dbcs_p = _dwconv_bwd_w("conv_ssm_dw", proj, p_x, d_xbc, _pad_rows(wcs_p, kp_ssm), k_ssm, bcs_p,
                                          [dxbc_f, dxbc_b], True)
    dxbc = _dwconv("conv_ssm_dx", dpre, 0, d_xbc, _pad_rows(wcs_p[::-1], kp_ssm), k_ssm, zero_x, False, BF16)
    dproj = jnp.concatenate([dz, dxbc, dga, dgb, dla, dlb], axis=1)
    dh1 = _matmul("in_proj_dx", dproj, w_big, "nt", F32)
    dh1 = _matmul("in_proj_dt_dx", ddt_raw, w_dt, "nt", BF16, acc_in=dh1)
    dw_big = _matmul("in_proj_dw", h1, dproj, "tn", BF16)
    dw_dt = _matmul("in_proj_dt_dw", h1, ddt_raw, "tn", BF16)

    def prenorm_bwd_fn(x_t, dh_t, dxa_t, g, sc, sh):
        rows, prm = _vjp_fn(_t_prenorm, 1, 1, (0,), (0, 1, 2))(x_t, dh_t, g, sc, sh)
        return (rows[0] + _f(dxa_t),), prm

    (grad_x,), (d_gpre1, d_sc1, d_sh1) = _rowcall(
        "prenorm_mix_bwd", prenorm_bwd_fn, [(x2, 0, d), (dh1, 0, d), (dxa, 0, d)], [g_pre_mix, sc1, sh1], [(d, F32)], [d] * 3)

    dmod = jnp.concatenate([d_sh1, d_sc1, d_g1, d_sh2, d_sc2, d_g2], axis=1)
    small_g = {
        "b_ada": dmod, "g_pre_mix": d_gpre1, "g_post_mix": d_gpost1, "b_conv_ssm": _unpermute(dbcs_p, xbc_runs),
        "dt_bias_fwd": d_dtb[:n_heads].reshape(1, n_heads), "dt_bias_bwd": d_dtb[n_heads:].reshape(1, n_heads),
        "a_log_fwd": d_alog[:n_heads].reshape(1, n_heads), "a_log_bwd": d_alog[n_heads:].reshape(1, n_heads),
        "d_skip": ddsk.reshape(n_heads, HEAD_DIM).sum(axis=1).reshape(1, n_heads), "g_ssm_norm": d_gn,
        "b_glu": jnp.concatenate([d_bglu_a, d_bglu_b], axis=1), "b_dw": d_bdw, "ln_g": d_lng, "ln_b": d_lnb,
        "b_conv_out": d_bco, "b_gate": jnp.concatenate([d_bga, d_bgb], axis=1), "g_pre_ffn": d_gpre2, "g_post_ffn": d_gpost2,
        "w_conv_ssm": _unpermute(dwcs_p[:k_ssm], xbc_runs).reshape(1, k_ssm * d_xbc), "w_dw": dwdw_p[:k_dw].reshape(1, k_dw * d),
    }
    small_names = list(small_g)
    sizes = [small_g[k].shape[1] for k in small_names]
    offs = np.concatenate([[0], np.cumsum(sizes)])
    (small_parts,) = _all_gather("gather_small_grads", [_pack([small_g[k] for k in small_names])])
    dmod_all = small_parts.reshape(N_DEV, -1)[:, :6 * d]
    small_tot = _reduce_sum("sum_small_grads", small_parts).reshape(1, -1)
    grads = {k: small_tot[:, int(offs[i]):int(offs[i + 1])] for i, k in enumerate(small_names)}
    grads["w_conv_ssm"] = lax.dynamic_slice(grads["w_conv_ssm"].reshape(k_ssm, d_xbc), (0, me * (d_xbc // N_DEV)), (k_ssm, d_xbc // N_DEV))
    grads["w_dw"] = lax.dynamic_slice(grads["w_dw"].reshape(k_dw, d), (0, me * (d // N_DEV)), (k_dw, d // N_DEV))
    grads["w_ada"] = _matmul("mod_dw", c_act, lax.dynamic_slice(dmod_all, (0, me * mc), (N_DEV, mc)), "tn", F32, highest=True)

    dw_in = _unpermute(jnp.concatenate([dw_big, dw_dt], axis=1), in_runs)
    send = [_cols_to_shards(dw_in), dw_ssm.reshape(N_DEV, -1, d), dw_co.reshape(N_DEV, -1, d), dw_mo.reshape(N_DEV, -1, d),
            _cols_to_shards(dw_gu), dw_dn.reshape(N_DEV, -1, d)]
    recv = dict(zip(big, _all_to_all("scatter_grads", send)))

    delta, new_m, new_v = {}, {}, {}
    for k in big:
        grads[k], delta[k], new_m[k], new_v[k] = _adam("adam_" + k, weights[k][0], None, mom_m[k][0], mom_v[k][0], parts=recv[k])
    for k in ["w_ada", "w_conv_ssm", "w_dw"]:
        delta[k], new_m[k], new_v[k] = _adam("adam_" + k, weights[k][0], grads[k], mom_m[k][0], mom_v[k][0])
    rep = [k for k in names if k not in big and k not in ("w_ada", "w_conv_ssm", "w_dw")]
    r_sizes = [weights[k].shape[1] for k in rep]
    r_offs = np.concatenate([[0], np.cumsum(r_sizes)])
    packed = [_pack([t[k] for k in rep]) for t in (weights, grads, mom_m, mom_v)]
    r_delta, r_m, r_v = [t.reshape(1, -1) for t in _adam("adam_small", *packed)]
    for i, k in enumerate(rep):
        sl = slice(int(r_offs[i]), int(r_offs[i + 1]))
        delta[k], new_m[k], new_v[k] = r_delta[:, sl], r_m[:, sl], r_v[:, sl]

    def shaped(t, k):
        return t[k].reshape(weights[k].shape)

    return (loss, grad_x.reshape(x.shape), *[shaped(grads, k) for k in names], *[shaped(delta, k) for k in names],
            *[shaped(new_m, k) for k in names], *[shaped(new_v, k) for k in names])
```

```python
import functools

import numpy as np
import jax
import jax.numpy as jnp
from jax import lax
from jax.experimental import pallas as pl
from jax.experimental.pallas import tpu as pltpu

F32 = jnp.float32
BF16 = jnp.bfloat16
HIGHEST = lax.Precision.HIGHEST
MESH = pl.DeviceIdType.MESH
N_DEV = 8
EPS = 1e-6
CHUNK = 128
HEAD_DIM = 64
D_STATE = 128
HEADS_PER_GROUP = 8
GROUP_X = HEADS_PER_GROUP * HEAD_DIM
GROUP_W = GROUP_X + 2 * D_STATE
COL_BLOCK = 512
HALO = 16
VMEM_LIMIT = 56 * 1024 * 1024
ADAM_LR, ADAM_B1, ADAM_B2, ADAM_EPS, ADAM_WD, ADAM_STEP = 0.001, 0.9, 0.999, 1e-08, 0.01, 10

NN = (((1,), (0,)), ((), ()))
NT = (((1,), (1,)), ((), ()))
TN = (((0,), (0,)), ((), ()))


def _tile(dim, prefs):
    for p in prefs:
        if p <= dim and dim % p == 0:
            return p
    return dim


def _params(n_grid):
    return pltpu.CompilerParams(dimension_semantics=("arbitrary",) * n_grid, vmem_limit_bytes=VMEM_LIMIT)


def _f(v):
    return v.astype(F32)


def _matmul(name, a, b, mode, out_dtype, acc_in=None, highest=False, carry=None):
    if mode == "nn":
        (m, k), (k2, n) = a.shape, b.shape
    elif mode == "nt":
        (m, k), (n, k2) = a.shape, b.shape
    else:
        (k, m), (k2, n) = a.shape, b.shape
    assert k == k2, (name, a.shape, b.shape, mode)
    tm = _tile(m, (1024, 512, 256, 128))
    tn = _tile(n, (1024, 512, 256, 128))
    tk = _tile(k, (1024, 512, 256, 128))
    nk = k // tk
    dims = {"nn": NN, "nt": NT, "tn": TN}[mode]
    a_spec = pl.BlockSpec((tk, tm), lambda i, j, kk: (kk, i)) if mode == "tn" else pl.BlockSpec((tm, tk), lambda i, j, kk: (i, kk))
    b_spec = pl.BlockSpec((tn, tk), lambda i, j, kk: (j, kk)) if mode == "nt" else pl.BlockSpec((tk, tn), lambda i, j, kk: (kk, j))
    o_spec = pl.BlockSpec((tm, tn), lambda i, j, kk: (i, j))
    has_acc = acc_in is not None
    n_in = 3 if has_acc else 2
    ops, c_arrs = carry if carry is not None else (None, [])
    nc_ = len(c_arrs)
    grid = (m // tm, n // tn, nk)

    def body(*refs):
        a_ref, b_ref = refs[0], refs[1]
        c_ref = refs[2] if has_acc else None
        c_ins = refs[n_in:n_in + nc_]
        o_ref = refs[n_in + nc_]
        c_outs = refs[n_in + nc_ + 1:n_in + 2 * nc_ + 1]
        acc = refs[n_in + 2 * nc_ + 1]
        i, j, kk = pl.program_id(0), pl.program_id(1), pl.program_id(2)
        if nc_:
            start, finish = ops(c_ins, c_outs, *refs[n_in + 2 * nc_ + 2:])
            pl.when((i == 0) & (j == 0) & (kk == 0))(start)

        @pl.when(kk == 0)
        def _():
            acc[...] = _f(c_ref[...]) if has_acc else jnp.zeros_like(acc)

        if highest:
            acc[...] += lax.dot_general(_f(a_ref[...]), _f(b_ref[...]), dims, precision=HIGHEST, preferred_element_type=F32)
        else:
            acc[...] += lax.dot_general(a_ref[...].astype(BF16), b_ref[...].astype(BF16), dims, preferred_element_type=F32)

        @pl.when(kk == nk - 1)
        def _():
            o_ref[...] = acc[...].astype(out_dtype)

        if nc_:
            pl.when((i == grid[0] - 1) & (j == grid[1] - 1) & (kk == nk - 1))(finish)

    any_spec = pl.BlockSpec(memory_space=pl.ANY)
    c_shapes = ops.out_shapes(c_arrs) if nc_ else []
    res = pl.pallas_call(
        body, name=name, grid=grid,
        in_specs=[a_spec, b_spec] + ([o_spec] if has_acc else []) + [any_spec] * nc_, out_specs=[o_spec] + [any_spec] * nc_,
        out_shape=[jax.ShapeDtypeStruct((m, n), out_dtype)] + c_shapes,
        scratch_shapes=[pltpu.VMEM((tm, tn), F32)] + (_exchange_sems(nc_) if nc_ else []), compiler_params=_params(3),
    )(*([a, b] + ([acc_in] if has_acc else []) + list(c_arrs)))
    return (res[0], res[1:]) if nc_ else res[0]


def _rowcall(name, fn, rows, params, out_rows, out_accs, ncol=1):
    s = rows[0][0].shape[0]
    tm = _tile(s, (256 if ncol == 1 else 1024, 128, 64, 32, 16, 8))
    args, in_specs = [], []
    for arr, off, w in rows:
        bw = w // ncol
        assert w % ncol == 0 and off % bw == 0, (name, off, w, ncol)
        in_specs.append(pl.BlockSpec((tm, bw), functools.partial(lambda j, i, ob: (i, ob + j), ob=off // bw)))
        args.append(arr)
    for p in params:
        assert p.shape[1] % ncol == 0, (name, p.shape)
        in_specs.append(pl.BlockSpec((p.shape[0], p.shape[1] // ncol), lambda j, i: (0, j)))
        args.append(p)
    out_shape, out_specs = [], []
    for w, dt in out_rows:
        out_shape.append(jax.ShapeDtypeStruct((s, w), dt))
        out_specs.append(pl.BlockSpec((tm, w // ncol), lambda j, i: (i, j)))
    for w in out_accs:
        out_shape.append(jax.ShapeDtypeStruct((1, w), F32))
        out_specs.append(pl.BlockSpec((1, w // ncol), lambda j, i: (0, j)))
    n_in, n_ro = len(args), len(out_rows)

    def body(*refs):
        i = pl.program_id(1)
        ro, ao = fn(*[r[...] for r in refs[:n_in]])
        outs = refs[n_in:]
        assert len(ro) == n_ro and len(ao) == len(out_accs), name
        for ref, v in zip(outs[:n_ro], ro):
            ref[...] = v.astype(ref.dtype)
        for ref, v in zip(outs[n_ro:], ao):
            @pl.when(i == 0)
            def _(ref=ref, v=v):
                ref[...] = v

            @pl.when(i > 0)
            def _(ref=ref, v=v):
                ref[...] += v

    res = pl.pallas_call(
        body, name=name, grid=(ncol, s // tm), in_specs=in_specs, out_specs=out_specs,
        out_shape=out_shape, compiler_params=_params(2),
    )(*args)
    return res[:n_ro], res[n_ro:]


def _fwd_fn(f):
    def fn(*vals):
        out = f(*[_f(v) for v in vals])
        return (out if isinstance(out, tuple) else (out,)), ()
    return fn


def _vjp_fn(f, n_prim, n_ct, want_rows, want_params):
    def fn(*vals):
        prims = [_f(v) for v in vals[:n_prim]]
        cts = [_f(v) for v in vals[n_prim:n_prim + n_ct]]
        prms = [_f(v) for v in vals[n_prim + n_ct:]]
        out, vjp = jax.vjp(f, *prims, *prms)
        g = vjp(tuple(cts) if isinstance(out, tuple) else cts[0])
        return tuple(g[k] for k in want_rows), tuple(g[n_prim + k] for k in want_params)
    return fn


def _rms(v):
    return v * lax.rsqrt(jnp.mean(v * v, axis=-1, keepdims=True) + EPS)


def _silu(v):
    return v * jax.nn.sigmoid(v)


def _t_prenorm(x, g, sc, sh):
    return _rms(x) * g * (1.0 + sc) + sh


def _t_gnorm(yf, yb, z, gn):
    return _rms((yf + yb) * _silu(z)) * gn


def _t_glu(a, b, ba, bb):
    return (a + ba) * jax.nn.sigmoid(b + bb)


def _t_lnsilu(u, g, b):
    uc = u - jnp.mean(u, axis=-1, keepdims=True)
    return _silu(uc * lax.rsqrt(jnp.mean(uc * uc, axis=-1, keepdims=True) + EPS) * g + b)


def _t_gatemix(ya, yb, la, lb, bco, bga, bgb):
    return jax.nn.sigmoid(la + bga) * ya + jax.nn.sigmoid(lb + bgb) * (yb + bco)


def _t_postpre(x, mix, gpost, g1, gpre, sc, sh):
    x1 = x + g1 * (_rms(mix) * gpost)
    return x1, _rms(x1) * gpre * (1.0 + sc) + sh


def _t_swiglu(gt, up):
    return _silu(gt) * up


def _t_loss(x1, f, tgt, gpost, g2):
    e = x1 + g2 * (_rms(f) * gpost) - tgt
    return 0.5 * jnp.sum(jnp.mean(e * e, axis=-1))


def _conv_specs(s, tm, cw, ob, w_rows):
    nh, last_h = tm // HALO, s // HALO - 1
    return [
        pl.BlockSpec((tm, cw), lambda j, i: (i, ob + j)),
        pl.BlockSpec((HALO, cw), lambda j, i: (jnp.maximum(i * nh - 1, 0), ob + j)),
        pl.BlockSpec((HALO, cw), lambda j, i: (jnp.minimum((i + 1) * nh, last_h), ob + j)),
        pl.BlockSpec((w_rows, cw), lambda j, i: (0, j)),
        pl.BlockSpec((1, cw), lambda j, i: (0, j)),
    ]


def _window(x_ref, p_ref, n_ref, i, nrow):
    prev = _f(p_ref[...]) * jnp.where(i > 0, 1.0, 0.0).astype(F32)
    nxt = _f(n_ref[...]) * jnp.where(i < nrow - 1, 1.0, 0.0).astype(F32)
    return jnp.concatenate([prev, _f(x_ref[...]), nxt], axis=0)


def _taps(win, w_ref, k_taps, tm):
    pad = (k_taps - 1) // 2
    acc = w_ref[0:1, :] * win[HALO - pad:HALO - pad + tm, :]
    for k in range(1, k_taps):
        acc = acc + w_ref[k:k + 1, :] * win[HALO + k - pad:HALO + k - pad + tm, :]
    return acc


def _conv_rows(k_taps):
    return 1024 if k_taps <= 8 else 512


def _dwconv(name, x, xoff, c, w_pad, k_taps, b, act, out_dtype, cw=256):
    s = x.shape[0]
    tm, cw = _tile(s, (_conv_rows(k_taps), 256, 128, 64, 32, 16)), _tile(c, (cw, 128))
    assert xoff % cw == 0 and tm % HALO == 0 and (k_taps - 1) // 2 <= HALO
    nrow = s // tm

    def body(x_ref, p_ref, n_ref, w_ref, b_ref, o_ref):
        win = _window(x_ref, p_ref, n_ref, pl.program_id(1), nrow)
        acc = _taps(win, w_ref, k_taps, tm) + b_ref[...]
        o_ref[...] = (_silu(acc) if act else acc).astype(out_dtype)

    return pl.pallas_call(
        body, name=name, grid=(c // cw, nrow), in_specs=_conv_specs(s, tm, cw, xoff // cw, w_pad.shape[0]),
        out_specs=pl.BlockSpec((tm, cw), lambda j, i: (i, j)), out_shape=jax.ShapeDtypeStruct((s, c), out_dtype),
        compiler_params=_params(2),
    )(x, x, x, w_pad, b)


def _dwconv_bwd_w(name, x, xoff, c, w_pad, k_taps, b, douts, act, cw=256):
    s = x.shape[0]
    tm, cw = _tile(s, (_conv_rows(k_taps), 256, 128, 64, 32, 16)), _tile(c, (cw, 128))
    assert xoff % cw == 0 and tm % HALO == 0
    nrow, pad, n_d = s // tm, (k_taps - 1) // 2, len(douts)

    def body(x_ref, p_ref, n_ref, w_ref, b_ref, *rest):
        d_refs, outs = rest[:n_d], rest[n_d:]
        i = pl.program_id(1)
        win = _window(x_ref, p_ref, n_ref, i, nrow)
        d = _f(d_refs[0][...])
        for r in d_refs[1:]:
            d = d + _f(r[...])
        if act:
            pre = _taps(win, w_ref, k_taps, tm) + b_ref[...]
            sg = jax.nn.sigmoid(pre)
            d = d * (sg * (1.0 + pre * (1.0 - sg)))
            outs[0][...] = d.astype(outs[0].dtype)
        dw_ref, db_ref = outs[-2], outs[-1]

        @pl.when(i == 0)
        def _():
            dw_ref[...] = jnp.zeros_like(dw_ref)
            db_ref[...] = jnp.zeros_like(db_ref)

        for k in range(k_taps):
            dw_ref[k:k + 1, :] += jnp.sum(d * win[HALO + k - pad:HALO + k - pad + tm, :], axis=0, keepdims=True)
        db_ref[...] += jnp.sum(d, axis=0, keepdims=True)

    row_spec = pl.BlockSpec((tm, cw), lambda j, i: (i, j))
    out_shape = [jax.ShapeDtypeStruct((w_pad.shape[0], c), F32), jax.ShapeDtypeStruct((1, c), F32)]
    out_specs = [pl.BlockSpec((w_pad.shape[0], cw), lambda j, i: (0, j)), pl.BlockSpec((1, cw), lambda j, i: (0, j))]
    if act:
        out_shape, out_specs = [jax.ShapeDtypeStruct((s, c), BF16)] + out_shape, [row_spec] + out_specs
    return pl.pallas_call(
        body, name=name, grid=(c // cw, nrow),
        in_specs=_conv_specs(s, tm, cw, xoff // cw, w_pad.shape[0]) + [row_spec] * n_d,
        out_specs=out_specs, out_shape=out_shape, compiler_params=_params(2),
    )(x, x, x, w_pad, b, *douts)


def _dtprep(name, dt_raw, bias, alog_col):
    s, h2 = dt_raw.shape
    ts = _tile(s, (512, 256, 128))

    def body(r_ref, b_ref, al_ref, dt_ref, a_ref):
        v = r_ref[...] + b_ref[...]
        dt = (jnp.maximum(v, 0.0) + jnp.log(1.0 + jnp.exp(-jnp.abs(v)))).T
        dt_ref[...] = dt
        a_ref[...] = dt * (-jnp.exp(al_ref[...]))

    t_spec = pl.BlockSpec((h2, ts), lambda i: (0, i))
    return pl.pallas_call(
        body, name=name, grid=(s // ts,),
        in_specs=[pl.BlockSpec((ts, h2), lambda i: (i, 0)), pl.BlockSpec((1, h2), lambda i: (0, 0)), pl.BlockSpec((h2, 1), lambda i: (0, 0))],
        out_specs=[t_spec, t_spec], out_shape=[jax.ShapeDtypeStruct((h2, s), F32)] * 2, compiler_params=_params(1),
    )(dt_raw, bias, alog_col)


def _dtprep_bwd(name, ddt_f, ddt_b, da_f, da_b, dt_t, alog_col):
    h2, s = dt_t.shape
    h = h2 // 2
    ts = _tile(s, (512, 256, 128))

    def body(ddf, ddb, daf, dab, dt_ref, al_ref, raw_ref, db_ref, dal_ref):
        i = pl.program_id(0)
        a_col = -jnp.exp(al_ref[...])
        da = jnp.concatenate([daf[...], dab[...]], axis=0)
        dt = dt_ref[...]
        d_raw = (jnp.concatenate([ddf[...], ddb[...]], axis=0) + da * a_col) * (1.0 - jnp.exp(-dt))
        raw_ref[...] = d_raw.T

        @pl.when(i == 0)
        def _():
            db_ref[...] = jnp.zeros_like(db_ref)
            dal_ref[...] = jnp.zeros_like(dal_ref)

        db_ref[...] += jnp.sum(d_raw, axis=1, keepdims=True)
        dal_ref[...] += jnp.sum(da * dt, axis=1, keepdims=True) * a_col

    half = pl.BlockSpec((h, ts), lambda i: (0, i))
    col = pl.BlockSpec((h2, 1), lambda i: (0, 0))
    return pl.pallas_call(
        body, name=name, grid=(s // ts,),
        in_specs=[half, half, half, half, pl.BlockSpec((h2, ts), lambda i: (0, i)), col],
        out_specs=[pl.BlockSpec((ts, h2), lambda i: (i, 0)), col, col],
        out_shape=[jax.ShapeDtypeStruct((s, h2), F32), jax.ShapeDtypeStruct((h2, 1), F32), jax.ShapeDtypeStruct((h2, 1), F32)],
        compiler_params=_params(1),
    )(ddt_f, ddt_b, da_f, da_b, dt_t, alog_col)


def _chunk_consts(a_r, dt_r, rev):
    ii = lax.broadcasted_iota(jnp.int32, (CHUNK, CHUNK), 0)
    jj = lax.broadcasted_iota(jnp.int32, (CHUNK, CHUNK), 1)
    w = (jj >= ii) if rev else (jj <= ii)
    wt = (ii >= jj) if rev else (ii <= jj)
    wf = w.astype(F32)
    eye = (ii == jj).astype(F32)
    cs_col = lax.dot_general(wf, a_r, NT, precision=HIGHEST, preferred_element_type=F32)
    cs_row = lax.dot_general(a_r, wf, NT, precision=HIGHEST, preferred_element_type=F32)
    dt_col = lax.dot_general(eye, dt_r, NT, precision=HIGHEST, preferred_element_type=F32)
    tot = jnp.sum(a_r, axis=1, keepdims=True)
    lo_half = lax.broadcasted_iota(jnp.int32, (CHUNK, 128), 1) < HEAD_DIM

    def lanes(col):
        return [jnp.broadcast_to(col[:, j:j + 1], (CHUNK, 128)) for j in range(HEADS_PER_GROUP)]

    def spread(bs):
        return jnp.concatenate([jnp.where(lo_half, bs[2 * k], bs[2 * k + 1]) for k in range(HEADS_PER_GROUP // 2)], axis=1)

    csc = lanes(cs_col)
    cs_x, dt_x = spread(csc), spread(lanes(dt_col))
    tot_x = cs_x[0:1, :] if rev else cs_x[CHUNK - 1:CHUNK, :]
    return w, wt, eye, csc, cs_row, tot, cs_x, dt_x, tot_x


def _hi_lo(v):
    hi = v.astype(BF16)
    return jnp.concatenate([hi, (v - _f(hi)).astype(BF16)], axis=1)


def _head_sums(v):
    rr = lax.broadcasted_iota(jnp.int32, (GROUP_X, HEADS_PER_GROUP), 0)
    cc = lax.broadcasted_iota(jnp.int32, (GROUP_X, HEADS_PER_GROUP), 1)
    et = (lax.shift_right_logical(rr, 6) == cc).astype(BF16)
    hi = v.astype(BF16)
    r1 = v - _f(hi)
    mid = r1.astype(BF16)
    lo = (r1 - _f(mid)).astype(BF16)
    return (jnp.dot(hi, et, preferred_element_type=F32) + jnp.dot(mid, et, preferred_element_type=F32)
            + jnp.dot(lo, et, preferred_element_type=F32))


def _ssd_fwd(name, xbc, dt_t, a_t, dsk, rev, skip, n_groups):
    s = xbc.shape[0]
    nc = s // CHUNK
    d_ssm = n_groups * GROUP_X
    dir_off = n_groups if rev else 0

    def zmap(zi):
        return nc - 1 - zi if rev else zi

    def body(*refs):
        if skip:
            x_ref, dt_ref, a_ref, dsk_ref, y_ref, hin_ref, h_scr = refs
        else:
            x_ref, dt_ref, a_ref, y_ref, hin_ref, h_scr = refs

        @pl.when(pl.program_id(1) == 0)
        def _():
            h_scr[...] = jnp.zeros_like(h_scr)

        xs = _f(x_ref[:, 0:GROUP_X])
        bm = x_ref[:, GROUP_X:GROUP_X + D_STATE]
        cm = x_ref[:, GROUP_X + D_STATE:GROUP_W]
        w, _, _, csc, cs_row, tot, cs_x, dt_x, tot_x = _chunk_consts(a_ref[...], dt_ref[...], rev)
        cb = lax.dot_general(cm, bm, NT, preferred_element_type=F32)
        h = h_scr[...]
        h_b = h.astype(BF16)
        hin_ref[0, 0] = h_b
        xdt = xs * dt_x
        xdt_b = xdt.astype(BF16)
        yo = lax.dot_general(cm, h_b, NT, preferred_element_type=F32) * jnp.exp(cs_x)
        st = lax.dot_general((xdt * jnp.exp(tot_x - cs_x)).astype(BF16), bm, TN, preferred_element_type=F32)
        yds = []
        for j in range(HEADS_PER_GROUP):
            lo = HEAD_DIM * j
            h_scr[lo:lo + HEAD_DIM, :] = h[lo:lo + HEAD_DIM, :] * jnp.exp(tot[j:j + 1, 0:1]) + st[lo:lo + HEAD_DIM, :]
            lm = jnp.exp(jnp.where(w, csc[j] - cs_row[j:j + 1, :], -jnp.inf))
            yds.append(jnp.dot((cb * lm).astype(BF16), xdt_b[:, lo:lo + HEAD_DIM], preferred_element_type=F32))
        y = jnp.concatenate(yds, axis=1) + yo
        if skip:
            y = y + dsk_ref[...] * xs
        y_ref[...] = y.astype(y_ref.dtype)

    t_spec = pl.BlockSpec((HEADS_PER_GROUP, CHUNK), lambda g, zi: (dir_off + g, zmap(zi)))
    in_specs = [pl.BlockSpec((CHUNK, GROUP_W), lambda g, zi: (zmap(zi), g)), t_spec, t_spec]
    args = [xbc, dt_t, a_t]
    if skip:
        in_specs.append(pl.BlockSpec((1, GROUP_X), lambda g, zi: (0, g)))
        args.append(dsk)
    return pl.pallas_call(
        body, name=name, grid=(n_groups, nc), in_specs=in_specs,
        out_specs=[pl.BlockSpec((CHUNK, GROUP_X), lambda g, zi: (zmap(zi), g)),
                   pl.BlockSpec((1, 1, GROUP_X, D_STATE), lambda g, zi: (g, zmap(zi), 0, 0))],
        out_shape=[jax.ShapeDtypeStruct((s, d_ssm), BF16), jax.ShapeDtypeStruct((n_groups, nc, GROUP_X, D_STATE), BF16)],
        scratch_shapes=[pltpu.VMEM((GROUP_X, D_STATE), F32)], compiler_params=_params(2),
    )(*args)


def _ssd_bwd(name, xbc, dt_t, a_t, dy, hin, dsk, rev, skip, n_groups):
    s = xbc.shape[0]
    nc = s // CHUNK
    n_heads = n_groups * HEADS_PER_GROUP
    dir_off = n_groups if rev else 0

    def zmap(zi):
        return zi if rev else nc - 1 - zi

    def body(*refs):
        if skip:
            x_ref, dt_ref, a_ref, dy_ref, hin_ref, dsk_ref, dx_ref, ddt_ref, da_ref, ddsk_ref, g_scr = refs
        else:
            x_ref, dt_ref, a_ref, dy_ref, hin_ref, dx_ref, ddt_ref, da_ref, g_scr = refs
        zi = pl.program_id(1)

        @pl.when(zi == 0)
        def _():
            g_scr[...] = jnp.zeros_like(g_scr)

        xs = _f(x_ref[:, 0:GROUP_X])
        bm = x_ref[:, GROUP_X:GROUP_X + D_STATE]
        cm = x_ref[:, GROUP_X + D_STATE:GROUP_W]
        dyv = _f(dy_ref[...])
        w, wt, eye, csc, cs_row, tot, cs_x, dt_x, tot_x = _chunk_consts(a_ref[...], dt_ref[...], rev)
        cb = lax.dot_general(cm, bm, NT, preferred_element_type=F32)
        cbt = lax.dot_general(bm, cm, NT, preferred_element_type=F32)
        onehot = lax.broadcasted_iota(jnp.int32, (1, HEADS_PER_GROUP), 1)
        hin = hin_ref[0, 0]
        g = g_scr[...]
        g_b = g.astype(BF16)
        din, dte = jnp.exp(cs_x), jnp.exp(tot_x - cs_x)
        xdt = xs * dt_x
        xdt_b, dy_b = xdt.astype(BF16), dyv.astype(BF16)
        dyo_b = (dyv * din).astype(BF16)
        yo = lax.dot_general(cm, hin, NT, preferred_element_type=F32) * din
        dxdt_s = lax.dot_general(bm, g_b, NT, preferred_element_type=F32) * dte
        d_c = jnp.dot(dyo_b, hin, preferred_element_type=F32)
        d_b = jnp.dot((xdt * dte).astype(BF16), g_b, preferred_element_type=F32)
        dhin = lax.dot_general(dyo_b, cm, TN, preferred_element_type=F32)
        dcb = jnp.zeros((CHUNK, CHUNK), F32)
        dtot = jnp.zeros((1, HEADS_PER_GROUP), F32)
        yds, dxds = [], []
        for j in range(HEADS_PER_GROUP):
            lo = HEAD_DIM * j
            cd = jnp.exp(tot[j:j + 1, 0:1])
            gj = g[lo:lo + HEAD_DIM, :]
            dcd = jnp.sum(jnp.sum(gj * _f(hin[lo:lo + HEAD_DIM, :]), axis=1, keepdims=True), axis=0, keepdims=True)
            dtot = dtot + (dcd * cd) * (onehot == j).astype(F32)
            g_scr[lo:lo + HEAD_DIM, :] = dhin[lo:lo + HEAD_DIM, :] + gj * cd
            dd = csc[j] - cs_row[j:j + 1, :]
            lm = jnp.exp(jnp.where(w, dd, -jnp.inf))
            lmt = jnp.exp(jnp.where(wt, -dd, -jnp.inf))
            xj_b, dyj_b = xdt_b[:, lo:lo + HEAD_DIM], dy_b[:, lo:lo + HEAD_DIM]
            yds.append(jnp.dot(_hi_lo(cb * lm), jnp.concatenate([xj_b, xj_b], axis=0), preferred_element_type=F32))
            dxds.append(jnp.dot(_hi_lo(cbt * lmt), jnp.concatenate([dyj_b, dyj_b], axis=0), preferred_element_type=F32))
            dcb = dcb + lax.dot_general(dyj_b, xj_b, NT, preferred_element_type=F32) * lm
        dxdt_d = jnp.concatenate(dxds, axis=1)
        hs_state = _head_sums(xdt * dxdt_s)
        yd = jnp.concatenate(yds, axis=1)
        dcs = _head_sums(dyv * yo + _f(dy_b) * yd - _f(xdt_b) * dxdt_d) - hs_state
        dtot = dtot + jnp.sum(hs_state, axis=0, keepdims=True)
        dxdt = dxdt_s + dxdt_d
        ddt_c = _head_sums(dxdt * xs)
        dx = dxdt * dt_x
        if skip:
            dx = dx + dsk_ref[...] * dyv
        dcb_b = dcb.astype(BF16)
        d_c = d_c + jnp.dot(dcb_b, bm, preferred_element_type=F32)
        d_b = d_b + lax.dot_general(dcb_b, cm, TN, preferred_element_type=F32)
        da_c = jnp.dot(wt.astype(F32), dcs, precision=HIGHEST, preferred_element_type=F32) + dtot
        ddt_ref[...] = lax.dot_general(ddt_c, eye, TN, precision=HIGHEST, preferred_element_type=F32)
        da_ref[...] = lax.dot_general(da_c, eye, TN, precision=HIGHEST, preferred_element_type=F32)
        dx_ref[...] = jnp.concatenate([dx, d_b, d_c], axis=1).astype(dx_ref.dtype)
        if skip:
            part = jnp.sum(dyv * xs, axis=0, keepdims=True)

            @pl.when(zi == 0)
            def _():
                ddsk_ref[...] = part

            @pl.when(zi > 0)
            def _():
                ddsk_ref[...] += part

    t_in = pl.BlockSpec((HEADS_PER_GROUP, CHUNK), lambda g, zi: (dir_off + g, zmap(zi)))
    t_out = pl.BlockSpec((HEADS_PER_GROUP, CHUNK), lambda g, zi: (g, zmap(zi)))
    x_spec = pl.BlockSpec((CHUNK, GROUP_W), lambda g, zi: (zmap(zi), g))
    in_specs = [x_spec, t_in, t_in, pl.BlockSpec((CHUNK, GROUP_X), lambda g, zi: (zmap(zi), g)),
                pl.BlockSpec((1, 1, GROUP_X, D_STATE), lambda g, zi: (g, zmap(zi), 0, 0))]
    args = [xbc, dt_t, a_t, dy, hin]
    out_specs = [x_spec, t_out, t_out]
    out_shape = [jax.ShapeDtypeStruct((s, n_groups * GROUP_W), BF16), jax.ShapeDtypeStruct((n_heads, s), F32),
                 jax.ShapeDtypeStruct((n_heads, s), F32)]
    if skip:
        in_specs.append(pl.BlockSpec((1, GROUP_X), lambda g, zi: (0, g)))
        args.append(dsk)
        out_specs.append(pl.BlockSpec((1, GROUP_X), lambda g, zi: (0, g)))
        out_shape.append(jax.ShapeDtypeStruct((1, n_groups * GROUP_X), F32))
    return pl.pallas_call(
        body, name=name, grid=(n_groups, nc), in_specs=in_specs, out_specs=out_specs, out_shape=out_shape,
        scratch_shapes=[pltpu.VMEM((GROUP_X, D_STATE), F32)], compiler_params=_params(2),
    )(*args)


def _mesh_pos():
    return lax.axis_index("x"), lax.axis_index("y"), lax.axis_index("c")


def _slot(p):
    return 4 * p[0] + 2 * p[1] + p[2]


def _exchange_sems(n):
    return [pltpu.SemaphoreType.DMA((7 * n,)), pltpu.SemaphoreType.DMA((7 * n,)), pltpu.SemaphoreType.DMA((n,))]


def _gather_ops(ins, outs, send, recv, loc):
    n = len(ins)
    x, y, c = _mesh_pos()
    me, sib = (x, y, c), (x, y, 1 - c)
    chips = [(1 - x, y), (x, 1 - y), (1 - x, 1 - y)]

    def cp(a, k, block, to, src=None):
        dst = outs[a].at[_slot(block)]
        return pltpu.make_async_remote_copy(
            src_ref=dst if src is None else src, dst_ref=dst, send_sem=send.at[7 * a + k], recv_sem=recv.at[7 * a + k],
            device_id=to, device_id_type=MESH)

    mine = [pltpu.make_async_copy(ins[a], outs[a].at[_slot(me)], loc.at[a]) for a in range(n)]
    first = []
    for a in range(n):
        first.append(cp(a, 0, me, sib, src=ins[a]))
        first += [cp(a, 1 + j, me, (*chip, c), src=ins[a]) for j, chip in enumerate(chips)]

    def start():
        for d in mine + first:
            d.start()

    def finish():
        passed = []
        for a in range(n):
            for j, chip in enumerate(chips):
                cp(a, 1 + j, (*chip, c), me).wait_recv()
                p = cp(a, 4 + j, (*chip, c), sib)
                p.start()
                passed.append(p)
        for a in range(n):
            cp(a, 0, sib, me).wait_recv()
            for j, chip in enumerate(chips):
                cp(a, 4 + j, (*chip, 1 - c), me).wait_recv()
        for f in first + passed:
            f.wait_send()
        for m in mine:
            m.wait()

    return start, finish


_gather_ops.out_shapes = lambda arrs: [jax.ShapeDtypeStruct((N_DEV,) + a.shape, a.dtype) for a in arrs]


def _a2a_ops(ins, outs, send, recv, loc):
    n = len(ins)
    me = _mesh_pos()
    flips = [(fx, fy, fc) for fx in (0, 1) for fy in (0, 1) for fc in (0, 1)][1:]
    peers = [tuple(1 - p if fl else p for p, fl in zip(me, fl3)) for fl3 in flips]

    def cp(a, k, peer, dst_slot):
        return pltpu.make_async_remote_copy(
            src_ref=ins[a].at[_slot(peer)], dst_ref=outs[a].at[dst_slot], send_sem=send.at[7 * a + k],
            recv_sem=recv.at[7 * a + k], device_id=peer, device_id_type=MESH)

    mine = [pltpu.make_async_copy(ins[a].at[_slot(me)], outs[a].at[_slot(me)], loc.at[a]) for a in range(n)]
    sends = [cp(a, k, peer, _slot(me)) for a in range(n) for k, peer in enumerate(peers)]

    def start():
        for d in mine + sends:
            d.start()

    def finish():
        for a in range(n):
            for k, peer in enumerate(peers):
                cp(a, k, peer, _slot(peer)).wait_recv()
        for s_ in sends:
            s_.wait_send()
        for m in mine:
            m.wait()

    return start, finish


_a2a_ops.out_shapes = lambda arrs: [jax.ShapeDtypeStruct(a.shape, a.dtype) for a in arrs]


def _exchange(name, ops, arrs):
    n = len(arrs)

    def body(*refs):
        start, finish = ops(refs[:n], refs[n:2 * n], *refs[2 * n:])
        start()
        finish()

    any_spec = pl.BlockSpec(memory_space=pl.ANY)
    return pl.pallas_call(
        body, name=name, in_specs=[any_spec] * n, out_specs=[any_spec] * n, out_shape=ops.out_shapes(arrs),
        scratch_shapes=_exchange_sems(n),
    )(*arrs)


def _all_gather(name, arrs):
    return _exchange(name, _gather_ops, arrs)


def _all_to_all(name, arrs):
    return _exchange(name, _a2a_ops, arrs)


def _sum8(parts_ref):
    g = _f(parts_ref[0])
    for j in range(1, N_DEV):
        g = g + _f(parts_ref[j])
    return g


def _adam_math(w, g, m, v):
    m = ADAM_B1 * m + (1.0 - ADAM_B1) * g
    v = ADAM_B2 * v + (1.0 - ADAM_B2) * (g * g)
    m_hat = m / (1.0 - ADAM_B1 ** ADAM_STEP)
    v_hat = v / (1.0 - ADAM_B2 ** ADAM_STEP)
    return -ADAM_LR * (m_hat / (jnp.sqrt(v_hat) + ADAM_EPS) + ADAM_WD * w), m, v


def _reduce_sum(name, parts):
    _, r, c = parts.shape

    def body(p_ref, o_ref):
        o_ref[...] = _sum8(p_ref)

    return pl.pallas_call(body, name=name, out_shape=jax.ShapeDtypeStruct((r, c), F32))(parts)


def _adam(name, w, g, m, v, parts=None):
    r, c = w.shape
    tr = _tile(r, (128, 64, 32, 16)) if r * c > 2 ** 18 else r
    spec = pl.BlockSpec((tr, c), lambda i: (i, 0))
    summed = parts is not None

    def body(*refs):
        if summed:
            p_ref, w_ref, m_ref, v_ref, g_out, d_out, m_out, v_out = refs
            gv = _sum8(p_ref)
            g_out[...] = gv
        else:
            g_ref, w_ref, m_ref, v_ref, d_out, m_out, v_out = refs
            gv = g_ref[...]
        d, mn, vn = _adam_math(w_ref[...], gv, m_ref[...], v_ref[...])
        d_out[...] = d
        m_out[...] = mn
        v_out[...] = vn

    first = [pl.BlockSpec((N_DEV, tr, c), lambda i: (0, i, 0))] if summed else [spec]
    n_out = 4 if summed else 3
    return pl.pallas_call(
        body, name=name, grid=(r // tr,), in_specs=first + [spec] * 3, out_specs=[spec] * n_out,
        out_shape=[jax.ShapeDtypeStruct((r, c), F32)] * n_out, compiler_params=_params(1),
    )(parts if summed else g, w, m, v)


def _xbc_runs(d_ssm, n_groups):
    runs = []
    for g in range(n_groups):
        runs += [(GROUP_X * g, GROUP_X), (d_ssm + D_STATE * g, D_STATE), (d_ssm + n_groups * D_STATE + D_STATE * g, D_STATE)]
    return runs


def _in_runs(d, d_ssm, n_groups, n_heads):
    d_xbc = d_ssm + 2 * n_groups * D_STATE
    o_dt = d_ssm + d_xbc
    o_glu = o_dt + 2 * n_heads
    runs = [(0, d_ssm)] + [(d_ssm + s, l) for s, l in _xbc_runs(d_ssm, n_groups)]
    return runs + [(o_glu, 4 * d), (o_dt, 2 * n_heads)]


def _permute(a, runs):
    return jnp.concatenate([a[..., s:s + l] for s, l in runs], axis=-1)


def _unpermute(a, runs):
    offs = np.concatenate([[0], np.cumsum([l for _, l in runs])])
    order = sorted(range(len(runs)), key=lambda k: runs[k][0])
    return jnp.concatenate([a[..., int(offs[k]):int(offs[k]) + runs[k][1]] for k in order], axis=-1)


def _cols_from_shards(g):
    return g.transpose(1, 0, 2).reshape(g.shape[1], -1)


def _cols_to_shards(a):
    return a.reshape(a.shape[0], N_DEV, -1).transpose(1, 0, 2)


def _pad_rows(a, rows):
    return jnp.concatenate([a, jnp.zeros((rows - a.shape[0], a.shape[1]), a.dtype)], axis=0)


def _pack(vecs):
    a = jnp.concatenate(vecs, axis=1)
    return jnp.concatenate([a, jnp.zeros((1, -a.shape[1] % 1024), a.dtype)], axis=1).reshape(-1, 128)


def kernel(x, c, w_ada, b_ada, g_pre_mix, g_post_mix, w_in, w_conv_ssm, b_conv_ssm, dt_bias_fwd, dt_bias_bwd, a_log_fwd, a_log_bwd, d_skip, g_ssm_norm, w_ssm_out, b_glu, w_dw, b_dw, ln_g, ln_b, w_conv_out, b_conv_out, b_gate, w_mix_out, g_pre_ffn, g_post_ffn, w_gate_up, w_down, loss_target, m_w_ada, m_b_ada, m_g_pre_mix, m_g_post_mix, m_w_in, m_w_conv_ssm, m_b_conv_ssm, m_dt_bias_fwd, m_dt_bias_bwd, m_a_log_fwd, m_a_log_bwd, m_d_skip, m_g_ssm_norm, m_w_ssm_out, m_b_glu, m_w_dw, m_b_dw, m_ln_g, m_ln_b, m_w_conv_out, m_b_conv_out, m_b_gate, m_w_mix_out, m_g_pre_ffn, m_g_post_ffn, m_w_gate_up, m_w_down, v_w_ada, v_b_ada, v_g_pre_mix, v_g_post_mix, v_w_in, v_w_conv_ssm, v_b_conv_ssm, v_dt_bias_fwd, v_dt_bias_bwd, v_a_log_fwd, v_a_log_bwd, v_d_skip, v_g_ssm_norm, v_w_ssm_out, v_b_glu, v_w_dw, v_b_dw, v_ln_g, v_ln_b, v_w_conv_out, v_b_conv_out, v_b_gate, v_w_mix_out, v_g_pre_ffn, v_g_post_ffn, v_w_gate_up, v_w_down):
    weights = dict(w_ada=w_ada, b_ada=b_ada, g_pre_mix=g_pre_mix, g_post_mix=g_post_mix, w_in=w_in, w_conv_ssm=w_conv_ssm, b_conv_ssm=b_conv_ssm, dt_bias_fwd=dt_bias_fwd, dt_bias_bwd=dt_bias_bwd, a_log_fwd=a_log_fwd, a_log_bwd=a_log_bwd, d_skip=d_skip, g_ssm_norm=g_ssm_norm, w_ssm_out=w_ssm_out, b_glu=b_glu, w_dw=w_dw, b_dw=b_dw, ln_g=ln_g, ln_b=ln_b, w_conv_out=w_conv_out, b_conv_out=b_conv_out, b_gate=b_gate, w_mix_out=w_mix_out, g_pre_ffn=g_pre_ffn, g_post_ffn=g_post_ffn, w_gate_up=w_gate_up, w_down=w_down)
    mom_m = dict(w_ada=m_w_ada, b_ada=m_b_ada, g_pre_mix=m_g_pre_mix, g_post_mix=m_g_post_mix, w_in=m_w_in, w_conv_ssm=m_w_conv_ssm, b_conv_ssm=m_b_conv_ssm, dt_bias_fwd=m_dt_bias_fwd, dt_bias_bwd=m_dt_bias_bwd, a_log_fwd=m_a_log_fwd, a_log_bwd=m_a_log_bwd, d_skip=m_d_skip, g_ssm_norm=m_g_ssm_norm, w_ssm_out=m_w_ssm_out, b_glu=m_b_glu, w_dw=m_w_dw, b_dw=m_b_dw, ln_g=m_ln_g, ln_b=m_ln_b, w_conv_out=m_w_conv_out, b_conv_out=m_b_conv_out, b_gate=m_b_gate, w_mix_out=m_w_mix_out, g_pre_ffn=m_g_pre_ffn, g_post_ffn=m_g_post_ffn, w_gate_up=m_w_gate_up, w_down=m_w_down)
    mom_v = dict(w_ada=v_w_ada, b_ada=v_b_ada, g_pre_mix=v_g_pre_mix, g_post_mix=v_g_post_mix, w_in=v_w_in, w_conv_ssm=v_w_conv_ssm, b_conv_ssm=v_b_conv_ssm, dt_bias_fwd=v_dt_bias_fwd, dt_bias_bwd=v_dt_bias_bwd, a_log_fwd=v_a_log_fwd, a_log_bwd=v_a_log_bwd, d_skip=v_d_skip, g_ssm_norm=v_g_ssm_norm, w_ssm_out=v_w_ssm_out, b_glu=v_b_glu, w_dw=v_w_dw, b_dw=v_b_dw, ln_g=v_ln_g, ln_b=v_ln_b, w_conv_out=v_w_conv_out, b_conv_out=v_b_conv_out, b_gate=v_b_gate, w_mix_out=v_w_mix_out, g_pre_ffn=v_g_pre_ffn, g_post_ffn=v_g_post_ffn, w_gate_up=v_w_gate_up, w_down=v_w_down)
    names = list(weights)

    s, d = x.shape[1], x.shape[2]
    d_ssm, n_heads, d_xbc = g_ssm_norm.shape[-1], d_skip.shape[-1], b_conv_ssm.shape[-1]
    n_groups = n_heads // HEADS_PER_GROUP
    d_ff = w_down.shape[1] * N_DEV
    k_ssm, k_dw = w_conv_ssm.shape[1], w_dw.shape[1]
    assert d_ssm == n_groups * GROUP_X and d_xbc == n_groups * GROUP_W and s % CHUNK == 0
    assert d % COL_BLOCK == 0 and d_ff % COL_BLOCK == 0
    mc = w_ada.shape[-1]
    me = _slot(_mesh_pos())
    in_runs, xbc_runs = _in_runs(d, d_ssm, n_groups, n_heads), _xbc_runs(d_ssm, n_groups)
    wb = d_ssm + d_xbc + 4 * d
    p_x, p_ga, p_gb, p_la, p_lb = d_ssm, d_ssm + d_xbc, d_ssm + d_xbc + d, d_ssm + d_xbc + 2 * d, d_ssm + d_xbc + 3 * d
    ncb, nfb = d // COL_BLOCK, d_ff // COL_BLOCK
    x2, tgt = x[0], loss_target[0]

    big = ["w_in", "w_ssm_out", "w_conv_out", "w_mix_out", "w_gate_up", "w_down"]
    (w_in_g,) = _all_gather("gather_w_in", [w_in[0].astype(BF16)])
    w_in_p = _permute(_cols_from_shards(w_in_g), in_runs)
    w_big, w_dt = w_in_p[:, :wb], w_in_p[:, wb:]

    n_cs, n_dw = k_ssm * (d_xbc // N_DEV), k_dw * (d // N_DEV)
    (small_all,) = _all_gather("gather_small", [_pack([c, w_conv_ssm.reshape(1, n_cs), w_dw.reshape(1, n_dw)])])
    small_all = small_all.reshape(N_DEV, -1)
    c_all = small_all[:, :d]
    wcs = _cols_from_shards(small_all[:, d:d + n_cs].reshape(N_DEV, k_ssm, -1))
    wdw = _cols_from_shards(small_all[:, d + n_cs:d + n_cs + n_dw].reshape(N_DEV, k_dw, -1))
    wcs_p = _permute(wcs, xbc_runs)
    bcs_p = _permute(b_conv_ssm, xbc_runs)
    kp_ssm, kp_dw = -(-k_ssm // 8) * 8, -(-k_dw // 8) * 8
    zero_x, zero_d = jnp.zeros((1, d_xbc), F32), jnp.zeros((1, d), F32)

    (c_act,), _ = _rowcall("c_act", _fwd_fn(_silu), [(c_all, 0, d)], [], [(d, F32)], [])
    b_my = lax.dynamic_slice(b_ada, (0, me * mc), (1, mc))
    mod_cols = _matmul("mod", c_act, w_ada[0], "nn", F32, acc_in=jnp.broadcast_to(b_my, (N_DEV, mc)))
    (mod_rows,) = _all_to_all("mod_to_owner", [mod_cols.reshape(N_DEV, 1, mc)])
    mod = mod_rows.reshape(1, N_DEV * mc)
    sh1, sc1, g1, sh2, sc2, g2 = [mod[:, k * d:(k + 1) * d] for k in range(6)]

    (h1,), _ = _rowcall("prenorm_mix", _fwd_fn(_t_prenorm), [(x2, 0, d)], [g_pre_mix, sc1, sh1], [(d, BF16)], [])
    proj, rest = _matmul("in_proj", h1, w_big, "nn", BF16, carry=(_gather_ops, [weights[k][0].astype(BF16) for k in big[1:]]))
    gath = dict(zip(big[1:], rest))
    w_ssm, w_co, w_mo = gath["w_ssm_out"].reshape(d_ssm, d), gath["w_conv_out"].reshape(d, d), gath["w_mix_out"].reshape(d, d)
    w_gu, w_dn = _cols_from_shards(gath["w_gate_up"]), gath["w_down"].reshape(d_ff, d)
    dt_raw = _matmul("in_proj_dt", h1, w_dt, "nn", F32)
    xbc_c = _dwconv("conv_ssm", proj, p_x, d_xbc, _pad_rows(wcs_p, kp_ssm), k_ssm, bcs_p, True, BF16)
    alog_col = jnp.concatenate([a_log_fwd, a_log_bwd], axis=1).reshape(2 * n_heads, 1)
    dt_t, a_t = _dtprep("dt_prep", dt_raw, jnp.concatenate([dt_bias_fwd, dt_bias_bwd], axis=1), alog_col)
    dsk = jnp.repeat(d_skip, HEAD_DIM, axis=1)
    y_f, hin_f = _ssd_fwd("ssd_fwd", xbc_c, dt_t, a_t, dsk, False, True, n_groups)
    y_b, hin_b = _ssd_fwd("ssd_rev", xbc_c, dt_t, a_t, dsk, True, False, n_groups)
    gn_rows = [(y_f, 0, d_ssm), (y_b, 0, d_ssm), (proj, 0, d_ssm)]
    (yn,), _ = _rowcall("ssm_norm", _fwd_fn(_t_gnorm), gn_rows, [g_ssm_norm], [(d_ssm, BF16)], [], ncol=n_groups)
    ya = _matmul("ssm_out", yn, w_ssm, "nn", BF16)
    glu_rows, glu_prm = [(proj, p_ga, d), (proj, p_gb, d)], [b_glu[:, :d], b_glu[:, d:]]
    (u0,), _ = _rowcall("glu", _fwd_fn(_t_glu), glu_rows, glu_prm, [(d, BF16)], [], ncol=ncb)
    wdw_pad = _pad_rows(wdw, kp_dw)
    u1 = _dwconv("conv_dw", u0, 0, d, wdw_pad, k_dw, b_dw, False, BF16)
    (u2,), _ = _rowcall("ln_silu", _fwd_fn(_t_lnsilu), [(u1, 0, d)], [ln_g, ln_b], [(d, BF16)], [])
    yb = _matmul("conv_out", u2, w_co, "nn", BF16)
    gm_rows, gm_prm = [(ya, 0, d), (yb, 0, d), (proj, p_la, d), (proj, p_lb, d)], [b_conv_out, b_gate[:, :d], b_gate[:, d:]]
    (m_in,), _ = _rowcall("gate_mix", _fwd_fn(_t_gatemix), gm_rows, gm_prm, [(d, BF16)], [], ncol=ncb)
    mix = _matmul("mix_out", m_in, w_mo, "nn", BF16)
    pp_prm = [g_post_mix, g1, g_pre_ffn, sc2, sh2]
    (x1, h2), _ = _rowcall("post_mix", _fwd_fn(_t_postpre), [(x2, 0, d), (mix, 0, d)], pp_prm, [(d, F32), (d, BF16)], [])
    gu = _matmul("gate_up", h2, w_gu, "nn", BF16)
    sw_rows = [(gu, 0, d_ff), (gu, d_ff, d_ff)]
    (act,), _ = _rowcall("swiglu", _fwd_fn(_t_swiglu), sw_rows, [], [(d_ff, BF16)], [], ncol=nfb)
    f = _matmul("down", act, w_dn, "nn", BF16)

    def loss_fn(x1_t, f_t, tgt_t, gpost, g2_t):
        args = [_f(x1_t), _f(f_t), _f(tgt_t), _f(gpost), _f(g2_t)]
        val, vjp = jax.vjp(_t_loss, *args)
        gr = vjp(jnp.ones((), F32))
        return (gr[0], gr[1]), (jnp.zeros((1, 128), F32) + val, gr[3], gr[4])

    (dx1a, df), (loss_v, d_gpost2, d_g2) = _rowcall(
        "loss", loss_fn, [(x1, 0, d), (f, 0, d), (tgt, 0, d)], [g_post_ffn, g2], [(d, F32), (d, BF16)], [128, d, d])
    loss = lax.psum(loss_v[0, 0], ("x", "y", "c"))

    dact = _matmul("down_dx", df, w_dn, "nt", BF16)
    dw_dn = _matmul("down_dw", act, df, "tn", BF16)
    (dgt, dup), _ = _rowcall("swiglu_bwd", _vjp_fn(_t_swiglu, 2, 1, (0, 1), ()), sw_rows + [(dact, 0, d_ff)], [],
                             [(d_ff, BF16), (d_ff, BF16)], [], ncol=nfb)
    dgu = jnp.concatenate([dgt, dup], axis=1)
    recv = {}
    dh2, (recv["w_down"],) = _matmul("gate_up_dx", dgu, w_gu, "nt", BF16, carry=(_a2a_ops, [dw_dn.reshape(N_DEV, -1, d)]))
    dw_gu = _matmul("gate_up_dw", h2, dgu, "tn", BF16)
    (dxa, dmix), (d_gpost1, d_g1, d_gpre2, d_sc2, d_sh2) = _rowcall(
        "post_mix_bwd", _vjp_fn(_t_postpre, 2, 2, (0, 1), (0, 1, 2, 3, 4)),
        [(x2, 0, d), (mix, 0, d), (dx1a, 0, d), (dh2, 0, d)], pp_prm, [(d, F32), (d, BF16)], [d] * 5)
    dm_in = _matmul("mix_out_dx", dmix, w_mo, "nt", BF16)
    dw_mo = _matmul("mix_out_dw", m_in, dmix, "tn", BF16)
    (dya, dyb, dla, dlb), (d_bco, d_bga, d_bgb) = _rowcall(
        "gate_mix_bwd", _vjp_fn(_t_gatemix, 4, 1, (0, 1, 2, 3), (0, 1, 2)), gm_rows + [(dm_in, 0, d)], gm_prm,
        [(d, BF16)] * 4, [d] * 3, ncol=ncb)
    du2 = _matmul("conv_out_dx", dyb, w_co, "nt", BF16)
    dw_co = _matmul("conv_out_dw", u2, dyb, "tn", BF16)
    (du1,), (d_lng, d_lnb) = _rowcall("ln_silu_bwd", _vjp_fn(_t_lnsilu, 1, 1, (0,), (0, 1)), [(u1, 0, d), (du2, 0, d)],
                                      [ln_g, ln_b], [(d, BF16)], [d, d])
    dwdw_p, d_bdw = _dwconv_bwd_w("conv_dw_dw", u0, 0, d, wdw_pad, k_dw, b_dw, [du1], False)
    du0 = _dwconv("conv_dw_dx", du1, 0, d, _pad_rows(wdw[::-1], kp_dw), k_dw, zero_d, False, BF16)
    (dga, dgb), (d_bglu_a, d_bglu_b) = _rowcall("glu_bwd", _vjp_fn(_t_glu, 2, 1, (0, 1), (0, 1)), glu_rows + [(du0, 0, d)],
                                                glu_prm, [(d, BF16)] * 2, [d, d], ncol=ncb)
    dyn, (recv["w_mix_out"], recv["w_conv_out"]) = _matmul(
        "ssm_out_dx", dya, w_ssm, "nt", BF16, carry=(_a2a_ops, [dw_mo.reshape(N_DEV, -1, d), dw_co.reshape(N_DEV, -1, d)]))
    dw_ssm = _matmul("ssm_out_dw", yn, dya, "tn", BF16)
    (dy_ssd, dz), (d_gn,) = _rowcall("ssm_norm_bwd", _vjp_fn(_t_gnorm, 3, 1, (0, 2), (0,)), gn_rows + [(dyn, 0, d_ssm)],
                                     [g_ssm_norm], [(d_ssm, BF16)] * 2, [d_ssm], ncol=n_groups)
    dxbc_f, ddt_f, da_f, ddsk = _ssd_bwd("ssd_fwd_bwd", xbc_c, dt_t, a_t, dy_ssd, hin_f, dsk, False, True, n_groups)
    dxbc_b, ddt_b, da_b = _ssd_bwd("ssd_rev_bwd", xbc_c, dt_t, a_t, dy_ssd, hin_b, dsk, True, False, n_groups)
    ddt_raw, d_dtb, d_alog = _dtprep_bwd("dt_prep_bwd", ddt_f, ddt_b, da_f, da_b, dt_t, alog_col)
    dpre, dwcs_p, dbcs_p = _dwconv_bwd_w("conv_ssm_dw", proj, p_x, d_xbc, _pad_rows(wcs_p, kp_ssm), k_ssm, bcs_p,
                                          [dxbc_f, dxbc_b], True)
    dxbc = _dwconv("conv_ssm_dx", dpre, 0, d_xbc, _pad_rows(wcs_p[::-1], kp_ssm), k_ssm, zero_x, False, BF16)
    dproj = jnp.concatenate([dz, dxbc, dga, dgb, dla, dlb], axis=1)
    dw_big, (recv["w_gate_up"], recv["w_ssm_out"]) = _matmul(
        "in_proj_dw", h1, dproj, "tn", BF16, carry=(_a2a_ops, [_cols_to_shards(dw_gu), dw_ssm.reshape(N_DEV, -1, d)]))
    dw_dt = _matmul("in_proj_dt_dw", h1, ddt_raw, "tn", BF16)
    dw_in = _unpermute(jnp.concatenate([dw_big, dw_dt], axis=1), in_runs)
    dh1, (recv["w_in"],) = _matmul("in_proj_dx", dproj, w_big, "nt", F32, carry=(_a2a_ops, [_cols_to_shards(dw_in)]))
    dh1 = _matmul("in_proj_dt_dx", ddt_raw, w_dt, "nt", BF16, acc_in=dh1)

    def prenorm_bwd_fn(x_t, dh_t, dxa_t, g, sc, sh):
        rows, prm = _vjp_fn(_t_prenorm, 1, 1, (0,), (0, 1, 2))(x_t, dh_t, g, sc, sh)
        return (rows[0] + _f(dxa_t),), prm

    (grad_x,), (d_gpre1, d_sc1, d_sh1) = _rowcall(
        "prenorm_mix_bwd", prenorm_bwd_fn, [(x2, 0, d), (dh1, 0, d), (dxa, 0, d)], [g_pre_mix, sc1, sh1], [(d, F32)], [d] * 3)

    dmod = jnp.concatenate([d_sh1, d_sc1, d_g1, d_sh2, d_sc2, d_g2], axis=1)
    small_g = {
        "b_ada": dmod, "g_pre_mix": d_gpre1, "g_post_mix": d_gpost1, "b_conv_ssm": _unpermute(dbcs_p, xbc_runs),
        "dt_bias_fwd": d_dtb[:n_heads].reshape(1, n_heads), "dt_bias_bwd": d_dtb[n_heads:].reshape(1, n_heads),
        "a_log_fwd": d_alog[:n_heads].reshape(1, n_heads), "a_log_bwd": d_alog[n_heads:].reshape(1, n_heads),
        "d_skip": ddsk.reshape(n_heads, HEAD_DIM).sum(axis=1).reshape(1, n_heads), "g_ssm_norm": d_gn,
        "b_glu": jnp.concatenate([d_bglu_a, d_bglu_b], axis=1), "b_dw": d_bdw, "ln_g": d_lng, "ln_b": d_lnb,
        "b_conv_out": d_bco, "b_gate": jnp.concatenate([d_bga, d_bgb], axis=1), "g_pre_ffn": d_gpre2, "g_post_ffn": d_gpost2,
        "w_conv_ssm": _unpermute(dwcs_p[:k_ssm], xbc_runs).reshape(1, k_ssm * d_xbc), "w_dw": dwdw_p[:k_dw].reshape(1, k_dw * d),
    }
    small_names = list(small_g)
    sizes = [small_g[k].shape[1] for k in small_names]
    offs = np.concatenate([[0], np.cumsum(sizes)])
    (small_parts,) = _all_gather("gather_small_grads", [_pack([small_g[k] for k in small_names])])
    dmod_all = small_parts.reshape(N_DEV, -1)[:, :6 * d]
    small_tot = _reduce_sum("sum_small_grads", small_parts).reshape(1, -1)
    grads = {k: small_tot[:, int(offs[i]):int(offs[i + 1])] for i, k in enumerate(small_names)}
    grads["w_conv_ssm"] = lax.dynamic_slice(grads["w_conv_ssm"].reshape(k_ssm, d_xbc), (0, me * (d_xbc // N_DEV)), (k_ssm, d_xbc // N_DEV))
    grads["w_dw"] = lax.dynamic_slice(grads["w_dw"].reshape(k_dw, d), (0, me * (d // N_DEV)), (k_dw, d // N_DEV))
    grads["w_ada"] = _matmul("mod_dw", c_act, lax.dynamic_slice(dmod_all, (0, me * mc), (N_DEV, mc)), "tn", F32, highest=True)

    delta, new_m, new_v = {}, {}, {}
    for k in big:
        grads[k], delta[k], new_m[k], new_v[k] = _adam("adam_" + k, weights[k][0], None, mom_m[k][0], mom_v[k][0], parts=recv[k])
    for k in ["w_ada", "w_conv_ssm", "w_dw"]:
        delta[k], new_m[k], new_v[k] = _adam("adam_" + k, weights[k][0], grads[k], mom_m[k][0], mom_v[k][0])
    rep = [k for k in names if k not in big and k not in ("w_ada", "w_conv_ssm", "w_dw")]
    r_sizes = [weights[k].shape[1] for k in rep]
    r_offs = np.concatenate([[0], np.cumsum(r_sizes)])
    packed = [_pack([t[k] for k in rep]) for t in (weights, grads, mom_m, mom_v)]
    r_delta, r_m, r_v = [t.reshape(1, -1) for t in _adam("adam_small", *packed)]
    for i, k in enumerate(rep):
        sl = slice(int(r_offs[i]), int(r_offs[i + 1]))
        delta[k], new_m[k], new_v[k] = r_delta[:, sl], r_m[:, sl], r_v[:, sl]

    def shaped(t, k):
        return t[k].reshape(weights[k].shape)

    return (loss, grad_x.reshape(x.shape), *[shaped(grads, k) for k in names], *[shaped(delta, k) for k in names],
            *[shaped(new_m, k) for k in names], *[shaped(new_v, k) for k in names])
```

```python
import functools

import numpy as np
import jax
import jax.numpy as jnp
from jax import lax
from jax.experimental import pallas as pl
from jax.experimental.pallas import tpu as pltpu

F32 = jnp.float32
BF16 = jnp.bfloat16
HIGHEST = lax.Precision.HIGHEST
MESH = pl.DeviceIdType.MESH
N_DEV = 8
EPS = 1e-6
CHUNK = 128
HEAD_DIM = 64
D_STATE = 128
HEADS_PER_GROUP = 8
GROUP_X = HEADS_PER_GROUP * HEAD_DIM
GROUP_W = GROUP_X + 2 * D_STATE
COL_BLOCK = 512
HALO = 16
VMEM_LIMIT = 56 * 1024 * 1024
ADAM_LR, ADAM_B1, ADAM_B2, ADAM_EPS, ADAM_WD, ADAM_STEP = 0.001, 0.9, 0.999, 1e-08, 0.01, 10

NN = (((1,), (0,)), ((), ()))
NT = (((1,), (1,)), ((), ()))
TN = (((0,), (0,)), ((), ()))


def _tile(dim, prefs):
    for p in prefs:
        if p <= dim and dim % p == 0:
            return p
    return dim


def _params(n_grid):
    return pltpu.CompilerParams(dimension_semantics=("arbitrary",) * n_grid, vmem_limit_bytes=VMEM_LIMIT)


def _f(v):
    return v.astype(F32)


def _matmul(name, a, b, mode, out_dtype, acc_in=None, highest=False, carry=None):
    if mode == "nn":
        (m, k), (k2, n) = a.shape, b.shape
    elif mode == "nt":
        (m, k), (n, k2) = a.shape, b.shape
    else:
        (k, m), (k2, n) = a.shape, b.shape
    assert k == k2, (name, a.shape, b.shape, mode)
    tm = _tile(m, (1024, 512, 256, 128))
    tn = _tile(n, (1024, 512, 256, 128))
    tk = _tile(k, (2048, 1024, 512, 256, 128))
    nk = k // tk
    dims = {"nn": NN, "nt": NT, "tn": TN}[mode]
    a_spec = pl.BlockSpec((tk, tm), lambda i, j, kk: (kk, i)) if mode == "tn" else pl.BlockSpec((tm, tk), lambda i, j, kk: (i, kk))
    b_spec = pl.BlockSpec((tn, tk), lambda i, j, kk: (j, kk)) if mode == "nt" else pl.BlockSpec((tk, tn), lambda i, j, kk: (kk, j))
    o_spec = pl.BlockSpec((tm, tn), lambda i, j, kk: (i, j))
    has_acc = acc_in is not None
    n_in = 3 if has_acc else 2
    ops, c_arrs = carry if carry is not None else (None, [])
    nc_ = len(c_arrs)
    grid = (m // tm, n // tn, nk)

    def body(*refs):
        a_ref, b_ref = refs[0], refs[1]
        c_ref = refs[2] if has_acc else None
        c_ins = refs[n_in:n_in + nc_]
        o_ref = refs[n_in + nc_]
        c_outs = refs[n_in + nc_ + 1:n_in + 2 * nc_ + 1]
        acc = refs[n_in + 2 * nc_ + 1]
        i, j, kk = pl.program_id(0), pl.program_id(1), pl.program_id(2)
        if nc_:
            start, finish = ops(c_ins, c_outs, *refs[n_in + 2 * nc_ + 2:])
            pl.when((i == 0) & (j == 0) & (kk == 0))(start)

        def product(first):
            if highest:
                p = lax.dot_general(_f(a_ref[...]), _f(b_ref[...]), dims, precision=HIGHEST, preferred_element_type=F32)
            else:
                p = lax.dot_general(a_ref[...].astype(BF16), b_ref[...].astype(BF16), dims, preferred_element_type=F32)
            return p + _f(c_ref[...]) if (first and has_acc) else p

        if nk == 1:
            o_ref[...] = product(True).astype(out_dtype)
        else:
            @pl.when(kk == 0)
            def _():
                acc[...] = product(True)

            if nk > 2:
                @pl.when((kk > 0) & (kk < nk - 1))
                def _():
                    acc[...] += product(False)

            @pl.when(kk == nk - 1)
            def _():
                o_ref[...] = (acc[...] + product(False)).astype(out_dtype)

        if nc_:
            pl.when((i == grid[0] - 1) & (j == grid[1] - 1) & (kk == nk - 1))(finish)

    any_spec = pl.BlockSpec(memory_space=pl.ANY)
    c_shapes = ops.out_shapes(c_arrs) if nc_ else []
    res = pl.pallas_call(
        body, name=name, grid=grid,
        in_specs=[a_spec, b_spec] + ([o_spec] if has_acc else []) + [any_spec] * nc_, out_specs=[o_spec] + [any_spec] * nc_,
        out_shape=[jax.ShapeDtypeStruct((m, n), out_dtype)] + c_shapes,
        scratch_shapes=[pltpu.VMEM((tm, tn) if nk > 1 else (8, 128), F32)] + (_exchange_sems(nc_) if nc_ else []),
        compiler_params=_params(3),
    )(*([a, b] + ([acc_in] if has_acc else []) + list(c_arrs)))
    return (res[0], res[1:]) if nc_ else res[0]


def _rowcall(name, fn, rows, params, out_rows, out_accs, ncol=1):
    s = rows[0][0].shape[0]
    tm = _tile(s, (256 if ncol == 1 else 1024, 128, 64, 32, 16, 8))
    args, in_specs = [], []
    for arr, off, w in rows:
        bw = w // ncol
        assert w % ncol == 0 and off % bw == 0, (name, off, w, ncol)
        in_specs.append(pl.BlockSpec((tm, bw), functools.partial(lambda j, i, ob: (i, ob + j), ob=off // bw)))
        args.append(arr)
    for p in params:
        assert p.shape[1] % ncol == 0, (name, p.shape)
        in_specs.append(pl.BlockSpec((p.shape[0], p.shape[1] // ncol), lambda j, i: (0, j)))
        args.append(p)
    out_shape, out_specs = [], []
    for w, dt in out_rows:
        out_shape.append(jax.ShapeDtypeStruct((s, w), dt))
        out_specs.append(pl.BlockSpec((tm, w // ncol), lambda j, i: (i, j)))
    for w in out_accs:
        out_shape.append(jax.ShapeDtypeStruct((1, w), F32))
        out_specs.append(pl.BlockSpec((1, w // ncol), lambda j, i: (0, j)))
    n_in, n_ro = len(args), len(out_rows)

    def body(*refs):
        i = pl.program_id(1)
        ro, ao = fn(*[r[...] for r in refs[:n_in]])
        outs = refs[n_in:]
        assert len(ro) == n_ro and len(ao) == len(out_accs), name
        for ref, v in zip(outs[:n_ro], ro):
            ref[...] = v.astype(ref.dtype)
        for ref, v in zip(outs[n_ro:], ao):
            @pl.when(i == 0)
            def _(ref=ref, v=v):
                ref[...] = v

            @pl.when(i > 0)
            def _(ref=ref, v=v):
                ref[...] += v

    res = pl.pallas_call(
        body, name=name, grid=(ncol, s // tm), in_specs=in_specs, out_specs=out_specs,
        out_shape=out_shape, compiler_params=_params(2),
    )(*args)
    return res[:n_ro], res[n_ro:]


def _fwd_fn(f):
    def fn(*vals):
        out = f(*[_f(v) for v in vals])
        return (out if isinstance(out, tuple) else (out,)), ()
    return fn


def _vjp_fn(f, n_prim, n_ct, want_rows, want_params):
    def fn(*vals):
        prims = [_f(v) for v in vals[:n_prim]]
        cts = [_f(v) for v in vals[n_prim:n_prim + n_ct]]
        prms = [_f(v) for v in vals[n_prim + n_ct:]]
        out, vjp = jax.vjp(f, *prims, *prms)
        g = vjp(tuple(cts) if isinstance(out, tuple) else cts[0])
        return tuple(g[k] for k in want_rows), tuple(g[n_prim + k] for k in want_params)
    return fn


def _rms(v):
    return v * lax.rsqrt(jnp.mean(v * v, axis=-1, keepdims=True) + EPS)


def _silu(v):
    return v * jax.nn.sigmoid(v)


def _t_prenorm(x, g, sc, sh):
    return _rms(x) * g * (1.0 + sc) + sh


def _t_gnorm(yf, yb, z, gn):
    return _rms((yf + yb) * _silu(z)) * gn


def _t_glu(a, b, ba, bb):
    return (a + ba) * jax.nn.sigmoid(b + bb)


def _t_lnsilu(u, g, b):
    uc = u - jnp.mean(u, axis=-1, keepdims=True)
    return _silu(uc * lax.rsqrt(jnp.mean(uc * uc, axis=-1, keepdims=True) + EPS) * g + b)


def _t_gatemix(ya, yb, la, lb, bco, bga, bgb):
    return jax.nn.sigmoid(la + bga) * ya + jax.nn.sigmoid(lb + bgb) * (yb + bco)


def _t_postpre(x, mix, gpost, g1, gpre, sc, sh):
    x1 = x + g1 * (_rms(mix) * gpost)
    return x1, _rms(x1) * gpre * (1.0 + sc) + sh


def _t_swiglu(gt, up):
    return _silu(gt) * up


def _t_loss(x1, f, tgt, gpost, g2):
    e = x1 + g2 * (_rms(f) * gpost) - tgt
    return 0.5 * jnp.sum(jnp.mean(e * e, axis=-1))


def _conv_specs(s, tm, cw, ob, w_rows):
    nh, last_h = tm // HALO, s // HALO - 1
    return [
        pl.BlockSpec((tm, cw), lambda j, i: (i, ob + j)),
        pl.BlockSpec((HALO, cw), lambda j, i: (jnp.maximum(i * nh - 1, 0), ob + j)),
        pl.BlockSpec((HALO, cw), lambda j, i: (jnp.minimum((i + 1) * nh, last_h), ob + j)),
        pl.BlockSpec((w_rows, cw), lambda j, i: (0, j)),
        pl.BlockSpec((1, cw), lambda j, i: (0, j)),
    ]


def _window(x_ref, p_ref, n_ref, i, nrow):
    prev = _f(p_ref[...]) * jnp.where(i > 0, 1.0, 0.0).astype(F32)
    nxt = _f(n_ref[...]) * jnp.where(i < nrow - 1, 1.0, 0.0).astype(F32)
    return jnp.concatenate([prev, _f(x_ref[...]), nxt], axis=0)


def _taps(win, w_ref, k_taps, tm):
    pad = (k_taps - 1) // 2
    acc = w_ref[0:1, :] * win[HALO - pad:HALO - pad + tm, :]
    for k in range(1, k_taps):
        acc = acc + w_ref[k:k + 1, :] * win[HALO + k - pad:HALO + k - pad + tm, :]
    return acc


def _conv_rows(k_taps):
    return 1024 if k_taps <= 8 else 512


def _dwconv(name, x, xoff, c, w_pad, k_taps, b, act, out_dtype, cw=256):
    s = x.shape[0]
    tm, cw = _tile(s, (_conv_rows(k_taps), 256, 128, 64, 32, 16)), _tile(c, (cw, 128))
    assert xoff % cw == 0 and tm % HALO == 0 and (k_taps - 1) // 2 <= HALO
    nrow = s // tm

    def body(x_ref, p_ref, n_ref, w_ref, b_ref, o_ref):
        win = _window(x_ref, p_ref, n_ref, pl.program_id(1), nrow)
        acc = _taps(win, w_ref, k_taps, tm) + b_ref[...]
        o_ref[...] = (_silu(acc) if act else acc).astype(out_dtype)

    return pl.pallas_call(
        body, name=name, grid=(c // cw, nrow), in_specs=_conv_specs(s, tm, cw, xoff // cw, w_pad.shape[0]),
        out_specs=pl.BlockSpec((tm, cw), lambda j, i: (i, j)), out_shape=jax.ShapeDtypeStruct((s, c), out_dtype),
        compiler_params=_params(2),
    )(x, x, x, w_pad, b)


def _dwconv_bwd_w(name, x, xoff, c, w_pad, k_taps, b, douts, act, cw=256):
    s = x.shape[0]
    tm, cw = _tile(s, (_conv_rows(k_taps), 256, 128, 64, 32, 16)), _tile(c, (cw, 128))
    assert xoff % cw == 0 and tm % HALO == 0
    nrow, pad, n_d = s // tm, (k_taps - 1) // 2, len(douts)

    def body(x_ref, p_ref, n_ref, w_ref, b_ref, *rest):
        d_refs, outs = rest[:n_d], rest[n_d:]
        i = pl.program_id(1)
        win = _window(x_ref, p_ref, n_ref, i, nrow)
        d = _f(d_refs[0][...])
        for r in d_refs[1:]:
            d = d + _f(r[...])
        if act:
            pre = _taps(win, w_ref, k_taps, tm) + b_ref[...]
            sg = jax.nn.sigmoid(pre)
            d = d * (sg * (1.0 + pre * (1.0 - sg)))
            outs[0][...] = d.astype(outs[0].dtype)
        dw_ref, db_ref = outs[-2], outs[-1]

        @pl.when(i == 0)
        def _():
            dw_ref[...] = jnp.zeros_like(dw_ref)
            db_ref[...] = jnp.zeros_like(db_ref)

        for k in range(k_taps):
            dw_ref[k:k + 1, :] += jnp.sum(d * win[HALO + k - pad:HALO + k - pad + tm, :], axis=0, keepdims=True)
        db_ref[...] += jnp.sum(d, axis=0, keepdims=True)

    row_spec = pl.BlockSpec((tm, cw), lambda j, i: (i, j))
    out_shape = [jax.ShapeDtypeStruct((w_pad.shape[0], c), F32), jax.ShapeDtypeStruct((1, c), F32)]
    out_specs = [pl.BlockSpec((w_pad.shape[0], cw), lambda j, i: (0, j)), pl.BlockSpec((1, cw), lambda j, i: (0, j))]
    if act:
        out_shape, out_specs = [jax.ShapeDtypeStruct((s, c), BF16)] + out_shape, [row_spec] + out_specs
    return pl.pallas_call(
        body, name=name, grid=(c // cw, nrow),
        in_specs=_conv_specs(s, tm, cw, xoff // cw, w_pad.shape[0]) + [row_spec] * n_d,
        out_specs=out_specs, out_shape=out_shape, compiler_params=_params(2),
    )(x, x, x, w_pad, b, *douts)


def _dtprep(name, dt_raw, bias, alog_col):
    s, h2 = dt_raw.shape
    ts = _tile(s, (512, 256, 128))

    def body(r_ref, b_ref, al_ref, dt_ref, a_ref):
        v = r_ref[...] + b_ref[...]
        dt = (jnp.maximum(v, 0.0) + jnp.log(1.0 + jnp.exp(-jnp.abs(v)))).T
        dt_ref[...] = dt
        a_ref[...] = dt * (-jnp.exp(al_ref[...]))

    t_spec = pl.BlockSpec((h2, ts), lambda i: (0, i))
    return pl.pallas_call(
        body, name=name, grid=(s // ts,),
        in_specs=[pl.BlockSpec((ts, h2), lambda i: (i, 0)), pl.BlockSpec((1, h2), lambda i: (0, 0)), pl.BlockSpec((h2, 1), lambda i: (0, 0))],
        out_specs=[t_spec, t_spec], out_shape=[jax.ShapeDtypeStruct((h2, s), F32)] * 2, compiler_params=_params(1),
    )(dt_raw, bias, alog_col)


def _dtprep_bwd(name, ddt_f, ddt_b, da_f, da_b, dt_t, alog_col):
    h2, s = dt_t.shape
    h = h2 // 2
    ts = _tile(s, (512, 256, 128))

    def body(ddf, ddb, daf, dab, dt_ref, al_ref, raw_ref, db_ref, dal_ref):
        i = pl.program_id(0)
        a_col = -jnp.exp(al_ref[...])
        da = jnp.concatenate([daf[...], dab[...]], axis=0)
        dt = dt_ref[...]
        d_raw = (jnp.concatenate([ddf[...], ddb[...]], axis=0) + da * a_col) * (1.0 - jnp.exp(-dt))
        raw_ref[...] = d_raw.T

        @pl.when(i == 0)
        def _():
            db_ref[...] = jnp.zeros_like(db_ref)
            dal_ref[...] = jnp.zeros_like(dal_ref)

        db_ref[...] += jnp.sum(d_raw, axis=1, keepdims=True)
        dal_ref[...] += jnp.sum(da * dt, axis=1, keepdims=True) * a_col

    half = pl.BlockSpec((h, ts), lambda i: (0, i))
    col = pl.BlockSpec((h2, 1), lambda i: (0, 0))
    return pl.pallas_call(
        body, name=name, grid=(s // ts,),
        in_specs=[half, half, half, half, pl.BlockSpec((h2, ts), lambda i: (0, i)), col],
        out_specs=[pl.BlockSpec((ts, h2), lambda i: (i, 0)), col, col],
        out_shape=[jax.ShapeDtypeStruct((s, h2), F32), jax.ShapeDtypeStruct((h2, 1), F32), jax.ShapeDtypeStruct((h2, 1), F32)],
        compiler_params=_params(1),
    )(ddt_f, ddt_b, da_f, da_b, dt_t, alog_col)


def _chunk_consts(a_r, dt_r, rev):
    ii = lax.broadcasted_iota(jnp.int32, (CHUNK, CHUNK), 0)
    jj = lax.broadcasted_iota(jnp.int32, (CHUNK, CHUNK), 1)
    w = (jj >= ii) if rev else (jj <= ii)
    wt = (ii >= jj) if rev else (ii <= jj)
    wf = w.astype(F32)
    eye = (ii == jj).astype(F32)
    cs_col = lax.dot_general(wf, a_r, NT, precision=HIGHEST, preferred_element_type=F32)
    cs_row = lax.dot_general(a_r, wf, NT, precision=HIGHEST, preferred_element_type=F32)
    dt_col = lax.dot_general(eye, dt_r, NT, precision=HIGHEST, preferred_element_type=F32)
    tot = jnp.sum(a_r, axis=1, keepdims=True)
    lo_half = lax.broadcasted_iota(jnp.int32, (CHUNK, 128), 1) < HEAD_DIM

    def lanes(col):
        return [jnp.broadcast_to(col[:, j:j + 1], (CHUNK, 128)) for j in range(HEADS_PER_GROUP)]

    def spread(bs):
        return jnp.concatenate([jnp.where(lo_half, bs[2 * k], bs[2 * k + 1]) for k in range(HEADS_PER_GROUP // 2)], axis=1)

    csc = lanes(cs_col)
    cs_x, dt_x = spread(csc), spread(lanes(dt_col))
    tot_x = cs_x[0:1, :] if rev else cs_x[CHUNK - 1:CHUNK, :]
    return w, wt, eye, csc, cs_row, tot, cs_x, dt_x, tot_x


def _hi_lo(v):
    hi = v.astype(BF16)
    return jnp.concatenate([hi, (v - _f(hi)).astype(BF16)], axis=1)


def _head_sums(v, terms=3):
    rr = lax.broadcasted_iota(jnp.int32, (GROUP_X, HEADS_PER_GROUP), 0)
    cc = lax.broadcasted_iota(jnp.int32, (GROUP_X, HEADS_PER_GROUP), 1)
    et = (lax.shift_right_logical(rr, 6) == cc).astype(BF16)
    out, rest = None, v
    for t in range(terms):
        part = rest.astype(BF16)
        if t < terms - 1:
            rest = rest - _f(part)
        p = jnp.dot(part, et, preferred_element_type=F32)
        out = p if out is None else out + p
    return out


def _ssd_fwd(name, xbc, dt_t, a_t, dsk, n_groups):
    s = xbc.shape[0]
    nc = s // CHUNK
    d_ssm = n_groups * GROUP_X

    def one(x_ref, dt_ref, a_ref, dsk_ref, y_ref, hin_ref, h_scr, rev):
        skip = dsk_ref is not None
        xs = _f(x_ref[:, 0:GROUP_X])
        bm = x_ref[:, GROUP_X:GROUP_X + D_STATE]
        cm = x_ref[:, GROUP_X + D_STATE:GROUP_W]
        w, _, _, csc, cs_row, tot, cs_x, dt_x, tot_x = _chunk_consts(a_ref[...], dt_ref[...], rev)
        cb = lax.dot_general(cm, bm, NT, preferred_element_type=F32)
        h = h_scr[...]
        h_b = h.astype(BF16)
        hin_ref[0, 0] = h_b
        xdt = xs * dt_x
        xdt_b = xdt.astype(BF16)
        yo = lax.dot_general(cm, h_b, NT, preferred_element_type=F32) * jnp.exp(cs_x)
        st = lax.dot_general((xdt * jnp.exp(tot_x - cs_x)).astype(BF16), bm, TN, preferred_element_type=F32)
        yds = []
        for j in range(HEADS_PER_GROUP):
            lo = HEAD_DIM * j
            h_scr[lo:lo + HEAD_DIM, :] = h[lo:lo + HEAD_DIM, :] * jnp.exp(tot[j:j + 1, 0:1]) + st[lo:lo + HEAD_DIM, :]
            lm = jnp.exp(jnp.where(w, csc[j] - cs_row[j:j + 1, :], -jnp.inf))
            yds.append(jnp.dot((cb * lm).astype(BF16), xdt_b[:, lo:lo + HEAD_DIM], preferred_element_type=F32))
        y = jnp.concatenate(yds, axis=1) + yo
        if skip:
            y = y + dsk_ref[...] * xs
        y_ref[...] = y.astype(y_ref.dtype)

    def body(xf_ref, dtf_ref, af_ref, xr_ref, dtr_ref, ar_ref, dsk_ref, yf_ref, hf_ref, yr_ref, hr_ref, hf_scr, hr_scr):
        @pl.when(pl.program_id(1) == 0)
        def _():
            hf_scr[...] = jnp.zeros_like(hf_scr)
            hr_scr[...] = jnp.zeros_like(hr_scr)

        one(xf_ref, dtf_ref, af_ref, dsk_ref, yf_ref, hf_ref, hf_scr, False)
        one(xr_ref, dtr_ref, ar_ref, None, yr_ref, hr_ref, hr_scr, True)

    def specs(rev):
        def z(zi):
            return nc - 1 - zi if rev else zi
        t_spec = pl.BlockSpec((HEADS_PER_GROUP, CHUNK), lambda g, zi: ((n_groups if rev else 0) + g, z(zi)))
        ins = [pl.BlockSpec((CHUNK, GROUP_W), lambda g, zi: (z(zi), g)), t_spec, t_spec]
        outs = [pl.BlockSpec((CHUNK, GROUP_X), lambda g, zi: (z(zi), g)),
                pl.BlockSpec((1, 1, GROUP_X, D_STATE), lambda g, zi: (g, z(zi), 0, 0))]
        return ins, outs

    (in_f, out_f), (in_r, out_r) = specs(False), specs(True)
    shapes = [jax.ShapeDtypeStruct((s, d_ssm), BF16), jax.ShapeDtypeStruct((n_groups, nc, GROUP_X, D_STATE), BF16)]
    return pl.pallas_call(
        body, name=name, grid=(n_groups, nc), in_specs=in_f + in_r + [pl.BlockSpec((1, GROUP_X), lambda g, zi: (0, g))],
        out_specs=out_f + out_r, out_shape=shapes + shapes,
        scratch_shapes=[pltpu.VMEM((GROUP_X, D_STATE), F32)] * 2, compiler_params=_params(2),
    )(xbc, dt_t, a_t, xbc, dt_t, a_t, dsk)


def _ssd_bwd(name, xbc, dt_t, a_t, dy, hin_f, hin_r, dsk, n_groups):
    s = xbc.shape[0]
    nc = s // CHUNK
    n_heads = n_groups * HEADS_PER_GROUP

    def one(x_ref, dt_ref, a_ref, dy_ref, hin_ref, dsk_ref, dx_ref, ddt_ref, da_ref, ddsk_ref, g_scr, rev):
        skip = dsk_ref is not None
        xs = _f(x_ref[:, 0:GROUP_X])
        bm = x_ref[:, GROUP_X:GROUP_X + D_STATE]
        cm = x_ref[:, GROUP_X + D_STATE:GROUP_W]
        dyv = _f(dy_ref[...])
        w, wt, eye, csc, cs_row, tot, cs_x, dt_x, tot_x = _chunk_consts(a_ref[...], dt_ref[...], rev)
        cb = lax.dot_general(cm, bm, NT, preferred_element_type=F32)
        cbt = lax.dot_general(bm, cm, NT, preferred_element_type=F32)
        onehot = lax.broadcasted_iota(jnp.int32, (1, HEADS_PER_GROUP), 1)
        hin = hin_ref[0, 0]
        g = g_scr[...]
        g_b = g.astype(BF16)
        din, dte = jnp.exp(cs_x), jnp.exp(tot_x - cs_x)
        xdt = xs * dt_x
        xdt_b, dy_b = xdt.astype(BF16), dyv.astype(BF16)
        dyo_b = (dyv * din).astype(BF16)
        yo = lax.dot_general(cm, hin, NT, preferred_element_type=F32) * din
        dxdt_s = lax.dot_general(bm, g_b, NT, preferred_element_type=F32) * dte
        d_c = jnp.dot(dyo_b, hin, preferred_element_type=F32)
        d_b = jnp.dot((xdt * dte).astype(BF16), g_b, preferred_element_type=F32)
        dhin = lax.dot_general(dyo_b, cm, TN, preferred_element_type=F32)
        dcb = jnp.zeros((CHUNK, CHUNK), F32)
        dtot = jnp.zeros((1, HEADS_PER_GROUP), F32)
        yds, dxds = [], []
        for j in range(HEADS_PER_GROUP):
            lo = HEAD_DIM * j
            cd = jnp.exp(tot[j:j + 1, 0:1])
            gj = g[lo:lo + HEAD_DIM, :]
            dcd = jnp.sum(jnp.sum(gj * _f(hin[lo:lo + HEAD_DIM, :]), axis=1, keepdims=True), axis=0, keepdims=True)
            dtot = dtot + (dcd * cd) * (onehot == j).astype(F32)
            g_scr[lo:lo + HEAD_DIM, :] = dhin[lo:lo + HEAD_DIM, :] + gj * cd
            dd = csc[j] - cs_row[j:j + 1, :]
            lm = jnp.exp(jnp.where(w, dd, -jnp.inf))
            lmt = jnp.exp(jnp.where(wt, -dd, -jnp.inf))
            xj_b, dyj_b = xdt_b[:, lo:lo + HEAD_DIM], dy_b[:, lo:lo + HEAD_DIM]
            yds.append(jnp.dot(_hi_lo(cb * lm), jnp.concatenate([xj_b, xj_b], axis=0), preferred_element_type=F32))
            dxds.append(jnp.dot(_hi_lo(cbt * lmt), jnp.concatenate([dyj_b, dyj_b], axis=0), preferred_element_type=F32))
            dcb = dcb + lax.dot_general(dyj_b, xj_b, NT, preferred_element_type=F32) * lm
        dxdt_d = jnp.concatenate(dxds, axis=1)
        z_state = xdt * dxdt_s
        yd = jnp.concatenate(yds, axis=1)
        dcs = _head_sums(dyv * yo + _f(dy_b) * yd - _f(xdt_b) * dxdt_d - z_state)
        dtot = dtot + _head_sums(jnp.broadcast_to(jnp.sum(z_state, axis=0, keepdims=True), (8, GROUP_X)))[0:1, :]
        dxdt = dxdt_s + dxdt_d
        ddt_c = _head_sums(dxdt * xs, terms=2)
        dx = dxdt * dt_x
        if skip:
            dx = dx + dsk_ref[...] * dyv
        dcb_b = dcb.astype(BF16)
        d_c = d_c + jnp.dot(dcb_b, bm, preferred_element_type=F32)
        d_b = d_b + lax.dot_general(dcb_b, cm, TN, preferred_element_type=F32)
        da_c = jnp.dot(wt.astype(F32), dcs, precision=HIGHEST, preferred_element_type=F32) + dtot
        ddt_ref[...] = lax.dot_general(ddt_c, eye, TN, precision=HIGHEST, preferred_element_type=F32)
        da_ref[...] = lax.dot_general(da_c, eye, TN, precision=HIGHEST, preferred_element_type=F32)
        dx_ref[...] = jnp.concatenate([dx, d_b, d_c], axis=1).astype(dx_ref.dtype)
        if skip:
            ddsk_ref[...] += jnp.sum(dyv * xs, axis=0, keepdims=True)

    def body(xf, dtf, af, dyf, hf, xr, dtr, ar, dyr, hr, dsk_ref, dxf, ddtf, daf, dxr, ddtr, dar, ddsk_ref, gf_scr, gr_scr):
        @pl.when(pl.program_id(1) == 0)
        def _():
            gf_scr[...] = jnp.zeros_like(gf_scr)
            gr_scr[...] = jnp.zeros_like(gr_scr)
            ddsk_ref[...] = jnp.zeros_like(ddsk_ref)

        one(xf, dtf, af, dyf, hf, dsk_ref, dxf, ddtf, daf, ddsk_ref, gf_scr, False)
        one(xr, dtr, ar, dyr, hr, None, dxr, ddtr, dar, None, gr_scr, True)

    def specs(rev):
        def z(zi):
            return zi if rev else nc - 1 - zi
        t_in = pl.BlockSpec((HEADS_PER_GROUP, CHUNK), lambda g, zi: ((n_groups if rev else 0) + g, z(zi)))
        t_out = pl.BlockSpec((HEADS_PER_GROUP, CHUNK), lambda g, zi: (g, z(zi)))
        x_spec = pl.BlockSpec((CHUNK, GROUP_W), lambda g, zi: (z(zi), g))
        ins = [x_spec, t_in, t_in, pl.BlockSpec((CHUNK, GROUP_X), lambda g, zi: (z(zi), g)),
               pl.BlockSpec((1, 1, GROUP_X, D_STATE), lambda g, zi: (g, z(zi), 0, 0))]
        return ins, [x_spec, t_out, t_out]

    (in_f, out_f), (in_r, out_r) = specs(False), specs(True)
    lane_spec = pl.BlockSpec((1, GROUP_X), lambda g, zi: (0, g))
    shapes = [jax.ShapeDtypeStruct((s, n_groups * GROUP_W), BF16), jax.ShapeDtypeStruct((n_heads, s), F32),
              jax.ShapeDtypeStruct((n_heads, s), F32)]
    return pl.pallas_call(
        body, name=name, grid=(n_groups, nc), in_specs=in_f + in_r + [lane_spec], out_specs=out_f + out_r + [lane_spec],
        out_shape=shapes + shapes + [jax.ShapeDtypeStruct((1, n_groups * GROUP_X), F32)],
        scratch_shapes=[pltpu.VMEM((GROUP_X, D_STATE), F32)] * 2, compiler_params=_params(2),
    )(xbc, dt_t, a_t, dy, hin_f, xbc, dt_t, a_t, dy, hin_r, dsk)


def _mesh_pos():
    return lax.axis_index("x"), lax.axis_index("y"), lax.axis_index("c")


def _slot(p):
    return 4 * p[0] + 2 * p[1] + p[2]


def _exchange_sems(n):
    return [pltpu.SemaphoreType.DMA((7 * n,)), pltpu.SemaphoreType.DMA((7 * n,)), pltpu.SemaphoreType.DMA((n,))]


def _gather_ops(ins, outs, send, recv, loc):
    n = len(ins)
    x, y, c = _mesh_pos()
    me, sib = (x, y, c), (x, y, 1 - c)
    chips = [(1 - x, y), (x, 1 - y), (1 - x, 1 - y)]

    def cp(a, k, block, to, src=None):
        dst = outs[a].at[_slot(block)]
        return pltpu.make_async_remote_copy(
            src_ref=dst if src is None else src, dst_ref=dst, send_sem=send.at[7 * a + k], recv_sem=recv.at[7 * a + k],
            device_id=to, device_id_type=MESH)

    mine = [pltpu.make_async_copy(ins[a], outs[a].at[_slot(me)], loc.at[a]) for a in range(n)]
    first = []
    for a in range(n):
        first.append(cp(a, 0, me, sib, src=ins[a]))
        first += [cp(a, 1 + j, me, (*chip, c), src=ins[a]) for j, chip in enumerate(chips)]

    def start():
        for d in mine + first:
            d.start()

    def finish():
        passed = []
        for a in range(n):
            for j, chip in enumerate(chips):
                cp(a, 1 + j, (*chip, c), me).wait_recv()
                p = cp(a, 4 + j, (*chip, c), sib)
                p.start()
                passed.append(p)
        for a in range(n):
            cp(a, 0, sib, me).wait_recv()
            for j, chip in enumerate(chips):
                cp(a, 4 + j, (*chip, 1 - c), me).wait_recv()
        for f in first + passed:
            f.wait_send()
        for m in mine:
            m.wait()

    return start, finish


_gather_ops.out_shapes = lambda arrs: [jax.ShapeDtypeStruct((N_DEV,) + a.shape, a.dtype) for a in arrs]


def _a2a_ops(ins, outs, send, recv, loc):
    n = len(ins)
    me = _mesh_pos()
    flips = [(fx, fy, fc) for fx in (0, 1) for fy in (0, 1) for fc in (0, 1)][1:]
    peers = [tuple(1 - p if fl else p for p, fl in zip(me, fl3)) for fl3 in flips]

    def cp(a, k, peer, dst_slot):
        return pltpu.make_async_remote_copy(
            src_ref=ins[a].at[_slot(peer)], dst_ref=outs[a].at[dst_slot], send_sem=send.at[7 * a + k],
            recv_sem=recv.at[7 * a + k], device_id=peer, device_id_type=MESH)

    mine = [pltpu.make_async_copy(ins[a].at[_slot(me)], outs[a].at[_slot(me)], loc.at[a]) for a in range(n)]
    sends = [cp(a, k, peer, _slot(me)) for a in range(n) for k, peer in enumerate(peers)]

    def start():
        for d in mine + sends:
            d.start()

    def finish():
        for a in range(n):
            for k, peer in enumerate(peers):
                cp(a, k, peer, _slot(peer)).wait_recv()
        for s_ in sends:
            s_.wait_send()
        for m in mine:
            m.wait()

    return start, finish


_a2a_ops.out_shapes = lambda arrs: [jax.ShapeDtypeStruct(a.shape, a.dtype) for a in arrs]


def _exchange(name, ops, arrs):
    n = len(arrs)

    def body(*refs):
        start, finish = ops(refs[:n], refs[n:2 * n], *refs[2 * n:])
        start()
        finish()

    any_spec = pl.BlockSpec(memory_space=pl.ANY)
    return pl.pallas_call(
        body, name=name, in_specs=[any_spec] * n, out_specs=[any_spec] * n, out_shape=ops.out_shapes(arrs),
        scratch_shapes=_exchange_sems(n),
    )(*arrs)


def _all_gather(name, arrs):
    return _exchange(name, _gather_ops, arrs)


def _all_to_all(name, arrs):
    return _exchange(name, _a2a_ops, arrs)


def _sum8(parts_ref):
    g = _f(parts_ref[0])
    for j in range(1, N_DEV):
        g = g + _f(parts_ref[j])
    return g


def _adam_math(w, g, m, v):
    m = ADAM_B1 * m + (1.0 - ADAM_B1) * g
    v = ADAM_B2 * v + (1.0 - ADAM_B2) * (g * g)
    m_hat = m / (1.0 - ADAM_B1 ** ADAM_STEP)
    v_hat = v / (1.0 - ADAM_B2 ** ADAM_STEP)
    return -ADAM_LR * (m_hat / (jnp.sqrt(v_hat) + ADAM_EPS) + ADAM_WD * w), m, v


def _reduce_sum(name, parts):
    _, r, c = parts.shape

    def body(p_ref, o_ref):
        o_ref[...] = _sum8(p_ref)

    return pl.pallas_call(body, name=name, out_shape=jax.ShapeDtypeStruct((r, c), F32))(parts)


def _adam(name, w, g, m, v, parts=None):
    r, c = w.shape
    tr = _tile(r, (128, 64, 32, 16)) if r * c > 2 ** 18 else r
    spec = pl.BlockSpec((tr, c), lambda i: (i, 0))
    summed = parts is not None

    def body(*refs):
        if summed:
            p_ref, w_ref, m_ref, v_ref, g_out, d_out, m_out, v_out = refs
            gv = _sum8(p_ref)
            g_out[...] = gv
        else:
            g_ref, w_ref, m_ref, v_ref, d_out, m_out, v_out = refs
            gv = g_ref[...]
        d, mn, vn = _adam_math(w_ref[...], gv, m_ref[...], v_ref[...])
        d_out[...] = d
        m_out[...] = mn
        v_out[...] = vn

    first = [pl.BlockSpec((N_DEV, tr, c), lambda i: (0, i, 0))] if summed else [spec]
    n_out = 4 if summed else 3
    return pl.pallas_call(
        body, name=name, grid=(r // tr,), in_specs=first + [spec] * 3, out_specs=[spec] * n_out,
        out_shape=[jax.ShapeDtypeStruct((r, c), F32)] * n_out, compiler_params=_params(1),
    )(parts if summed else g, w, m, v)


def _xbc_runs(d_ssm, n_groups):
    runs = []
    for g in range(n_groups):
        runs += [(GROUP_X * g, GROUP_X), (d_ssm + D_STATE * g, D_STATE), (d_ssm + n_groups * D_STATE + D_STATE * g, D_STATE)]
    return runs


def _in_runs(d, d_ssm, n_groups, n_heads):
    d_xbc = d_ssm + 2 * n_groups * D_STATE
    o_dt = d_ssm + d_xbc
    o_glu = o_dt + 2 * n_heads
    runs = [(0, d_ssm)] + [(d_ssm + s, l) for s, l in _xbc_runs(d_ssm, n_groups)]
    return runs + [(o_glu, 4 * d), (o_dt, 2 * n_heads)]


def _permute(a, runs):
    return jnp.concatenate([a[..., s:s + l] for s, l in runs], axis=-1)


def _unpermute(a, runs):
    offs = np.concatenate([[0], np.cumsum([l for _, l in runs])])
    order = sorted(range(len(runs)), key=lambda k: runs[k][0])
    return jnp.concatenate([a[..., int(offs[k]):int(offs[k]) + runs[k][1]] for k in order], axis=-1)


def _cols_from_shards(g):
    return g.transpose(1, 0, 2).reshape(g.shape[1], -1)


def _cols_to_shards(a):
    return a.reshape(a.shape[0], N_DEV, -1).transpose(1, 0, 2)


def _pad_rows(a, rows):
    return jnp.concatenate([a, jnp.zeros((rows - a.shape[0], a.shape[1]), a.dtype)], axis=0)


def _pack(vecs):
    a = jnp.concatenate(vecs, axis=1)
    return jnp.concatenate([a, jnp.zeros((1, -a.shape[1] % 1024), a.dtype)], axis=1).reshape(-1, 128)


def kernel(x, c, w_ada, b_ada, g_pre_mix, g_post_mix, w_in, w_conv_ssm, b_conv_ssm, dt_bias_fwd, dt_bias_bwd, a_log_fwd, a_log_bwd, d_skip, g_ssm_norm, w_ssm_out, b_glu, w_dw, b_dw, ln_g, ln_b, w_conv_out, b_conv_out, b_gate, w_mix_out, g_pre_ffn, g_post_ffn, w_gate_up, w_down, loss_target, m_w_ada, m_b_ada, m_g_pre_mix, m_g_post_mix, m_w_in, m_w_conv_ssm, m_b_conv_ssm, m_dt_bias_fwd, m_dt_bias_bwd, m_a_log_fwd, m_a_log_bwd, m_d_skip, m_g_ssm_norm, m_w_ssm_out, m_b_glu, m_w_dw, m_b_dw, m_ln_g, m_ln_b, m_w_conv_out, m_b_conv_out, m_b_gate, m_w_mix_out, m_g_pre_ffn, m_g_post_ffn, m_w_gate_up, m_w_down, v_w_ada, v_b_ada, v_g_pre_mix, v_g_post_mix, v_w_in, v_w_conv_ssm, v_b_conv_ssm, v_dt_bias_fwd, v_dt_bias_bwd, v_a_log_fwd, v_a_log_bwd, v_d_skip, v_g_ssm_norm, v_w_ssm_out, v_b_glu, v_w_dw, v_b_dw, v_ln_g, v_ln_b, v_w_conv_out, v_b_conv_out, v_b_gate, v_w_mix_out, v_g_pre_ffn, v_g_post_ffn, v_w_gate_up, v_w_down):
    weights = dict(w_ada=w_ada, b_ada=b_ada, g_pre_mix=g_pre_mix, g_post_mix=g_post_mix, w_in=w_in, w_conv_ssm=w_conv_ssm, b_conv_ssm=b_conv_ssm, dt_bias_fwd=dt_bias_fwd, dt_bias_bwd=dt_bias_bwd, a_log_fwd=a_log_fwd, a_log_bwd=a_log_bwd, d_skip=d_skip, g_ssm_norm=g_ssm_norm, w_ssm_out=w_ssm_out, b_glu=b_glu, w_dw=w_dw, b_dw=b_dw, ln_g=ln_g, ln_b=ln_b, w_conv_out=w_conv_out, b_conv_out=b_conv_out, b_gate=b_gate, w_mix_out=w_mix_out, g_pre_ffn=g_pre_ffn, g_post_ffn=g_post_ffn, w_gate_up=w_gate_up, w_down=w_down)
    mom_m = dict(w_ada=m_w_ada, b_ada=m_b_ada, g_pre_mix=m_g_pre_mix, g_post_mix=m_g_post_mix, w_in=m_w_in, w_conv_ssm=m_w_conv_ssm, b_conv_ssm=m_b_conv_ssm, dt_bias_fwd=m_dt_bias_fwd, dt_bias_bwd=m_dt_bias_bwd, a_log_fwd=m_a_log_fwd, a_log_bwd=m_a_log_bwd, d_skip=m_d_skip, g_ssm_norm=m_g_ssm_norm, w_ssm_out=m_w_ssm_out, b_glu=m_b_glu, w_dw=m_w_dw, b_dw=m_b_dw, ln_g=m_ln_g, ln_b=m_ln_b, w_conv_out=m_w_conv_out, b_conv_out=m_b_conv_out, b_gate=m_b_gate, w_mix_out=m_w_mix_out, g_pre_ffn=m_g_pre_ffn, g_post_ffn=m_g_post_ffn, w_gate_up=m_w_gate_up, w_down=m_w_down)
    mom_v = dict(w_ada=v_w_ada, b_ada=v_b_ada, g_pre_mix=v_g_pre_mix, g_post_mix=v_g_post_mix, w_in=v_w_in, w_conv_ssm=v_w_conv_ssm, b_conv_ssm=v_b_conv_ssm, dt_bias_fwd=v_dt_bias_fwd, dt_bias_bwd=v_dt_bias_bwd, a_log_fwd=v_a_log_fwd, a_log_bwd=v_a_log_bwd, d_skip=v_d_skip, g_ssm_norm=v_g_ssm_norm, w_ssm_out=v_w_ssm_out, b_glu=v_b_glu, w_dw=v_w_dw, b_dw=v_b_dw, ln_g=v_ln_g, ln_b=v_ln_b, w_conv_out=v_w_conv_out, b_conv_out=v_b_conv_out, b_gate=v_b_gate, w_mix_out=v_w_mix_out, g_pre_ffn=v_g_pre_ffn, g_post_ffn=v_g_post_ffn, w_gate_up=v_w_gate_up, w_down=v_w_down)
    names = list(weights)

    s, d = x.shape[1], x.shape[2]
    d_ssm, n_heads, d_xbc = g_ssm_norm.shape[-1], d_skip.shape[-1], b_conv_ssm.shape[-1]
    n_groups = n_heads // HEADS_PER_GROUP
    d_ff = w_down.shape[1] * N_DEV
    k_ssm, k_dw = w_conv_ssm.shape[1], w_dw.shape[1]
    assert d_ssm == n_groups * GROUP_X and d_xbc == n_groups * GROUP_W and s % CHUNK == 0
    assert d % COL_BLOCK == 0 and d_ff % COL_BLOCK == 0
    mc = w_ada.shape[-1]
    me = _slot(_mesh_pos())
    in_runs, xbc_runs = _in_runs(d, d_ssm, n_groups, n_heads), _xbc_runs(d_ssm, n_groups)
    wb = d_ssm + d_xbc + 4 * d
    p_x, p_ga, p_gb, p_la, p_lb = d_ssm, d_ssm + d_xbc, d_ssm + d_xbc + d, d_ssm + d_xbc + 2 * d, d_ssm + d_xbc + 3 * d
    ncb, nfb = d // COL_BLOCK, d_ff // COL_BLOCK
    x2, tgt = x[0], loss_target[0]

    big = ["w_in", "w_ssm_out", "w_conv_out", "w_mix_out", "w_gate_up", "w_down"]
    (w_in_g,) = _all_gather("gather_w_in", [w_in[0].astype(BF16)])
    w_in_p = _permute(_cols_from_shards(w_in_g), in_runs)
    w_big, w_dt = w_in_p[:, :wb], w_in_p[:, wb:]

    n_cs, n_dw = k_ssm * (d_xbc // N_DEV), k_dw * (d // N_DEV)
    (small_all,) = _all_gather("gather_small", [_pack([c, w_conv_ssm.reshape(1, n_cs), w_dw.reshape(1, n_dw)])])
    small_all = small_all.reshape(N_DEV, -1)
    c_all = small_all[:, :d]
    wcs = _cols_from_shards(small_all[:, d:d + n_cs].reshape(N_DEV, k_ssm, -1))
    wdw = _cols_from_shards(small_all[:, d + n_cs:d + n_cs + n_dw].reshape(N_DEV, k_dw, -1))
    wcs_p = _permute(wcs, xbc_runs)
    bcs_p = _permute(b_conv_ssm, xbc_runs)
    kp_ssm, kp_dw = -(-k_ssm // 8) * 8, -(-k_dw // 8) * 8
    zero_x, zero_d = jnp.zeros((1, d_xbc), F32), jnp.zeros((1, d), F32)

    (c_act,), _ = _rowcall("c_act", _fwd_fn(_silu), [(c_all, 0, d)], [], [(d, F32)], [])
    b_my = lax.dynamic_slice(b_ada, (0, me * mc), (1, mc))
    mod_cols = _matmul("mod", c_act, w_ada[0], "nn", F32, acc_in=jnp.broadcast_to(b_my, (N_DEV, mc)))
    (mod_rows,) = _all_to_all("mod_to_owner", [mod_cols.reshape(N_DEV, 1, mc)])
    mod = mod_rows.reshape(1, N_DEV * mc)
    sh1, sc1, g1, sh2, sc2, g2 = [mod[:, k * d:(k + 1) * d] for k in range(6)]

    (h1,), _ = _rowcall("prenorm_mix", _fwd_fn(_t_prenorm), [(x2, 0, d)], [g_pre_mix, sc1, sh1], [(d, BF16)], [])
    proj, rest = _matmul("in_proj", h1, w_big, "nn", BF16, carry=(_gather_ops, [weights[k][0].astype(BF16) for k in big[1:]]))
    gath = dict(zip(big[1:], rest))
    w_ssm, w_co, w_mo = gath["w_ssm_out"].reshape(d_ssm, d), gath["w_conv_out"].reshape(d, d), gath["w_mix_out"].reshape(d, d)
    w_gu, w_dn = _cols_from_shards(gath["w_gate_up"]), gath["w_down"].reshape(d_ff, d)
    dt_raw = _matmul("in_proj_dt", h1, w_dt, "nn", F32)
    xbc_c = _dwconv("conv_ssm", proj, p_x, d_xbc, _pad_rows(wcs_p, kp_ssm), k_ssm, bcs_p, True, BF16)
    alog_col = jnp.concatenate([a_log_fwd, a_log_bwd], axis=1).reshape(2 * n_heads, 1)
    dt_t, a_t = _dtprep("dt_prep", dt_raw, jnp.concatenate([dt_bias_fwd, dt_bias_bwd], axis=1), alog_col)
    dsk = jnp.repeat(d_skip, HEAD_DIM, axis=1)
    y_f, hin_f, y_b, hin_b = _ssd_fwd("ssd", xbc_c, dt_t, a_t, dsk, n_groups)
    gn_rows = [(y_f, 0, d_ssm), (y_b, 0, d_ssm), (proj, 0, d_ssm)]
    (yn,), _ = _rowcall("ssm_norm", _fwd_fn(_t_gnorm), gn_rows, [g_ssm_norm], [(d_ssm, BF16)], [], ncol=n_groups)
    ya = _matmul("ssm_out", yn, w_ssm, "nn", BF16)
    glu_rows, glu_prm = [(proj, p_ga, d), (proj, p_gb, d)], [b_glu[:, :d], b_glu[:, d:]]
    (u0,), _ = _rowcall("glu", _fwd_fn(_t_glu), glu_rows, glu_prm, [(d, BF16)], [], ncol=ncb)
    wdw_pad = _pad_rows(wdw, kp_dw)
    u1 = _dwconv("conv_dw", u0, 0, d, wdw_pad, k_dw, b_dw, False, BF16)
    (u2,), _ = _rowcall("ln_silu", _fwd_fn(_t_lnsilu), [(u1, 0, d)], [ln_g, ln_b], [(d, BF16)], [])
    yb = _matmul("conv_out", u2, w_co, "nn", BF16)
    gm_rows, gm_prm = [(ya, 0, d), (yb, 0, d), (proj, p_la, d), (proj, p_lb, d)], [b_conv_out, b_gate[:, :d], b_gate[:, d:]]
    (m_in,), _ = _rowcall("gate_mix", _fwd_fn(_t_gatemix), gm_rows, gm_prm, [(d, BF16)], [], ncol=ncb)
    mix = _matmul("mix_out", m_in, w_mo, "nn", BF16)
    pp_prm = [g_post_mix, g1, g_pre_ffn, sc2, sh2]
    (x1, h2), _ = _rowcall("post_mix", _fwd_fn(_t_postpre), [(x2, 0, d), (mix, 0, d)], pp_prm, [(d, F32), (d, BF16)], [])
    gu = _matmul("gate_up", h2, w_gu, "nn", BF16)
    sw_rows = [(gu, 0, d_ff), (gu, d_ff, d_ff)]
    (act,), _ = _rowcall("swiglu", _fwd_fn(_t_swiglu), sw_rows, [], [(d_ff, BF16)], [], ncol=nfb)
    f = _matmul("down", act, w_dn, "nn", BF16)

    def loss_fn(x1_t, f_t, tgt_t, gpost, g2_t):
        args = [_f(x1_t), _f(f_t), _f(tgt_t), _f(gpost), _f(g2_t)]
        val, vjp = jax.vjp(_t_loss, *args)
        gr = vjp(jnp.ones((), F32))
        return (gr[0], gr[1]), (jnp.zeros((1, 128), F32) + val, gr[3], gr[4])

    (dx1a, df), (loss_v, d_gpost2, d_g2) = _rowcall(
        "loss", loss_fn, [(x1, 0, d), (f, 0, d), (tgt, 0, d)], [g_post_ffn, g2], [(d, F32), (d, BF16)], [128, d, d])
    loss = lax.psum(loss_v[0, 0], ("x", "y", "c"))

    dact = _matmul("down_dx", df, w_dn, "nt", BF16)
    dw_dn = _matmul("down_dw", act, df, "tn", BF16)
    (dgt, dup), _ = _rowcall("swiglu_bwd", _vjp_fn(_t_swiglu, 2, 1, (0, 1), ()), sw_rows + [(dact, 0, d_ff)], [],
                             [(d_ff, BF16), (d_ff, BF16)], [], ncol=nfb)
    dgu = jnp.concatenate([dgt, dup], axis=1)
    recv = {}
    dh2, (recv["w_down"],) = _matmul("gate_up_dx", dgu, w_gu, "nt", BF16, carry=(_a2a_ops, [dw_dn.reshape(N_DEV, -1, d)]))
    dw_gu = _matmul("gate_up_dw", h2, dgu, "tn", BF16)
    (dxa, dmix), (d_gpost1, d_g1, d_gpre2, d_sc2, d_sh2) = _rowcall(
        "post_mix_bwd", _vjp_fn(_t_postpre, 2, 2, (0, 1), (0, 1, 2, 3, 4)),
        [(x2, 0, d), (mix, 0, d), (dx1a, 0, d), (dh2, 0, d)], pp_prm, [(d, F32), (d, BF16)], [d] * 5)
    dm_in = _matmul("mix_out_dx", dmix, w_mo, "nt", BF16)
    dw_mo = _matmul("mix_out_dw", m_in, dmix, "tn", BF16)
    (dya, dyb, dla, dlb), (d_bco, d_bga, d_bgb) = _rowcall(
        "gate_mix_bwd", _vjp_fn(_t_gatemix, 4, 1, (0, 1, 2, 3), (0, 1, 2)), gm_rows + [(dm_in, 0, d)], gm_prm,
        [(d, BF16)] * 4, [d] * 3, ncol=ncb)
    du2 = _matmul("conv_out_dx", dyb, w_co, "nt", BF16)
    dw_co = _matmul("conv_out_dw", u2, dyb, "tn", BF16)
    (du1,), (d_lng, d_lnb) = _rowcall("ln_silu_bwd", _vjp_fn(_t_lnsilu, 1, 1, (0,), (0, 1)), [(u1, 0, d), (du2, 0, d)],
                                      [ln_g, ln_b], [(d, BF16)], [d, d])
    dwdw_p, d_bdw = _dwconv_bwd_w("conv_dw_dw", u0, 0, d, wdw_pad, k_dw, b_dw, [du1], False)
    du0 = _dwconv("conv_dw_dx", du1, 0, d, _pad_rows(wdw[::-1], kp_dw), k_dw, zero_d, False, BF16)
    (dga, dgb), (d_bglu_a, d_bglu_b) = _rowcall("glu_bwd", _vjp_fn(_t_glu, 2, 1, (0, 1), (0, 1)), glu_rows + [(du0, 0, d)],
                                                glu_prm, [(d, BF16)] * 2, [d, d], ncol=ncb)
    dyn, (recv["w_mix_out"], recv["w_conv_out"]) = _matmul(
        "ssm_out_dx", dya, w_ssm, "nt", BF16, carry=(_a2a_ops, [dw_mo.reshape(N_DEV, -1, d), dw_co.reshape(N_DEV, -1, d)]))
    dw_ssm = _matmul("ssm_out_dw", yn, dya, "tn", BF16)
    (dy_ssd, dz), (d_gn,) = _rowcall("ssm_norm_bwd", _vjp_fn(_t_gnorm, 3, 1, (0, 2), (0,)), gn_rows + [(dyn, 0, d_ssm)],
                                     [g_ssm_norm], [(d_ssm, BF16)] * 2, [d_ssm], ncol=n_groups)
    dxbc_f, ddt_f, da_f, dxbc_b, ddt_b, da_b, ddsk = _ssd_bwd("ssd_bwd", xbc_c, dt_t, a_t, dy_ssd, hin_f, hin_b, dsk, n_groups)
    ddt_raw, d_dtb, d_alog = _dtprep_bwd("dt_prep_bwd", ddt_f, ddt_b, da_f, da_b, dt_t, alog_col)
    dpre, dwcs_p, dbcs_p = _dwconv_bwd_w("conv_ssm_dw", proj, p_x, d_xbc, _pad_rows(wcs_p, kp_ssm), k_ssm, bcs_p,
                                          [dxbc_f, dxbc_b], True)
    dxbc = _dwconv("conv_ssm_dx", dpre, 0, d_xbc, _pad_rows(wcs_p[::-1], kp_ssm), k_ssm, zero_x, False, BF16)
    dproj = jnp.concatenate([dz, dxbc, dga, dgb, dla, dlb], axis=1)
    dw_big, (recv["w_gate_up"], recv["w_ssm_out"]) = _matmul(
        "in_proj_dw", h1, dproj, "tn", BF16, carry=(_a2a_ops, [_cols_to_shards(dw_gu), dw_ssm.reshape(N_DEV, -1, d)]))
    dw_dt = _matmul("in_proj_dt_dw", h1, ddt_raw, "tn", BF16)
    dw_in = _unpermute(jnp.concatenate([dw_big, dw_dt], axis=1), in_runs)
    dh1, (recv["w_in"],) = _matmul("in_proj_dx", dproj, w_big, "nt", F32, carry=(_a2a_ops, [_cols_to_shards(dw_in)]))
    dh1 = _matmul("in_proj_dt_dx", ddt_raw, w_dt, "nt", BF16, acc_in=dh1)

    def prenorm_bwd_fn(x_t, dh_t, dxa_t, g, sc, sh):
        rows, prm = _vjp_fn(_t_prenorm, 1, 1, (0,), (0, 1, 2))(x_t, dh_t, g, sc, sh)
        return (rows[0] + _f(dxa_t),), prm

    (grad_x,), (d_gpre1, d_sc1, d_sh1) = _rowcall(
        "prenorm_mix_bwd", prenorm_bwd_fn, [(x2, 0, d), (dh1, 0, d), (dxa, 0, d)], [g_pre_mix, sc1, sh1], [(d, F32)], [d] * 3)

    dmod = jnp.concatenate([d_sh1, d_sc1, d_g1, d_sh2, d_sc2, d_g2], axis=1)
    small_g = {
        "b_ada": dmod, "g_pre_mix": d_gpre1, "g_post_mix": d_gpost1, "b_conv_ssm": _unpermute(dbcs_p, xbc_runs),
        "dt_bias_fwd": d_dtb[:n_heads].reshape(1, n_heads), "dt_bias_bwd": d_dtb[n_heads:].reshape(1, n_heads),
        "a_log_fwd": d_alog[:n_heads].reshape(1, n_heads), "a_log_bwd": d_alog[n_heads:].reshape(1, n_heads),
        "d_skip": ddsk.reshape(n_heads, HEAD_DIM).sum(axis=1).reshape(1, n_heads), "g_ssm_norm": d_gn,
        "b_glu": jnp.concatenate([d_bglu_a, d_bglu_b], axis=1), "b_dw": d_bdw, "ln_g": d_lng, "ln_b": d_lnb,
        "b_conv_out": d_bco, "b_gate": jnp.concatenate([d_bga, d_bgb], axis=1), "g_pre_ffn": d_gpre2, "g_post_ffn": d_gpost2,
        "w_conv_ssm": _unpermute(dwcs_p[:k_ssm], xbc_runs).reshape(1, k_ssm * d_xbc), "w_dw": dwdw_p[:k_dw].reshape(1, k_dw * d),
    }
    small_names = list(small_g)
    sizes = [small_g[k].shape[1] for k in small_names]
    offs = np.concatenate([[0], np.cumsum(sizes)])
    (small_parts,) = _all_gather("gather_small_grads", [_pack([small_g[k] for k in small_names])])
    dmod_all = small_parts.reshape(N_DEV, -1)[:, :6 * d]
    small_tot = _reduce_sum("sum_small_grads", small_parts).reshape(1, -1)
    grads = {k: small_tot[:, int(offs[i]):int(offs[i + 1])] for i, k in enumerate(small_names)}
    grads["w_conv_ssm"] = lax.dynamic_slice(grads["w_conv_ssm"].reshape(k_ssm, d_xbc), (0, me * (d_xbc // N_DEV)), (k_ssm, d_xbc // N_DEV))
    grads["w_dw"] = lax.dynamic_slice(grads["w_dw"].reshape(k_dw, d), (0, me * (d // N_DEV)), (k_dw, d // N_DEV))
    grads["w_ada"] = _matmul("mod_dw", c_act, lax.dynamic_slice(dmod_all, (0, me * mc), (N_DEV, mc)), "tn", F32, highest=True)

    delta, new_m, new_v = {}, {}, {}
    for k in big:
        grads[k], delta[k], new_m[k], new_v[k] = _adam("adam_" + k, weights[k][0], None, mom_m[k][0], mom_v[k][0], parts=recv[k])
    for k in ["w_ada", "w_conv_ssm", "w_dw"]:
        delta[k], new_m[k], new_v[k] = _adam("adam_" + k, weights[k][0], grads[k], mom_m[k][0], mom_v[k][0])
    rep = [k for k in names if k not in big and k not in ("w_ada", "w_conv_ssm", "w_dw")]
    r_sizes = [weights[k].shape[1] for k in rep]
    r_offs = np.concatenate([[0], np.cumsum(r_sizes)])
    packed = [_pack([t[k] for k in rep]) for t in (weights, grads, mom_m, mom_v)]
    r_delta, r_m, r_v = [t.reshape(1, -1) for t in _adam("adam_small", *packed)]
    for i, k in enumerate(rep):
        sl = slice(int(r_offs[i]), int(r_offs[i + 1]))
        delta[k], new_m[k], new_v[k] = r_delta[:, sl], r_m[:, sl], r_v[:, sl]

    def shaped(t, k):
        return t[k].reshape(weights[k].shape)

    return (loss, grad_x.reshape(x.shape), *[shaped(grads, k) for k in names], *[shaped(delta, k) for k in names],
            *[shaped(new_m, k) for k in names], *[shaped(new_v, k) for k in names])
```

```python
import functools

import numpy as np
import jax
import jax.numpy as jnp
from jax import lax
from jax.experimental import pallas as pl
from jax.experimental.pallas import tpu as pltpu

F32 = jnp.float32
BF16 = jnp.bfloat16
HIGHEST = lax.Precision.HIGHEST
MESH = pl.DeviceIdType.MESH
N_DEV = 8
EPS = 1e-6
CHUNK = 128
HEAD_DIM = 64
D_STATE = 128
HEADS_PER_GROUP = 8
GROUP_X = HEADS_PER_GROUP * HEAD_DIM
GROUP_W = GROUP_X + 2 * D_STATE
COL_BLOCK = 512
HALO = 16
VMEM_LIMIT = 56 * 1024 * 1024
ADAM_LR, ADAM_B1, ADAM_B2, ADAM_EPS, ADAM_WD, ADAM_STEP = 0.001, 0.9, 0.999, 1e-08, 0.01, 10

NN = (((1,), (0,)), ((), ()))
NT = (((1,), (1,)), ((), ()))
TN = (((0,), (0,)), ((), ()))


def _tile(dim, prefs):
    for p in prefs:
        if p <= dim and dim % p == 0:
            return p
    return dim


def _params(n_grid):
    return pltpu.CompilerParams(dimension_semantics=("arbitrary",) * n_grid, vmem_limit_bytes=VMEM_LIMIT)


def _f(v):
    return v.astype(F32)


def _matmul(name, a, b, mode, out_dtype, acc_in=None, highest=False, carry=None):
    if mode == "nn":
        (m, k), (k2, n) = a.shape, b.shape
    elif mode == "nt":
        (m, k), (n, k2) = a.shape, b.shape
    else:
        (k, m), (k2, n) = a.shape, b.shape
    assert k == k2, (name, a.shape, b.shape, mode)
    tm = _tile(m, (1024, 512, 256, 128))
    tn = _tile(n, (1024, 512, 256, 128))
    tk = _tile(k, (2048, 1024, 512, 256, 128))
    nk = k // tk
    dims = {"nn": NN, "nt": NT, "tn": TN}[mode]
    a_spec = pl.BlockSpec((tk, tm), lambda i, j, kk: (kk, i)) if mode == "tn" else pl.BlockSpec((tm, tk), lambda i, j, kk: (i, kk))
    b_spec = pl.BlockSpec((tn, tk), lambda i, j, kk: (j, kk)) if mode == "nt" else pl.BlockSpec((tk, tn), lambda i, j, kk: (kk, j))
    o_spec = pl.BlockSpec((tm, tn), lambda i, j, kk: (i, j))
    has_acc = acc_in is not None
    n_in = 3 if has_acc else 2
    ops, c_arrs = carry if carry is not None else (None, [])
    nc_ = len(c_arrs)
    grid = (m // tm, n // tn, nk)

    def body(*refs):
        a_ref, b_ref = refs[0], refs[1]
        c_ref = refs[2] if has_acc else None
        c_ins = refs[n_in:n_in + nc_]
        o_ref = refs[n_in + nc_]
        c_outs = refs[n_in + nc_ + 1:n_in + 2 * nc_ + 1]
        acc = refs[n_in + 2 * nc_ + 1]
        i, j, kk = pl.program_id(0), pl.program_id(1), pl.program_id(2)
        if nc_:
            start, finish = ops(c_ins, c_outs, *refs[n_in + 2 * nc_ + 2:])
            pl.when((i == 0) & (j == 0) & (kk == 0))(start)

        def product(first):
            if highest:
                p = lax.dot_general(_f(a_ref[...]), _f(b_ref[...]), dims, precision=HIGHEST, preferred_element_type=F32)
            else:
                p = lax.dot_general(a_ref[...].astype(BF16), b_ref[...].astype(BF16), dims, preferred_element_type=F32)
            return p + _f(c_ref[...]) if (first and has_acc) else p

        if nk == 1:
            o_ref[...] = product(True).astype(out_dtype)
        else:
            @pl.when(kk == 0)
            def _():
                acc[...] = product(True)

            if nk > 2:
                @pl.when((kk > 0) & (kk < nk - 1))
                def _():
                    acc[...] += product(False)

            @pl.when(kk == nk - 1)
            def _():
                o_ref[...] = (acc[...] + product(False)).astype(out_dtype)

        if nc_:
            pl.when((i == grid[0] - 1) & (j == grid[1] - 1) & (kk == nk - 1))(finish)

    any_spec = pl.BlockSpec(memory_space=pl.ANY)
    c_shapes = ops.out_shapes(c_arrs) if nc_ else []
    res = pl.pallas_call(
        body, name=name, grid=grid,
        in_specs=[a_spec, b_spec] + ([o_spec] if has_acc else []) + [any_spec] * nc_, out_specs=[o_spec] + [any_spec] * nc_,
        out_shape=[jax.ShapeDtypeStruct((m, n), out_dtype)] + c_shapes,
        scratch_shapes=[pltpu.VMEM((tm, tn) if nk > 1 else (8, 128), F32)] + (_exchange_sems(nc_) if nc_ else []),
        compiler_params=_params(3),
    )(*([a, b] + ([acc_in] if has_acc else []) + list(c_arrs)))
    return (res[0], res[1:]) if nc_ else res[0]


def _rowcall(name, fn, rows, params, out_rows, out_accs, ncol=1):
    s = rows[0][0].shape[0]
    tm = _tile(s, (256 if ncol == 1 else 1024, 128, 64, 32, 16, 8))
    args, in_specs = [], []
    for arr, off, w in rows:
        bw = w // ncol
        assert w % ncol == 0 and off % bw == 0, (name, off, w, ncol)
        in_specs.append(pl.BlockSpec((tm, bw), functools.partial(lambda j, i, ob: (i, ob + j), ob=off // bw)))
        args.append(arr)
    for p in params:
        assert p.shape[1] % ncol == 0, (name, p.shape)
        in_specs.append(pl.BlockSpec((p.shape[0], p.shape[1] // ncol), lambda j, i: (0, j)))
        args.append(p)
    n_in, n_ro = len(args), len(out_rows)
    out_shape, out_specs, aliases = [], [], {}
    for k, spec in enumerate(out_rows):
        w, dt = spec[0], spec[1]
        bw = w // ncol
        if len(spec) == 4:
            buf, off = spec[2], spec[3]
            assert off % bw == 0 and buf.dtype == dt, (name, off, bw)
            out_shape.append(jax.ShapeDtypeStruct(buf.shape, dt))
            out_specs.append(pl.BlockSpec((tm, bw), functools.partial(lambda j, i, ob: (i, ob + j), ob=off // bw)))
            aliases[len(args)] = k
            in_specs.append(pl.BlockSpec(memory_space=pl.ANY))
            args.append(buf)
        else:
            out_shape.append(jax.ShapeDtypeStruct((s, w), dt))
            out_specs.append(pl.BlockSpec((tm, bw), lambda j, i: (i, j)))
    for w in out_accs:
        out_shape.append(jax.ShapeDtypeStruct((1, w), F32))
        out_specs.append(pl.BlockSpec((1, w // ncol), lambda j, i: (0, j)))
    n_args = len(args)

    def body(*refs):
        i = pl.program_id(1)
        ro, ao = fn(*[r[...] for r in refs[:n_in]])
        outs = refs[n_args:]
        assert len(ro) == n_ro and len(ao) == len(out_accs), name
        for ref, v in zip(outs[:n_ro], ro):
            ref[...] = v.astype(ref.dtype)
        for ref, v in zip(outs[n_ro:], ao):
            @pl.when(i == 0)
            def _(ref=ref, v=v):
                ref[...] = v

            @pl.when(i > 0)
            def _(ref=ref, v=v):
                ref[...] += v

    res = pl.pallas_call(
        body, name=name, grid=(ncol, s // tm), in_specs=in_specs, out_specs=out_specs,
        out_shape=out_shape, input_output_aliases=aliases, compiler_params=_params(2),
    )(*args)
    return res[:n_ro], res[n_ro:]


def _fwd_fn(f):
    def fn(*vals):
        out = f(*[_f(v) for v in vals])
        return (out if isinstance(out, tuple) else (out,)), ()
    return fn


def _vjp_fn(f, n_prim, n_ct, want_rows, want_params):
    def fn(*vals):
        prims = [_f(v) for v in vals[:n_prim]]
        cts = [_f(v) for v in vals[n_prim:n_prim + n_ct]]
        prms = [_f(v) for v in vals[n_prim + n_ct:]]
        out, vjp = jax.vjp(f, *prims, *prms)
        g = vjp(tuple(cts) if isinstance(out, tuple) else cts[0])
        return tuple(g[k] for k in want_rows), tuple(g[n_prim + k] for k in want_params)
    return fn


def _rms(v):
    return v * lax.rsqrt(jnp.mean(v * v, axis=-1, keepdims=True) + EPS)


def _silu(v):
    return v * jax.nn.sigmoid(v)


def _t_prenorm(x, g, sc, sh):
    return _rms(x) * g * (1.0 + sc) + sh


def _t_gnorm(yf, yb, z, gn):
    return _rms((yf + yb) * _silu(z)) * gn


def _t_glu(a, b, ba, bb):
    return (a + ba) * jax.nn.sigmoid(b + bb)


def _t_lnsilu(u, g, b):
    uc = u - jnp.mean(u, axis=-1, keepdims=True)
    return _silu(uc * lax.rsqrt(jnp.mean(uc * uc, axis=-1, keepdims=True) + EPS) * g + b)


def _t_gatemix(ya, yb, la, lb, bco, bga, bgb):
    return jax.nn.sigmoid(la + bga) * ya + jax.nn.sigmoid(lb + bgb) * (yb + bco)


def _t_postpre(x, mix, gpost, g1, gpre, sc, sh):
    x1 = x + g1 * (_rms(mix) * gpost)
    return x1, _rms(x1) * gpre * (1.0 + sc) + sh


def _t_swiglu(gt, up):
    return _silu(gt) * up


def _halves(v):
    return _f(v[:, :COL_BLOCK]), _f(v[:, COL_BLOCK:])


def _glu_fwd(ab, bias):
    return (_t_glu(*_halves(ab), *_halves(bias)),), ()


def _glu_bwd(ab, du, bias):
    _, vjp = jax.vjp(_t_glu, *_halves(ab), *_halves(bias))
    da, db, dba, dbb = vjp(_f(du))
    return (jnp.concatenate([da, db], axis=1),), (jnp.concatenate([dba, dbb], axis=1),)


def _gatemix_fwd(ya, yb, logits, bco, bg):
    return (_t_gatemix(_f(ya), _f(yb), *_halves(logits), _f(bco), *_halves(bg)),), ()


def _gatemix_bwd(ya, yb, logits, dm, bco, bg):
    _, vjp = jax.vjp(_t_gatemix, _f(ya), _f(yb), *_halves(logits), _f(bco), *_halves(bg))
    dya, dyb, dla, dlb, dbco, dbga, dbgb = vjp(_f(dm))
    return (dya, dyb, jnp.concatenate([dla, dlb], axis=1)), (dbco, jnp.concatenate([dbga, dbgb], axis=1))


def _swiglu_fwd(gu):
    return (_t_swiglu(*_halves(gu)),), ()


def _swiglu_bwd(gu, dact):
    _, vjp = jax.vjp(_t_swiglu, *_halves(gu))
    return (jnp.concatenate(vjp(_f(dact)), axis=1),), ()


def _t_loss(x1, f, tgt, gpost, g2):
    e = x1 + g2 * (_rms(f) * gpost) - tgt
    return 0.5 * jnp.sum(jnp.mean(e * e, axis=-1))


def _conv_specs(s, tm, cw, ob, w_rows):
    nh, last_h = tm // HALO, s // HALO - 1
    return [
        pl.BlockSpec((tm, cw), lambda j, i: (i, ob + j)),
        pl.BlockSpec((HALO, cw), lambda j, i: (jnp.maximum(i * nh - 1, 0), ob + j)),
        pl.BlockSpec((HALO, cw), lambda j, i: (jnp.minimum((i + 1) * nh, last_h), ob + j)),
        pl.BlockSpec((w_rows, cw), lambda j, i: (0, j)),
        pl.BlockSpec((1, cw), lambda j, i: (0, j)),
    ]


def _window(x_ref, p_ref, n_ref, i, nrow):
    prev = _f(p_ref[...]) * jnp.where(i > 0, 1.0, 0.0).astype(F32)
    nxt = _f(n_ref[...]) * jnp.where(i < nrow - 1, 1.0, 0.0).astype(F32)
    return jnp.concatenate([prev, _f(x_ref[...]), nxt], axis=0)


def _taps(win, w_ref, k_taps, tm):
    pad = (k_taps - 1) // 2
    acc = w_ref[0:1, :] * win[HALO - pad:HALO - pad + tm, :]
    for k in range(1, k_taps):
        acc = acc + w_ref[k:k + 1, :] * win[HALO + k - pad:HALO + k - pad + tm, :]
    return acc


def _conv_rows(k_taps):
    return 1024 if k_taps <= 8 else 512


def _dwconv(name, x, xoff, c, w_pad, k_taps, b, act, out_dtype, cw=256, into=None):
    s = x.shape[0]
    tm, cw = _tile(s, (_conv_rows(k_taps), 256, 128, 64, 32, 16)), _tile(c, (cw, 128))
    assert xoff % cw == 0 and tm % HALO == 0 and (k_taps - 1) // 2 <= HALO
    nrow = s // tm

    def body(x_ref, p_ref, n_ref, w_ref, b_ref, *rest):
        win = _window(x_ref, p_ref, n_ref, pl.program_id(1), nrow)
        acc = _taps(win, w_ref, k_taps, tm) + b_ref[...]
        rest[-1][...] = (_silu(acc) if act else acc).astype(out_dtype)

    in_specs, args = _conv_specs(s, tm, cw, xoff // cw, w_pad.shape[0]), [x, x, x, w_pad, b]
    if into is None:
        ob, out_shape, aliases = 0, jax.ShapeDtypeStruct((s, c), out_dtype), {}
    else:
        buf, first = into
        assert first % cw == 0 and buf.dtype == out_dtype
        ob, out_shape, aliases = first // cw, jax.ShapeDtypeStruct(buf.shape, out_dtype), {len(args): 0}
        in_specs, args = in_specs + [pl.BlockSpec(memory_space=pl.ANY)], args + [buf]
    return pl.pallas_call(
        body, name=name, grid=(c // cw, nrow), in_specs=in_specs,
        out_specs=pl.BlockSpec((tm, cw), lambda j, i: (i, ob + j)), out_shape=out_shape,
        input_output_aliases=aliases, compiler_params=_params(2),
    )(*args)


def _dwconv_bwd_w(name, x, xoff, c, w_pad, k_taps, b, douts, act, cw=256):
    s = x.shape[0]
    tm, cw = _tile(s, (_conv_rows(k_taps), 256, 128, 64, 32, 16)), _tile(c, (cw, 128))
    assert xoff % cw == 0 and tm % HALO == 0
    nrow, pad, n_d = s // tm, (k_taps - 1) // 2, len(douts)

    def body(x_ref, p_ref, n_ref, w_ref, b_ref, *rest):
        d_refs, outs = rest[:n_d], rest[n_d:]
        i = pl.program_id(1)
        win = _window(x_ref, p_ref, n_ref, i, nrow)
        d = _f(d_refs[0][...])
        for r in d_refs[1:]:
            d = d + _f(r[...])
        if act:
            pre = _taps(win, w_ref, k_taps, tm) + b_ref[...]
            sg = jax.nn.sigmoid(pre)
            d = d * (sg * (1.0 + pre * (1.0 - sg)))
            outs[0][...] = d.astype(outs[0].dtype)
        dw_ref, db_ref = outs[-2], outs[-1]

        @pl.when(i == 0)
        def _():
            dw_ref[...] = jnp.zeros_like(dw_ref)
            db_ref[...] = jnp.zeros_like(db_ref)

        for k in range(k_taps):
            dw_ref[k:k + 1, :] += jnp.sum(d * win[HALO + k - pad:HALO + k - pad + tm, :], axis=0, keepdims=True)
        db_ref[...] += jnp.sum(d, axis=0, keepdims=True)

    row_spec = pl.BlockSpec((tm, cw), lambda j, i: (i, j))
    out_shape = [jax.ShapeDtypeStruct((w_pad.shape[0], c), F32), jax.ShapeDtypeStruct((1, c), F32)]
    out_specs = [pl.BlockSpec((w_pad.shape[0], cw), lambda j, i: (0, j)), pl.BlockSpec((1, cw), lambda j, i: (0, j))]
    if act:
        out_shape, out_specs = [jax.ShapeDtypeStruct((s, c), BF16)] + out_shape, [row_spec] + out_specs
    return pl.pallas_call(
        body, name=name, grid=(c // cw, nrow),
        in_specs=_conv_specs(s, tm, cw, xoff // cw, w_pad.shape[0]) + [row_spec] * n_d,
        out_specs=out_specs, out_shape=out_shape, compiler_params=_params(2),
    )(x, x, x, w_pad, b, *douts)


def _dtprep(name, dt_raw, bias, alog_col):
    s, h2 = dt_raw.shape
    ts = _tile(s, (512, 256, 128))

    def body(r_ref, b_ref, al_ref, dt_ref, a_ref):
        v = r_ref[...] + b_ref[...]
        dt = (jnp.maximum(v, 0.0) + jnp.log(1.0 + jnp.exp(-jnp.abs(v)))).T
        dt_ref[...] = dt
        a_ref[...] = dt * (-jnp.exp(al_ref[...]))

    t_spec = pl.BlockSpec((h2, ts), lambda i: (0, i))
    return pl.pallas_call(
        body, name=name, grid=(s // ts,),
        in_specs=[pl.BlockSpec((ts, h2), lambda i: (i, 0)), pl.BlockSpec((1, h2), lambda i: (0, 0)), pl.BlockSpec((h2, 1), lambda i: (0, 0))],
        out_specs=[t_spec, t_spec], out_shape=[jax.ShapeDtypeStruct((h2, s), F32)] * 2, compiler_params=_params(1),
    )(dt_raw, bias, alog_col)


def _dtprep_bwd(name, ddt_f, ddt_b, da_f, da_b, dt_t, alog_col):
    h2, s = dt_t.shape
    h = h2 // 2
    ts = _tile(s, (512, 256, 128))

    def body(ddf, ddb, daf, dab, dt_ref, al_ref, raw_ref, db_ref, dal_ref):
        i = pl.program_id(0)
        a_col = -jnp.exp(al_ref[...])
        da = jnp.concatenate([daf[...], dab[...]], axis=0)
        dt = dt_ref[...]
        d_raw = (jnp.concatenate([ddf[...], ddb[...]], axis=0) + da * a_col) * (1.0 - jnp.exp(-dt))
        raw_ref[...] = d_raw.T

        @pl.when(i == 0)
        def _():
            db_ref[...] = jnp.zeros_like(db_ref)
            dal_ref[...] = jnp.zeros_like(dal_ref)

        db_ref[...] += jnp.sum(d_raw, axis=1, keepdims=True)
        dal_ref[...] += jnp.sum(da * dt, axis=1, keepdims=True) * a_col

    half = pl.BlockSpec((h, ts), lambda i: (0, i))
    col = pl.BlockSpec((h2, 1), lambda i: (0, 0))
    return pl.pallas_call(
        body, name=name, grid=(s // ts,),
        in_specs=[half, half, half, half, pl.BlockSpec((h2, ts), lambda i: (0, i)), col],
        out_specs=[pl.BlockSpec((ts, h2), lambda i: (i, 0)), col, col],
        out_shape=[jax.ShapeDtypeStruct((s, h2), F32), jax.ShapeDtypeStruct((h2, 1), F32), jax.ShapeDtypeStruct((h2, 1), F32)],
        compiler_params=_params(1),
    )(ddt_f, ddt_b, da_f, da_b, dt_t, alog_col)


def _chunk_consts(a_r, dt_r, rev):
    ii = lax.broadcasted_iota(jnp.int32, (CHUNK, CHUNK), 0)
    jj = lax.broadcasted_iota(jnp.int32, (CHUNK, CHUNK), 1)
    w = (jj >= ii) if rev else (jj <= ii)
    wt = (ii >= jj) if rev else (ii <= jj)
    wf = w.astype(F32)
    eye = (ii == jj).astype(F32)
    cs_col = lax.dot_general(wf, a_r, NT, precision=HIGHEST, preferred_element_type=F32)
    cs_row = lax.dot_general(a_r, wf, NT, precision=HIGHEST, preferred_element_type=F32)
    dt_col = lax.dot_general(eye, dt_r, NT, precision=HIGHEST, preferred_element_type=F32)
    tot = jnp.sum(a_r, axis=1, keepdims=True)
    lo_half = lax.broadcasted_iota(jnp.int32, (CHUNK, 128), 1) < HEAD_DIM

    def lanes(col):
        return [jnp.broadcast_to(col[:, j:j + 1], (CHUNK, 128)) for j in range(HEADS_PER_GROUP)]

    def spread(bs):
        return jnp.concatenate([jnp.where(lo_half, bs[2 * k], bs[2 * k + 1]) for k in range(HEADS_PER_GROUP // 2)], axis=1)

    csc = lanes(cs_col)
    cs_x, dt_x = spread(csc), spread(lanes(dt_col))
    tot_x = cs_x[0:1, :] if rev else cs_x[CHUNK - 1:CHUNK, :]
    return w, wt, eye, csc, cs_row, tot, cs_x, dt_x, tot_x


def _hi_lo(v):
    hi = v.astype(BF16)
    return jnp.concatenate([hi, (v - _f(hi)).astype(BF16)], axis=1)


def _head_sums(v, terms=3):
    rr = lax.broadcasted_iota(jnp.int32, (GROUP_X, HEADS_PER_GROUP), 0)
    cc = lax.broadcasted_iota(jnp.int32, (GROUP_X, HEADS_PER_GROUP), 1)
    et = (lax.shift_right_logical(rr, 6) == cc).astype(BF16)
    out, rest = None, v
    for t in range(terms):
        part = rest.astype(BF16)
        if t < terms - 1:
            rest = rest - _f(part)
        p = jnp.dot(part, et, preferred_element_type=F32)
        out = p if out is None else out + p
    return out


def _ssd_fwd(name, xbc, dt_t, a_t, dsk, n_groups):
    s = xbc.shape[0]
    nc = s // CHUNK
    d_ssm = n_groups * GROUP_X

    def one(x_ref, dt_ref, a_ref, dsk_ref, y_ref, hin_ref, h_scr, rev):
        skip = dsk_ref is not None
        xs = _f(x_ref[:, 0:GROUP_X])
        bm = x_ref[:, GROUP_X:GROUP_X + D_STATE]
        cm = x_ref[:, GROUP_X + D_STATE:GROUP_W]
        w, _, _, csc, cs_row, tot, cs_x, dt_x, tot_x = _chunk_consts(a_ref[...], dt_ref[...], rev)
        cb = lax.dot_general(cm, bm, NT, preferred_element_type=F32)
        h = h_scr[...]
        h_b = h.astype(BF16)
        hin_ref[0, 0] = h_b
        xdt = xs * dt_x
        xdt_b = xdt.astype(BF16)
        yo = lax.dot_general(cm, h_b, NT, preferred_element_type=F32) * jnp.exp(cs_x)
        st = lax.dot_general((xdt * jnp.exp(tot_x - cs_x)).astype(BF16), bm, TN, preferred_element_type=F32)
        yds = []
        for j in range(HEADS_PER_GROUP):
            lo = HEAD_DIM * j
            h_scr[lo:lo + HEAD_DIM, :] = h[lo:lo + HEAD_DIM, :] * jnp.exp(tot[j:j + 1, 0:1]) + st[lo:lo + HEAD_DIM, :]
            lm = jnp.exp(jnp.where(w, csc[j] - cs_row[j:j + 1, :], -jnp.inf))
            yds.append(jnp.dot((cb * lm).astype(BF16), xdt_b[:, lo:lo + HEAD_DIM], preferred_element_type=F32))
        y = jnp.concatenate(yds, axis=1) + yo
        if skip:
            y = y + dsk_ref[...] * xs
        y_ref[...] = y.astype(y_ref.dtype)

    def body(xf_ref, dtf_ref, af_ref, xr_ref, dtr_ref, ar_ref, dsk_ref, yf_ref, hf_ref, yr_ref, hr_ref, hf_scr, hr_scr):
        @pl.when(pl.program_id(1) == 0)
        def _():
            hf_scr[...] = jnp.zeros_like(hf_scr)
            hr_scr[...] = jnp.zeros_like(hr_scr)

        one(xf_ref, dtf_ref, af_ref, dsk_ref, yf_ref, hf_ref, hf_scr, False)
        one(xr_ref, dtr_ref, ar_ref, None, yr_ref, hr_ref, hr_scr, True)

    def specs(rev):
        def z(zi):
            return nc - 1 - zi if rev else zi
        t_spec = pl.BlockSpec((HEADS_PER_GROUP, CHUNK), lambda g, zi: ((n_groups if rev else 0) + g, z(zi)))
        ins = [pl.BlockSpec((CHUNK, GROUP_W), lambda g, zi: (z(zi), g)), t_spec, t_spec]
        outs = [pl.BlockSpec((CHUNK, GROUP_X), lambda g, zi: (z(zi), g)),
                pl.BlockSpec((1, 1, GROUP_X, D_STATE), lambda g, zi: (g, z(zi), 0, 0))]
        return ins, outs

    (in_f, out_f), (in_r, out_r) = specs(False), specs(True)
    shapes = [jax.ShapeDtypeStruct((s, d_ssm), BF16), jax.ShapeDtypeStruct((n_groups, nc, GROUP_X, D_STATE), BF16)]
    return pl.pallas_call(
        body, name=name, grid=(n_groups, nc), in_specs=in_f + in_r + [pl.BlockSpec((1, GROUP_X), lambda g, zi: (0, g))],
        out_specs=out_f + out_r, out_shape=shapes + shapes,
        scratch_shapes=[pltpu.VMEM((GROUP_X, D_STATE), F32)] * 2, compiler_params=_params(2),
    )(xbc, dt_t, a_t, xbc, dt_t, a_t, dsk)


def _ssd_bwd(name, xbc, dt_t, a_t, dy, hin_f, hin_r, dsk, n_groups):
    s = xbc.shape[0]
    nc = s // CHUNK
    n_heads = n_groups * HEADS_PER_GROUP

    def one(x_ref, dt_ref, a_ref, dy_ref, hin_ref, dsk_ref, dx_ref, ddt_ref, da_ref, ddsk_ref, g_scr, rev):
        skip = dsk_ref is not None
        xs = _f(x_ref[:, 0:GROUP_X])
        bm = x_ref[:, GROUP_X:GROUP_X + D_STATE]
        cm = x_ref[:, GROUP_X + D_STATE:GROUP_W]
        dyv = _f(dy_ref[...])
        w, wt, eye, csc, cs_row, tot, cs_x, dt_x, tot_x = _chunk_consts(a_ref[...], dt_ref[...], rev)
        cb = lax.dot_general(cm, bm, NT, preferred_element_type=F32)
        cbt = lax.dot_general(bm, cm, NT, preferred_element_type=F32)
        onehot = lax.broadcasted_iota(jnp.int32, (1, HEADS_PER_GROUP), 1)
        hin = hin_ref[0, 0]
        g = g_scr[...]
        g_b = g.astype(BF16)
        din, dte = jnp.exp(cs_x), jnp.exp(tot_x - cs_x)
        xdt = xs * dt_x
        xdt_b, dy_b = xdt.astype(BF16), dyv.astype(BF16)
        dyo_b = (dyv * din).astype(BF16)
        yo = lax.dot_general(cm, hin, NT, preferred_element_type=F32) * din
        dxdt_s = lax.dot_general(bm, g_b, NT, preferred_element_type=F32) * dte
        d_c = jnp.dot(dyo_b, hin, preferred_element_type=F32)
        d_b = jnp.dot((xdt * dte).astype(BF16), g_b, preferred_element_type=F32)
        dhin = lax.dot_general(dyo_b, cm, TN, preferred_element_type=F32)
        dcb = jnp.zeros((CHUNK, CHUNK), F32)
        dtot = jnp.zeros((1, HEADS_PER_GROUP), F32)
        yds, dxds = [], []
        for j in range(HEADS_PER_GROUP):
            lo = HEAD_DIM * j
            cd = jnp.exp(tot[j:j + 1, 0:1])
            gj = g[lo:lo + HEAD_DIM, :]
            dcd = jnp.sum(jnp.sum(gj * _f(hin[lo:lo + HEAD_DIM, :]), axis=1, keepdims=True), axis=0, keepdims=True)
            dtot = dtot + (dcd * cd) * (onehot == j).astype(F32)
            g_scr[lo:lo + HEAD_DIM, :] = dhin[lo:lo + HEAD_DIM, :] + gj * cd
            dd = csc[j] - cs_row[j:j + 1, :]
            lm = jnp.exp(jnp.where(w, dd, -jnp.inf))
            lmt = jnp.exp(jnp.where(wt, -dd, -jnp.inf))
            xj_b, dyj_b = xdt_b[:, lo:lo + HEAD_DIM], dy_b[:, lo:lo + HEAD_DIM]
            yds.append(jnp.dot(_hi_lo(cb * lm), jnp.concatenate([xj_b, xj_b], axis=0), preferred_element_type=F32))
            dxds.append(jnp.dot(_hi_lo(cbt * lmt), jnp.concatenate([dyj_b, dyj_b], axis=0), preferred_element_type=F32))
            dcb = dcb + lax.dot_general(dyj_b, xj_b, NT, preferred_element_type=F32) * lm
        dxdt_d = jnp.concatenate(dxds, axis=1)
        z_state = xdt * dxdt_s
        yd = jnp.concatenate(yds, axis=1)
        dcs = _head_sums(dyv * yo + _f(dy_b) * yd - _f(xdt_b) * dxdt_d - z_state)
        dtot = dtot + _head_sums(jnp.broadcast_to(jnp.sum(z_state, axis=0, keepdims=True), (8, GROUP_X)))[0:1, :]
        dxdt = dxdt_s + dxdt_d
        ddt_c = _head_sums(dxdt * xs, terms=2)
        dx = dxdt * dt_x
        if skip:
            dx = dx + dsk_ref[...] * dyv
        dcb_b = dcb.astype(BF16)
        d_c = d_c + jnp.dot(dcb_b, bm, preferred_element_type=F32)
        d_b = d_b + lax.dot_general(dcb_b, cm, TN, preferred_element_type=F32)
        da_c = jnp.dot(wt.astype(F32), dcs, precision=HIGHEST, preferred_element_type=F32) + dtot
        ddt_ref[...] = lax.dot_general(ddt_c, eye, TN, precision=HIGHEST, preferred_element_type=F32)
        da_ref[...] = lax.dot_general(da_c, eye, TN, precision=HIGHEST, preferred_element_type=F32)
        dx_ref[...] = jnp.concatenate([dx, d_b, d_c], axis=1).astype(dx_ref.dtype)
        if skip:
            ddsk_ref[...] += jnp.sum(dyv * xs, axis=0, keepdims=True)

    def body(xf, dtf, af, dyf, hf, xr, dtr, ar, dyr, hr, dsk_ref, dxf, ddtf, daf, dxr, ddtr, dar, ddsk_ref, gf_scr, gr_scr):
        @pl.when(pl.program_id(1) == 0)
        def _():
            gf_scr[...] = jnp.zeros_like(gf_scr)
            gr_scr[...] = jnp.zeros_like(gr_scr)
            ddsk_ref[...] = jnp.zeros_like(ddsk_ref)

        one(xf, dtf, af, dyf, hf, dsk_ref, dxf, ddtf, daf, ddsk_ref, gf_scr, False)
        one(xr, dtr, ar, dyr, hr, None, dxr, ddtr, dar, None, gr_scr, True)

    def specs(rev):
        def z(zi):
            return zi if rev else nc - 1 - zi
        t_in = pl.BlockSpec((HEADS_PER_GROUP, CHUNK), lambda g, zi: ((n_groups if rev else 0) + g, z(zi)))
        t_out = pl.BlockSpec((HEADS_PER_GROUP, CHUNK), lambda g, zi: (g, z(zi)))
        x_spec = pl.BlockSpec((CHUNK, GROUP_W), lambda g, zi: (z(zi), g))
        ins = [x_spec, t_in, t_in, pl.BlockSpec((CHUNK, GROUP_X), lambda g, zi: (z(zi), g)),
               pl.BlockSpec((1, 1, GROUP_X, D_STATE), lambda g, zi: (g, z(zi), 0, 0))]
        return ins, [x_spec, t_out, t_out]

    (in_f, out_f), (in_r, out_r) = specs(False), specs(True)
    lane_spec = pl.BlockSpec((1, GROUP_X), lambda g, zi: (0, g))
    shapes = [jax.ShapeDtypeStruct((s, n_groups * GROUP_W), BF16), jax.ShapeDtypeStruct((n_heads, s), F32),
              jax.ShapeDtypeStruct((n_heads, s), F32)]
    return pl.pallas_call(
        body, name=name, grid=(n_groups, nc), in_specs=in_f + in_r + [lane_spec], out_specs=out_f + out_r + [lane_spec],
        out_shape=shapes + shapes + [jax.ShapeDtypeStruct((1, n_groups * GROUP_X), F32)],
        scratch_shapes=[pltpu.VMEM((GROUP_X, D_STATE), F32)] * 2, compiler_params=_params(2),
    )(xbc, dt_t, a_t, dy, hin_f, xbc, dt_t, a_t, dy, hin_r, dsk)


def _mesh_pos():
    return lax.axis_index("x"), lax.axis_index("y"), lax.axis_index("c")


def _slot(p):
    return 4 * p[0] + 2 * p[1] + p[2]


def _exchange_sems(n):
    return [pltpu.SemaphoreType.DMA((7 * n,)), pltpu.SemaphoreType.DMA((7 * n,)), pltpu.SemaphoreType.DMA((n,))]


def _gather_ops(ins, outs, send, recv, loc):
    n = len(ins)
    x, y, c = _mesh_pos()
    me, sib = (x, y, c), (x, y, 1 - c)
    chips = [(1 - x, y), (x, 1 - y), (1 - x, 1 - y)]

    def cp(a, k, block, to, src=None):
        dst = outs[a].at[_slot(block)]
        return pltpu.make_async_remote_copy(
            src_ref=dst if src is None else src, dst_ref=dst, send_sem=send.at[7 * a + k], recv_sem=recv.at[7 * a + k],
            device_id=to, device_id_type=MESH)

    mine = [pltpu.make_async_copy(ins[a], outs[a].at[_slot(me)], loc.at[a]) for a in range(n)]
    first = []
    for a in range(n):
        first.append(cp(a, 0, me, sib, src=ins[a]))
        first += [cp(a, 1 + j, me, (*chip, c), src=ins[a]) for j, chip in enumerate(chips)]

    def start():
        for d in mine + first:
            d.start()

    def finish():
        passed = []
        for a in range(n):
            for j, chip in enumerate(chips):
                cp(a, 1 + j, (*chip, c), me).wait_recv()
                p = cp(a, 4 + j, (*chip, c), sib)
                p.start()
                passed.append(p)
        for a in range(n):
            cp(a, 0, sib, me).wait_recv()
            for j, chip in enumerate(chips):
                cp(a, 4 + j, (*chip, 1 - c), me).wait_recv()
        for f in first + passed:
            f.wait_send()
        for m in mine:
            m.wait()

    return start, finish


_gather_ops.out_shapes = lambda arrs: [jax.ShapeDtypeStruct((N_DEV,) + a.shape, a.dtype) for a in arrs]


def _a2a_ops(ins, outs, send, recv, loc):
    n = len(ins)
    me = _mesh_pos()
    flips = [(fx, fy, fc) for fx in (0, 1) for fy in (0, 1) for fc in (0, 1)][1:]
    peers = [tuple(1 - p if fl else p for p, fl in zip(me, fl3)) for fl3 in flips]

    def cp(a, k, peer, dst_slot):
        return pltpu.make_async_remote_copy(
            src_ref=ins[a].at[_slot(peer)], dst_ref=outs[a].at[dst_slot], send_sem=send.at[7 * a + k],
            recv_sem=recv.at[7 * a + k], device_id=peer, device_id_type=MESH)

    mine = [pltpu.make_async_copy(ins[a].at[_slot(me)], outs[a].at[_slot(me)], loc.at[a]) for a in range(n)]
    sends = [cp(a, k, peer, _slot(me)) for a in range(n) for k, peer in enumerate(peers)]

    def start():
        for d in mine + sends:
            d.start()

    def finish():
        for a in range(n):
            for k, peer in enumerate(peers):
                cp(a, k, peer, _slot(peer)).wait_recv()
        for s_ in sends:
            s_.wait_send()
        for m in mine:
            m.wait()

    return start, finish


_a2a_ops.out_shapes = lambda arrs: [jax.ShapeDtypeStruct(a.shape, a.dtype) for a in arrs]


def _exchange(name, ops, arrs):
    n = len(arrs)

    def body(*refs):
        start, finish = ops(refs[:n], refs[n:2 * n], *refs[2 * n:])
        start()
        finish()

    any_spec = pl.BlockSpec(memory_space=pl.ANY)
    return pl.pallas_call(
        body, name=name, in_specs=[any_spec] * n, out_specs=[any_spec] * n, out_shape=ops.out_shapes(arrs),
        scratch_shapes=_exchange_sems(n),
    )(*arrs)


def _all_gather(name, arrs):
    return _exchange(name, _gather_ops, arrs)


def _all_to_all(name, arrs):
    return _exchange(name, _a2a_ops, arrs)


def _sum8(parts_ref):
    g = _f(parts_ref[0])
    for j in range(1, N_DEV):
        g = g + _f(parts_ref[j])
    return g


def _adam_math(w, g, m, v):
    m = ADAM_B1 * m + (1.0 - ADAM_B1) * g
    v = ADAM_B2 * v + (1.0 - ADAM_B2) * (g * g)
    m_hat = m / (1.0 - ADAM_B1 ** ADAM_STEP)
    v_hat = v / (1.0 - ADAM_B2 ** ADAM_STEP)
    return -ADAM_LR * (m_hat / (jnp.sqrt(v_hat) + ADAM_EPS) + ADAM_WD * w), m, v


def _reduce_sum(name, parts):
    _, r, c = parts.shape

    def body(p_ref, o_ref):
        o_ref[...] = _sum8(p_ref)

    return pl.pallas_call(body, name=name, out_shape=jax.ShapeDtypeStruct((r, c), F32))(parts)


def _adam(name, w, g, m, v, parts=None):
    r, c = w.shape
    tr = _tile(r, (128, 64, 32, 16)) if r * c > 2 ** 18 else r
    spec = pl.BlockSpec((tr, c), lambda i: (i, 0))
    summed = parts is not None

    def body(*refs):
        if summed:
            p_ref, w_ref, m_ref, v_ref, g_out, d_out, m_out, v_out = refs
            gv = _sum8(p_ref)
            g_out[...] = gv
        else:
            g_ref, w_ref, m_ref, v_ref, d_out, m_out, v_out = refs
            gv = g_ref[...]
        d, mn, vn = _adam_math(w_ref[...], gv, m_ref[...], v_ref[...])
        d_out[...] = d
        m_out[...] = mn
        v_out[...] = vn

    first = [pl.BlockSpec((N_DEV, tr, c), lambda i: (0, i, 0))] if summed else [spec]
    n_out = 4 if summed else 3
    return pl.pallas_call(
        body, name=name, grid=(r // tr,), in_specs=first + [spec] * 3, out_specs=[spec] * n_out,
        out_shape=[jax.ShapeDtypeStruct((r, c), F32)] * n_out, compiler_params=_params(1),
    )(parts if summed else g, w, m, v)


def _adam_many(name, ws, gs, ms, vs):
    n = len(ws)

    def body(*refs):
        outs = refs[4 * n:]
        for k in range(n):
            d, mn, vn = _adam_math(refs[k][...], refs[n + k][...], refs[2 * n + k][...], refs[3 * n + k][...])
            outs[k][...] = d
            outs[n + k][...] = mn
            outs[2 * n + k][...] = vn

    res = pl.pallas_call(
        body, name=name, out_shape=[jax.ShapeDtypeStruct(w.shape, F32) for w in ws] * 3,
        compiler_params=pltpu.CompilerParams(vmem_limit_bytes=VMEM_LIMIT),
    )(*ws, *gs, *ms, *vs)
    return res[:n], res[n:2 * n], res[2 * n:]


def _slot_offsets(sizes):
    offs, o = [], 0
    for sz in sizes:
        offs.append(o)
        o += -(-sz // 128) * 128
    return offs, o + (-o % 1024)


def _pack_vectors(name, vecs):
    sizes = [v.shape[1] for v in vecs]
    offs, total = _slot_offsets(sizes)

    def body(*refs):
        out = refs[-1]
        out[...] = jnp.zeros_like(out)
        for r, o, sz in zip(refs[:-1], offs, sizes):
            out[:, o:o + sz] = r[...]

    return pl.pallas_call(body, name=name, out_shape=jax.ShapeDtypeStruct((1, total), F32),
                          compiler_params=pltpu.CompilerParams(vmem_limit_bytes=VMEM_LIMIT))(*vecs).reshape(-1, 128)


def _xbc_runs(n_groups, base=0):
    ds, nb = n_groups * GROUP_X, n_groups * D_STATE
    runs = []
    for g in range(n_groups):
        runs += [(base + GROUP_X * g, GROUP_X), (base + ds + D_STATE * g, D_STATE), (base + ds + nb + D_STATE * g, D_STATE)]
    return runs


def _interleave_runs(width, base=0):
    half = width // 2
    runs = []
    for j in range(half // COL_BLOCK):
        runs += [(base + COL_BLOCK * j, COL_BLOCK), (base + half + COL_BLOCK * j, COL_BLOCK)]
    return runs


def _w_in_runs(d, d_ssm, d_xbc, n_heads, n_groups):
    o_dt = d_ssm + d_xbc
    o_glu = o_dt + 2 * n_heads
    return ([(0, d_ssm)] + _xbc_runs(n_groups, d_ssm) + _interleave_runs(2 * d, o_glu) + _interleave_runs(2 * d, o_glu + 2 * d)
            + [(o_dt, 2 * n_heads)])


def _inverse_runs(runs):
    offs = np.concatenate([[0], np.cumsum([l for _, l in runs])])
    order = sorted(range(len(runs)), key=lambda k: runs[k][0])
    return [(int(offs[k]), runs[k][1]) for k in order]


def _permute_cols(name, a, runs):
    merged = []
    for s_, l in runs:
        if merged and merged[-1][0] + merged[-1][1] == s_:
            merged[-1] = (merged[-1][0], merged[-1][1] + l)
        else:
            merged.append((s_, l))
    r, c = a.shape
    c_out = sum(l for _, l in merged)
    tr = r
    while tr % 16 == 0 and tr * max(c, c_out) * a.dtype.itemsize > 4 * 2 ** 20:
        tr //= 2

    def body(a_ref, o_ref):
        o = 0
        for s_, l in merged:
            o_ref[:, o:o + l] = a_ref[:, s_:s_ + l]
            o += l

    return pl.pallas_call(
        body, name=name, grid=(r // tr,), in_specs=[pl.BlockSpec((tr, c), lambda i: (i, 0))],
        out_specs=pl.BlockSpec((tr, c_out), lambda i: (i, 0)), out_shape=jax.ShapeDtypeStruct((r, c_out), a.dtype),
        compiler_params=_params(1),
    )(a)


def _cols_from_shards(g):
    return g.transpose(1, 0, 2).reshape(g.shape[1], -1)


def _cols_to_shards(a):
    return a.reshape(a.shape[0], N_DEV, -1).transpose(1, 0, 2)


def _pad_rows(a, rows):
    return jnp.concatenate([a, jnp.zeros((rows - a.shape[0], a.shape[1]), a.dtype)], axis=0)


def _pack(vecs):
    a = jnp.concatenate(vecs, axis=1)
    return jnp.concatenate([a, jnp.zeros((1, -a.shape[1] % 1024), a.dtype)], axis=1).reshape(-1, 128)


def kernel(x, c, w_ada, b_ada, g_pre_mix, g_post_mix, w_in, w_conv_ssm, b_conv_ssm, dt_bias_fwd, dt_bias_bwd, a_log_fwd, a_log_bwd, d_skip, g_ssm_norm, w_ssm_out, b_glu, w_dw, b_dw, ln_g, ln_b, w_conv_out, b_conv_out, b_gate, w_mix_out, g_pre_ffn, g_post_ffn, w_gate_up, w_down, loss_target, m_w_ada, m_b_ada, m_g_pre_mix, m_g_post_mix, m_w_in, m_w_conv_ssm, m_b_conv_ssm, m_dt_bias_fwd, m_dt_bias_bwd, m_a_log_fwd, m_a_log_bwd, m_d_skip, m_g_ssm_norm, m_w_ssm_out, m_b_glu, m_w_dw, m_b_dw, m_ln_g, m_ln_b, m_w_conv_out, m_b_conv_out, m_b_gate, m_w_mix_out, m_g_pre_ffn, m_g_post_ffn, m_w_gate_up, m_w_down, v_w_ada, v_b_ada, v_g_pre_mix, v_g_post_mix, v_w_in, v_w_conv_ssm, v_b_conv_ssm, v_dt_bias_fwd, v_dt_bias_bwd, v_a_log_fwd, v_a_log_bwd, v_d_skip, v_g_ssm_norm, v_w_ssm_out, v_b_glu, v_w_dw, v_b_dw, v_ln_g, v_ln_b, v_w_conv_out, v_b_conv_out, v_b_gate, v_w_mix_out, v_g_pre_ffn, v_g_post_ffn, v_w_gate_up, v_w_down):
    weights = dict(w_ada=w_ada, b_ada=b_ada, g_pre_mix=g_pre_mix, g_post_mix=g_post_mix, w_in=w_in, w_conv_ssm=w_conv_ssm, b_conv_ssm=b_conv_ssm, dt_bias_fwd=dt_bias_fwd, dt_bias_bwd=dt_bias_bwd, a_log_fwd=a_log_fwd, a_log_bwd=a_log_bwd, d_skip=d_skip, g_ssm_norm=g_ssm_norm, w_ssm_out=w_ssm_out, b_glu=b_glu, w_dw=w_dw, b_dw=b_dw, ln_g=ln_g, ln_b=ln_b, w_conv_out=w_conv_out, b_conv_out=b_conv_out, b_gate=b_gate, w_mix_out=w_mix_out, g_pre_ffn=g_pre_ffn, g_post_ffn=g_post_ffn, w_gate_up=w_gate_up, w_down=w_down)
    mom_m = dict(w_ada=m_w_ada, b_ada=m_b_ada, g_pre_mix=m_g_pre_mix, g_post_mix=m_g_post_mix, w_in=m_w_in, w_conv_ssm=m_w_conv_ssm, b_conv_ssm=m_b_conv_ssm, dt_bias_fwd=m_dt_bias_fwd, dt_bias_bwd=m_dt_bias_bwd, a_log_fwd=m_a_log_fwd, a_log_bwd=m_a_log_bwd, d_skip=m_d_skip, g_ssm_norm=m_g_ssm_norm, w_ssm_out=m_w_ssm_out, b_glu=m_b_glu, w_dw=m_w_dw, b_dw=m_b_dw, ln_g=m_ln_g, ln_b=m_ln_b, w_conv_out=m_w_conv_out, b_conv_out=m_b_conv_out, b_gate=m_b_gate, w_mix_out=m_w_mix_out, g_pre_ffn=m_g_pre_ffn, g_post_ffn=m_g_post_ffn, w_gate_up=m_w_gate_up, w_down=m_w_down)
    mom_v = dict(w_ada=v_w_ada, b_ada=v_b_ada, g_pre_mix=v_g_pre_mix, g_post_mix=v_g_post_mix, w_in=v_w_in, w_conv_ssm=v_w_conv_ssm, b_conv_ssm=v_b_conv_ssm, dt_bias_fwd=v_dt_bias_fwd, dt_bias_bwd=v_dt_bias_bwd, a_log_fwd=v_a_log_fwd, a_log_bwd=v_a_log_bwd, d_skip=v_d_skip, g_ssm_norm=v_g_ssm_norm, w_ssm_out=v_w_ssm_out, b_glu=v_b_glu, w_dw=v_w_dw, b_dw=v_b_dw, ln_g=v_ln_g, ln_b=v_ln_b, w_conv_out=v_w_conv_out, b_conv_out=v_b_conv_out, b_gate=v_b_gate, w_mix_out=v_w_mix_out, g_pre_ffn=v_g_pre_ffn, g_post_ffn=v_g_post_ffn, w_gate_up=v_w_gate_up, w_down=v_w_down)
    names = list(weights)

    s, d = x.shape[1], x.shape[2]
    d_ssm, n_heads, d_xbc = g_ssm_norm.shape[-1], d_skip.shape[-1], b_conv_ssm.shape[-1]
    n_groups = n_heads // HEADS_PER_GROUP
    d_ff = w_down.shape[1] * N_DEV
    k_ssm, k_dw = w_conv_ssm.shape[1], w_dw.shape[1]
    assert d_ssm == n_groups * GROUP_X and d_xbc == n_groups * GROUP_W and s % CHUNK == 0
    assert d % COL_BLOCK == 0 and d_ff % COL_BLOCK == 0
    mc = w_ada.shape[-1]
    me = _slot(_mesh_pos())
    wb = d_ssm + d_xbc + 4 * d
    p_x, p_glu, p_gate = d_ssm, d_ssm + d_xbc, d_ssm + d_xbc + 2 * d
    ncb, nfb = d // COL_BLOCK, d_ff // COL_BLOCK
    x2, tgt = x[0], loss_target[0]

    big = ["w_in", "w_ssm_out", "w_conv_out", "w_mix_out", "w_gate_up", "w_down"]
    (w_in_g,) = _all_gather("gather_w_in", [w_in[0].astype(BF16)])
    w_in_runs, xbc_runs, il_runs = _w_in_runs(d, d_ssm, d_xbc, n_heads, n_groups), _xbc_runs(n_groups), _interleave_runs(2 * d)
    w_in_p = _permute_cols("order_w_in", _cols_from_shards(w_in_g), w_in_runs)
    w_big, w_dt = w_in_p[:, :wb], w_in_p[:, wb:]

    n_cs, n_dw = k_ssm * (d_xbc // N_DEV), k_dw * (d // N_DEV)
    (small_all,) = _all_gather("gather_small", [_pack([c, w_conv_ssm.reshape(1, n_cs), w_dw.reshape(1, n_dw)])])
    small_all = small_all.reshape(N_DEV, -1)
    c_all = small_all[:, :d]
    wcs = _cols_from_shards(small_all[:, d:d + n_cs].reshape(N_DEV, k_ssm, -1))
    wdw = _cols_from_shards(small_all[:, d + n_cs:d + n_cs + n_dw].reshape(N_DEV, k_dw, -1))
    wcs_p = _permute_cols("order_conv_taps", wcs, xbc_runs)
    bcs_p = _permute_cols("order_conv_bias", b_conv_ssm, xbc_runs)
    b_glu_il, b_gate_il = _permute_cols("order_b_glu", b_glu, il_runs), _permute_cols("order_b_gate", b_gate, il_runs)
    kp_ssm, kp_dw = -(-k_ssm // 8) * 8, -(-k_dw // 8) * 8
    zero_x, zero_d = jnp.zeros((1, d_xbc), F32), jnp.zeros((1, d), F32)

    (c_act,), _ = _rowcall("c_act", _fwd_fn(_silu), [(c_all, 0, d)], [], [(d, F32)], [])
    b_my = lax.dynamic_slice(b_ada, (0, me * mc), (1, mc))
    mod_cols = _matmul("mod", c_act, w_ada[0], "nn", F32, acc_in=jnp.broadcast_to(b_my, (N_DEV, mc)))
    (mod_rows,) = _all_to_all("mod_to_owner", [mod_cols.reshape(N_DEV, 1, mc)])
    mod = mod_rows.reshape(1, N_DEV * mc)
    sh1, sc1, g1, sh2, sc2, g2 = [mod[:, k * d:(k + 1) * d] for k in range(6)]

    (h1,), _ = _rowcall("prenorm_mix", _fwd_fn(_t_prenorm), [(x2, 0, d)], [g_pre_mix, sc1, sh1], [(d, BF16)], [])
    proj, rest = _matmul("in_proj", h1, w_big, "nn", BF16, carry=(_gather_ops, [weights[k][0].astype(BF16) for k in big[1:]]))
    gath = dict(zip(big[1:], rest))
    w_ssm, w_co, w_mo = gath["w_ssm_out"].reshape(d_ssm, d), gath["w_conv_out"].reshape(d, d), gath["w_mix_out"].reshape(d, d)
    gu_runs = _interleave_runs(2 * d_ff)
    w_gu = _permute_cols("order_w_gate_up", _cols_from_shards(gath["w_gate_up"]), gu_runs)
    w_dn = gath["w_down"].reshape(d_ff, d)
    dt_raw = _matmul("in_proj_dt", h1, w_dt, "nn", F32)
    xbc_c = _dwconv("conv_ssm", proj, p_x, d_xbc, _pad_rows(wcs_p, kp_ssm), k_ssm, bcs_p, True, BF16)
    alog_col = jnp.concatenate([a_log_fwd, a_log_bwd], axis=1).reshape(2 * n_heads, 1)
    dt_t, a_t = _dtprep("dt_prep", dt_raw, jnp.concatenate([dt_bias_fwd, dt_bias_bwd], axis=1), alog_col)
    dsk = jnp.repeat(d_skip, HEAD_DIM, axis=1)
    y_f, hin_f, y_b, hin_b = _ssd_fwd("ssd", xbc_c, dt_t, a_t, dsk, n_groups)
    gn_rows = [(y_f, 0, d_ssm), (y_b, 0, d_ssm), (proj, 0, d_ssm)]
    (yn,), _ = _rowcall("ssm_norm", _fwd_fn(_t_gnorm), gn_rows, [g_ssm_norm], [(d_ssm, BF16)], [], ncol=n_groups)
    ya = _matmul("ssm_out", yn, w_ssm, "nn", BF16)
    (u0,), _ = _rowcall("glu", _glu_fwd, [(proj, p_glu, 2 * d)], [b_glu_il], [(d, BF16)], [], ncol=ncb)
    wdw_pad = _pad_rows(wdw, kp_dw)
    u1 = _dwconv("conv_dw", u0, 0, d, wdw_pad, k_dw, b_dw, False, BF16)
    (u2,), _ = _rowcall("ln_silu", _fwd_fn(_t_lnsilu), [(u1, 0, d)], [ln_g, ln_b], [(d, BF16)], [])
    yb = _matmul("conv_out", u2, w_co, "nn", BF16)
    gm_rows, gm_prm = [(ya, 0, d), (yb, 0, d), (proj, p_gate, 2 * d)], [b_conv_out, b_gate_il]
    (m_in,), _ = _rowcall("gate_mix", _gatemix_fwd, gm_rows, gm_prm, [(d, BF16)], [], ncol=ncb)
    mix = _matmul("mix_out", m_in, w_mo, "nn", BF16)
    pp_prm = [g_post_mix, g1, g_pre_ffn, sc2, sh2]
    (x1, h2), _ = _rowcall("post_mix", _fwd_fn(_t_postpre), [(x2, 0, d), (mix, 0, d)], pp_prm, [(d, F32), (d, BF16)], [])
    gu = _matmul("gate_up", h2, w_gu, "nn", BF16)
    (act,), _ = _rowcall("swiglu", _swiglu_fwd, [(gu, 0, 2 * d_ff)], [], [(d_ff, BF16)], [], ncol=nfb)
    f = _matmul("down", act, w_dn, "nn", BF16)

    def loss_fn(x1_t, f_t, tgt_t, gpost, g2_t):
        args = [_f(x1_t), _f(f_t), _f(tgt_t), _f(gpost), _f(g2_t)]
        val, vjp = jax.vjp(_t_loss, *args)
        gr = vjp(jnp.ones((), F32))
        return (gr[0], gr[1]), (jnp.zeros((1, 128), F32) + val, gr[3], gr[4])

    (dx1a, df), (loss_v, d_gpost2, d_g2) = _rowcall(
        "loss", loss_fn, [(x1, 0, d), (f, 0, d), (tgt, 0, d)], [g_post_ffn, g2], [(d, F32), (d, BF16)], [128, d, d])
    loss = lax.psum(loss_v[0, 0], ("x", "y", "c"))

    dact = _matmul("down_dx", df, w_dn, "nt", BF16)
    dw_dn = _matmul("down_dw", act, df, "tn", BF16)
    (dgu,), _ = _rowcall("swiglu_bwd", _swiglu_bwd, [(gu, 0, 2 * d_ff), (dact, 0, d_ff)], [], [(2 * d_ff, BF16)], [], ncol=nfb)
    recv = {}
    dh2, (recv["w_down"],) = _matmul("gate_up_dx", dgu, w_gu, "nt", BF16, carry=(_a2a_ops, [dw_dn.reshape(N_DEV, -1, d)]))
    dw_gu = _matmul("gate_up_dw", h2, dgu, "tn", BF16)
    (dxa, dmix), (d_gpost1, d_g1, d_gpre2, d_sc2, d_sh2) = _rowcall(
        "post_mix_bwd", _vjp_fn(_t_postpre, 2, 2, (0, 1), (0, 1, 2, 3, 4)),
        [(x2, 0, d), (mix, 0, d), (dx1a, 0, d), (dh2, 0, d)], pp_prm, [(d, F32), (d, BF16)], [d] * 5)
    dm_in = _matmul("mix_out_dx", dmix, w_mo, "nt", BF16)
    dw_mo = _matmul("mix_out_dw", m_in, dmix, "tn", BF16)
    dproj = lax.empty((s, wb), BF16)
    (dya, dyb, dproj), (d_bco, d_bgate) = _rowcall(
        "gate_mix_bwd", _gatemix_bwd, gm_rows + [(dm_in, 0, d)], gm_prm,
        [(d, BF16), (d, BF16), (2 * d, BF16, dproj, p_gate)], [d, 2 * d], ncol=ncb)
    du2 = _matmul("conv_out_dx", dyb, w_co, "nt", BF16)
    dw_co = _matmul("conv_out_dw", u2, dyb, "tn", BF16)
    (du1,), (d_lng, d_lnb) = _rowcall("ln_silu_bwd", _vjp_fn(_t_lnsilu, 1, 1, (0,), (0, 1)), [(u1, 0, d), (du2, 0, d)],
                                      [ln_g, ln_b], [(d, BF16)], [d, d])
    dwdw_p, d_bdw = _dwconv_bwd_w("conv_dw_dw", u0, 0, d, wdw_pad, k_dw, b_dw, [du1], False)
    du0 = _dwconv("conv_dw_dx", du1, 0, d, _pad_rows(wdw[::-1], kp_dw), k_dw, zero_d, False, BF16)
    (dproj,), (d_bglu,) = _rowcall("glu_bwd", _glu_bwd, [(proj, p_glu, 2 * d), (du0, 0, d)], [b_glu_il],
                                   [(2 * d, BF16, dproj, p_glu)], [2 * d], ncol=ncb)
    dyn, (recv["w_mix_out"], recv["w_conv_out"]) = _matmul(
        "ssm_out_dx", dya, w_ssm, "nt", BF16, carry=(_a2a_ops, [dw_mo.reshape(N_DEV, -1, d), dw_co.reshape(N_DEV, -1, d)]))
    dw_ssm = _matmul("ssm_out_dw", yn, dya, "tn", BF16)
    (dy_ssd, dproj), (d_gn,) = _rowcall("ssm_norm_bwd", _vjp_fn(_t_gnorm, 3, 1, (0, 2), (0,)), gn_rows + [(dyn, 0, d_ssm)],
                                        [g_ssm_norm], [(d_ssm, BF16), (d_ssm, BF16, dproj, 0)], [d_ssm], ncol=n_groups)
    dxbc_f, ddt_f, da_f, dxbc_b, ddt_b, da_b, ddsk = _ssd_bwd("ssd_bwd", xbc_c, dt_t, a_t, dy_ssd, hin_f, hin_b, dsk, n_groups)
    ddt_raw, d_dtb, d_alog = _dtprep_bwd("dt_prep_bwd", ddt_f, ddt_b, da_f, da_b, dt_t, alog_col)
    dpre, dwcs_p, dbcs_p = _dwconv_bwd_w("conv_ssm_dw", proj, p_x, d_xbc, _pad_rows(wcs_p, kp_ssm), k_ssm, bcs_p,
                                          [dxbc_f, dxbc_b], True)
    dproj = _dwconv("conv_ssm_dx", dpre, 0, d_xbc, _pad_rows(wcs_p[::-1], kp_ssm), k_ssm, zero_x, False, BF16, into=(dproj, p_x))
    dw_big, (recv["w_gate_up"], recv["w_ssm_out"]) = _matmul(
        "in_proj_dw", h1, dproj, "tn", BF16,
        carry=(_a2a_ops, [_cols_to_shards(_permute_cols("restore_dw_gate_up", dw_gu, _inverse_runs(gu_runs))),
                          dw_ssm.reshape(N_DEV, -1, d)]))
    dw_dt = _matmul("in_proj_dt_dw", h1, ddt_raw, "tn", BF16)
    dw_in = _permute_cols("restore_dw_in", jnp.concatenate([dw_big, dw_dt], axis=1), _inverse_runs(w_in_runs))
    dh1, (recv["w_in"],) = _matmul("in_proj_dx", dproj, w_big, "nt", F32, carry=(_a2a_ops, [_cols_to_shards(dw_in)]))
    dh1 = _matmul("in_proj_dt_dx", ddt_raw, w_dt, "nt", BF16, acc_in=dh1)

    def prenorm_bwd_fn(x_t, dh_t, dxa_t, g, sc, sh):
        rows, prm = _vjp_fn(_t_prenorm, 1, 1, (0,), (0, 1, 2))(x_t, dh_t, g, sc, sh)
        return (rows[0] + _f(dxa_t),), prm

    (grad_x,), (d_gpre1, d_sc1, d_sh1) = _rowcall(
        "prenorm_mix_bwd", prenorm_bwd_fn, [(x2, 0, d), (dh1, 0, d), (dxa, 0, d)], [g_pre_mix, sc1, sh1], [(d, F32)], [d] * 3)

    dmod = jnp.concatenate([d_sh1, d_sc1, d_g1, d_sh2, d_sc2, d_g2], axis=1)
    small_g = {
        "b_ada": dmod, "g_pre_mix": d_gpre1, "g_post_mix": d_gpost1,
        "b_conv_ssm": _permute_cols("restore_db_conv", dbcs_p, _inverse_runs(xbc_runs)),
        "dt_bias_fwd": d_dtb[:n_heads].reshape(1, n_heads), "dt_bias_bwd": d_dtb[n_heads:].reshape(1, n_heads),
        "a_log_fwd": d_alog[:n_heads].reshape(1, n_heads), "a_log_bwd": d_alog[n_heads:].reshape(1, n_heads),
        "d_skip": ddsk.reshape(n_heads, HEAD_DIM).sum(axis=1).reshape(1, n_heads), "g_ssm_norm": d_gn,
        "b_glu": _permute_cols("restore_db_glu", d_bglu, _inverse_runs(il_runs)), "b_dw": d_bdw, "ln_g": d_lng, "ln_b": d_lnb,
        "b_conv_out": d_bco, "b_gate": _permute_cols("restore_db_gate", d_bgate, _inverse_runs(il_runs)),
        "g_pre_ffn": d_gpre2, "g_post_ffn": d_gpost2,
        "w_conv_ssm": _permute_cols("restore_dw_conv", dwcs_p, _inverse_runs(xbc_runs))[:k_ssm].reshape(1, k_ssm * d_xbc),
        "w_dw": dwdw_p[:k_dw].reshape(1, k_dw * d),
    }
    small_names = list(small_g)
    sizes = [small_g[k].shape[1] for k in small_names]
    offs, _ = _slot_offsets(sizes)
    (small_parts,) = _all_gather("gather_small_grads", [_pack_vectors("pack_small_grads", [small_g[k] for k in small_names])])
    dmod_all = small_parts.reshape(N_DEV, -1)[:, :6 * d]
    small_tot = _reduce_sum("sum_small_grads", small_parts).reshape(1, -1)
    grads = {k: small_tot[:, offs[i]:offs[i] + sizes[i]] for i, k in enumerate(small_names)}
    grads["w_conv_ssm"] = lax.dynamic_slice(grads["w_conv_ssm"].reshape(k_ssm, d_xbc), (0, me * (d_xbc // N_DEV)), (k_ssm, d_xbc // N_DEV))
    grads["w_dw"] = lax.dynamic_slice(grads["w_dw"].reshape(k_dw, d), (0, me * (d // N_DEV)), (k_dw, d // N_DEV))
    grads["w_ada"] = _matmul("mod_dw", c_act, lax.dynamic_slice(dmod_all, (0, me * mc), (N_DEV, mc)), "tn", F32, highest=True)

    delta, new_m, new_v = {}, {}, {}
    for k in big:
        grads[k], delta[k], new_m[k], new_v[k] = _adam("adam_" + k, weights[k][0], None, mom_m[k][0], mom_v[k][0], parts=recv[k])
    for k in ["w_ada", "w_conv_ssm", "w_dw"]:
        delta[k], new_m[k], new_v[k] = _adam("adam_" + k, weights[k][0], grads[k], mom_m[k][0], mom_v[k][0])
    rep = [k for k in names if k not in big and k not in ("w_ada", "w_conv_ssm", "w_dw")]
    r_delta, r_m, r_v = _adam_many("adam_small", *[[t[k] for k in rep] for t in (weights, grads, mom_m, mom_v)])
    for i, k in enumerate(rep):
        delta[k], new_m[k], new_v[k] = r_delta[i], r_m[i], r_v[i]

    def shaped(t, k):
        return t[k].reshape(weights[k].shape)

    return (loss, grad_x.reshape(x.shape), *[shaped(grads, k) for k in names], *[shaped(delta, k) for k in names],
            *[shaped(new_m, k) for k in names], *[shaped(new_v, k) for k in names])
```

```python
import functools

import numpy as np
import jax
import jax.numpy as jnp
from jax import lax
from jax.experimental import pallas as pl
from jax.experimental.pallas import tpu as pltpu

F32 = jnp.float32
BF16 = jnp.bfloat16
HIGHEST = lax.Precision.HIGHEST
MESH = pl.DeviceIdType.MESH
N_DEV = 8
EPS = 1e-6
CHUNK = 128
HEAD_DIM = 64
D_STATE = 128
HEADS_PER_GROUP = 8
GROUP_X = HEADS_PER_GROUP * HEAD_DIM
GROUP_W = GROUP_X + 2 * D_STATE
COL_BLOCK = 512
HALO = 16
VMEM_LIMIT = 56 * 1024 * 1024
ADAM_LR, ADAM_B1, ADAM_B2, ADAM_EPS, ADAM_WD, ADAM_STEP = 0.001, 0.9, 0.999, 1e-08, 0.01, 10

NN = (((1,), (0,)), ((), ()))
NT = (((1,), (1,)), ((), ()))
TN = (((0,), (0,)), ((), ()))


def _tile(dim, prefs):
    for p in prefs:
        if p <= dim and dim % p == 0:
            return p
    return dim


def _params(n_grid):
    return pltpu.CompilerParams(dimension_semantics=("arbitrary",) * n_grid, vmem_limit_bytes=VMEM_LIMIT)


def _f(v):
    return v.astype(F32)


def _matmul(name, a, b, mode, out_dtype, acc_in=None, highest=False, carry=None, tn=None, epilogue=None):
    if mode == "nn":
        (m, k), (k2, n) = a.shape, b.shape
    elif mode == "nt":
        (m, k), (n, k2) = a.shape, b.shape
    else:
        (k, m), (k2, n) = a.shape, b.shape
    assert k == k2, (name, a.shape, b.shape, mode)
    tm = _tile(m, (1024, 512, 256, 128))
    tn = _tile(n, (1024, 512, 256, 128)) if tn is None else tn
    tk = _tile(k, (2048, 1024, 512, 256, 128))
    nk, nj = k // tk, n // tn
    assert n % tn == 0
    dims = {"nn": NN, "nt": NT, "tn": TN}[mode]
    a_spec = pl.BlockSpec((tk, tm), lambda i, j, kk: (kk, i)) if mode == "tn" else pl.BlockSpec((tm, tk), lambda i, j, kk: (i, kk))
    b_spec = pl.BlockSpec((tn, tk), lambda i, j, kk: (j, kk)) if mode == "nt" else pl.BlockSpec((tk, tn), lambda i, j, kk: (kk, j))
    o_spec = pl.BlockSpec((tm, tn), lambda i, j, kk: (i, j))
    has_acc = acc_in is not None
    epi_fn, e_arrs, e_outs = epilogue if epilogue is not None else (None, [], [])
    ne = len(e_arrs)
    n_in = (3 if has_acc else 2) + ne
    if epilogue is None:
        o_specs, o_shapes = [o_spec], [jax.ShapeDtypeStruct((m, n), out_dtype)]
    else:
        o_specs = [pl.BlockSpec((tm, w // nj), lambda i, j, kk: (i, j)) for w, _ in e_outs]
        o_shapes = [jax.ShapeDtypeStruct((m, w), dt) for w, dt in e_outs]
    e_specs = [pl.BlockSpec((tm, arr.shape[1] // nj), lambda i, j, kk: (i, j)) for arr in e_arrs]
    n_out = len(o_specs)
    ops, c_arrs = carry if carry is not None else (None, [])
    nc_ = len(c_arrs)
    grid = (m // tm, nj, nk)

    def body(*refs):
        a_ref, b_ref = refs[0], refs[1]
        c_ref = refs[2] if has_acc else None
        e_refs = refs[n_in - ne:n_in]
        c_ins = refs[n_in:n_in + nc_]
        o_refs = refs[n_in + nc_:n_in + nc_ + n_out]
        c_outs = refs[n_in + nc_ + n_out:n_in + 2 * nc_ + n_out]
        acc = refs[n_in + 2 * nc_ + n_out]
        i, j, kk = pl.program_id(0), pl.program_id(1), pl.program_id(2)
        if nc_:
            start, finish = ops(c_ins, c_outs, *refs[n_in + 2 * nc_ + n_out + 1:])
            pl.when((i == 0) & (j == 0) & (kk == 0))(start)

        def write(val):
            if epi_fn is None:
                o_refs[0][...] = val.astype(out_dtype)
            else:
                for ref, v in zip(o_refs, epi_fn(val, *[r[...] for r in e_refs])):
                    ref[...] = v.astype(ref.dtype)

        def product(first):
            if highest:
                p = lax.dot_general(_f(a_ref[...]), _f(b_ref[...]), dims, precision=HIGHEST, preferred_element_type=F32)
            else:
                p = lax.dot_general(a_ref[...].astype(BF16), b_ref[...].astype(BF16), dims, preferred_element_type=F32)
            return p + _f(c_ref[...]) if (first and has_acc) else p

        if nk == 1:
            write(product(True))
        else:
            @pl.when(kk == 0)
            def _():
                acc[...] = product(True)

            if nk > 2:
                @pl.when((kk > 0) & (kk < nk - 1))
                def _():
                    acc[...] += product(False)

            @pl.when(kk == nk - 1)
            def _():
                write(acc[...] + product(False))

        if nc_:
            pl.when((i == grid[0] - 1) & (j == grid[1] - 1) & (kk == nk - 1))(finish)

    any_spec = pl.BlockSpec(memory_space=pl.ANY)
    c_shapes = ops.out_shapes(c_arrs) if nc_ else []
    res = pl.pallas_call(
        body, name=name, grid=grid,
        in_specs=[a_spec, b_spec] + ([o_spec] if has_acc else []) + e_specs + [any_spec] * nc_,
        out_specs=o_specs + [any_spec] * nc_, out_shape=o_shapes + c_shapes,
        scratch_shapes=[pltpu.VMEM((tm, tn) if nk > 1 else (8, 128), F32)] + (_exchange_sems(nc_) if nc_ else []),
        compiler_params=_params(3),
    )(*([a, b] + ([acc_in] if has_acc else []) + list(e_arrs) + list(c_arrs)))
    outs = res[0] if n_out == 1 else res[:n_out]
    return (outs, res[n_out:]) if nc_ else outs


def _rowcall(name, fn, rows, params, out_rows, out_accs, ncol=1):
    s = rows[0][0].shape[0]
    tm = _tile(s, (256 if ncol == 1 else 1024, 128, 64, 32, 16, 8))
    args, in_specs = [], []
    for arr, off, w in rows:
        bw = w // ncol
        assert w % ncol == 0 and off % bw == 0, (name, off, w, ncol)
        in_specs.append(pl.BlockSpec((tm, bw), functools.partial(lambda j, i, ob: (i, ob + j), ob=off // bw)))
        args.append(arr)
    for p in params:
        assert p.shape[1] % ncol == 0, (name, p.shape)
        in_specs.append(pl.BlockSpec((p.shape[0], p.shape[1] // ncol), lambda j, i: (0, j)))
        args.append(p)
    n_in, n_ro = len(args), len(out_rows)
    out_shape, out_specs, aliases = [], [], {}
    for k, spec in enumerate(out_rows):
        w, dt = spec[0], spec[1]
        bw = w // ncol
        if len(spec) == 4:
            buf, off = spec[2], spec[3]
            assert off % bw == 0 and buf.dtype == dt, (name, off, bw)
            out_shape.append(jax.ShapeDtypeStruct(buf.shape, dt))
            out_specs.append(pl.BlockSpec((tm, bw), functools.partial(lambda j, i, ob: (i, ob + j), ob=off // bw)))
            aliases[len(args)] = k
            in_specs.append(pl.BlockSpec(memory_space=pl.ANY))
            args.append(buf)
        else:
            out_shape.append(jax.ShapeDtypeStruct((s, w), dt))
            out_specs.append(pl.BlockSpec((tm, bw), lambda j, i: (i, j)))
    for w in out_accs:
        out_shape.append(jax.ShapeDtypeStruct((1, w), F32))
        out_specs.append(pl.BlockSpec((1, w // ncol), lambda j, i: (0, j)))
    n_args = len(args)

    def body(*refs):
        i = pl.program_id(1)
        ro, ao = fn(*[r[...] for r in refs[:n_in]])
        outs = refs[n_args:]
        assert len(ro) == n_ro and len(ao) == len(out_accs), name
        for ref, v in zip(outs[:n_ro], ro):
            ref[...] = v.astype(ref.dtype)
        for ref, v in zip(outs[n_ro:], ao):
            @pl.when(i == 0)
            def _(ref=ref, v=v):
                ref[...] = v

            @pl.when(i > 0)
            def _(ref=ref, v=v):
                ref[...] += v

    res = pl.pallas_call(
        body, name=name, grid=(ncol, s // tm), in_specs=in_specs, out_specs=out_specs,
        out_shape=out_shape, input_output_aliases=aliases, compiler_params=_params(2),
    )(*args)
    return res[:n_ro], res[n_ro:]


def _fwd_fn(f):
    def fn(*vals):
        out = f(*[_f(v) for v in vals])
        return (out if isinstance(out, tuple) else (out,)), ()
    return fn


def _vjp_fn(f, n_prim, n_ct, want_rows, want_params):
    def fn(*vals):
        prims = [_f(v) for v in vals[:n_prim]]
        cts = [_f(v) for v in vals[n_prim:n_prim + n_ct]]
        prms = [_f(v) for v in vals[n_prim + n_ct:]]
        out, vjp = jax.vjp(f, *prims, *prms)
        g = vjp(tuple(cts) if isinstance(out, tuple) else cts[0])
        return tuple(g[k] for k in want_rows), tuple(g[n_prim + k] for k in want_params)
    return fn


def _rms(v):
    return v * lax.rsqrt(jnp.mean(v * v, axis=-1, keepdims=True) + EPS)


def _silu(v):
    return v * jax.nn.sigmoid(v)


def _t_prenorm(x, g, sc, sh):
    return _rms(x) * g * (1.0 + sc) + sh


def _t_gnorm(yf, yb, z, gn):
    return _rms((yf + yb) * _silu(z)) * gn


def _t_glu(a, b, ba, bb):
    return (a + ba) * jax.nn.sigmoid(b + bb)


def _t_lnsilu(u, g, b):
    uc = u - jnp.mean(u, axis=-1, keepdims=True)
    return _silu(uc * lax.rsqrt(jnp.mean(uc * uc, axis=-1, keepdims=True) + EPS) * g + b)


def _t_gatemix(ya, yb, la, lb, bco, bga, bgb):
    return jax.nn.sigmoid(la + bga) * ya + jax.nn.sigmoid(lb + bgb) * (yb + bco)


def _t_postpre(x, mix, gpost, g1, gpre, sc, sh):
    x1 = x + g1 * (_rms(mix) * gpost)
    return x1, _rms(x1) * gpre * (1.0 + sc) + sh


def _t_swiglu(gt, up):
    return _silu(gt) * up


def _halves(v):
    return _f(v[:, :COL_BLOCK]), _f(v[:, COL_BLOCK:])


def _glu_fwd(ab, bias):
    return (_t_glu(*_halves(ab), *_halves(bias)),), ()


def _glu_bwd(ab, du, bias):
    _, vjp = jax.vjp(_t_glu, *_halves(ab), *_halves(bias))
    da, db, dba, dbb = vjp(_f(du))
    return (jnp.concatenate([da, db], axis=1),), (jnp.concatenate([dba, dbb], axis=1),)


def _gatemix_fwd(ya, yb, logits, bco, bg):
    return (_t_gatemix(_f(ya), _f(yb), *_halves(logits), _f(bco), *_halves(bg)),), ()


def _gatemix_bwd(ya, yb, logits, dm, bco, bg):
    _, vjp = jax.vjp(_t_gatemix, _f(ya), _f(yb), *_halves(logits), _f(bco), *_halves(bg))
    dya, dyb, dla, dlb, dbco, dbga, dbgb = vjp(_f(dm))
    return (dya, dyb, jnp.concatenate([dla, dlb], axis=1)), (dbco, jnp.concatenate([dbga, dbgb], axis=1))


def _swiglu_fwd(gu):
    return (_t_swiglu(*_halves(gu)),), ()


def _swiglu_bwd(gu, dact):
    _, vjp = jax.vjp(_t_swiglu, *_halves(gu))
    return (jnp.concatenate(vjp(_f(dact)), axis=1),), ()


def _t_loss(x1, f, tgt, gpost, g2):
    e = x1 + g2 * (_rms(f) * gpost) - tgt
    return 0.5 * jnp.sum(jnp.mean(e * e, axis=-1))


def _conv_specs(s, tm, cw, ob, w_rows):
    nh, last_h = tm // HALO, s // HALO - 1
    return [
        pl.BlockSpec((tm, cw), lambda j, i: (i, ob + j)),
        pl.BlockSpec((HALO, cw), lambda j, i: (jnp.maximum(i * nh - 1, 0), ob + j)),
        pl.BlockSpec((HALO, cw), lambda j, i: (jnp.minimum((i + 1) * nh, last_h), ob + j)),
        pl.BlockSpec((w_rows, cw), lambda j, i: (0, j)),
        pl.BlockSpec((1, cw), lambda j, i: (0, j)),
    ]


def _window(x_ref, p_ref, n_ref, i, nrow):
    prev = _f(p_ref[...]) * jnp.where(i > 0, 1.0, 0.0).astype(F32)
    nxt = _f(n_ref[...]) * jnp.where(i < nrow - 1, 1.0, 0.0).astype(F32)
    return jnp.concatenate([prev, _f(x_ref[...]), nxt], axis=0)


def _taps(win, w_ref, k_taps, tm):
    pad = (k_taps - 1) // 2
    acc = w_ref[0:1, :] * win[HALO - pad:HALO - pad + tm, :]
    for k in range(1, k_taps):
        acc = acc + w_ref[k:k + 1, :] * win[HALO + k - pad:HALO + k - pad + tm, :]
    return acc


def _conv_rows(k_taps):
    return 1024 if k_taps <= 8 else 512


def _dwconv(name, x, xoff, c, w_pad, k_taps, b, act, out_dtype, cw=256, into=None):
    s = x.shape[0]
    tm, cw = _tile(s, (_conv_rows(k_taps), 256, 128, 64, 32, 16)), _tile(c, (cw, 128))
    assert xoff % cw == 0 and tm % HALO == 0 and (k_taps - 1) // 2 <= HALO
    nrow = s // tm

    def body(x_ref, p_ref, n_ref, w_ref, b_ref, *rest):
        win = _window(x_ref, p_ref, n_ref, pl.program_id(1), nrow)
        acc = _taps(win, w_ref, k_taps, tm) + b_ref[...]
        rest[-1][...] = (_silu(acc) if act else acc).astype(out_dtype)

    in_specs, args = _conv_specs(s, tm, cw, xoff // cw, w_pad.shape[0]), [x, x, x, w_pad, b]
    if into is None:
        ob, out_shape, aliases = 0, jax.ShapeDtypeStruct((s, c), out_dtype), {}
    else:
        buf, first = into
        assert first % cw == 0 and buf.dtype == out_dtype
        ob, out_shape, aliases = first // cw, jax.ShapeDtypeStruct(buf.shape, out_dtype), {len(args): 0}
        in_specs, args = in_specs + [pl.BlockSpec(memory_space=pl.ANY)], args + [buf]
    return pl.pallas_call(
        body, name=name, grid=(c // cw, nrow), in_specs=in_specs,
        out_specs=pl.BlockSpec((tm, cw), lambda j, i: (i, ob + j)), out_shape=out_shape,
        input_output_aliases=aliases, compiler_params=_params(2),
    )(*args)


def _dwconv_bwd_w(name, x, xoff, c, w_pad, k_taps, b, douts, act, cw=256):
    s = x.shape[0]
    tm, cw = _tile(s, (_conv_rows(k_taps), 256, 128, 64, 32, 16)), _tile(c, (cw, 128))
    assert xoff % cw == 0 and tm % HALO == 0
    nrow, pad, n_d = s // tm, (k_taps - 1) // 2, len(douts)

    def body(x_ref, p_ref, n_ref, w_ref, b_ref, *rest):
        d_refs, outs = rest[:n_d], rest[n_d:]
        i = pl.program_id(1)
        win = _window(x_ref, p_ref, n_ref, i, nrow)
        d = _f(d_refs[0][...])
        for r in d_refs[1:]:
            d = d + _f(r[...])
        shifted = [win[HALO + k - pad:HALO + k - pad + tm, :] for k in range(k_taps)]
        if act:
            pre = b_ref[...] + w_ref[0:1, :] * shifted[0]
            for k in range(1, k_taps):
                pre = pre + w_ref[k:k + 1, :] * shifted[k]
            sg = jax.nn.sigmoid(pre)
            d = d * (sg * (1.0 + pre * (1.0 - sg)))
            outs[0][...] = d.astype(outs[0].dtype)
        dw_ref, db_ref = outs[-2], outs[-1]

        @pl.when(i == 0)
        def _():
            dw_ref[...] = jnp.zeros_like(dw_ref)
            db_ref[...] = jnp.zeros_like(db_ref)

        for k in range(k_taps):
            dw_ref[k:k + 1, :] += jnp.sum(d * shifted[k], axis=0, keepdims=True)
        db_ref[...] += jnp.sum(d, axis=0, keepdims=True)

    row_spec = pl.BlockSpec((tm, cw), lambda j, i: (i, j))
    out_shape = [jax.ShapeDtypeStruct((w_pad.shape[0], c), F32), jax.ShapeDtypeStruct((1, c), F32)]
    out_specs = [pl.BlockSpec((w_pad.shape[0], cw), lambda j, i: (0, j)), pl.BlockSpec((1, cw), lambda j, i: (0, j))]
    if act:
        out_shape, out_specs = [jax.ShapeDtypeStruct((s, c), BF16)] + out_shape, [row_spec] + out_specs
    return pl.pallas_call(
        body, name=name, grid=(c // cw, nrow),
        in_specs=_conv_specs(s, tm, cw, xoff // cw, w_pad.shape[0]) + [row_spec] * n_d,
        out_specs=out_specs, out_shape=out_shape, compiler_params=_params(2),
    )(x, x, x, w_pad, b, *douts)


def _dtprep(name, dt_raw, bias, alog_col):
    s, h2 = dt_raw.shape
    ts = _tile(s, (512, 256, 128))

    def body(r_ref, b_ref, al_ref, dt_ref, a_ref):
        v = r_ref[...] + b_ref[...]
        dt = (jnp.maximum(v, 0.0) + jnp.log(1.0 + jnp.exp(-jnp.abs(v)))).T
        dt_ref[...] = dt
        a_ref[...] = dt * (-jnp.exp(al_ref[...]))

    t_spec = pl.BlockSpec((h2, ts), lambda i: (0, i))
    return pl.pallas_call(
        body, name=name, grid=(s // ts,),
        in_specs=[pl.BlockSpec((ts, h2), lambda i: (i, 0)), pl.BlockSpec((1, h2), lambda i: (0, 0)), pl.BlockSpec((h2, 1), lambda i: (0, 0))],
        out_specs=[t_spec, t_spec], out_shape=[jax.ShapeDtypeStruct((h2, s), F32)] * 2, compiler_params=_params(1),
    )(dt_raw, bias, alog_col)


def _dtprep_bwd(name, ddt_f, ddt_b, da_f, da_b, dt_t, alog_col):
    h2, s = dt_t.shape
    h = h2 // 2
    ts = _tile(s, (512, 256, 128))

    def body(ddf, ddb, daf, dab, dt_ref, al_ref, raw_ref, db_ref, dal_ref):
        i = pl.program_id(0)
        a_col = -jnp.exp(al_ref[...])
        da = jnp.concatenate([daf[...], dab[...]], axis=0)
        dt = dt_ref[...]
        d_raw = (jnp.concatenate([ddf[...], ddb[...]], axis=0) + da * a_col) * (1.0 - jnp.exp(-dt))
        raw_ref[...] = d_raw.T

        @pl.when(i == 0)
        def _():
            db_ref[...] = jnp.zeros_like(db_ref)
            dal_ref[...] = jnp.zeros_like(dal_ref)

        db_ref[...] += jnp.sum(d_raw, axis=1, keepdims=True)
        dal_ref[...] += jnp.sum(da * dt, axis=1, keepdims=True) * a_col

    half = pl.BlockSpec((h, ts), lambda i: (0, i))
    col = pl.BlockSpec((h2, 1), lambda i: (0, 0))
    return pl.pallas_call(
        body, name=name, grid=(s // ts,),
        in_specs=[half, half, half, half, pl.BlockSpec((h2, ts), lambda i: (0, i)), col],
        out_specs=[pl.BlockSpec((ts, h2), lambda i: (i, 0)), col, col],
        out_shape=[jax.ShapeDtypeStruct((s, h2), F32), jax.ShapeDtypeStruct((h2, 1), F32), jax.ShapeDtypeStruct((h2, 1), F32)],
        compiler_params=_params(1),
    )(ddt_f, ddt_b, da_f, da_b, dt_t, alog_col)


def _chunk_consts(a_r, dt_r, rev):
    ii = lax.broadcasted_iota(jnp.int32, (CHUNK, CHUNK), 0)
    jj = lax.broadcasted_iota(jnp.int32, (CHUNK, CHUNK), 1)
    w = (jj >= ii) if rev else (jj <= ii)
    wt = (ii >= jj) if rev else (ii <= jj)
    wf = w.astype(F32)
    eye = (ii == jj).astype(F32)
    cs_col = lax.dot_general(wf, a_r, NT, precision=HIGHEST, preferred_element_type=F32)
    cs_row = lax.dot_general(a_r, wf, NT, precision=HIGHEST, preferred_element_type=F32)
    tot = jnp.sum(a_r, axis=1, keepdims=True)
    lo_half = lax.broadcasted_iota(jnp.int32, (CHUNK, 128), 1) < HEAD_DIM

    def lanes(col):
        return [jnp.broadcast_to(col[:, j:j + 1], (CHUNK, 128)) for j in range(HEADS_PER_GROUP)]

    def spread(bs):
        return jnp.concatenate([jnp.where(lo_half, bs[2 * k], bs[2 * k + 1]) for k in range(HEADS_PER_GROUP // 2)], axis=1)

    csc = lanes(cs_col)
    cs_x = spread(csc)
    hh = lax.broadcasted_iota(jnp.int32, (HEADS_PER_GROUP, GROUP_X), 0)
    ll = lax.broadcasted_iota(jnp.int32, (HEADS_PER_GROUP, GROUP_X), 1)
    expand = (lax.shift_right_logical(ll, 6) == hh).astype(BF16)
    dt_hi = dt_r.astype(BF16)
    dt_x = (lax.dot_general(dt_hi, expand, TN, preferred_element_type=F32)
            + lax.dot_general((dt_r - _f(dt_hi)).astype(BF16), expand, TN, preferred_element_type=F32))
    tot_x = cs_x[0:1, :] if rev else cs_x[CHUNK - 1:CHUNK, :]
    return w, wt, eye, csc, cs_row, tot, cs_x, dt_x, tot_x


def _hi_lo(v):
    hi = v.astype(BF16)
    return jnp.concatenate([hi, (v - _f(hi)).astype(BF16)], axis=1)


def _head_sums(v, terms=3):
    rr = lax.broadcasted_iota(jnp.int32, (GROUP_X, HEADS_PER_GROUP), 0)
    cc = lax.broadcasted_iota(jnp.int32, (GROUP_X, HEADS_PER_GROUP), 1)
    et = (lax.shift_right_logical(rr, 6) == cc).astype(BF16)
    out, rest = None, v
    for t in range(terms):
        part = rest.astype(BF16)
        if t < terms - 1:
            rest = rest - _f(part)
        p = jnp.dot(part, et, preferred_element_type=F32)
        out = p if out is None else out + p
    return out


def _ssd_fwd(name, xbc, dt_t, a_t, dsk, n_groups):
    s = xbc.shape[0]
    nc = s // CHUNK
    d_ssm = n_groups * GROUP_X

    def one(x_ref, dt_ref, a_ref, dsk_ref, y_ref, hin_ref, h_scr, rev):
        skip = dsk_ref is not None
        xs = _f(x_ref[:, 0:GROUP_X])
        bm = x_ref[:, GROUP_X:GROUP_X + D_STATE]
        cm = x_ref[:, GROUP_X + D_STATE:GROUP_W]
        w, _, _, csc, cs_row, tot, cs_x, dt_x, tot_x = _chunk_consts(a_ref[...], dt_ref[...], rev)
        cb = lax.dot_general(cm, bm, NT, preferred_element_type=F32)
        h = h_scr[...]
        h_b = h.astype(BF16)
        hin_ref[0, 0] = h_b
        xdt = xs * dt_x
        xdt_b = xdt.astype(BF16)
        yo = lax.dot_general(cm, h_b, NT, preferred_element_type=F32) * jnp.exp(cs_x)
        st = lax.dot_general((xdt * jnp.exp(tot_x - cs_x)).astype(BF16), bm, TN, preferred_element_type=F32)
        yds = []
        for j in range(HEADS_PER_GROUP):
            lo = HEAD_DIM * j
            h_scr[lo:lo + HEAD_DIM, :] = h[lo:lo + HEAD_DIM, :] * jnp.exp(tot[j:j + 1, 0:1]) + st[lo:lo + HEAD_DIM, :]
            lm = jnp.exp(jnp.where(w, csc[j] - cs_row[j:j + 1, :], -jnp.inf))
            yds.append(jnp.dot((cb * lm).astype(BF16), xdt_b[:, lo:lo + HEAD_DIM], preferred_element_type=F32))
        y = jnp.concatenate(yds, axis=1) + yo
        if skip:
            y = y + dsk_ref[...] * xs
        y_ref[...] = y.astype(y_ref.dtype)

    def body(xf_ref, dtf_ref, af_ref, xr_ref, dtr_ref, ar_ref, dsk_ref, yf_ref, hf_ref, yr_ref, hr_ref, hf_scr, hr_scr):
        @pl.when(pl.program_id(1) == 0)
        def _():
            hf_scr[...] = jnp.zeros_like(hf_scr)
            hr_scr[...] = jnp.zeros_like(hr_scr)

        one(xf_ref, dtf_ref, af_ref, dsk_ref, yf_ref, hf_ref, hf_scr, False)
        one(xr_ref, dtr_ref, ar_ref, None, yr_ref, hr_ref, hr_scr, True)

    def specs(rev):
        def z(zi):
            return nc - 1 - zi if rev else zi
        t_spec = pl.BlockSpec((HEADS_PER_GROUP, CHUNK), lambda g, zi: ((n_groups if rev else 0) + g, z(zi)))
        ins = [pl.BlockSpec((CHUNK, GROUP_W), lambda g, zi: (z(zi), g)), t_spec, t_spec]
        outs = [pl.BlockSpec((CHUNK, GROUP_X), lambda g, zi: (z(zi), g)),
                pl.BlockSpec((1, 1, GROUP_X, D_STATE), lambda g, zi: (g, z(zi), 0, 0))]
        return ins, outs

    (in_f, out_f), (in_r, out_r) = specs(False), specs(True)
    shapes = [jax.ShapeDtypeStruct((s, d_ssm), BF16), jax.ShapeDtypeStruct((n_groups, nc, GROUP_X, D_STATE), BF16)]
    return pl.pallas_call(
        body, name=name, grid=(n_groups, nc), in_specs=in_f + in_r + [pl.BlockSpec((1, GROUP_X), lambda g, zi: (0, g))],
        out_specs=out_f + out_r, out_shape=shapes + shapes,
        scratch_shapes=[pltpu.VMEM((GROUP_X, D_STATE), F32)] * 2, compiler_params=_params(2),
    )(xbc, dt_t, a_t, xbc, dt_t, a_t, dsk)


def _ssd_bwd(name, xbc, dt_t, a_t, dy, hin_f, hin_r, dsk, n_groups):
    s = xbc.shape[0]
    nc = s // CHUNK
    n_heads = n_groups * HEADS_PER_GROUP

    def one(x_ref, dt_ref, a_ref, dy_ref, hin_ref, dsk_ref, dx_ref, ddt_ref, da_ref, ddsk_ref, g_scr, rev):
        skip = dsk_ref is not None
        xs = _f(x_ref[:, 0:GROUP_X])
        bm = x_ref[:, GROUP_X:GROUP_X + D_STATE]
        cm = x_ref[:, GROUP_X + D_STATE:GROUP_W]
        dyv = _f(dy_ref[...])
        w, wt, eye, csc, cs_row, tot, cs_x, dt_x, tot_x = _chunk_consts(a_ref[...], dt_ref[...], rev)
        cb = lax.dot_general(cm, bm, NT, preferred_element_type=F32)
        cbt = lax.dot_general(bm, cm, NT, preferred_element_type=F32)
        onehot = lax.broadcasted_iota(jnp.int32, (1, HEADS_PER_GROUP), 1)
        hin = hin_ref[0, 0]
        g = g_scr[...]
        g_b = g.astype(BF16)
        din, dte = jnp.exp(cs_x), jnp.exp(tot_x - cs_x)
        xdt = xs * dt_x
        xdt_b, dy_b = xdt.astype(BF16), dyv.astype(BF16)
        dyo_b = (dyv * din).astype(BF16)
        yo = lax.dot_general(cm, hin, NT, preferred_element_type=F32) * din
        dxdt_s = lax.dot_general(bm, g_b, NT, preferred_element_type=F32) * dte
        d_c = jnp.dot(dyo_b, hin, preferred_element_type=F32)
        d_b = jnp.dot((xdt * dte).astype(BF16), g_b, preferred_element_type=F32)
        dhin = lax.dot_general(dyo_b, cm, TN, preferred_element_type=F32)
        dcb = jnp.zeros((CHUNK, CHUNK), F32)
        dtot = jnp.zeros((1, HEADS_PER_GROUP), F32)
        yds, dxds = [], []
        for j in range(HEADS_PER_GROUP):
            lo = HEAD_DIM * j
            cd = jnp.exp(tot[j:j + 1, 0:1])
            gj = g[lo:lo + HEAD_DIM, :]
            dcd = jnp.sum(jnp.sum(gj * _f(hin[lo:lo + HEAD_DIM, :]), axis=1, keepdims=True), axis=0, keepdims=True)
            dtot = dtot + (dcd * cd) * (onehot == j).astype(F32)
            g_scr[lo:lo + HEAD_DIM, :] = dhin[lo:lo + HEAD_DIM, :] + gj * cd
            dd = csc[j] - cs_row[j:j + 1, :]
            lm = jnp.exp(jnp.where(w, dd, -jnp.inf))
            lmt = jnp.exp(jnp.where(wt, -dd, -jnp.inf))
            xj_b, dyj_b = xdt_b[:, lo:lo + HEAD_DIM], dy_b[:, lo:lo + HEAD_DIM]
            yds.append(jnp.dot(_hi_lo(cb * lm), jnp.concatenate([xj_b, xj_b], axis=0), preferred_element_type=F32))
            dxds.append(jnp.dot(_hi_lo(cbt * lmt), jnp.concatenate([dyj_b, dyj_b], axis=0), preferred_element_type=F32))
            dcb = dcb + lax.dot_general(dyj_b, xj_b, NT, preferred_element_type=F32) * lm
        dxdt_d = jnp.concatenate(dxds, axis=1)
        z_state = xdt * dxdt_s
        yd = jnp.concatenate(yds, axis=1)
        dcs = _head_sums(dyv * yo + _f(dy_b) * yd - _f(xdt_b) * dxdt_d - z_state)
        dtot = dtot + _head_sums(jnp.broadcast_to(jnp.sum(z_state, axis=0, keepdims=True), (8, GROUP_X)))[0:1, :]
        dxdt = dxdt_s + dxdt_d
        ddt_c = _head_sums(dxdt * xs, terms=2)
        dx = dxdt * dt_x
        if skip:
            dx = dx + dsk_ref[...] * dyv
        dcb_b = dcb.astype(BF16)
        d_c = d_c + jnp.dot(dcb_b, bm, preferred_element_type=F32)
        d_b = d_b + lax.dot_general(dcb_b, cm, TN, preferred_element_type=F32)
        da_c = jnp.dot(wt.astype(F32), dcs, precision=HIGHEST, preferred_element_type=F32) + dtot
        ddt_ref[...] = lax.dot_general(ddt_c, eye, TN, precision=HIGHEST, preferred_element_type=F32)
        da_ref[...] = lax.dot_general(da_c, eye, TN, precision=HIGHEST, preferred_element_type=F32)
        dx_ref[...] = jnp.concatenate([dx, d_b, d_c], axis=1).astype(dx_ref.dtype)
        if skip:
            ddsk_ref[...] += jnp.sum(dyv * xs, axis=0, keepdims=True)

    def body(xf, dtf, af, dyf, hf, xr, dtr, ar, dyr, hr, dsk_ref, dxf, ddtf, daf, dxr, ddtr, dar, ddsk_ref, gf_scr, gr_scr):
        @pl.when(pl.program_id(1) == 0)
        def _():
            gf_scr[...] = jnp.zeros_like(gf_scr)
            gr_scr[...] = jnp.zeros_like(gr_scr)
            ddsk_ref[...] = jnp.zeros_like(ddsk_ref)

        one(xf, dtf, af, dyf, hf, dsk_ref, dxf, ddtf, daf, ddsk_ref, gf_scr, False)
        one(xr, dtr, ar, dyr, hr, None, dxr, ddtr, dar, None, gr_scr, True)

    def specs(rev):
        def z(zi):
            return zi if rev else nc - 1 - zi
        t_in = pl.BlockSpec((HEADS_PER_GROUP, CHUNK), lambda g, zi: ((n_groups if rev else 0) + g, z(zi)))
        t_out = pl.BlockSpec((HEADS_PER_GROUP, CHUNK), lambda g, zi: (g, z(zi)))
        x_spec = pl.BlockSpec((CHUNK, GROUP_W), lambda g, zi: (z(zi), g))
        ins = [x_spec, t_in, t_in, pl.BlockSpec((CHUNK, GROUP_X), lambda g, zi: (z(zi), g)),
               pl.BlockSpec((1, 1, GROUP_X, D_STATE), lambda g, zi: (g, z(zi), 0, 0))]
        return ins, [x_spec, t_out, t_out]

    (in_f, out_f), (in_r, out_r) = specs(False), specs(True)
    lane_spec = pl.BlockSpec((1, GROUP_X), lambda g, zi: (0, g))
    shapes = [jax.ShapeDtypeStruct((s, n_groups * GROUP_W), BF16), jax.ShapeDtypeStruct((n_heads, s), F32),
              jax.ShapeDtypeStruct((n_heads, s), F32)]
    return pl.pallas_call(
        body, name=name, grid=(n_groups, nc), in_specs=in_f + in_r + [lane_spec], out_specs=out_f + out_r + [lane_spec],
        out_shape=shapes + shapes + [jax.ShapeDtypeStruct((1, n_groups * GROUP_X), F32)],
        scratch_shapes=[pltpu.VMEM((GROUP_X, D_STATE), F32)] * 2, compiler_params=_params(2),
    )(xbc, dt_t, a_t, dy, hin_f, xbc, dt_t, a_t, dy, hin_r, dsk)


def _mesh_pos():
    return lax.axis_index("x"), lax.axis_index("y"), lax.axis_index("c")


def _slot(p):
    return 4 * p[0] + 2 * p[1] + p[2]


def _exchange_sems(n):
    return [pltpu.SemaphoreType.DMA((7 * n,)), pltpu.SemaphoreType.DMA((7 * n,)), pltpu.SemaphoreType.DMA((n,))]


def _gather_ops(ins, outs, send, recv, loc):
    n = len(ins)
    x, y, c = _mesh_pos()
    me, sib = (x, y, c), (x, y, 1 - c)
    chips = [(1 - x, y), (x, 1 - y), (1 - x, 1 - y)]

    def cp(a, k, block, to, src=None):
        dst = outs[a].at[_slot(block)]
        return pltpu.make_async_remote_copy(
            src_ref=dst if src is None else src, dst_ref=dst, send_sem=send.at[7 * a + k], recv_sem=recv.at[7 * a + k],
            device_id=to, device_id_type=MESH)

    mine = [pltpu.make_async_copy(ins[a], outs[a].at[_slot(me)], loc.at[a]) for a in range(n)]
    first = []
    for a in range(n):
        first.append(cp(a, 0, me, sib, src=ins[a]))
        first += [cp(a, 1 + j, me, (*chip, c), src=ins[a]) for j, chip in enumerate(chips)]

    def start():
        for d in mine + first:
            d.start()

    def finish():
        passed = []
        for a in range(n):
            for j, chip in enumerate(chips):
                cp(a, 1 + j, (*chip, c), me).wait_recv()
                p = cp(a, 4 + j, (*chip, c), sib)
                p.start()
                passed.append(p)
        for a in range(n):
            cp(a, 0, sib, me).wait_recv()
            for j, chip in enumerate(chips):
                cp(a, 4 + j, (*chip, 1 - c), me).wait_recv()
        for f in first + passed:
            f.wait_send()
        for m in mine:
            m.wait()

    return start, finish


_gather_ops.out_shapes = lambda arrs: [jax.ShapeDtypeStruct((N_DEV,) + a.shape, a.dtype) for a in arrs]


def _a2a_ops(ins, outs, send, recv, loc):
    n = len(ins)
    me = _mesh_pos()
    flips = [(fx, fy, fc) for fx in (0, 1) for fy in (0, 1) for fc in (0, 1)][1:]
    peers = [tuple(1 - p if fl else p for p, fl in zip(me, fl3)) for fl3 in flips]

    def cp(a, k, peer, dst_slot):
        return pltpu.make_async_remote_copy(
            src_ref=ins[a].at[_slot(peer)], dst_ref=outs[a].at[dst_slot], send_sem=send.at[7 * a + k],
            recv_sem=recv.at[7 * a + k], device_id=peer, device_id_type=MESH)

    mine = [pltpu.make_async_copy(ins[a].at[_slot(me)], outs[a].at[_slot(me)], loc.at[a]) for a in range(n)]
    sends = [cp(a, k, peer, _slot(me)) for a in range(n) for k, peer in enumerate(peers)]

    def start():
        for d in mine + sends:
            d.start()

    def finish():
        for a in range(n):
            for k, peer in enumerate(peers):
                cp(a, k, peer, _slot(peer)).wait_recv()
        for s_ in sends:
            s_.wait_send()
        for m in mine:
            m.wait()

    return start, finish


_a2a_ops.out_shapes = lambda arrs: [jax.ShapeDtypeStruct(a.shape, a.dtype) for a in arrs]


def _exchange(name, ops, arrs):
    n = len(arrs)

    def body(*refs):
        start, finish = ops(refs[:n], refs[n:2 * n], *refs[2 * n:])
        start()
        finish()

    any_spec = pl.BlockSpec(memory_space=pl.ANY)
    return pl.pallas_call(
        body, name=name, in_specs=[any_spec] * n, out_specs=[any_spec] * n, out_shape=ops.out_shapes(arrs),
        scratch_shapes=_exchange_sems(n),
    )(*arrs)


def _all_gather(name, arrs):
    return _exchange(name, _gather_ops, arrs)


def _all_to_all(name, arrs):
    return _exchange(name, _a2a_ops, arrs)


def _sum8(parts_ref):
    g = _f(parts_ref[0])
    for j in range(1, N_DEV):
        g = g + _f(parts_ref[j])
    return g


def _adam_math(w, g, m, v):
    m = ADAM_B1 * m + (1.0 - ADAM_B1) * g
    v = ADAM_B2 * v + (1.0 - ADAM_B2) * (g * g)
    m_hat = m / (1.0 - ADAM_B1 ** ADAM_STEP)
    v_hat = v / (1.0 - ADAM_B2 ** ADAM_STEP)
    return -ADAM_LR * (m_hat / (jnp.sqrt(v_hat) + ADAM_EPS) + ADAM_WD * w), m, v


def _reduce_sum(name, parts):
    _, r, c = parts.shape

    def body(p_ref, o_ref):
        o_ref[...] = _sum8(p_ref)

    return pl.pallas_call(body, name=name, out_shape=jax.ShapeDtypeStruct((r, c), F32))(parts)


def _adam(name, w, g, m, v, parts=None):
    r, c = w.shape
    tr = _tile(r, (128, 64, 32, 16)) if r * c > 2 ** 18 else r
    spec = pl.BlockSpec((tr, c), lambda i: (i, 0))
    summed = parts is not None

    def body(*refs):
        if summed:
            p_ref, w_ref, m_ref, v_ref, g_out, d_out, m_out, v_out = refs
            gv = _sum8(p_ref)
            g_out[...] = gv
        else:
            g_ref, w_ref, m_ref, v_ref, d_out, m_out, v_out = refs
            gv = g_ref[...]
        d, mn, vn = _adam_math(w_ref[...], gv, m_ref[...], v_ref[...])
        d_out[...] = d
        m_out[...] = mn
        v_out[...] = vn

    first = [pl.BlockSpec((N_DEV, tr, c), lambda i: (0, i, 0))] if summed else [spec]
    n_out = 4 if summed else 3
    return pl.pallas_call(
        body, name=name, grid=(r // tr,), in_specs=first + [spec] * 3, out_specs=[spec] * n_out,
        out_shape=[jax.ShapeDtypeStruct((r, c), F32)] * n_out, compiler_params=_params(1),
    )(parts if summed else g, w, m, v)


def _adam_many(name, ws, gs, ms, vs):
    n = len(ws)

    def body(*refs):
        outs = refs[4 * n:]
        for k in range(n):
            d, mn, vn = _adam_math(refs[k][...], refs[n + k][...], refs[2 * n + k][...], refs[3 * n + k][...])
            outs[k][...] = d
            outs[n + k][...] = mn
            outs[2 * n + k][...] = vn

    res = pl.pallas_call(
        body, name=name, out_shape=[jax.ShapeDtypeStruct(w.shape, F32) for w in ws] * 3,
        compiler_params=pltpu.CompilerParams(vmem_limit_bytes=VMEM_LIMIT),
    )(*ws, *gs, *ms, *vs)
    return res[:n], res[n:2 * n], res[2 * n:]


def _slot_offsets(sizes):
    offs, o = [], 0
    for sz in sizes:
        offs.append(o)
        o += -(-sz // 128) * 128
    return offs, o + (-o % 1024)


def _pack_vectors(name, vecs):
    sizes = [v.shape[1] for v in vecs]
    offs, total = _slot_offsets(sizes)

    def body(*refs):
        out = refs[-1]
        out[...] = jnp.zeros_like(out)
        for r, o, sz in zip(refs[:-1], offs, sizes):
            out[:, o:o + sz] = r[...]

    return pl.pallas_call(body, name=name, out_shape=jax.ShapeDtypeStruct((1, total), F32),
                          compiler_params=pltpu.CompilerParams(vmem_limit_bytes=VMEM_LIMIT))(*vecs).reshape(-1, 128)


def _xbc_runs(n_groups, base=0):
    ds, nb = n_groups * GROUP_X, n_groups * D_STATE
    runs = []
    for g in range(n_groups):
        runs += [(base + GROUP_X * g, GROUP_X), (base + ds + D_STATE * g, D_STATE), (base + ds + nb + D_STATE * g, D_STATE)]
    return runs


def _interleave_runs(width, base=0):
    half = width // 2
    runs = []
    for j in range(half // COL_BLOCK):
        runs += [(base + COL_BLOCK * j, COL_BLOCK), (base + half + COL_BLOCK * j, COL_BLOCK)]
    return runs


def _w_in_runs(d, d_ssm, d_xbc, n_heads, n_groups):
    o_dt = d_ssm + d_xbc
    o_glu = o_dt + 2 * n_heads
    return ([(0, d_ssm)] + _xbc_runs(n_groups, d_ssm) + _interleave_runs(2 * d, o_glu) + _interleave_runs(2 * d, o_glu + 2 * d)
            + [(o_dt, 2 * n_heads)])


def _inverse_runs(runs):
    offs = np.concatenate([[0], np.cumsum([l for _, l in runs])])
    order = sorted(range(len(runs)), key=lambda k: runs[k][0])
    return [(int(offs[k]), runs[k][1]) for k in order]


def _permute_cols(name, a, runs):
    merged = []
    for s_, l in runs:
        if merged and merged[-1][0] + merged[-1][1] == s_:
            merged[-1] = (merged[-1][0], merged[-1][1] + l)
        else:
            merged.append((s_, l))
    r, c = a.shape
    c_out = sum(l for _, l in merged)
    tr = r
    while tr % 16 == 0 and tr * max(c, c_out) * a.dtype.itemsize > 4 * 2 ** 20:
        tr //= 2

    def body(a_ref, o_ref):
        o = 0
        for s_, l in merged:
            o_ref[:, o:o + l] = a_ref[:, s_:s_ + l]
            o += l

    return pl.pallas_call(
        body, name=name, grid=(r // tr,), in_specs=[pl.BlockSpec((tr, c), lambda i: (i, 0))],
        out_specs=pl.BlockSpec((tr, c_out), lambda i: (i, 0)), out_shape=jax.ShapeDtypeStruct((r, c_out), a.dtype),
        compiler_params=_params(1),
    )(a)


def _cols_from_shards(g):
    return g.transpose(1, 0, 2).reshape(g.shape[1], -1)


def _cols_to_shards(a):
    return a.reshape(a.shape[0], N_DEV, -1).transpose(1, 0, 2)


def _pad_rows(a, rows):
    return jnp.concatenate([a, jnp.zeros((rows - a.shape[0], a.shape[1]), a.dtype)], axis=0)


def _pack(vecs):
    a = jnp.concatenate(vecs, axis=1)
    return jnp.concatenate([a, jnp.zeros((1, -a.shape[1] % 1024), a.dtype)], axis=1).reshape(-1, 128)


def kernel(x, c, w_ada, b_ada, g_pre_mix, g_post_mix, w_in, w_conv_ssm, b_conv_ssm, dt_bias_fwd, dt_bias_bwd, a_log_fwd, a_log_bwd, d_skip, g_ssm_norm, w_ssm_out, b_glu, w_dw, b_dw, ln_g, ln_b, w_conv_out, b_conv_out, b_gate, w_mix_out, g_pre_ffn, g_post_ffn, w_gate_up, w_down, loss_target, m_w_ada, m_b_ada, m_g_pre_mix, m_g_post_mix, m_w_in, m_w_conv_ssm, m_b_conv_ssm, m_dt_bias_fwd, m_dt_bias_bwd, m_a_log_fwd, m_a_log_bwd, m_d_skip, m_g_ssm_norm, m_w_ssm_out, m_b_glu, m_w_dw, m_b_dw, m_ln_g, m_ln_b, m_w_conv_out, m_b_conv_out, m_b_gate, m_w_mix_out, m_g_pre_ffn, m_g_post_ffn, m_w_gate_up, m_w_down, v_w_ada, v_b_ada, v_g_pre_mix, v_g_post_mix, v_w_in, v_w_conv_ssm, v_b_conv_ssm, v_dt_bias_fwd, v_dt_bias_bwd, v_a_log_fwd, v_a_log_bwd, v_d_skip, v_g_ssm_norm, v_w_ssm_out, v_b_glu, v_w_dw, v_b_dw, v_ln_g, v_ln_b, v_w_conv_out, v_b_conv_out, v_b_gate, v_w_mix_out, v_g_pre_ffn, v_g_post_ffn, v_w_gate_up, v_w_down):
    weights = dict(w_ada=w_ada, b_ada=b_ada, g_pre_mix=g_pre_mix, g_post_mix=g_post_mix, w_in=w_in, w_conv_ssm=w_conv_ssm, b_conv_ssm=b_conv_ssm, dt_bias_fwd=dt_bias_fwd, dt_bias_bwd=dt_bias_bwd, a_log_fwd=a_log_fwd, a_log_bwd=a_log_bwd, d_skip=d_skip, g_ssm_norm=g_ssm_norm, w_ssm_out=w_ssm_out, b_glu=b_glu, w_dw=w_dw, b_dw=b_dw, ln_g=ln_g, ln_b=ln_b, w_conv_out=w_conv_out, b_conv_out=b_conv_out, b_gate=b_gate, w_mix_out=w_mix_out, g_pre_ffn=g_pre_ffn, g_post_ffn=g_post_ffn, w_gate_up=w_gate_up, w_down=w_down)
    mom_m = dict(w_ada=m_w_ada, b_ada=m_b_ada, g_pre_mix=m_g_pre_mix, g_post_mix=m_g_post_mix, w_in=m_w_in, w_conv_ssm=m_w_conv_ssm, b_conv_ssm=m_b_conv_ssm, dt_bias_fwd=m_dt_bias_fwd, dt_bias_bwd=m_dt_bias_bwd, a_log_fwd=m_a_log_fwd, a_log_bwd=m_a_log_bwd, d_skip=m_d_skip, g_ssm_norm=m_g_ssm_norm, w_ssm_out=m_w_ssm_out, b_glu=m_b_glu, w_dw=m_w_dw, b_dw=m_b_dw, ln_g=m_ln_g, ln_b=m_ln_b, w_conv_out=m_w_conv_out, b_conv_out=m_b_conv_out, b_gate=m_b_gate, w_mix_out=m_w_mix_out, g_pre_ffn=m_g_pre_ffn, g_post_ffn=m_g_post_ffn, w_gate_up=m_w_gate_up, w_down=m_w_down)
    mom_v = dict(w_ada=v_w_ada, b_ada=v_b_ada, g_pre_mix=v_g_pre_mix, g_post_mix=v_g_post_mix, w_in=v_w_in, w_conv_ssm=v_w_conv_ssm, b_conv_ssm=v_b_conv_ssm, dt_bias_fwd=v_dt_bias_fwd, dt_bias_bwd=v_dt_bias_bwd, a_log_fwd=v_a_log_fwd, a_log_bwd=v_a_log_bwd, d_skip=v_d_skip, g_ssm_norm=v_g_ssm_norm, w_ssm_out=v_w_ssm_out, b_glu=v_b_glu, w_dw=v_w_dw, b_dw=v_b_dw, ln_g=v_ln_g, ln_b=v_ln_b, w_conv_out=v_w_conv_out, b_conv_out=v_b_conv_out, b_gate=v_b_gate, w_mix_out=v_w_mix_out, g_pre_ffn=v_g_pre_ffn, g_post_ffn=v_g_post_ffn, w_gate_up=v_w_gate_up, w_down=v_w_down)
    names = list(weights)

    s, d = x.shape[1], x.shape[2]
    d_ssm, n_heads, d_xbc = g_ssm_norm.shape[-1], d_skip.shape[-1], b_conv_ssm.shape[-1]
    n_groups = n_heads // HEADS_PER_GROUP
    d_ff = w_down.shape[1] * N_DEV
    k_ssm, k_dw = w_conv_ssm.shape[1], w_dw.shape[1]
    assert d_ssm == n_groups * GROUP_X and d_xbc == n_groups * GROUP_W and s % CHUNK == 0
    assert d % COL_BLOCK == 0 and d_ff % COL_BLOCK == 0
    mc = w_ada.shape[-1]
    me = _slot(_mesh_pos())
    wb = d_ssm + d_xbc + 4 * d
    p_x, p_glu, p_gate = d_ssm, d_ssm + d_xbc, d_ssm + d_xbc + 2 * d
    ncb, nfb = d // COL_BLOCK, d_ff // COL_BLOCK
    x2, tgt = x[0], loss_target[0]

    big = ["w_in", "w_ssm_out", "w_conv_out", "w_mix_out", "w_gate_up", "w_down"]
    (w_in_g,) = _all_gather("gather_w_in", [w_in[0].astype(BF16)])
    w_in_runs, xbc_runs, il_runs = _w_in_runs(d, d_ssm, d_xbc, n_heads, n_groups), _xbc_runs(n_groups), _interleave_runs(2 * d)
    w_in_p = _permute_cols("order_w_in", _cols_from_shards(w_in_g), w_in_runs)
    w_big, w_dt = w_in_p[:, :wb], w_in_p[:, wb:]

    n_cs, n_dw = k_ssm * (d_xbc // N_DEV), k_dw * (d // N_DEV)
    (small_all,) = _all_gather("gather_small", [_pack([c, w_conv_ssm.reshape(1, n_cs), w_dw.reshape(1, n_dw)])])
    small_all = small_all.reshape(N_DEV, -1)
    c_all = small_all[:, :d]
    wcs = _cols_from_shards(small_all[:, d:d + n_cs].reshape(N_DEV, k_ssm, -1))
    wdw = _cols_from_shards(small_all[:, d + n_cs:d + n_cs + n_dw].reshape(N_DEV, k_dw, -1))
    wcs_p = _permute_cols("order_conv_taps", wcs, xbc_runs)
    bcs_p = _permute_cols("order_conv_bias", b_conv_ssm, xbc_runs)
    b_glu_il, b_gate_il = _permute_cols("order_b_glu", b_glu, il_runs), _permute_cols("order_b_gate", b_gate, il_runs)
    kp_ssm, kp_dw = -(-k_ssm // 8) * 8, -(-k_dw // 8) * 8
    zero_x, zero_d = jnp.zeros((1, d_xbc), F32), jnp.zeros((1, d), F32)

    (c_act,), _ = _rowcall("c_act", _fwd_fn(_silu), [(c_all, 0, d)], [], [(d, F32)], [])
    b_my = lax.dynamic_slice(b_ada, (0, me * mc), (1, mc))
    mod_cols = _matmul("mod", c_act, w_ada[0], "nn", F32, acc_in=jnp.broadcast_to(b_my, (N_DEV, mc)))
    (mod_rows,) = _all_to_all("mod_to_owner", [mod_cols.reshape(N_DEV, 1, mc)])
    mod = mod_rows.reshape(1, N_DEV * mc)
    sh1, sc1, g1, sh2, sc2, g2 = [mod[:, k * d:(k + 1) * d] for k in range(6)]

    (h1,), _ = _rowcall("prenorm_mix", _fwd_fn(_t_prenorm), [(x2, 0, d)], [g_pre_mix, sc1, sh1], [(d, BF16)], [])
    proj, rest = _matmul("in_proj", h1, w_big, "nn", BF16, carry=(_gather_ops, [weights[k][0].astype(BF16) for k in big[1:]]))
    gath = dict(zip(big[1:], rest))
    w_ssm, w_co, w_mo = gath["w_ssm_out"].reshape(d_ssm, d), gath["w_conv_out"].reshape(d, d), gath["w_mix_out"].reshape(d, d)
    gu_runs = _interleave_runs(2 * d_ff)
    w_gu = _permute_cols("order_w_gate_up", _cols_from_shards(gath["w_gate_up"]), gu_runs)
    w_dn = gath["w_down"].reshape(d_ff, d)
    dt_raw = _matmul("in_proj_dt", h1, w_dt, "nn", F32)
    xbc_c = _dwconv("conv_ssm", proj, p_x, d_xbc, _pad_rows(wcs_p, kp_ssm), k_ssm, bcs_p, True, BF16)
    alog_col = jnp.concatenate([a_log_fwd, a_log_bwd], axis=1).reshape(2 * n_heads, 1)
    dt_t, a_t = _dtprep("dt_prep", dt_raw, jnp.concatenate([dt_bias_fwd, dt_bias_bwd], axis=1), alog_col)
    dsk = jnp.repeat(d_skip, HEAD_DIM, axis=1)
    y_f, hin_f, y_b, hin_b = _ssd_fwd("ssd", xbc_c, dt_t, a_t, dsk, n_groups)
    gn_rows = [(y_f, 0, d_ssm), (y_b, 0, d_ssm), (proj, 0, d_ssm)]
    (yn,), _ = _rowcall("ssm_norm", _fwd_fn(_t_gnorm), gn_rows, [g_ssm_norm], [(d_ssm, BF16)], [], ncol=n_groups)
    ya = _matmul("ssm_out", yn, w_ssm, "nn", BF16)
    (u0,), _ = _rowcall("glu", _glu_fwd, [(proj, p_glu, 2 * d)], [b_glu_il], [(d, BF16)], [], ncol=ncb)
    wdw_pad = _pad_rows(wdw, kp_dw)
    u1 = _dwconv("conv_dw", u0, 0, d, wdw_pad, k_dw, b_dw, False, BF16)
    (u2,), _ = _rowcall("ln_silu", _fwd_fn(_t_lnsilu), [(u1, 0, d)], [ln_g, ln_b], [(d, BF16)], [])
    yb = _matmul("conv_out", u2, w_co, "nn", BF16)
    gm_rows, gm_prm = [(ya, 0, d), (yb, 0, d), (proj, p_gate, 2 * d)], [b_conv_out, b_gate_il]
    (m_in,), _ = _rowcall("gate_mix", _gatemix_fwd, gm_rows, gm_prm, [(d, BF16)], [], ncol=ncb)
    mix = _matmul("mix_out", m_in, w_mo, "nn", BF16)
    pp_prm = [g_post_mix, g1, g_pre_ffn, sc2, sh2]
    (x1, h2), _ = _rowcall("post_mix", _fwd_fn(_t_postpre), [(x2, 0, d), (mix, 0, d)], pp_prm, [(d, F32), (d, BF16)], [])
    gu, act = _matmul("gate_up", h2, w_gu, "nn", BF16, tn=2 * COL_BLOCK,
                      epilogue=(lambda p: (p, _swiglu_fwd(p)[0][0]), [], [(2 * d_ff, BF16), (d_ff, BF16)]))
    f = _matmul("down", act, w_dn, "nn", BF16)

    def loss_fn(x1_t, f_t, tgt_t, gpost, g2_t):
        args = [_f(x1_t), _f(f_t), _f(tgt_t), _f(gpost), _f(g2_t)]
        val, vjp = jax.vjp(_t_loss, *args)
        gr = vjp(jnp.ones((), F32))
        return (gr[0], gr[1]), (jnp.zeros((1, 128), F32) + val, gr[3], gr[4])

    (dx1a, df), (loss_v, d_gpost2, d_g2) = _rowcall(
        "loss", loss_fn, [(x1, 0, d), (f, 0, d), (tgt, 0, d)], [g_post_ffn, g2], [(d, F32), (d, BF16)], [128, d, d])
    loss = lax.psum(loss_v[0, 0], ("x", "y", "c"))

    dgu = _matmul("down_dx", df, w_dn, "nt", BF16, tn=COL_BLOCK,
                  epilogue=(lambda p, gu_t: _swiglu_bwd(gu_t, p)[0], [gu], [(2 * d_ff, BF16)]))
    dw_dn = _matmul("down_dw", act, df, "tn", BF16)
    recv = {}
    dh2, (recv["w_down"],) = _matmul("gate_up_dx", dgu, w_gu, "nt", BF16, carry=(_a2a_ops, [dw_dn.reshape(N_DEV, -1, d)]))
    dw_gu = _matmul("gate_up_dw", h2, dgu, "tn", BF16)
    (dxa, dmix), (d_gpost1, d_g1, d_gpre2, d_sc2, d_sh2) = _rowcall(
        "post_mix_bwd", _vjp_fn(_t_postpre, 2, 2, (0, 1), (0, 1, 2, 3, 4)),
        [(x2, 0, d), (mix, 0, d), (dx1a, 0, d), (dh2, 0, d)], pp_prm, [(d, F32), (d, BF16)], [d] * 5)
    dm_in = _matmul("mix_out_dx", dmix, w_mo, "nt", BF16)
    dw_mo = _matmul("mix_out_dw", m_in, dmix, "tn", BF16)
    dproj = lax.empty((s, wb), BF16)
    (dya, dyb, dproj), (d_bco, d_bgate) = _rowcall(
        "gate_mix_bwd", _gatemix_bwd, gm_rows + [(dm_in, 0, d)], gm_prm,
        [(d, BF16), (d, BF16), (2 * d, BF16, dproj, p_gate)], [d, 2 * d], ncol=ncb)
    du2 = _matmul("conv_out_dx", dyb, w_co, "nt", BF16)
    dw_co = _matmul("conv_out_dw", u2, dyb, "tn", BF16)
    (du1,), (d_lng, d_lnb) = _rowcall("ln_silu_bwd", _vjp_fn(_t_lnsilu, 1, 1, (0,), (0, 1)), [(u1, 0, d), (du2, 0, d)],
                                      [ln_g, ln_b], [(d, BF16)], [d, d])
    dwdw_p, d_bdw = _dwconv_bwd_w("conv_dw_dw", u0, 0, d, wdw_pad, k_dw, b_dw, [du1], False)
    du0 = _dwconv("conv_dw_dx", du1, 0, d, _pad_rows(wdw[::-1], kp_dw), k_dw, zero_d, False, BF16)
    (dproj,), (d_bglu,) = _rowcall("glu_bwd", _glu_bwd, [(proj, p_glu, 2 * d), (du0, 0, d)], [b_glu_il],
                                   [(2 * d, BF16, dproj, p_glu)], [2 * d], ncol=ncb)
    dyn, (recv["w_mix_out"], recv["w_conv_out"]) = _matmul(
        "ssm_out_dx", dya, w_ssm, "nt", BF16, carry=(_a2a_ops, [dw_mo.reshape(N_DEV, -1, d), dw_co.reshape(N_DEV, -1, d)]))
    dw_ssm = _matmul("ssm_out_dw", yn, dya, "tn", BF16)
    (dy_ssd, dproj), (d_gn,) = _rowcall("ssm_norm_bwd", _vjp_fn(_t_gnorm, 3, 1, (0, 2), (0,)), gn_rows + [(dyn, 0, d_ssm)],
                                        [g_ssm_norm], [(d_ssm, BF16), (d_ssm, BF16, dproj, 0)], [d_ssm], ncol=n_groups)
    dxbc_f, ddt_f, da_f, dxbc_b, ddt_b, da_b, ddsk = _ssd_bwd("ssd_bwd", xbc_c, dt_t, a_t, dy_ssd, hin_f, hin_b, dsk, n_groups)
    ddt_raw, d_dtb, d_alog = _dtprep_bwd("dt_prep_bwd", ddt_f, ddt_b, da_f, da_b, dt_t, alog_col)
    dpre, dwcs_p, dbcs_p = _dwconv_bwd_w("conv_ssm_dw", proj, p_x, d_xbc, _pad_rows(wcs_p, kp_ssm), k_ssm, bcs_p,
                                          [dxbc_f, dxbc_b], True)
    dproj = _dwconv("conv_ssm_dx", dpre, 0, d_xbc, _pad_rows(wcs_p[::-1], kp_ssm), k_ssm, zero_x, False, BF16, into=(dproj, p_x))
    dw_big, (recv["w_gate_up"], recv["w_ssm_out"]) = _matmul(
        "in_proj_dw", h1, dproj, "tn", BF16,
        carry=(_a2a_ops, [_cols_to_shards(_permute_cols("restore_dw_gate_up", dw_gu, _inverse_runs(gu_runs))),
                          dw_ssm.reshape(N_DEV, -1, d)]))
    dw_dt = _matmul("in_proj_dt_dw", h1, ddt_raw, "tn", BF16)
    dw_in = _permute_cols("restore_dw_in", jnp.concatenate([dw_big, dw_dt], axis=1), _inverse_runs(w_in_runs))
    dh1, (recv["w_in"],) = _matmul("in_proj_dx", dproj, w_big, "nt", F32, carry=(_a2a_ops, [_cols_to_shards(dw_in)]))
    dh1 = _matmul("in_proj_dt_dx", ddt_raw, w_dt, "nt", BF16, acc_in=dh1)

    def prenorm_bwd_fn(x_t, dh_t, dxa_t, g, sc, sh):
        rows, prm = _vjp_fn(_t_prenorm, 1, 1, (0,), (0, 1, 2))(x_t, dh_t, g, sc, sh)
        return (rows[0] + _f(dxa_t),), prm

    (grad_x,), (d_gpre1, d_sc1, d_sh1) = _rowcall(
        "prenorm_mix_bwd", prenorm_bwd_fn, [(x2, 0, d), (dh1, 0, d), (dxa, 0, d)], [g_pre_mix, sc1, sh1], [(d, F32)], [d] * 3)

    dmod = jnp.concatenate([d_sh1, d_sc1, d_g1, d_sh2, d_sc2, d_g2], axis=1)
    small_g = {
        "b_ada": dmod, "g_pre_mix": d_gpre1, "g_post_mix": d_gpost1,
        "b_conv_ssm": _permute_cols("restore_db_conv", dbcs_p, _inverse_runs(xbc_runs)),
        "dt_bias_fwd": d_dtb[:n_heads].reshape(1, n_heads), "dt_bias_bwd": d_dtb[n_heads:].reshape(1, n_heads),
        "a_log_fwd": d_alog[:n_heads].reshape(1, n_heads), "a_log_bwd": d_alog[n_heads:].reshape(1, n_heads),
        "d_skip": ddsk.reshape(n_heads, HEAD_DIM).sum(axis=1).reshape(1, n_heads), "g_ssm_norm": d_gn,
        "b_glu": _permute_cols("restore_db_glu", d_bglu, _inverse_runs(il_runs)), "b_dw": d_bdw, "ln_g": d_lng, "ln_b": d_lnb,
        "b_conv_out": d_bco, "b_gate": _permute_cols("restore_db_gate", d_bgate, _inverse_runs(il_runs)),
        "g_pre_ffn": d_gpre2, "g_post_ffn": d_gpost2,
        "w_conv_ssm": _permute_cols("restore_dw_conv", dwcs_p, _inverse_runs(xbc_runs))[:k_ssm].reshape(1, k_ssm * d_xbc),
        "w_dw": dwdw_p[:k_dw].reshape(1, k_dw * d),
    }
    small_names = list(small_g)
    sizes = [small_g[k].shape[1] for k in small_names]
    offs, _ = _slot_offsets(sizes)
    (small_parts,) = _all_gather("gather_small_grads", [_pack_vectors("pack_small_grads", [small_g[k] for k in small_names])])
    dmod_all = small_parts.reshape(N_DEV, -1)[:, :6 * d]
    small_tot = _reduce_sum("sum_small_grads", small_parts).reshape(1, -1)
    grads = {k: small_tot[:, offs[i]:offs[i] + sizes[i]] for i, k in enumerate(small_names)}
    grads["w_conv_ssm"] = lax.dynamic_slice(grads["w_conv_ssm"].reshape(k_ssm, d_xbc), (0, me * (d_xbc // N_DEV)), (k_ssm, d_xbc // N_DEV))
    grads["w_dw"] = lax.dynamic_slice(grads["w_dw"].reshape(k_dw, d), (0, me * (d // N_DEV)), (k_dw, d // N_DEV))
    grads["w_ada"] = _matmul("mod_dw", c_act, lax.dynamic_slice(dmod_all, (0, me * mc), (N_DEV, mc)), "tn", F32, highest=True)

    delta, new_m, new_v = {}, {}, {}
    for k in big:
        grads[k], delta[k], new_m[k], new_v[k] = _adam("adam_" + k, weights[k][0], None, mom_m[k][0], mom_v[k][0], parts=recv[k])
    for k in ["w_ada", "w_conv_ssm", "w_dw"]:
        delta[k], new_m[k], new_v[k] = _adam("adam_" + k, weights[k][0], grads[k], mom_m[k][0], mom_v[k][0])
    rep = [k for k in names if k not in big and k not in ("w_ada", "w_conv_ssm", "w_dw")]
    r_delta, r_m, r_v = _adam_many("adam_small", *[[t[k] for k in rep] for t in (weights, grads, mom_m, mom_v)])
    for i, k in enumerate(rep):
        delta[k], new_m[k], new_v[k] = r_delta[i], r_m[i], r_v[i]

    def shaped(t, k):
        return t[k].reshape(weights[k].shape)

    return (loss, grad_x.reshape(x.shape), *[shaped(grads, k) for k in names], *[shaped(delta, k) for k in names],
            *[shaped(new_m, k) for k in names], *[shaped(new_v, k) for k in names])
```

```python
import functools

import numpy as np
import jax
import jax.numpy as jnp
from jax import lax
from jax.experimental import pallas as pl
from jax.experimental.pallas import tpu as pltpu

F32 = jnp.float32
BF16 = jnp.bfloat16
HIGHEST = lax.Precision.HIGHEST
MESH = pl.DeviceIdType.MESH
N_DEV = 8
EPS = 1e-6
CHUNK = 128
HEAD_DIM = 64
D_STATE = 128
HEADS_PER_GROUP = 8
GROUP_X = HEADS_PER_GROUP * HEAD_DIM
GROUP_W = GROUP_X + 2 * D_STATE
COL_BLOCK = 512
HALO = 16
VMEM_LIMIT = 56 * 1024 * 1024
ADAM_LR, ADAM_B1, ADAM_B2, ADAM_EPS, ADAM_WD, ADAM_STEP = 0.001, 0.9, 0.999, 1e-08, 0.01, 10

NN = (((1,), (0,)), ((), ()))
NT = (((1,), (1,)), ((), ()))
TN = (((0,), (0,)), ((), ()))


def _tile(dim, prefs):
    for p in prefs:
        if p <= dim and dim % p == 0:
            return p
    return dim


def _params(n_grid):
    return pltpu.CompilerParams(dimension_semantics=("arbitrary",) * n_grid, vmem_limit_bytes=VMEM_LIMIT)


def _f(v):
    return v.astype(F32)


def _matmul(name, a, b, mode, out_dtype, acc_in=None, highest=False, carry=None, tn=None, epilogue=None):
    if mode == "nn":
        (m, k), (k2, n) = a.shape, b.shape
    elif mode == "nt":
        (m, k), (n, k2) = a.shape, b.shape
    else:
        (k, m), (k2, n) = a.shape, b.shape
    assert k == k2, (name, a.shape, b.shape, mode)
    tm = _tile(m, (1024, 512, 256, 128))
    tn = _tile(n, (1024, 512, 256, 128)) if tn is None else tn
    tk = _tile(k, (2048, 1024, 512, 256, 128))
    nk, nj = k // tk, n // tn
    assert n % tn == 0
    dims = {"nn": NN, "nt": NT, "tn": TN}[mode]
    a_spec = pl.BlockSpec((tk, tm), lambda i, j, kk: (kk, i)) if mode == "tn" else pl.BlockSpec((tm, tk), lambda i, j, kk: (i, kk))
    b_spec = pl.BlockSpec((tn, tk), lambda i, j, kk: (j, kk)) if mode == "nt" else pl.BlockSpec((tk, tn), lambda i, j, kk: (kk, j))
    o_spec = pl.BlockSpec((tm, tn), lambda i, j, kk: (i, j))
    has_acc = acc_in is not None
    epi_fn, e_arrs, e_outs = epilogue if epilogue is not None else (None, [], [])
    ne = len(e_arrs)
    n_in = (3 if has_acc else 2) + ne
    if epilogue is None:
        o_specs, o_shapes = [o_spec], [jax.ShapeDtypeStruct((m, n), out_dtype)]
    else:
        o_specs = [pl.BlockSpec((tm, w // nj), lambda i, j, kk: (i, j)) for w, _ in e_outs]
        o_shapes = [jax.ShapeDtypeStruct((m, w), dt) for w, dt in e_outs]
    e_specs = [pl.BlockSpec((tm, arr.shape[1] // nj), lambda i, j, kk: (i, j)) for arr in e_arrs]
    n_out = len(o_specs)
    ops, c_arrs = carry if carry is not None else (None, [])
    nc_ = len(c_arrs)
    grid = (m // tm, nj, nk)

    def body(*refs):
        a_ref, b_ref = refs[0], refs[1]
        c_ref = refs[2] if has_acc else None
        e_refs = refs[n_in - ne:n_in]
        c_ins = refs[n_in:n_in + nc_]
        o_refs = refs[n_in + nc_:n_in + nc_ + n_out]
        c_outs = refs[n_in + nc_ + n_out:n_in + 2 * nc_ + n_out]
        acc = refs[n_in + 2 * nc_ + n_out]
        i, j, kk = pl.program_id(0), pl.program_id(1), pl.program_id(2)
        if nc_:
            start, finish = ops(c_ins, c_outs, *refs[n_in + 2 * nc_ + n_out + 1:])
            pl.when((i == 0) & (j == 0) & (kk == 0))(start)

        def write(val):
            if epi_fn is None:
                o_refs[0][...] = val.astype(out_dtype)
            else:
                for ref, v in zip(o_refs, epi_fn(val, *[r[...] for r in e_refs])):
                    ref[...] = v.astype(ref.dtype)

        def product(first):
            if highest:
                p = lax.dot_general(_f(a_ref[...]), _f(b_ref[...]), dims, precision=HIGHEST, preferred_element_type=F32)
            else:
                p = lax.dot_general(a_ref[...].astype(BF16), b_ref[...].astype(BF16), dims, preferred_element_type=F32)
            return p + _f(c_ref[...]) if (first and has_acc) else p

        if nk == 1:
            write(product(True))
        else:
            @pl.when(kk == 0)
            def _():
                acc[...] = product(True)

            if nk > 2:
                @pl.when((kk > 0) & (kk < nk - 1))
                def _():
                    acc[...] += product(False)

            @pl.when(kk == nk - 1)
            def _():
                write(acc[...] + product(False))

        if nc_:
            pl.when((i == grid[0] - 1) & (j == grid[1] - 1) & (kk == nk - 1))(finish)

    any_spec = pl.BlockSpec(memory_space=pl.ANY)
    c_shapes = ops.out_shapes(c_arrs) if nc_ else []
    res = pl.pallas_call(
        body, name=name, grid=grid,
        in_specs=[a_spec, b_spec] + ([o_spec] if has_acc else []) + e_specs + [any_spec] * nc_,
        out_specs=o_specs + [any_spec] * nc_, out_shape=o_shapes + c_shapes,
        scratch_shapes=[pltpu.VMEM((tm, tn) if nk > 1 else (8, 128), F32)] + (_exchange_sems(nc_) if nc_ else []),
        compiler_params=_params(3),
    )(*([a, b] + ([acc_in] if has_acc else []) + list(e_arrs) + list(c_arrs)))
    outs = res[0] if n_out == 1 else res[:n_out]
    return (outs, res[n_out:]) if nc_ else outs


def _rowcall(name, fn, rows, params, out_rows, out_accs, ncol=1):
    s = rows[0][0].shape[0]
    tm = _tile(s, (256 if ncol == 1 else 1024, 128, 64, 32, 16, 8))
    args, in_specs = [], []
    for arr, off, w in rows:
        bw = w // ncol
        assert w % ncol == 0 and off % bw == 0, (name, off, w, ncol)
        in_specs.append(pl.BlockSpec((tm, bw), functools.partial(lambda j, i, ob: (i, ob + j), ob=off // bw)))
        args.append(arr)
    for p in params:
        assert p.shape[1] % ncol == 0, (name, p.shape)
        in_specs.append(pl.BlockSpec((p.shape[0], p.shape[1] // ncol), lambda j, i: (0, j)))
        args.append(p)
    n_in, n_ro = len(args), len(out_rows)
    out_shape, out_specs, aliases = [], [], {}
    for k, spec in enumerate(out_rows):
        w, dt = spec[0], spec[1]
        bw = w // ncol
        if len(spec) == 4:
            buf, off = spec[2], spec[3]
            assert off % bw == 0 and buf.dtype == dt, (name, off, bw)
            out_shape.append(jax.ShapeDtypeStruct(buf.shape, dt))
            out_specs.append(pl.BlockSpec((tm, bw), functools.partial(lambda j, i, ob: (i, ob + j), ob=off // bw)))
            aliases[len(args)] = k
            in_specs.append(pl.BlockSpec(memory_space=pl.ANY))
            args.append(buf)
        else:
            out_shape.append(jax.ShapeDtypeStruct((s, w), dt))
            out_specs.append(pl.BlockSpec((tm, bw), lambda j, i: (i, j)))
    for w in out_accs:
        out_shape.append(jax.ShapeDtypeStruct((1, w), F32))
        out_specs.append(pl.BlockSpec((1, w // ncol), lambda j, i: (0, j)))
    n_args = len(args)

    def body(*refs):
        i = pl.program_id(1)
        ro, ao = fn(*[r[...] for r in refs[:n_in]])
        outs = refs[n_args:]
        assert len(ro) == n_ro and len(ao) == len(out_accs), name
        for ref, v in zip(outs[:n_ro], ro):
            ref[...] = v.astype(ref.dtype)
        for ref, v in zip(outs[n_ro:], ao):
            @pl.when(i == 0)
            def _(ref=ref, v=v):
                ref[...] = v

            @pl.when(i > 0)
            def _(ref=ref, v=v):
                ref[...] += v

    res = pl.pallas_call(
        body, name=name, grid=(ncol, s // tm), in_specs=in_specs, out_specs=out_specs,
        out_shape=out_shape, input_output_aliases=aliases, compiler_params=_params(2),
    )(*args)
    return res[:n_ro], res[n_ro:]


def _fwd_fn(f):
    def fn(*vals):
        out = f(*[_f(v) for v in vals])
        return (out if isinstance(out, tuple) else (out,)), ()
    return fn


def _vjp_fn(f, n_prim, n_ct, want_rows, want_params):
    def fn(*vals):
        prims = [_f(v) for v in vals[:n_prim]]
        cts = [_f(v) for v in vals[n_prim:n_prim + n_ct]]
        prms = [_f(v) for v in vals[n_prim + n_ct:]]
        out, vjp = jax.vjp(f, *prims, *prms)
        g = vjp(tuple(cts) if isinstance(out, tuple) else cts[0])
        return tuple(g[k] for k in want_rows), tuple(g[n_prim + k] for k in want_params)
    return fn


def _rms(v):
    return v * lax.rsqrt(jnp.mean(v * v, axis=-1, keepdims=True) + EPS)


def _silu(v):
    return v * jax.nn.sigmoid(v)


def _t_prenorm(x, g, sc, sh):
    return _rms(x) * g * (1.0 + sc) + sh


def _t_gnorm(yf, yb, z, gn):
    return _rms((yf + yb) * _silu(z)) * gn


def _t_glu(a, b, ba, bb):
    return (a + ba) * jax.nn.sigmoid(b + bb)


def _t_lnsilu(u, g, b):
    uc = u - jnp.mean(u, axis=-1, keepdims=True)
    return _silu(uc * lax.rsqrt(jnp.mean(uc * uc, axis=-1, keepdims=True) + EPS) * g + b)


def _t_gatemix(ya, yb, la, lb, bco, bga, bgb):
    return jax.nn.sigmoid(la + bga) * ya + jax.nn.sigmoid(lb + bgb) * (yb + bco)


def _t_postpre(x, mix, gpost, g1, gpre, sc, sh):
    x1 = x + g1 * (_rms(mix) * gpost)
    return x1, _rms(x1) * gpre * (1.0 + sc) + sh


def _t_swiglu(gt, up):
    return _silu(gt) * up


def _halves(v):
    return _f(v[:, :COL_BLOCK]), _f(v[:, COL_BLOCK:])


def _glu_fwd(ab, bias):
    return (_t_glu(*_halves(ab), *_halves(bias)),), ()


def _glu_bwd(ab, du, bias):
    _, vjp = jax.vjp(_t_glu, *_halves(ab), *_halves(bias))
    da, db, dba, dbb = vjp(_f(du))
    return (jnp.concatenate([da, db], axis=1),), (jnp.concatenate([dba, dbb], axis=1),)


def _gatemix_fwd(ya, yb, logits, bco, bg):
    return (_t_gatemix(_f(ya), _f(yb), *_halves(logits), _f(bco), *_halves(bg)),), ()


def _gatemix_bwd(ya, yb, logits, dm, bco, bg):
    _, vjp = jax.vjp(_t_gatemix, _f(ya), _f(yb), *_halves(logits), _f(bco), *_halves(bg))
    dya, dyb, dla, dlb, dbco, dbga, dbgb = vjp(_f(dm))
    return (dya, dyb, jnp.concatenate([dla, dlb], axis=1)), (dbco, jnp.concatenate([dbga, dbgb], axis=1))


def _swiglu_fwd(gu):
    return (_t_swiglu(*_halves(gu)),), ()


def _swiglu_bwd(gu, dact):
    _, vjp = jax.vjp(_t_swiglu, *_halves(gu))
    return (jnp.concatenate(vjp(_f(dact)), axis=1),), ()


def _t_loss(x1, f, tgt, gpost, g2):
    e = x1 + g2 * (_rms(f) * gpost) - tgt
    return 0.5 * jnp.sum(jnp.mean(e * e, axis=-1))


def _conv_specs(s, tm, cw, ob, w_rows):
    nh, last_h = tm // HALO, s // HALO - 1
    return [
        pl.BlockSpec((tm, cw), lambda j, i: (i, ob + j)),
        pl.BlockSpec((HALO, cw), lambda j, i: (jnp.maximum(i * nh - 1, 0), ob + j)),
        pl.BlockSpec((HALO, cw), lambda j, i: (jnp.minimum((i + 1) * nh, last_h), ob + j)),
        pl.BlockSpec((w_rows, cw), lambda j, i: (0, j)),
        pl.BlockSpec((1, cw), lambda j, i: (0, j)),
    ]


def _window(x_ref, p_ref, n_ref, i, nrow):
    prev = _f(p_ref[...]) * jnp.where(i > 0, 1.0, 0.0).astype(F32)
    nxt = _f(n_ref[...]) * jnp.where(i < nrow - 1, 1.0, 0.0).astype(F32)
    return jnp.concatenate([prev, _f(x_ref[...]), nxt], axis=0)


def _shifted(win, k_taps, tm):
    pad, moved, out = (k_taps - 1) // 2, {}, []
    for k in range(k_taps):
        q, r = divmod(HALO + k - pad, 8)
        if r not in moved:
            moved[r] = pltpu.roll(win, win.shape[0] - r, 0) if r else win
        out.append(moved[r][8 * q:8 * q + tm, :])
    return out


def _taps(win, w_ref, k_taps, tm):
    sh = _shifted(win, k_taps, tm)
    acc = w_ref[0:1, :] * sh[0]
    for k in range(1, k_taps):
        acc = acc + w_ref[k:k + 1, :] * sh[k]
    return acc


def _conv_rows(k_taps):
    return 1024 if k_taps <= 8 else 512


def _dwconv(name, x, xoff, c, w_pad, k_taps, b, act, out_dtype, cw=256, into=None):
    s = x.shape[0]
    tm, cw = _tile(s, (_conv_rows(k_taps), 256, 128, 64, 32, 16)), _tile(c, (cw, 128))
    assert xoff % cw == 0 and tm % HALO == 0 and (k_taps - 1) // 2 <= HALO
    nrow = s // tm

    def body(x_ref, p_ref, n_ref, w_ref, b_ref, *rest):
        win = _window(x_ref, p_ref, n_ref, pl.program_id(1), nrow)
        acc = _taps(win, w_ref, k_taps, tm) + b_ref[...]
        rest[-1][...] = (_silu(acc) if act else acc).astype(out_dtype)

    in_specs, args = _conv_specs(s, tm, cw, xoff // cw, w_pad.shape[0]), [x, x, x, w_pad, b]
    if into is None:
        ob, out_shape, aliases = 0, jax.ShapeDtypeStruct((s, c), out_dtype), {}
    else:
        buf, first = into
        assert first % cw == 0 and buf.dtype == out_dtype
        ob, out_shape, aliases = first // cw, jax.ShapeDtypeStruct(buf.shape, out_dtype), {len(args): 0}
        in_specs, args = in_specs + [pl.BlockSpec(memory_space=pl.ANY)], args + [buf]
    return pl.pallas_call(
        body, name=name, grid=(c // cw, nrow), in_specs=in_specs,
        out_specs=pl.BlockSpec((tm, cw), lambda j, i: (i, ob + j)), out_shape=out_shape,
        input_output_aliases=aliases, compiler_params=_params(2),
    )(*args)


def _dwconv_bwd_w(name, x, xoff, c, w_pad, k_taps, b, douts, act, cw=256):
    s = x.shape[0]
    tm, cw = _tile(s, (_conv_rows(k_taps), 256, 128, 64, 32, 16)), _tile(c, (cw, 128))
    assert xoff % cw == 0 and tm % HALO == 0
    nrow, pad, n_d = s // tm, (k_taps - 1) // 2, len(douts)

    def body(x_ref, p_ref, n_ref, w_ref, b_ref, *rest):
        d_refs, outs = rest[:n_d], rest[n_d:]
        i = pl.program_id(1)
        win = _window(x_ref, p_ref, n_ref, i, nrow)
        d = _f(d_refs[0][...])
        for r in d_refs[1:]:
            d = d + _f(r[...])
        if act:
            shifted = _shifted(win, k_taps, tm)
        else:
            shifted = [win[HALO + k - pad:HALO + k - pad + tm, :] for k in range(k_taps)]
        if act:
            pre = b_ref[...] + w_ref[0:1, :] * shifted[0]
            for k in range(1, k_taps):
                pre = pre + w_ref[k:k + 1, :] * shifted[k]
            sg = jax.nn.sigmoid(pre)
            d = d * (sg * (1.0 + pre * (1.0 - sg)))
            outs[0][...] = d.astype(outs[0].dtype)
        dw_ref, db_ref = outs[-2], outs[-1]

        @pl.when(i == 0)
        def _():
            dw_ref[...] = jnp.zeros_like(dw_ref)
            db_ref[...] = jnp.zeros_like(db_ref)

        for k in range(k_taps):
            dw_ref[k:k + 1, :] += jnp.sum(d * shifted[k], axis=0, keepdims=True)
        db_ref[...] += jnp.sum(d, axis=0, keepdims=True)

    row_spec = pl.BlockSpec((tm, cw), lambda j, i: (i, j))
    out_shape = [jax.ShapeDtypeStruct((w_pad.shape[0], c), F32), jax.ShapeDtypeStruct((1, c), F32)]
    out_specs = [pl.BlockSpec((w_pad.shape[0], cw), lambda j, i: (0, j)), pl.BlockSpec((1, cw), lambda j, i: (0, j))]
    if act:
        out_shape, out_specs = [jax.ShapeDtypeStruct((s, c), BF16)] + out_shape, [row_spec] + out_specs
    return pl.pallas_call(
        body, name=name, grid=(c // cw, nrow),
        in_specs=_conv_specs(s, tm, cw, xoff // cw, w_pad.shape[0]) + [row_spec] * n_d,
        out_specs=out_specs, out_shape=out_shape, compiler_params=_params(2),
    )(x, x, x, w_pad, b, *douts)


def _dtprep(name, dt_raw, bias, alog_col):
    s, h2 = dt_raw.shape
    ts = _tile(s, (512, 256, 128))

    def body(r_ref, b_ref, al_ref, dt_ref, a_ref):
        v = r_ref[...] + b_ref[...]
        dt = (jnp.maximum(v, 0.0) + jnp.log(1.0 + jnp.exp(-jnp.abs(v)))).T
        dt_ref[...] = dt
        a_ref[...] = dt * (-jnp.exp(al_ref[...]))

    t_spec = pl.BlockSpec((h2, ts), lambda i: (0, i))
    return pl.pallas_call(
        body, name=name, grid=(s // ts,),
        in_specs=[pl.BlockSpec((ts, h2), lambda i: (i, 0)), pl.BlockSpec((1, h2), lambda i: (0, 0)), pl.BlockSpec((h2, 1), lambda i: (0, 0))],
        out_specs=[t_spec, t_spec], out_shape=[jax.ShapeDtypeStruct((h2, s), F32)] * 2, compiler_params=_params(1),
    )(dt_raw, bias, alog_col)


def _dtprep_bwd(name, ddt_f, ddt_b, da_f, da_b, dt_t, alog_col):
    h2, s = dt_t.shape
    h = h2 // 2
    ts = _tile(s, (512, 256, 128))

    def body(ddf, ddb, daf, dab, dt_ref, al_ref, raw_ref, db_ref, dal_ref):
        i = pl.program_id(0)
        a_col = -jnp.exp(al_ref[...])
        da = jnp.concatenate([daf[...], dab[...]], axis=0)
        dt = dt_ref[...]
        d_raw = (jnp.concatenate([ddf[...], ddb[...]], axis=0) + da * a_col) * (1.0 - jnp.exp(-dt))
        raw_ref[...] = d_raw.T

        @pl.when(i == 0)
        def _():
            db_ref[...] = jnp.zeros_like(db_ref)
            dal_ref[...] = jnp.zeros_like(dal_ref)

        db_ref[...] += jnp.sum(d_raw, axis=1, keepdims=True)
        dal_ref[...] += jnp.sum(da * dt, axis=1, keepdims=True) * a_col

    half = pl.BlockSpec((h, ts), lambda i: (0, i))
    col = pl.BlockSpec((h2, 1), lambda i: (0, 0))
    return pl.pallas_call(
        body, name=name, grid=(s // ts,),
        in_specs=[half, half, half, half, pl.BlockSpec((h2, ts), lambda i: (0, i)), col],
        out_specs=[pl.BlockSpec((ts, h2), lambda i: (i, 0)), col, col],
        out_shape=[jax.ShapeDtypeStruct((s, h2), F32), jax.ShapeDtypeStruct((h2, 1), F32), jax.ShapeDtypeStruct((h2, 1), F32)],
        compiler_params=_params(1),
    )(ddt_f, ddt_b, da_f, da_b, dt_t, alog_col)


def _chunk_consts(a_r, dt_r, rev):
    ii = lax.broadcasted_iota(jnp.int32, (CHUNK, CHUNK), 0)
    jj = lax.broadcasted_iota(jnp.int32, (CHUNK, CHUNK), 1)
    w = (jj >= ii) if rev else (jj <= ii)
    wt = (ii >= jj) if rev else (ii <= jj)
    wf = w.astype(F32)
    eye = (ii == jj).astype(F32)
    cs_col = lax.dot_general(wf, a_r, NT, precision=HIGHEST, preferred_element_type=F32)
    cs_row = lax.dot_general(a_r, wf, NT, precision=HIGHEST, preferred_element_type=F32)
    tot = jnp.sum(a_r, axis=1, keepdims=True)
    lo_half = lax.broadcasted_iota(jnp.int32, (CHUNK, 128), 1) < HEAD_DIM

    def lanes(col):
        return [jnp.broadcast_to(col[:, j:j + 1], (CHUNK, 128)) for j in range(HEADS_PER_GROUP)]

    def spread(bs):
        return jnp.concatenate([jnp.where(lo_half, bs[2 * k], bs[2 * k + 1]) for k in range(HEADS_PER_GROUP // 2)], axis=1)

    csc = lanes(cs_col)
    cs_x = spread(csc)
    hh = lax.broadcasted_iota(jnp.int32, (HEADS_PER_GROUP, GROUP_X), 0)
    ll = lax.broadcasted_iota(jnp.int32, (HEADS_PER_GROUP, GROUP_X), 1)
    expand = (lax.shift_right_logical(ll, 6) == hh).astype(BF16)
    dt_hi = dt_r.astype(BF16)
    dt_x = (lax.dot_general(dt_hi, expand, TN, preferred_element_type=F32)
            + lax.dot_general((dt_r - _f(dt_hi)).astype(BF16), expand, TN, preferred_element_type=F32))
    tot_x = cs_x[0:1, :] if rev else cs_x[CHUNK - 1:CHUNK, :]
    return w, wt, eye, csc, cs_row, tot, cs_x, dt_x, tot_x


def _hi_lo(v):
    hi = v.astype(BF16)
    return jnp.concatenate([hi, (v - _f(hi)).astype(BF16)], axis=1)


def _head_sums(v, terms=3):
    rr = lax.broadcasted_iota(jnp.int32, (GROUP_X, HEADS_PER_GROUP), 0)
    cc = lax.broadcasted_iota(jnp.int32, (GROUP_X, HEADS_PER_GROUP), 1)
    et = (lax.shift_right_logical(rr, 6) == cc).astype(BF16)
    out, rest = None, v
    for t in range(terms):
        part = rest.astype(BF16)
        if t < terms - 1:
            rest = rest - _f(part)
        p = jnp.dot(part, et, preferred_element_type=F32)
        out = p if out is None else out + p
    return out


def _ssd_fwd(name, xbc, dt_t, a_t, dsk, n_groups):
    s = xbc.shape[0]
    nc = s // CHUNK
    d_ssm = n_groups * GROUP_X

    def one(x_ref, dt_ref, a_ref, dsk_ref, y_ref, hin_ref, h_scr, rev):
        skip = dsk_ref is not None
        xs = _f(x_ref[:, 0:GROUP_X])
        bm = x_ref[:, GROUP_X:GROUP_X + D_STATE]
        cm = x_ref[:, GROUP_X + D_STATE:GROUP_W]
        w, _, _, csc, cs_row, tot, cs_x, dt_x, tot_x = _chunk_consts(a_ref[...], dt_ref[...], rev)
        cb = lax.dot_general(cm, bm, NT, preferred_element_type=F32)
        h = h_scr[...]
        h_b = h.astype(BF16)
        hin_ref[0, 0] = h_b
        xdt = xs * dt_x
        xdt_b = xdt.astype(BF16)
        yo = lax.dot_general(cm, h_b, NT, preferred_element_type=F32) * jnp.exp(cs_x)
        st = lax.dot_general((xdt * jnp.exp(tot_x - cs_x)).astype(BF16), bm, TN, preferred_element_type=F32)
        yds = []
        for j in range(HEADS_PER_GROUP):
            lo = HEAD_DIM * j
            h_scr[lo:lo + HEAD_DIM, :] = h[lo:lo + HEAD_DIM, :] * jnp.exp(tot[j:j + 1, 0:1]) + st[lo:lo + HEAD_DIM, :]
            lm = jnp.exp(jnp.where(w, csc[j] - cs_row[j:j + 1, :], -jnp.inf))
            yds.append(jnp.dot((cb * lm).astype(BF16), xdt_b[:, lo:lo + HEAD_DIM], preferred_element_type=F32))
        y = jnp.concatenate(yds, axis=1) + yo
        if skip:
            y = y + dsk_ref[...] * xs
        y_ref[...] = y.astype(y_ref.dtype)

    def body(xf_ref, dtf_ref, af_ref, xr_ref, dtr_ref, ar_ref, dsk_ref, yf_ref, hf_ref, yr_ref, hr_ref, hf_scr, hr_scr):
        @pl.when(pl.program_id(1) == 0)
        def _():
            hf_scr[...] = jnp.zeros_like(hf_scr)
            hr_scr[...] = jnp.zeros_like(hr_scr)

        one(xf_ref, dtf_ref, af_ref, dsk_ref, yf_ref, hf_ref, hf_scr, False)
        one(xr_ref, dtr_ref, ar_ref, None, yr_ref, hr_ref, hr_scr, True)

    def specs(rev):
        def z(zi):
            return nc - 1 - zi if rev else zi
        t_spec = pl.BlockSpec((HEADS_PER_GROUP, CHUNK), lambda g, zi: ((n_groups if rev else 0) + g, z(zi)))
        ins = [pl.BlockSpec((CHUNK, GROUP_W), lambda g, zi: (z(zi), g)), t_spec, t_spec]
        outs = [pl.BlockSpec((CHUNK, GROUP_X), lambda g, zi: (z(zi), g)),
                pl.BlockSpec((1, 1, GROUP_X, D_STATE), lambda g, zi: (g, z(zi), 0, 0))]
        return ins, outs

    (in_f, out_f), (in_r, out_r) = specs(False), specs(True)
    shapes = [jax.ShapeDtypeStruct((s, d_ssm), BF16), jax.ShapeDtypeStruct((n_groups, nc, GROUP_X, D_STATE), BF16)]
    return pl.pallas_call(
        body, name=name, grid=(n_groups, nc), in_specs=in_f + in_r + [pl.BlockSpec((1, GROUP_X), lambda g, zi: (0, g))],
        out_specs=out_f + out_r, out_shape=shapes + shapes,
        scratch_shapes=[pltpu.VMEM((GROUP_X, D_STATE), F32)] * 2, compiler_params=_params(2),
    )(xbc, dt_t, a_t, xbc, dt_t, a_t, dsk)


def _ssd_bwd(name, xbc, dt_t, a_t, dy, hin_f, hin_r, dsk, n_groups):
    s = xbc.shape[0]
    nc = s // CHUNK
    n_heads = n_groups * HEADS_PER_GROUP

    def one(x_ref, dt_ref, a_ref, dy_ref, hin_ref, dsk_ref, dx_ref, ddt_ref, da_ref, ddsk_ref, g_scr, rev):
        skip = dsk_ref is not None
        xs = _f(x_ref[:, 0:GROUP_X])
        bm = x_ref[:, GROUP_X:GROUP_X + D_STATE]
        cm = x_ref[:, GROUP_X + D_STATE:GROUP_W]
        dyv = _f(dy_ref[...])
        w, wt, eye, csc, cs_row, tot, cs_x, dt_x, tot_x = _chunk_consts(a_ref[...], dt_ref[...], rev)
        cb = lax.dot_general(cm, bm, NT, preferred_element_type=F32)
        cbt = lax.dot_general(bm, cm, NT, preferred_element_type=F32)
        onehot = lax.broadcasted_iota(jnp.int32, (1, HEADS_PER_GROUP), 1)
        hin = hin_ref[0, 0]
        g = g_scr[...]
        g_b = g.astype(BF16)
        din, dte = jnp.exp(cs_x), jnp.exp(tot_x - cs_x)
        xdt = xs * dt_x
        xdt_b, dy_b = xdt.astype(BF16), dyv.astype(BF16)
        dyo_b = (dyv * din).astype(BF16)
        yo = lax.dot_general(cm, hin, NT, preferred_element_type=F32) * din
        dxdt_s = lax.dot_general(bm, g_b, NT, preferred_element_type=F32) * dte
        d_c = jnp.dot(dyo_b, hin, preferred_element_type=F32)
        d_b = jnp.dot((xdt * dte).astype(BF16), g_b, preferred_element_type=F32)
        dhin = lax.dot_general(dyo_b, cm, TN, preferred_element_type=F32)
        dcb = jnp.zeros((CHUNK, CHUNK), F32)
        dtot = jnp.zeros((1, HEADS_PER_GROUP), F32)
        yds, dxds = [], []
        for j in range(HEADS_PER_GROUP):
            lo = HEAD_DIM * j
            cd = jnp.exp(tot[j:j + 1, 0:1])
            gj = g[lo:lo + HEAD_DIM, :]
            dcd = jnp.sum(jnp.sum(gj * _f(hin[lo:lo + HEAD_DIM, :]), axis=1, keepdims=True), axis=0, keepdims=True)
            dtot = dtot + (dcd * cd) * (onehot == j).astype(F32)
            g_scr[lo:lo + HEAD_DIM, :] = dhin[lo:lo + HEAD_DIM, :] + gj * cd
            dd = csc[j] - cs_row[j:j + 1, :]
            lm = jnp.exp(jnp.where(w, dd, -jnp.inf))
            lmt = jnp.exp(jnp.where(wt, -dd, -jnp.inf))
            xj_b, dyj_b = xdt_b[:, lo:lo + HEAD_DIM], dy_b[:, lo:lo + HEAD_DIM]
            yds.append(jnp.dot(_hi_lo(cb * lm), jnp.concatenate([xj_b, xj_b], axis=0), preferred_element_type=F32))
            dxds.append(jnp.dot(_hi_lo(cbt * lmt), jnp.concatenate([dyj_b, dyj_b], axis=0), preferred_element_type=F32))
            dcb = dcb + lax.dot_general(dyj_b, xj_b, NT, preferred_element_type=F32) * lm
        dxdt_d = jnp.concatenate(dxds, axis=1)
        z_state = xdt * dxdt_s
        yd = jnp.concatenate(yds, axis=1)
        dcs = _head_sums(dyv * yo + _f(dy_b) * yd - _f(xdt_b) * dxdt_d - z_state)
        dtot = dtot + _head_sums(jnp.broadcast_to(jnp.sum(z_state, axis=0, keepdims=True), (8, GROUP_X)))[0:1, :]
        dxdt = dxdt_s + dxdt_d
        ddt_c = _head_sums(dxdt * xs, terms=2)
        dx = dxdt * dt_x
        if skip:
            dx = dx + dsk_ref[...] * dyv
        dcb_b = dcb.astype(BF16)
        d_c = d_c + jnp.dot(dcb_b, bm, preferred_element_type=F32)
        d_b = d_b + lax.dot_general(dcb_b, cm, TN, preferred_element_type=F32)
        da_c = jnp.dot(wt.astype(F32), dcs, precision=HIGHEST, preferred_element_type=F32) + dtot
        ddt_ref[...] = lax.dot_general(ddt_c, eye, TN, precision=HIGHEST, preferred_element_type=F32)
        da_ref[...] = lax.dot_general(da_c, eye, TN, precision=HIGHEST, preferred_element_type=F32)
        dx_ref[...] = jnp.concatenate([dx, d_b, d_c], axis=1).astype(dx_ref.dtype)
        if skip:
            ddsk_ref[...] += jnp.sum(dyv * xs, axis=0, keepdims=True)

    def body(xf, dtf, af, dyf, hf, xr, dtr, ar, dyr, hr, dsk_ref, dxf, ddtf, daf, dxr, ddtr, dar, ddsk_ref, gf_scr, gr_scr):
        @pl.when(pl.program_id(1) == 0)
        def _():
            gf_scr[...] = jnp.zeros_like(gf_scr)
            gr_scr[...] = jnp.zeros_like(gr_scr)
            ddsk_ref[...] = jnp.zeros_like(ddsk_ref)

        one(xf, dtf, af, dyf, hf, dsk_ref, dxf, ddtf, daf, ddsk_ref, gf_scr, False)
        one(xr, dtr, ar, dyr, hr, None, dxr, ddtr, dar, None, gr_scr, True)

    def specs(rev):
        def z(zi):
            return zi if rev else nc - 1 - zi
        t_in = pl.BlockSpec((HEADS_PER_GROUP, CHUNK), lambda g, zi: ((n_groups if rev else 0) + g, z(zi)))
        t_out = pl.BlockSpec((HEADS_PER_GROUP, CHUNK), lambda g, zi: (g, z(zi)))
        x_spec = pl.BlockSpec((CHUNK, GROUP_W), lambda g, zi: (z(zi), g))
        ins = [x_spec, t_in, t_in, pl.BlockSpec((CHUNK, GROUP_X), lambda g, zi: (z(zi), g)),
               pl.BlockSpec((1, 1, GROUP_X, D_STATE), lambda g, zi: (g, z(zi), 0, 0))]
        return ins, [x_spec, t_out, t_out]

    (in_f, out_f), (in_r, out_r) = specs(False), specs(True)
    lane_spec = pl.BlockSpec((1, GROUP_X), lambda g, zi: (0, g))
    shapes = [jax.ShapeDtypeStruct((s, n_groups * GROUP_W), BF16), jax.ShapeDtypeStruct((n_heads, s), F32),
              jax.ShapeDtypeStruct((n_heads, s), F32)]
    return pl.pallas_call(
        body, name=name, grid=(n_groups, nc), in_specs=in_f + in_r + [lane_spec], out_specs=out_f + out_r + [lane_spec],
        out_shape=shapes + shapes + [jax.ShapeDtypeStruct((1, n_groups * GROUP_X), F32)],
        scratch_shapes=[pltpu.VMEM((GROUP_X, D_STATE), F32)] * 2, compiler_params=_params(2),
    )(xbc, dt_t, a_t, dy, hin_f, xbc, dt_t, a_t, dy, hin_r, dsk)


def _mesh_pos():
    return lax.axis_index("x"), lax.axis_index("y"), lax.axis_index("c")


def _slot(p):
    return 4 * p[0] + 2 * p[1] + p[2]


def _exchange_sems(n):
    return [pltpu.SemaphoreType.DMA((7 * n,)), pltpu.SemaphoreType.DMA((7 * n,)), pltpu.SemaphoreType.DMA((n,))]


def _gather_ops(ins, outs, send, recv, loc):
    n = len(ins)
    x, y, c = _mesh_pos()
    me, sib = (x, y, c), (x, y, 1 - c)
    chips = [(1 - x, y), (x, 1 - y), (1 - x, 1 - y)]

    def cp(a, k, block, to, src=None):
        dst = outs[a].at[_slot(block)]
        return pltpu.make_async_remote_copy(
            src_ref=dst if src is None else src, dst_ref=dst, send_sem=send.at[7 * a + k], recv_sem=recv.at[7 * a + k],
            device_id=to, device_id_type=MESH)

    mine = [pltpu.make_async_copy(ins[a], outs[a].at[_slot(me)], loc.at[a]) for a in range(n)]
    first = []
    for a in range(n):
        first.append(cp(a, 0, me, sib, src=ins[a]))
        first += [cp(a, 1 + j, me, (*chip, c), src=ins[a]) for j, chip in enumerate(chips)]

    def start():
        for d in mine + first:
            d.start()

    def finish():
        passed = []
        for a in range(n):
            for j, chip in enumerate(chips):
                cp(a, 1 + j, (*chip, c), me).wait_recv()
                p = cp(a, 4 + j, (*chip, c), sib)
                p.start()
                passed.append(p)
        for a in range(n):
            cp(a, 0, sib, me).wait_recv()
            for j, chip in enumerate(chips):
                cp(a, 4 + j, (*chip, 1 - c), me).wait_recv()
        for f in first + passed:
            f.wait_send()
        for m in mine:
            m.wait()

    return start, finish


_gather_ops.out_shapes = lambda arrs: [jax.ShapeDtypeStruct((N_DEV,) + a.shape, a.dtype) for a in arrs]


def _a2a_ops(ins, outs, send, recv, loc):
    n = len(ins)
    me = _mesh_pos()
    flips = [(fx, fy, fc) for fx in (0, 1) for fy in (0, 1) for fc in (0, 1)][1:]
    peers = [tuple(1 - p if fl else p for p, fl in zip(me, fl3)) for fl3 in flips]

    def cp(a, k, peer, dst_slot):
        return pltpu.make_async_remote_copy(
            src_ref=ins[a].at[_slot(peer)], dst_ref=outs[a].at[dst_slot], send_sem=send.at[7 * a + k],
            recv_sem=recv.at[7 * a + k], device_id=peer, device_id_type=MESH)

    mine = [pltpu.make_async_copy(ins[a].at[_slot(me)], outs[a].at[_slot(me)], loc.at[a]) for a in range(n)]
    sends = [cp(a, k, peer, _slot(me)) for a in range(n) for k, peer in enumerate(peers)]

    def start():
        for d in mine + sends:
            d.start()

    def finish():
        for a in range(n):
            for k, peer in enumerate(peers):
                cp(a, k, peer, _slot(peer)).wait_recv()
        for s_ in sends:
            s_.wait_send()
        for m in mine:
            m.wait()

    return start, finish


_a2a_ops.out_shapes = lambda arrs: [jax.ShapeDtypeStruct(a.shape, a.dtype) for a in arrs]


def _exchange(name, ops, arrs):
    n = len(arrs)

    def body(*refs):
        start, finish = ops(refs[:n], refs[n:2 * n], *refs[2 * n:])
        start()
        finish()

    any_spec = pl.BlockSpec(memory_space=pl.ANY)
    return pl.pallas_call(
        body, name=name, in_specs=[any_spec] * n, out_specs=[any_spec] * n, out_shape=ops.out_shapes(arrs),
        scratch_shapes=_exchange_sems(n),
    )(*arrs)


def _all_gather(name, arrs):
    return _exchange(name, _gather_ops, arrs)


def _all_to_all(name, arrs):
    return _exchange(name, _a2a_ops, arrs)


def _sum8(parts_ref):
    g = _f(parts_ref[0])
    for j in range(1, N_DEV):
        g = g + _f(parts_ref[j])
    return g


def _adam_math(w, g, m, v):
    m = ADAM_B1 * m + (1.0 - ADAM_B1) * g
    v = ADAM_B2 * v + (1.0 - ADAM_B2) * (g * g)
    m_hat = m / (1.0 - ADAM_B1 ** ADAM_STEP)
    v_hat = v / (1.0 - ADAM_B2 ** ADAM_STEP)
    return -ADAM_LR * (m_hat / (jnp.sqrt(v_hat) + ADAM_EPS) + ADAM_WD * w), m, v


def _reduce_sum(name, parts):
    _, r, c = parts.shape

    def body(p_ref, o_ref):
        o_ref[...] = _sum8(p_ref)

    return pl.pallas_call(body, name=name, out_shape=jax.ShapeDtypeStruct((r, c), F32))(parts)


def _adam(name, w, g, m, v, parts=None):
    r, c = w.shape
    tr = _tile(r, (128, 64, 32, 16)) if r * c > 2 ** 18 else r
    spec = pl.BlockSpec((tr, c), lambda i: (i, 0))
    summed = parts is not None

    def body(*refs):
        if summed:
            p_ref, w_ref, m_ref, v_ref, g_out, d_out, m_out, v_out = refs
            gv = _sum8(p_ref)
            g_out[...] = gv
        else:
            g_ref, w_ref, m_ref, v_ref, d_out, m_out, v_out = refs
            gv = g_ref[...]
        d, mn, vn = _adam_math(w_ref[...], gv, m_ref[...], v_ref[...])
        d_out[...] = d
        m_out[...] = mn
        v_out[...] = vn

    first = [pl.BlockSpec((N_DEV, tr, c), lambda i: (0, i, 0))] if summed else [spec]
    n_out = 4 if summed else 3
    return pl.pallas_call(
        body, name=name, grid=(r // tr,), in_specs=first + [spec] * 3, out_specs=[spec] * n_out,
        out_shape=[jax.ShapeDtypeStruct((r, c), F32)] * n_out, compiler_params=_params(1),
    )(parts if summed else g, w, m, v)


def _adam_many(name, ws, gs, ms, vs):
    n = len(ws)

    def body(*refs):
        outs = refs[4 * n:]
        for k in range(n):
            d, mn, vn = _adam_math(refs[k][...], refs[n + k][...], refs[2 * n + k][...], refs[3 * n + k][...])
            outs[k][...] = d
            outs[n + k][...] = mn
            outs[2 * n + k][...] = vn

    res = pl.pallas_call(
        body, name=name, out_shape=[jax.ShapeDtypeStruct(w.shape, F32) for w in ws] * 3,
        compiler_params=pltpu.CompilerParams(vmem_limit_bytes=VMEM_LIMIT),
    )(*ws, *gs, *ms, *vs)
    return res[:n], res[n:2 * n], res[2 * n:]


def _slot_offsets(sizes):
    offs, o = [], 0
    for sz in sizes:
        offs.append(o)
        o += -(-sz // 128) * 128
    return offs, o + (-o % 1024)


def _pack_vectors(name, vecs):
    sizes = [v.shape[1] for v in vecs]
    offs, total = _slot_offsets(sizes)

    def body(*refs):
        out = refs[-1]
        out[...] = jnp.zeros_like(out)
        for r, o, sz in zip(refs[:-1], offs, sizes):
            out[:, o:o + sz] = r[...]

    return pl.pallas_call(body, name=name, out_shape=jax.ShapeDtypeStruct((1, total), F32),
                          compiler_params=pltpu.CompilerParams(vmem_limit_bytes=VMEM_LIMIT))(*vecs).reshape(-1, 128)


def _xbc_runs(n_groups, base=0):
    ds, nb = n_groups * GROUP_X, n_groups * D_STATE
    runs = []
    for g in range(n_groups):
        runs += [(base + GROUP_X * g, GROUP_X), (base + ds + D_STATE * g, D_STATE), (base + ds + nb + D_STATE * g, D_STATE)]
    return runs


def _interleave_runs(width, base=0):
    half = width // 2
    runs = []
    for j in range(half // COL_BLOCK):
        runs += [(base + COL_BLOCK * j, COL_BLOCK), (base + half + COL_BLOCK * j, COL_BLOCK)]
    return runs


def _w_in_runs(d, d_ssm, d_xbc, n_heads, n_groups):
    o_dt = d_ssm + d_xbc
    o_glu = o_dt + 2 * n_heads
    return ([(0, d_ssm)] + _xbc_runs(n_groups, d_ssm) + _interleave_runs(2 * d, o_glu) + _interleave_runs(2 * d, o_glu + 2 * d)
            + [(o_dt, 2 * n_heads)])


def _inverse_runs(runs):
    offs = np.concatenate([[0], np.cumsum([l for _, l in runs])])
    order = sorted(range(len(runs)), key=lambda k: runs[k][0])
    return [(int(offs[k]), runs[k][1]) for k in order]


def _permute_cols(name, srcs, runs, out_widths=None, src_sharded=False, dst_sharded=False):
    rows = srcs[0].shape[-2]
    dtype = srcs[0].dtype
    total = sum(l for _, l in runs)
    src_w = [srcs[0].shape[2]] * N_DEV if src_sharded else [a.shape[1] for a in srcs]
    dst_w = [total // N_DEV] * N_DEV if dst_sharded else (out_widths or [total])
    assert sum(dst_w) == total and sum(src_w) >= max(s_ + l for s_, l in runs)
    src_edges, dst_edges = np.cumsum([0] + src_w), np.cumsum([0] + dst_w)

    def locate(edges, col):
        k = int(np.searchsorted(edges, col, side="right")) - 1
        return k, col - int(edges[k])

    pieces, o = [], 0
    for s_, l in runs:
        while l:
            (ks, cs_), (kd, cd) = locate(src_edges, s_), locate(dst_edges, o)
            n = min(l, src_w[ks] - cs_, dst_w[kd] - cd)
            if pieces and pieces[-1][0] == ks and pieces[-1][2] == kd and pieces[-1][1] + pieces[-1][4] == cs_:
                pieces[-1] = pieces[-1][:4] + (pieces[-1][4] + n,)
            else:
                pieces.append((ks, cs_, kd, cd, n))
            s_, o, l = s_ + n, o + n, l - n
    tr = rows
    while tr % 16 == 0 and tr * max(sum(src_w), total) * dtype.itemsize > 4 * 2 ** 20:
        tr //= 2
    n_src = len(srcs)

    def body(*refs):
        ins, outs = refs[:n_src], refs[n_src:]
        for ks, cs_, kd, cd, n in pieces:
            val = ins[0][ks, :, cs_:cs_ + n] if src_sharded else ins[ks][:, cs_:cs_ + n]
            if dst_sharded:
                outs[0][kd, :, cd:cd + n] = val
            else:
                outs[kd][:, cd:cd + n] = val

    def spec(sharded, width):
        if sharded:
            return pl.BlockSpec((N_DEV, tr, width), lambda i: (0, i, 0))
        return pl.BlockSpec((tr, width), lambda i: (i, 0))

    in_specs = [spec(True, src_w[0])] if src_sharded else [spec(False, w) for w in src_w]
    if dst_sharded:
        out_specs, out_shape = [spec(True, dst_w[0])], [jax.ShapeDtypeStruct((N_DEV, rows, dst_w[0]), dtype)]
    else:
        out_specs, out_shape = [spec(False, w) for w in dst_w], [jax.ShapeDtypeStruct((rows, w), dtype) for w in dst_w]
    return pl.pallas_call(body, name=name, grid=(rows // tr,), in_specs=in_specs, out_specs=out_specs, out_shape=out_shape,
                          compiler_params=_params(1))(*srcs)


def _cols_from_shards(g):
    return g.transpose(1, 0, 2).reshape(g.shape[1], -1)


def _cols_to_shards(a):
    return a.reshape(a.shape[0], N_DEV, -1).transpose(1, 0, 2)


def _pad_rows(a, rows):
    return jnp.concatenate([a, jnp.zeros((rows - a.shape[0], a.shape[1]), a.dtype)], axis=0)


def _pack(vecs):
    a = jnp.concatenate(vecs, axis=1)
    return jnp.concatenate([a, jnp.zeros((1, -a.shape[1] % 1024), a.dtype)], axis=1).reshape(-1, 128)


def kernel(x, c, w_ada, b_ada, g_pre_mix, g_post_mix, w_in, w_conv_ssm, b_conv_ssm, dt_bias_fwd, dt_bias_bwd, a_log_fwd, a_log_bwd, d_skip, g_ssm_norm, w_ssm_out, b_glu, w_dw, b_dw, ln_g, ln_b, w_conv_out, b_conv_out, b_gate, w_mix_out, g_pre_ffn, g_post_ffn, w_gate_up, w_down, loss_target, m_w_ada, m_b_ada, m_g_pre_mix, m_g_post_mix, m_w_in, m_w_conv_ssm, m_b_conv_ssm, m_dt_bias_fwd, m_dt_bias_bwd, m_a_log_fwd, m_a_log_bwd, m_d_skip, m_g_ssm_norm, m_w_ssm_out, m_b_glu, m_w_dw, m_b_dw, m_ln_g, m_ln_b, m_w_conv_out, m_b_conv_out, m_b_gate, m_w_mix_out, m_g_pre_ffn, m_g_post_ffn, m_w_gate_up, m_w_down, v_w_ada, v_b_ada, v_g_pre_mix, v_g_post_mix, v_w_in, v_w_conv_ssm, v_b_conv_ssm, v_dt_bias_fwd, v_dt_bias_bwd, v_a_log_fwd, v_a_log_bwd, v_d_skip, v_g_ssm_norm, v_w_ssm_out, v_b_glu, v_w_dw, v_b_dw, v_ln_g, v_ln_b, v_w_conv_out, v_b_conv_out, v_b_gate, v_w_mix_out, v_g_pre_ffn, v_g_post_ffn, v_w_gate_up, v_w_down):
    weights = dict(w_ada=w_ada, b_ada=b_ada, g_pre_mix=g_pre_mix, g_post_mix=g_post_mix, w_in=w_in, w_conv_ssm=w_conv_ssm, b_conv_ssm=b_conv_ssm, dt_bias_fwd=dt_bias_fwd, dt_bias_bwd=dt_bias_bwd, a_log_fwd=a_log_fwd, a_log_bwd=a_log_bwd, d_skip=d_skip, g_ssm_norm=g_ssm_norm, w_ssm_out=w_ssm_out, b_glu=b_glu, w_dw=w_dw, b_dw=b_dw, ln_g=ln_g, ln_b=ln_b, w_conv_out=w_conv_out, b_conv_out=b_conv_out, b_gate=b_gate, w_mix_out=w_mix_out, g_pre_ffn=g_pre_ffn, g_post_ffn=g_post_ffn, w_gate_up=w_gate_up, w_down=w_down)
    mom_m = dict(w_ada=m_w_ada, b_ada=m_b_ada, g_pre_mix=m_g_pre_mix, g_post_mix=m_g_post_mix, w_in=m_w_in, w_conv_ssm=m_w_conv_ssm, b_conv_ssm=m_b_conv_ssm, dt_bias_fwd=m_dt_bias_fwd, dt_bias_bwd=m_dt_bias_bwd, a_log_fwd=m_a_log_fwd, a_log_bwd=m_a_log_bwd, d_skip=m_d_skip, g_ssm_norm=m_g_ssm_norm, w_ssm_out=m_w_ssm_out, b_glu=m_b_glu, w_dw=m_w_dw, b_dw=m_b_dw, ln_g=m_ln_g, ln_b=m_ln_b, w_conv_out=m_w_conv_out, b_conv_out=m_b_conv_out, b_gate=m_b_gate, w_mix_out=m_w_mix_out, g_pre_ffn=m_g_pre_ffn, g_post_ffn=m_g_post_ffn, w_gate_up=m_w_gate_up, w_down=m_w_down)
    mom_v = dict(w_ada=v_w_ada, b_ada=v_b_ada, g_pre_mix=v_g_pre_mix, g_post_mix=v_g_post_mix, w_in=v_w_in, w_conv_ssm=v_w_conv_ssm, b_conv_ssm=v_b_conv_ssm, dt_bias_fwd=v_dt_bias_fwd, dt_bias_bwd=v_dt_bias_bwd, a_log_fwd=v_a_log_fwd, a_log_bwd=v_a_log_bwd, d_skip=v_d_skip, g_ssm_norm=v_g_ssm_norm, w_ssm_out=v_w_ssm_out, b_glu=v_b_glu, w_dw=v_w_dw, b_dw=v_b_dw, ln_g=v_ln_g, ln_b=v_ln_b, w_conv_out=v_w_conv_out, b_conv_out=v_b_conv_out, b_gate=v_b_gate, w_mix_out=v_w_mix_out, g_pre_ffn=v_g_pre_ffn, g_post_ffn=v_g_post_ffn, w_gate_up=v_w_gate_up, w_down=v_w_down)
    names = list(weights)

    s, d = x.shape[1], x.shape[2]
    d_ssm, n_heads, d_xbc = g_ssm_norm.shape[-1], d_skip.shape[-1], b_conv_ssm.shape[-1]
    n_groups = n_heads // HEADS_PER_GROUP
    d_ff = w_down.shape[1] * N_DEV
    k_ssm, k_dw = w_conv_ssm.shape[1], w_dw.shape[1]
    assert d_ssm == n_groups * GROUP_X and d_xbc == n_groups * GROUP_W and s % CHUNK == 0
    assert d % COL_BLOCK == 0 and d_ff % COL_BLOCK == 0
    mc = w_ada.shape[-1]
    me = _slot(_mesh_pos())
    wb = d_ssm + d_xbc + 4 * d
    p_x, p_glu, p_gate = d_ssm, d_ssm + d_xbc, d_ssm + d_xbc + 2 * d
    ncb, nfb = d // COL_BLOCK, d_ff // COL_BLOCK
    x2, tgt = x[0], loss_target[0]

    big = ["w_in", "w_ssm_out", "w_conv_out", "w_mix_out", "w_gate_up", "w_down"]
    (w_in_g,) = _all_gather("gather_w_in", [w_in[0].astype(BF16)])
    w_in_runs, xbc_runs, il_runs = _w_in_runs(d, d_ssm, d_xbc, n_heads, n_groups), _xbc_runs(n_groups), _interleave_runs(2 * d)
    w_big, w_dt = _permute_cols("order_w_in", [w_in_g], w_in_runs, out_widths=[wb, 2 * n_heads], src_sharded=True)

    n_cs, n_dw = k_ssm * (d_xbc // N_DEV), k_dw * (d // N_DEV)
    (small_all,) = _all_gather("gather_small", [_pack([c, w_conv_ssm.reshape(1, n_cs), w_dw.reshape(1, n_dw)])])
    small_all = small_all.reshape(N_DEV, -1)
    c_all = small_all[:, :d]
    wcs = _cols_from_shards(small_all[:, d:d + n_cs].reshape(N_DEV, k_ssm, -1))
    wdw = _cols_from_shards(small_all[:, d + n_cs:d + n_cs + n_dw].reshape(N_DEV, k_dw, -1))
    (wcs_p,) = _permute_cols("order_conv_taps", [wcs], xbc_runs)
    (bcs_p,) = _permute_cols("order_conv_bias", [b_conv_ssm], xbc_runs)
    (b_glu_il,), (b_gate_il,) = _permute_cols("order_b_glu", [b_glu], il_runs), _permute_cols("order_b_gate", [b_gate], il_runs)
    kp_ssm, kp_dw = -(-k_ssm // 8) * 8, -(-k_dw // 8) * 8
    zero_x, zero_d = jnp.zeros((1, d_xbc), F32), jnp.zeros((1, d), F32)

    (c_act,), _ = _rowcall("c_act", _fwd_fn(_silu), [(c_all, 0, d)], [], [(d, F32)], [])
    b_my = lax.dynamic_slice(b_ada, (0, me * mc), (1, mc))
    mod_cols = _matmul("mod", c_act, w_ada[0], "nn", F32, acc_in=jnp.broadcast_to(b_my, (N_DEV, mc)))
    (mod_rows,) = _all_to_all("mod_to_owner", [mod_cols.reshape(N_DEV, 1, mc)])
    mod = mod_rows.reshape(1, N_DEV * mc)
    sh1, sc1, g1, sh2, sc2, g2 = [mod[:, k * d:(k + 1) * d] for k in range(6)]

    (h1,), _ = _rowcall("prenorm_mix", _fwd_fn(_t_prenorm), [(x2, 0, d)], [g_pre_mix, sc1, sh1], [(d, BF16)], [])
    proj, rest = _matmul("in_proj", h1, w_big, "nn", BF16, carry=(_gather_ops, [weights[k][0].astype(BF16) for k in big[1:]]))
    gath = dict(zip(big[1:], rest))
    w_ssm, w_co, w_mo = gath["w_ssm_out"].reshape(d_ssm, d), gath["w_conv_out"].reshape(d, d), gath["w_mix_out"].reshape(d, d)
    gu_runs = _interleave_runs(2 * d_ff)
    (w_gu,) = _permute_cols("order_w_gate_up", [gath["w_gate_up"]], gu_runs, src_sharded=True)
    w_dn = gath["w_down"].reshape(d_ff, d)
    dt_raw = _matmul("in_proj_dt", h1, w_dt, "nn", F32)
    xbc_c = _dwconv("conv_ssm", proj, p_x, d_xbc, _pad_rows(wcs_p, kp_ssm), k_ssm, bcs_p, True, BF16)
    alog_col = jnp.concatenate([a_log_fwd, a_log_bwd], axis=1).reshape(2 * n_heads, 1)
    dt_t, a_t = _dtprep("dt_prep", dt_raw, jnp.concatenate([dt_bias_fwd, dt_bias_bwd], axis=1), alog_col)
    dsk = jnp.repeat(d_skip, HEAD_DIM, axis=1)
    y_f, hin_f, y_b, hin_b = _ssd_fwd("ssd", xbc_c, dt_t, a_t, dsk, n_groups)
    gn_rows = [(y_f, 0, d_ssm), (y_b, 0, d_ssm), (proj, 0, d_ssm)]
    (yn,), _ = _rowcall("ssm_norm", _fwd_fn(_t_gnorm), gn_rows, [g_ssm_norm], [(d_ssm, BF16)], [], ncol=n_groups)
    ya = _matmul("ssm_out", yn, w_ssm, "nn", BF16)
    (u0,), _ = _rowcall("glu", _glu_fwd, [(proj, p_glu, 2 * d)], [b_glu_il], [(d, BF16)], [], ncol=ncb)
    wdw_pad = _pad_rows(wdw, kp_dw)
    u1 = _dwconv("conv_dw", u0, 0, d, wdw_pad, k_dw, b_dw, False, BF16)
    (u2,), _ = _rowcall("ln_silu", _fwd_fn(_t_lnsilu), [(u1, 0, d)], [ln_g, ln_b], [(d, BF16)], [])
    yb = _matmul("conv_out", u2, w_co, "nn", BF16)
    gm_rows, gm_prm = [(ya, 0, d), (yb, 0, d), (proj, p_gate, 2 * d)], [b_conv_out, b_gate_il]
    (m_in,), _ = _rowcall("gate_mix", _gatemix_fwd, gm_rows, gm_prm, [(d, BF16)], [], ncol=ncb)
    mix = _matmul("mix_out", m_in, w_mo, "nn", BF16)
    pp_prm = [g_post_mix, g1, g_pre_ffn, sc2, sh2]
    (x1, h2), _ = _rowcall("post_mix", _fwd_fn(_t_postpre), [(x2, 0, d), (mix, 0, d)], pp_prm, [(d, F32), (d, BF16)], [])
    gu, act = _matmul("gate_up", h2, w_gu, "nn", BF16, tn=2 * COL_BLOCK,
                      epilogue=(lambda p: (p, _swiglu_fwd(p)[0][0]), [], [(2 * d_ff, BF16), (d_ff, BF16)]))
    f = _matmul("down", act, w_dn, "nn", BF16)

    def loss_fn(x1_t, f_t, tgt_t, gpost, g2_t):
        args = [_f(x1_t), _f(f_t), _f(tgt_t), _f(gpost), _f(g2_t)]
        val, vjp = jax.vjp(_t_loss, *args)
        gr = vjp(jnp.ones((), F32))
        return (gr[0], gr[1]), (jnp.zeros((1, 128), F32) + val, gr[3], gr[4])

    (dx1a, df), (loss_v, d_gpost2, d_g2) = _rowcall(
        "loss", loss_fn, [(x1, 0, d), (f, 0, d), (tgt, 0, d)], [g_post_ffn, g2], [(d, F32), (d, BF16)], [128, d, d])
    loss = lax.psum(loss_v[0, 0], ("x", "y", "c"))

    dgu = _matmul("down_dx", df, w_dn, "nt", BF16, tn=COL_BLOCK,
                  epilogue=(lambda p, gu_t: _swiglu_bwd(gu_t, p)[0], [gu], [(2 * d_ff, BF16)]))
    dw_dn = _matmul("down_dw", act, df, "tn", BF16)
    recv = {}
    dh2, (recv["w_down"],) = _matmul("gate_up_dx", dgu, w_gu, "nt", BF16, carry=(_a2a_ops, [dw_dn.reshape(N_DEV, -1, d)]))
    dw_gu = _matmul("gate_up_dw", h2, dgu, "tn", BF16)
    (dxa, dmix), (d_gpost1, d_g1, d_gpre2, d_sc2, d_sh2) = _rowcall(
        "post_mix_bwd", _vjp_fn(_t_postpre, 2, 2, (0, 1), (0, 1, 2, 3, 4)),
        [(x2, 0, d), (mix, 0, d), (dx1a, 0, d), (dh2, 0, d)], pp_prm, [(d, F32), (d, BF16)], [d] * 5)
    dm_in = _matmul("mix_out_dx", dmix, w_mo, "nt", BF16)
    dw_mo = _matmul("mix_out_dw", m_in, dmix, "tn", BF16)
    dproj = lax.empty((s, wb), BF16)
    (dya, dyb, dproj), (d_bco, d_bgate) = _rowcall(
        "gate_mix_bwd", _gatemix_bwd, gm_rows + [(dm_in, 0, d)], gm_prm,
        [(d, BF16), (d, BF16), (2 * d, BF16, dproj, p_gate)], [d, 2 * d], ncol=ncb)
    du2 = _matmul("conv_out_dx", dyb, w_co, "nt", BF16)
    dw_co = _matmul("conv_out_dw", u2, dyb, "tn", BF16)
    (du1,), (d_lng, d_lnb) = _rowcall("ln_silu_bwd", _vjp_fn(_t_lnsilu, 1, 1, (0,), (0, 1)), [(u1, 0, d), (du2, 0, d)],
                                      [ln_g, ln_b], [(d, BF16)], [d, d])
    dwdw_p, d_bdw = _dwconv_bwd_w("conv_dw_dw", u0, 0, d, wdw_pad, k_dw, b_dw, [du1], False)
    du0 = _dwconv("conv_dw_dx", du1, 0, d, _pad_rows(wdw[::-1], kp_dw), k_dw, zero_d, False, BF16)
    (dproj,), (d_bglu,) = _rowcall("glu_bwd", _glu_bwd, [(proj, p_glu, 2 * d), (du0, 0, d)], [b_glu_il],
                                   [(2 * d, BF16, dproj, p_glu)], [2 * d], ncol=ncb)
    dyn, (recv["w_mix_out"], recv["w_conv_out"]) = _matmul(
        "ssm_out_dx", dya, w_ssm, "nt", BF16, carry=(_a2a_ops, [dw_mo.reshape(N_DEV, -1, d), dw_co.reshape(N_DEV, -1, d)]))
    dw_ssm = _matmul("ssm_out_dw", yn, dya, "tn", BF16)
    (dy_ssd, dproj), (d_gn,) = _rowcall("ssm_norm_bwd", _vjp_fn(_t_gnorm, 3, 1, (0, 2), (0,)), gn_rows + [(dyn, 0, d_ssm)],
                                        [g_ssm_norm], [(d_ssm, BF16), (d_ssm, BF16, dproj, 0)], [d_ssm], ncol=n_groups)
    dxbc_f, ddt_f, da_f, dxbc_b, ddt_b, da_b, ddsk = _ssd_bwd("ssd_bwd", xbc_c, dt_t, a_t, dy_ssd, hin_f, hin_b, dsk, n_groups)
    ddt_raw, d_dtb, d_alog = _dtprep_bwd("dt_prep_bwd", ddt_f, ddt_b, da_f, da_b, dt_t, alog_col)
    dpre, dwcs_p, dbcs_p = _dwconv_bwd_w("conv_ssm_dw", proj, p_x, d_xbc, _pad_rows(wcs_p, kp_ssm), k_ssm, bcs_p,
                                          [dxbc_f, dxbc_b], True)
    dproj = _dwconv("conv_ssm_dx", dpre, 0, d_xbc, _pad_rows(wcs_p[::-1], kp_ssm), k_ssm, zero_x, False, BF16, into=(dproj, p_x))
    dw_big, (recv["w_gate_up"], recv["w_ssm_out"]) = _matmul(
        "in_proj_dw", h1, dproj, "tn", BF16,
        carry=(_a2a_ops, [_permute_cols("restore_dw_gate_up", [dw_gu], _inverse_runs(gu_runs), dst_sharded=True)[0],
                          dw_ssm.reshape(N_DEV, -1, d)]))
    dw_dt = _matmul("in_proj_dt_dw", h1, ddt_raw, "tn", BF16)
    (dw_in,) = _permute_cols("restore_dw_in", [dw_big, dw_dt], _inverse_runs(w_in_runs), dst_sharded=True)
    dh1, (recv["w_in"],) = _matmul("in_proj_dx", dproj, w_big, "nt", F32, carry=(_a2a_ops, [dw_in]))
    dh1 = _matmul("in_proj_dt_dx", ddt_raw, w_dt, "nt", BF16, acc_in=dh1)

    def prenorm_bwd_fn(x_t, dh_t, dxa_t, g, sc, sh):
        rows, prm = _vjp_fn(_t_prenorm, 1, 1, (0,), (0, 1, 2))(x_t, dh_t, g, sc, sh)
        return (rows[0] + _f(dxa_t),), prm

    (grad_x,), (d_gpre1, d_sc1, d_sh1) = _rowcall(
        "prenorm_mix_bwd", prenorm_bwd_fn, [(x2, 0, d), (dh1, 0, d), (dxa, 0, d)], [g_pre_mix, sc1, sh1], [(d, F32)], [d] * 3)

    dmod = jnp.concatenate([d_sh1, d_sc1, d_g1, d_sh2, d_sc2, d_g2], axis=1)
    small_g = {
        "b_ada": dmod, "g_pre_mix": d_gpre1, "g_post_mix": d_gpost1,
        "b_conv_ssm": _permute_cols("restore_db_conv", [dbcs_p], _inverse_runs(xbc_runs))[0],
        "dt_bias_fwd": d_dtb[:n_heads].reshape(1, n_heads), "dt_bias_bwd": d_dtb[n_heads:].reshape(1, n_heads),
        "a_log_fwd": d_alog[:n_heads].reshape(1, n_heads), "a_log_bwd": d_alog[n_heads:].reshape(1, n_heads),
        "d_skip": ddsk.reshape(n_heads, HEAD_DIM).sum(axis=1).reshape(1, n_heads), "g_ssm_norm": d_gn,
        "b_glu": _permute_cols("restore_db_glu", [d_bglu], _inverse_runs(il_runs))[0], "b_dw": d_bdw, "ln_g": d_lng, "ln_b": d_lnb,
        "b_conv_out": d_bco, "b_gate": _permute_cols("restore_db_gate", [d_bgate], _inverse_runs(il_runs))[0],
        "g_pre_ffn": d_gpre2, "g_post_ffn": d_gpost2,
        "w_conv_ssm": _permute_cols("restore_dw_conv", [dwcs_p], _inverse_runs(xbc_runs))[0][:k_ssm].reshape(1, k_ssm * d_xbc),
        "w_dw": dwdw_p[:k_dw].reshape(1, k_dw * d),
    }
    small_names = list(small_g)
    sizes = [small_g[k].shape[1] for k in small_names]
    offs, _ = _slot_offsets(sizes)
    (small_parts,) = _all_gather("gather_small_grads", [_pack_vectors("pack_small_grads", [small_g[k] for k in small_names])])
    dmod_all = small_parts.reshape(N_DEV, -1)[:, :6 * d]
    small_tot = _reduce_sum("sum_small_grads", small_parts).reshape(1, -1)
    grads = {k: small_tot[:, offs[i]:offs[i] + sizes[i]] for i, k in enumerate(small_names)}
    grads["w_conv_ssm"] = lax.dynamic_slice(grads["w_conv_ssm"].reshape(k_ssm, d_xbc), (0, me * (d_xbc // N_DEV)), (k_ssm, d_xbc // N_DEV))
    grads["w_dw"] = lax.dynamic_slice(grads["w_dw"].reshape(k_dw, d), (0, me * (d // N_DEV)), (k_dw, d // N_DEV))
    grads["w_ada"] = _matmul("mod_dw", c_act, lax.dynamic_slice(dmod_all, (0, me * mc), (N_DEV, mc)), "tn", F32, highest=True)

    delta, new_m, new_v = {}, {}, {}
    for k in big:
        grads[k], delta[k], new_m[k], new_v[k] = _adam("adam_" + k, weights[k][0], None, mom_m[k][0], mom_v[k][0], parts=recv[k])
    for k in ["w_ada", "w_conv_ssm", "w_dw"]:
        delta[k], new_m[k], new_v[k] = _adam("adam_" + k, weights[k][0], grads[k], mom_m[k][0], mom_v[k][0])
    rep = [k for k in names if k not in big and k not in ("w_ada", "w_conv_ssm", "w_dw")]
    r_delta, r_m, r_v = _adam_many("adam_small", *[[t[k] for k in rep] for t in (weights, grads, mom_m, mom_v)])
    for i, k in enumerate(rep):
        delta[k], new_m[k], new_v[k] = r_delta[i], r_m[i], r_v[i]

    def shaped(t, k):
        return t[k].reshape(weights[k].shape)

    return (loss, grad_x.reshape(x.shape), *[shaped(grads, k) for k in names], *[shaped(delta, k) for k in names],
            *[shaped(new_m, k) for k in names], *[shaped(new_v, k) for k in names])
```

```python
import functools

import numpy as np
import jax
import jax.numpy as jnp
from jax import lax
from jax.experimental import pallas as pl
from jax.experimental.pallas import tpu as pltpu

F32 = jnp.float32
BF16 = jnp.bfloat16
HIGHEST = lax.Precision.HIGHEST
MESH = pl.DeviceIdType.MESH
N_DEV = 8
EPS = 1e-6
CHUNK = 128
HEAD_DIM = 64
D_STATE = 128
HEADS_PER_GROUP = 8
GROUP_X = HEADS_PER_GROUP * HEAD_DIM
GROUP_W = GROUP_X + 2 * D_STATE
COL_BLOCK = 512
HALO = 16
VMEM_LIMIT = 56 * 1024 * 1024
ADAM_LR, ADAM_B1, ADAM_B2, ADAM_EPS, ADAM_WD, ADAM_STEP = 0.001, 0.9, 0.999, 1e-08, 0.01, 10

NN = (((1,), (0,)), ((), ()))
NT = (((1,), (1,)), ((), ()))
TN = (((0,), (0,)), ((), ()))


def _tile(dim, prefs):
    for p in prefs:
        if p <= dim and dim % p == 0:
            return p
    return dim


def _params(n_grid):
    return pltpu.CompilerParams(dimension_semantics=("arbitrary",) * n_grid, vmem_limit_bytes=VMEM_LIMIT)


def _f(v):
    return v.astype(F32)


def _matmul(name, a, b, mode, out_dtype, acc_in=None, highest=False, carry=None, tn=None, epilogue=None):
    if mode == "nn":
        (m, k), (k2, n) = a.shape, b.shape
    elif mode == "nt":
        (m, k), (n, k2) = a.shape, b.shape
    else:
        (k, m), (k2, n) = a.shape, b.shape
    assert k == k2, (name, a.shape, b.shape, mode)
    tm = _tile(m, (1024, 512, 256, 128))
    tn = _tile(n, (1024, 512, 256, 128)) if tn is None else tn
    tk = _tile(k, (2816, 2048, 1024, 512, 256, 128))
    nk, nj = k // tk, n // tn
    assert n % tn == 0
    dims = {"nn": NN, "nt": NT, "tn": TN}[mode]
    a_spec = pl.BlockSpec((tk, tm), lambda i, j, kk: (kk, i)) if mode == "tn" else pl.BlockSpec((tm, tk), lambda i, j, kk: (i, kk))
    b_spec = pl.BlockSpec((tn, tk), lambda i, j, kk: (j, kk)) if mode == "nt" else pl.BlockSpec((tk, tn), lambda i, j, kk: (kk, j))
    o_spec = pl.BlockSpec((tm, tn), lambda i, j, kk: (i, j))
    has_acc = acc_in is not None
    epi_fn, e_arrs, e_outs = epilogue if epilogue is not None else (None, [], [])
    ne = len(e_arrs)
    n_in = (3 if has_acc else 2) + ne
    if epilogue is None:
        o_specs, o_shapes = [o_spec], [jax.ShapeDtypeStruct((m, n), out_dtype)]
    else:
        o_specs = [pl.BlockSpec((tm, w // nj), lambda i, j, kk: (i, j)) for w, _ in e_outs]
        o_shapes = [jax.ShapeDtypeStruct((m, w), dt) for w, dt in e_outs]
    e_specs = [pl.BlockSpec((tm, arr.shape[1] // nj), lambda i, j, kk: (i, j)) for arr in e_arrs]
    n_out = len(o_specs)
    ops, c_arrs = carry if carry is not None else (None, [])
    nc_ = len(c_arrs)
    grid = (m // tm, nj, nk)

    def body(*refs):
        a_ref, b_ref = refs[0], refs[1]
        c_ref = refs[2] if has_acc else None
        e_refs = refs[n_in - ne:n_in]
        c_ins = refs[n_in:n_in + nc_]
        o_refs = refs[n_in + nc_:n_in + nc_ + n_out]
        c_outs = refs[n_in + nc_ + n_out:n_in + 2 * nc_ + n_out]
        acc = refs[n_in + 2 * nc_ + n_out]
        i, j, kk = pl.program_id(0), pl.program_id(1), pl.program_id(2)
        if nc_:
            start, finish = ops(c_ins, c_outs, *refs[n_in + 2 * nc_ + n_out + 1:])
            pl.when((i == 0) & (j == 0) & (kk == 0))(start)

        def write(val):
            if epi_fn is None:
                o_refs[0][...] = val.astype(out_dtype)
            else:
                for ref, v in zip(o_refs, epi_fn(val, *[r[...] for r in e_refs])):
                    ref[...] = v.astype(ref.dtype)

        def product(first):
            if highest:
                p = lax.dot_general(_f(a_ref[...]), _f(b_ref[...]), dims, precision=HIGHEST, preferred_element_type=F32)
            else:
                p = lax.dot_general(a_ref[...].astype(BF16), b_ref[...].astype(BF16), dims, preferred_element_type=F32)
            return p + _f(c_ref[...]) if (first and has_acc) else p

        if nk == 1:
            write(product(True))
        else:
            @pl.when(kk == 0)
            def _():
                acc[...] = product(True)

            if nk > 2:
                @pl.when((kk > 0) & (kk < nk - 1))
                def _():
                    acc[...] += product(False)

            @pl.when(kk == nk - 1)
            def _():
                write(acc[...] + product(False))

        if nc_:
            pl.when((i == grid[0] - 1) & (j == grid[1] - 1) & (kk == nk - 1))(finish)

    any_spec = pl.BlockSpec(memory_space=pl.ANY)
    c_shapes = ops.out_shapes(c_arrs) if nc_ else []
    res = pl.pallas_call(
        body, name=name, grid=grid,
        in_specs=[a_spec, b_spec] + ([o_spec] if has_acc else []) + e_specs + [any_spec] * nc_,
        out_specs=o_specs + [any_spec] * nc_, out_shape=o_shapes + c_shapes,
        scratch_shapes=[pltpu.VMEM((tm, tn) if nk > 1 else (8, 128), F32)] + (_exchange_sems(nc_) if nc_ else []),
        compiler_params=_params(3),
    )(*([a, b] + ([acc_in] if has_acc else []) + list(e_arrs) + list(c_arrs)))
    outs = res[0] if n_out == 1 else res[:n_out]
    return (outs, res[n_out:]) if nc_ else outs


def _rowcall(name, fn, rows, params, out_rows, out_accs, ncol=1):
    s = rows[0][0].shape[0]
    tm = _tile(s, (256 if ncol == 1 else 1024, 128, 64, 32, 16, 8))
    args, in_specs = [], []
    for arr, off, w in rows:
        bw = w // ncol
        assert w % ncol == 0 and off % bw == 0, (name, off, w, ncol)
        in_specs.append(pl.BlockSpec((tm, bw), functools.partial(lambda j, i, ob: (i, ob + j), ob=off // bw)))
        args.append(arr)
    for p in params:
        assert p.shape[1] % ncol == 0, (name, p.shape)
        in_specs.append(pl.BlockSpec((p.shape[0], p.shape[1] // ncol), lambda j, i: (0, j)))
        args.append(p)
    n_in, n_ro = len(args), len(out_rows)
    out_shape, out_specs, aliases = [], [], {}
    for k, spec in enumerate(out_rows):
        w, dt = spec[0], spec[1]
        bw = w // ncol
        if len(spec) == 4:
            buf, off = spec[2], spec[3]
            assert off % bw == 0 and buf.dtype == dt, (name, off, bw)
            out_shape.append(jax.ShapeDtypeStruct(buf.shape, dt))
            out_specs.append(pl.BlockSpec((tm, bw), functools.partial(lambda j, i, ob: (i, ob + j), ob=off // bw)))
            aliases[len(args)] = k
            in_specs.append(pl.BlockSpec(memory_space=pl.ANY))
            args.append(buf)
        else:
            out_shape.append(jax.ShapeDtypeStruct((s, w), dt))
            out_specs.append(pl.BlockSpec((tm, bw), lambda j, i: (i, j)))
    for w in out_accs:
        out_shape.append(jax.ShapeDtypeStruct((1, w), F32))
        out_specs.append(pl.BlockSpec((1, w // ncol), lambda j, i: (0, j)))
    n_args = len(args)

    def body(*refs):
        i = pl.program_id(1)
        ro, ao = fn(*[r[...] for r in refs[:n_in]])
        outs = refs[n_args:]
        assert len(ro) == n_ro and len(ao) == len(out_accs), name
        for ref, v in zip(outs[:n_ro], ro):
            ref[...] = v.astype(ref.dtype)
        for ref, v in zip(outs[n_ro:], ao):
            @pl.when(i == 0)
            def _(ref=ref, v=v):
                ref[...] = v

            @pl.when(i > 0)
            def _(ref=ref, v=v):
                ref[...] += v

    res = pl.pallas_call(
        body, name=name, grid=(ncol, s // tm), in_specs=in_specs, out_specs=out_specs,
        out_shape=out_shape, input_output_aliases=aliases, compiler_params=_params(2),
    )(*args)
    return res[:n_ro], res[n_ro:]


def _fwd_fn(f):
    def fn(*vals):
        out = f(*[_f(v) for v in vals])
        return (out if isinstance(out, tuple) else (out,)), ()
    return fn


def _vjp_fn(f, n_prim, n_ct, want_rows, want_params):
    def fn(*vals):
        prims = [_f(v) for v in vals[:n_prim]]
        cts = [_f(v) for v in vals[n_prim:n_prim + n_ct]]
        prms = [_f(v) for v in vals[n_prim + n_ct:]]
        out, vjp = jax.vjp(f, *prims, *prms)
        g = vjp(tuple(cts) if isinstance(out, tuple) else cts[0])
        return tuple(g[k] for k in want_rows), tuple(g[n_prim + k] for k in want_params)
    return fn


def _rms(v):
    return v * lax.rsqrt(jnp.mean(v * v, axis=-1, keepdims=True) + EPS)


def _silu(v):
    return v * jax.nn.sigmoid(v)


def _t_prenorm(x, g, sc, sh):
    return _rms(x) * g * (1.0 + sc) + sh


def _t_gnorm(yf, yb, z, gn):
    return _rms((yf + yb) * _silu(z)) * gn


def _t_glu(a, b, ba, bb):
    return (a + ba) * jax.nn.sigmoid(b + bb)


def _t_lnsilu(u, g, b):
    uc = u - jnp.mean(u, axis=-1, keepdims=True)
    return _silu(uc * lax.rsqrt(jnp.mean(uc * uc, axis=-1, keepdims=True) + EPS) * g + b)


def _t_gatemix(ya, yb, la, lb, bco, bga, bgb):
    return jax.nn.sigmoid(la + bga) * ya + jax.nn.sigmoid(lb + bgb) * (yb + bco)


def _t_postpre(x, mix, gpost, g1, gpre, sc, sh):
    x1 = x + g1 * (_rms(mix) * gpost)
    return x1, _rms(x1) * gpre * (1.0 + sc) + sh


def _t_swiglu(gt, up):
    return _silu(gt) * up


def _halves(v):
    return _f(v[:, :COL_BLOCK]), _f(v[:, COL_BLOCK:])


def _glu_fwd(ab, bias):
    return (_t_glu(*_halves(ab), *_halves(bias)),), ()


def _glu_bwd(ab, du, bias):
    _, vjp = jax.vjp(_t_glu, *_halves(ab), *_halves(bias))
    da, db, dba, dbb = vjp(_f(du))
    return (jnp.concatenate([da, db], axis=1),), (jnp.concatenate([dba, dbb], axis=1),)


def _gatemix_fwd(ya, yb, logits, bco, bg):
    return (_t_gatemix(_f(ya), _f(yb), *_halves(logits), _f(bco), *_halves(bg)),), ()


def _gatemix_bwd(ya, yb, logits, dm, bco, bg):
    _, vjp = jax.vjp(_t_gatemix, _f(ya), _f(yb), *_halves(logits), _f(bco), *_halves(bg))
    dya, dyb, dla, dlb, dbco, dbga, dbgb = vjp(_f(dm))
    return (dya, dyb, jnp.concatenate([dla, dlb], axis=1)), (dbco, jnp.concatenate([dbga, dbgb], axis=1))


def _swiglu_fwd(gu):
    return (_t_swiglu(*_halves(gu)),), ()


def _swiglu_bwd(gu, dact):
    _, vjp = jax.vjp(_t_swiglu, *_halves(gu))
    return (jnp.concatenate(vjp(_f(dact)), axis=1),), ()


def _t_loss(x1, f, tgt, gpost, g2):
    e = x1 + g2 * (_rms(f) * gpost) - tgt
    return 0.5 * jnp.sum(jnp.mean(e * e, axis=-1))


def _conv_specs(s, tm, cw, ob, w_rows):
    nh, last_h = tm // HALO, s // HALO - 1
    return [
        pl.BlockSpec((tm, cw), lambda j, i: (i, ob + j)),
        pl.BlockSpec((HALO, cw), lambda j, i: (jnp.maximum(i * nh - 1, 0), ob + j)),
        pl.BlockSpec((HALO, cw), lambda j, i: (jnp.minimum((i + 1) * nh, last_h), ob + j)),
        pl.BlockSpec((w_rows, cw), lambda j, i: (0, j)),
        pl.BlockSpec((1, cw), lambda j, i: (0, j)),
    ]


def _window(x_ref, p_ref, n_ref, i, nrow):
    prev = _f(p_ref[...]) * jnp.where(i > 0, 1.0, 0.0).astype(F32)
    nxt = _f(n_ref[...]) * jnp.where(i < nrow - 1, 1.0, 0.0).astype(F32)
    return jnp.concatenate([prev, _f(x_ref[...]), nxt], axis=0)


def _shifted(win, k_taps, tm):
    pad, moved, out = (k_taps - 1) // 2, {}, []
    for k in range(k_taps):
        q, r = divmod(HALO + k - pad, 8)
        if r not in moved:
            moved[r] = pltpu.roll(win, win.shape[0] - r, 0) if r else win
        out.append(moved[r][8 * q:8 * q + tm, :])
    return out


def _taps(win, w_ref, k_taps, tm):
    sh = _shifted(win, k_taps, tm)
    acc = w_ref[0:1, :] * sh[0]
    for k in range(1, k_taps):
        acc = acc + w_ref[k:k + 1, :] * sh[k]
    return acc


def _conv_rows(k_taps):
    return 1024 if k_taps <= 8 else 512


def _dwconv(name, x, xoff, c, w_pad, k_taps, b, act, out_dtype, cw=256, into=None):
    s = x.shape[0]
    tm, cw = _tile(s, (_conv_rows(k_taps), 256, 128, 64, 32, 16)), _tile(c, (cw, 128))
    assert xoff % cw == 0 and tm % HALO == 0 and (k_taps - 1) // 2 <= HALO
    nrow = s // tm

    def body(x_ref, p_ref, n_ref, w_ref, b_ref, *rest):
        win = _window(x_ref, p_ref, n_ref, pl.program_id(1), nrow)
        acc = _taps(win, w_ref, k_taps, tm) + b_ref[...]
        rest[-1][...] = (_silu(acc) if act else acc).astype(out_dtype)

    in_specs, args = _conv_specs(s, tm, cw, xoff // cw, w_pad.shape[0]), [x, x, x, w_pad, b]
    if into is None:
        ob, out_shape, aliases = 0, jax.ShapeDtypeStruct((s, c), out_dtype), {}
    else:
        buf, first = into
        assert first % cw == 0 and buf.dtype == out_dtype
        ob, out_shape, aliases = first // cw, jax.ShapeDtypeStruct(buf.shape, out_dtype), {len(args): 0}
        in_specs, args = in_specs + [pl.BlockSpec(memory_space=pl.ANY)], args + [buf]
    return pl.pallas_call(
        body, name=name, grid=(c // cw, nrow), in_specs=in_specs,
        out_specs=pl.BlockSpec((tm, cw), lambda j, i: (i, ob + j)), out_shape=out_shape,
        input_output_aliases=aliases, compiler_params=_params(2),
    )(*args)


def _dwconv_bwd_w(name, x, xoff, c, w_pad, k_taps, b, douts, act, cw=256):
    s = x.shape[0]
    tm, cw = _tile(s, (_conv_rows(k_taps), 256, 128, 64, 32, 16)), _tile(c, (cw, 128))
    assert xoff % cw == 0 and tm % HALO == 0
    nrow, pad, n_d = s // tm, (k_taps - 1) // 2, len(douts)

    def body(x_ref, p_ref, n_ref, w_ref, b_ref, *rest):
        d_refs, outs = rest[:n_d], rest[n_d:]
        i = pl.program_id(1)
        win = _window(x_ref, p_ref, n_ref, i, nrow)
        d = _f(d_refs[0][...])
        for r in d_refs[1:]:
            d = d + _f(r[...])
        if act:
            shifted = _shifted(win, k_taps, tm)
        else:
            shifted = [win[HALO + k - pad:HALO + k - pad + tm, :] for k in range(k_taps)]
        if act:
            pre = b_ref[...] + w_ref[0:1, :] * shifted[0]
            for k in range(1, k_taps):
                pre = pre + w_ref[k:k + 1, :] * shifted[k]
            sg = jax.nn.sigmoid(pre)
            d = d * (sg * (1.0 + pre * (1.0 - sg)))
            outs[0][...] = d.astype(outs[0].dtype)
        dw_ref, db_ref = outs[-2], outs[-1]

        @pl.when(i == 0)
        def _():
            dw_ref[...] = jnp.zeros_like(dw_ref)
            db_ref[...] = jnp.zeros_like(db_ref)

        for k in range(k_taps):
            dw_ref[k:k + 1, :] += jnp.sum(d * shifted[k], axis=0, keepdims=True)
        db_ref[...] += jnp.sum(d, axis=0, keepdims=True)

    row_spec = pl.BlockSpec((tm, cw), lambda j, i: (i, j))
    out_shape = [jax.ShapeDtypeStruct((w_pad.shape[0], c), F32), jax.ShapeDtypeStruct((1, c), F32)]
    out_specs = [pl.BlockSpec((w_pad.shape[0], cw), lambda j, i: (0, j)), pl.BlockSpec((1, cw), lambda j, i: (0, j))]
    if act:
        out_shape, out_specs = [jax.ShapeDtypeStruct((s, c), BF16)] + out_shape, [row_spec] + out_specs
    return pl.pallas_call(
        body, name=name, grid=(c // cw, nrow),
        in_specs=_conv_specs(s, tm, cw, xoff // cw, w_pad.shape[0]) + [row_spec] * n_d,
        out_specs=out_specs, out_shape=out_shape, compiler_params=_params(2),
    )(x, x, x, w_pad, b, *douts)


def _dtprep(name, dt_raw, bias, alog_col):
    s, h2 = dt_raw.shape
    ts = _tile(s, (512, 256, 128))

    def body(r_ref, b_ref, al_ref, dt_ref, a_ref):
        v = r_ref[...] + b_ref[...]
        dt = (jnp.maximum(v, 0.0) + jnp.log(1.0 + jnp.exp(-jnp.abs(v)))).T
        dt_ref[...] = dt
        a_ref[...] = dt * (-jnp.exp(al_ref[...]))

    t_spec = pl.BlockSpec((h2, ts), lambda i: (0, i))
    return pl.pallas_call(
        body, name=name, grid=(s // ts,),
        in_specs=[pl.BlockSpec((ts, h2), lambda i: (i, 0)), pl.BlockSpec((1, h2), lambda i: (0, 0)), pl.BlockSpec((h2, 1), lambda i: (0, 0))],
        out_specs=[t_spec, t_spec], out_shape=[jax.ShapeDtypeStruct((h2, s), F32)] * 2, compiler_params=_params(1),
    )(dt_raw, bias, alog_col)


def _dtprep_bwd(name, ddt_f, ddt_b, da_f, da_b, dt_t, alog_col):
    h2, s = dt_t.shape
    h = h2 // 2
    ts = _tile(s, (512, 256, 128))

    def body(ddf, ddb, daf, dab, dt_ref, al_ref, raw_ref, db_ref, dal_ref):
        i = pl.program_id(0)
        a_col = -jnp.exp(al_ref[...])
        da = jnp.concatenate([daf[...], dab[...]], axis=0)
        dt = dt_ref[...]
        d_raw = (jnp.concatenate([ddf[...], ddb[...]], axis=0) + da * a_col) * (1.0 - jnp.exp(-dt))
        raw_ref[...] = d_raw.T

        @pl.when(i == 0)
        def _():
            db_ref[...] = jnp.zeros_like(db_ref)
            dal_ref[...] = jnp.zeros_like(dal_ref)

        db_ref[...] += jnp.sum(d_raw, axis=1, keepdims=True)
        dal_ref[...] += jnp.sum(da * dt, axis=1, keepdims=True) * a_col

    half = pl.BlockSpec((h, ts), lambda i: (0, i))
    col = pl.BlockSpec((h2, 1), lambda i: (0, 0))
    return pl.pallas_call(
        body, name=name, grid=(s // ts,),
        in_specs=[half, half, half, half, pl.BlockSpec((h2, ts), lambda i: (0, i)), col],
        out_specs=[pl.BlockSpec((ts, h2), lambda i: (i, 0)), col, col],
        out_shape=[jax.ShapeDtypeStruct((s, h2), F32), jax.ShapeDtypeStruct((h2, 1), F32), jax.ShapeDtypeStruct((h2, 1), F32)],
        compiler_params=_params(1),
    )(ddt_f, ddt_b, da_f, da_b, dt_t, alog_col)


def _dot_exact(a, b, dims, exact_lhs):
    other = b if exact_lhs else a
    out = None
    for _ in range(3):
        part = other.astype(BF16)
        other = other - _f(part)
        p = lax.dot_general(a if exact_lhs else part, part if exact_lhs else b, dims, preferred_element_type=F32)
        out = p if out is None else out + p
    return out


def _chunk_consts(a_r, dt_r, rev):
    ii = lax.broadcasted_iota(jnp.int32, (CHUNK, CHUNK), 0)
    jj = lax.broadcasted_iota(jnp.int32, (CHUNK, CHUNK), 1)
    w = (jj >= ii) if rev else (jj <= ii)
    wt = (ii >= jj) if rev else (ii <= jj)
    eye = (ii == jj).astype(F32)
    wb_ = w.astype(BF16)
    cs_col = _dot_exact(wb_, a_r, NT, exact_lhs=True)
    cs_row = _dot_exact(a_r, wb_, NT, exact_lhs=False)
    tot = jnp.sum(a_r, axis=1, keepdims=True)
    lo_half = lax.broadcasted_iota(jnp.int32, (CHUNK, 128), 1) < HEAD_DIM

    def lanes(col):
        return [jnp.broadcast_to(col[:, j:j + 1], (CHUNK, 128)) for j in range(HEADS_PER_GROUP)]

    def spread(bs):
        return jnp.concatenate([jnp.where(lo_half, bs[2 * k], bs[2 * k + 1]) for k in range(HEADS_PER_GROUP // 2)], axis=1)

    csc = lanes(cs_col)
    cs_x = spread(csc)
    hh = lax.broadcasted_iota(jnp.int32, (HEADS_PER_GROUP, GROUP_X), 0)
    ll = lax.broadcasted_iota(jnp.int32, (HEADS_PER_GROUP, GROUP_X), 1)
    expand = (lax.shift_right_logical(ll, 6) == hh).astype(BF16)
    dt_hi = dt_r.astype(BF16)
    dt_x = (lax.dot_general(dt_hi, expand, TN, preferred_element_type=F32)
            + lax.dot_general((dt_r - _f(dt_hi)).astype(BF16), expand, TN, preferred_element_type=F32))
    tot_x = cs_x[0:1, :] if rev else cs_x[CHUNK - 1:CHUNK, :]
    return w, wt, eye, csc, cs_row, tot, cs_x, dt_x, tot_x


def _hi_lo(v):
    hi = v.astype(BF16)
    return jnp.concatenate([hi, (v - _f(hi)).astype(BF16)], axis=1)


def _head_sums(v, terms=3):
    rr = lax.broadcasted_iota(jnp.int32, (GROUP_X, HEADS_PER_GROUP), 0)
    cc = lax.broadcasted_iota(jnp.int32, (GROUP_X, HEADS_PER_GROUP), 1)
    et = (lax.shift_right_logical(rr, 6) == cc).astype(BF16)
    out, rest = None, v
    for t in range(terms):
        part = rest.astype(BF16)
        if t < terms - 1:
            rest = rest - _f(part)
        p = jnp.dot(part, et, preferred_element_type=F32)
        out = p if out is None else out + p
    return out


def _ssd_fwd(name, xbc, dt_t, a_t, dsk, n_groups):
    s = xbc.shape[0]
    nc = s // CHUNK
    d_ssm = n_groups * GROUP_X

    def one(x_ref, dt_ref, a_ref, dsk_ref, y_ref, hin_ref, h_scr, rev):
        skip = dsk_ref is not None
        xs = _f(x_ref[:, 0:GROUP_X])
        bm = x_ref[:, GROUP_X:GROUP_X + D_STATE]
        cm = x_ref[:, GROUP_X + D_STATE:GROUP_W]
        w, _, _, csc, cs_row, tot, cs_x, dt_x, tot_x = _chunk_consts(a_ref[...], dt_ref[...], rev)
        cb = lax.dot_general(cm, bm, NT, preferred_element_type=F32)
        h = h_scr[...]
        h_b = h.astype(BF16)
        hin_ref[0, 0] = h_b
        xdt = xs * dt_x
        xdt_b = xdt.astype(BF16)
        yo = lax.dot_general(cm, h_b, NT, preferred_element_type=F32) * jnp.exp(cs_x)
        st = lax.dot_general((xdt * jnp.exp(tot_x - cs_x)).astype(BF16), bm, TN, preferred_element_type=F32)
        yds = []
        for j in range(HEADS_PER_GROUP):
            lo = HEAD_DIM * j
            h_scr[lo:lo + HEAD_DIM, :] = h[lo:lo + HEAD_DIM, :] * jnp.exp(tot[j:j + 1, 0:1]) + st[lo:lo + HEAD_DIM, :]
            lm = jnp.exp(jnp.where(w, csc[j] - cs_row[j:j + 1, :], -jnp.inf))
            yds.append(jnp.dot((cb * lm).astype(BF16), xdt_b[:, lo:lo + HEAD_DIM], preferred_element_type=F32))
        y = jnp.concatenate(yds, axis=1) + yo
        if skip:
            y = y + dsk_ref[...] * xs
        y_ref[...] = y.astype(y_ref.dtype)

    def body(xf_ref, dtf_ref, af_ref, xr_ref, dtr_ref, ar_ref, dsk_ref, yf_ref, hf_ref, yr_ref, hr_ref, hf_scr, hr_scr):
        @pl.when(pl.program_id(1) == 0)
        def _():
            hf_scr[...] = jnp.zeros_like(hf_scr)
            hr_scr[...] = jnp.zeros_like(hr_scr)

        one(xf_ref, dtf_ref, af_ref, dsk_ref, yf_ref, hf_ref, hf_scr, False)
        one(xr_ref, dtr_ref, ar_ref, None, yr_ref, hr_ref, hr_scr, True)

    def specs(rev):
        def z(zi):
            return nc - 1 - zi if rev else zi
        t_spec = pl.BlockSpec((HEADS_PER_GROUP, CHUNK), lambda g, zi: ((n_groups if rev else 0) + g, z(zi)))
        ins = [pl.BlockSpec((CHUNK, GROUP_W), lambda g, zi: (z(zi), g)), t_spec, t_spec]
        outs = [pl.BlockSpec((CHUNK, GROUP_X), lambda g, zi: (z(zi), g)),
                pl.BlockSpec((1, 1, GROUP_X, D_STATE), lambda g, zi: (g, z(zi), 0, 0))]
        return ins, outs

    (in_f, out_f), (in_r, out_r) = specs(False), specs(True)
    shapes = [jax.ShapeDtypeStruct((s, d_ssm), BF16), jax.ShapeDtypeStruct((n_groups, nc, GROUP_X, D_STATE), BF16)]
    return pl.pallas_call(
        body, name=name, grid=(n_groups, nc), in_specs=in_f + in_r + [pl.BlockSpec((1, GROUP_X), lambda g, zi: (0, g))],
        out_specs=out_f + out_r, out_shape=shapes + shapes,
        scratch_shapes=[pltpu.VMEM((GROUP_X, D_STATE), F32)] * 2, compiler_params=_params(2),
    )(xbc, dt_t, a_t, xbc, dt_t, a_t, dsk)


def _ssd_bwd(name, xbc, dt_t, a_t, dy, hin_f, hin_r, dsk, n_groups):
    s = xbc.shape[0]
    nc = s // CHUNK
    n_heads = n_groups * HEADS_PER_GROUP

    def one(x_ref, dt_ref, a_ref, dy_ref, hin_ref, dsk_ref, dx_ref, ddt_ref, da_ref, ddsk_ref, g_scr, rev):
        skip = dsk_ref is not None
        xs = _f(x_ref[:, 0:GROUP_X])
        bm = x_ref[:, GROUP_X:GROUP_X + D_STATE]
        cm = x_ref[:, GROUP_X + D_STATE:GROUP_W]
        dyv = _f(dy_ref[...])
        w, wt, eye, csc, cs_row, tot, cs_x, dt_x, tot_x = _chunk_consts(a_ref[...], dt_ref[...], rev)
        cb = lax.dot_general(cm, bm, NT, preferred_element_type=F32)
        cbt = lax.dot_general(bm, cm, NT, preferred_element_type=F32)
        onehot = lax.broadcasted_iota(jnp.int32, (1, HEADS_PER_GROUP), 1)
        hin = hin_ref[0, 0]
        g = g_scr[...]
        g_b = g.astype(BF16)
        din, dte = jnp.exp(cs_x), jnp.exp(tot_x - cs_x)
        xdt = xs * dt_x
        xdt_b, dy_b = xdt.astype(BF16), dyv.astype(BF16)
        dyo_b = (dyv * din).astype(BF16)
        yo = lax.dot_general(cm, hin, NT, preferred_element_type=F32) * din
        dxdt_s = lax.dot_general(bm, g_b, NT, preferred_element_type=F32) * dte
        d_c = jnp.dot(dyo_b, hin, preferred_element_type=F32)
        d_b = jnp.dot((xdt * dte).astype(BF16), g_b, preferred_element_type=F32)
        dhin = lax.dot_general(dyo_b, cm, TN, preferred_element_type=F32)
        dcb = jnp.zeros((CHUNK, CHUNK), F32)
        dtot = jnp.zeros((1, HEADS_PER_GROUP), F32)
        yds, dxds = [], []
        for j in range(HEADS_PER_GROUP):
            lo = HEAD_DIM * j
            cd = jnp.exp(tot[j:j + 1, 0:1])
            gj = g[lo:lo + HEAD_DIM, :]
            dcd = jnp.sum(jnp.sum(gj * _f(hin[lo:lo + HEAD_DIM, :]), axis=1, keepdims=True), axis=0, keepdims=True)
            dtot = dtot + (dcd * cd) * (onehot == j).astype(F32)
            g_scr[lo:lo + HEAD_DIM, :] = dhin[lo:lo + HEAD_DIM, :] + gj * cd
            dd = csc[j] - cs_row[j:j + 1, :]
            lm = jnp.exp(jnp.where(w, dd, -jnp.inf))
            lmt = jnp.exp(jnp.where(wt, -dd, -jnp.inf))
            xj_b, dyj_b = xdt_b[:, lo:lo + HEAD_DIM], dy_b[:, lo:lo + HEAD_DIM]
            yds.append(jnp.dot(_hi_lo(cb * lm), jnp.concatenate([xj_b, xj_b], axis=0), preferred_element_type=F32))
            dxds.append(jnp.dot(_hi_lo(cbt * lmt), jnp.concatenate([dyj_b, dyj_b], axis=0), preferred_element_type=F32))
            dcb = dcb + lax.dot_general(dyj_b, xj_b, NT, preferred_element_type=F32) * lm
        dxdt_d = jnp.concatenate(dxds, axis=1)
        z_state = xdt * dxdt_s
        yd = jnp.concatenate(yds, axis=1)
        dcs = _head_sums(dyv * yo + _f(dy_b) * yd - _f(xdt_b) * dxdt_d - z_state)
        dtot = dtot + _head_sums(jnp.broadcast_to(jnp.sum(z_state, axis=0, keepdims=True), (8, GROUP_X)))[0:1, :]
        dxdt = dxdt_s + dxdt_d
        ddt_c = _head_sums(dxdt * xs, terms=2)
        dx = dxdt * dt_x
        if skip:
            dx = dx + dsk_ref[...] * dyv
        dcb_b = dcb.astype(BF16)
        d_c = d_c + jnp.dot(dcb_b, bm, preferred_element_type=F32)
        d_b = d_b + lax.dot_general(dcb_b, cm, TN, preferred_element_type=F32)
        eye_b = eye.astype(BF16)
        da_c = _dot_exact(wt.astype(BF16), dcs, NN, exact_lhs=True) + dtot
        ddt_ref[...] = _dot_exact(ddt_c, eye_b, TN, exact_lhs=False)
        da_ref[...] = _dot_exact(da_c, eye_b, TN, exact_lhs=False)
        dx_ref[...] = jnp.concatenate([dx, d_b, d_c], axis=1).astype(dx_ref.dtype)
        if skip:
            ddsk_ref[...] += jnp.sum(dyv * xs, axis=0, keepdims=True)

    def body(xf, dtf, af, dyf, hf, xr, dtr, ar, dyr, hr, dsk_ref, dxf, ddtf, daf, dxr, ddtr, dar, ddsk_ref, gf_scr, gr_scr):
        @pl.when(pl.program_id(1) == 0)
        def _():
            gf_scr[...] = jnp.zeros_like(gf_scr)
            gr_scr[...] = jnp.zeros_like(gr_scr)
            ddsk_ref[...] = jnp.zeros_like(ddsk_ref)

        one(xf, dtf, af, dyf, hf, dsk_ref, dxf, ddtf, daf, ddsk_ref, gf_scr, False)
        one(xr, dtr, ar, dyr, hr, None, dxr, ddtr, dar, None, gr_scr, True)

    def specs(rev):
        def z(zi):
            return zi if rev else nc - 1 - zi
        t_in = pl.BlockSpec((HEADS_PER_GROUP, CHUNK), lambda g, zi: ((n_groups if rev else 0) + g, z(zi)))
        t_out = pl.BlockSpec((HEADS_PER_GROUP, CHUNK), lambda g, zi: (g, z(zi)))
        x_spec = pl.BlockSpec((CHUNK, GROUP_W), lambda g, zi: (z(zi), g))
        ins = [x_spec, t_in, t_in, pl.BlockSpec((CHUNK, GROUP_X), lambda g, zi: (z(zi), g)),
               pl.BlockSpec((1, 1, GROUP_X, D_STATE), lambda g, zi: (g, z(zi), 0, 0))]
        return ins, [x_spec, t_out, t_out]

    (in_f, out_f), (in_r, out_r) = specs(False), specs(True)
    lane_spec = pl.BlockSpec((1, GROUP_X), lambda g, zi: (0, g))
    shapes = [jax.ShapeDtypeStruct((s, n_groups * GROUP_W), BF16), jax.ShapeDtypeStruct((n_heads, s), F32),
              jax.ShapeDtypeStruct((n_heads, s), F32)]
    return pl.pallas_call(
        body, name=name, grid=(n_groups, nc), in_specs=in_f + in_r + [lane_spec], out_specs=out_f + out_r + [lane_spec],
        out_shape=shapes + shapes + [jax.ShapeDtypeStruct((1, n_groups * GROUP_X), F32)],
        scratch_shapes=[pltpu.VMEM((GROUP_X, D_STATE), F32)] * 2, compiler_params=_params(2),
    )(xbc, dt_t, a_t, dy, hin_f, xbc, dt_t, a_t, dy, hin_r, dsk)


def _mesh_pos():
    return lax.axis_index("x"), lax.axis_index("y"), lax.axis_index("c")


def _slot(p):
    return 4 * p[0] + 2 * p[1] + p[2]


def _exchange_sems(n):
    return [pltpu.SemaphoreType.DMA((7 * n,)), pltpu.SemaphoreType.DMA((7 * n,)), pltpu.SemaphoreType.DMA((n,))]


def _gather_ops(ins, outs, send, recv, loc):
    n = len(ins)
    x, y, c = _mesh_pos()
    me, sib = (x, y, c), (x, y, 1 - c)
    chips = [(1 - x, y), (x, 1 - y), (1 - x, 1 - y)]

    def cp(a, k, block, to, src=None):
        dst = outs[a].at[_slot(block)]
        return pltpu.make_async_remote_copy(
            src_ref=dst if src is None else src, dst_ref=dst, send_sem=send.at[7 * a + k], recv_sem=recv.at[7 * a + k],
            device_id=to, device_id_type=MESH)

    mine = [pltpu.make_async_copy(ins[a], outs[a].at[_slot(me)], loc.at[a]) for a in range(n)]
    first = []
    for a in range(n):
        first.append(cp(a, 0, me, sib, src=ins[a]))
        first += [cp(a, 1 + j, me, (*chip, c), src=ins[a]) for j, chip in enumerate(chips)]

    def start():
        for d in mine + first:
            d.start()

    def finish():
        passed = []
        for a in range(n):
            for j, chip in enumerate(chips):
                cp(a, 1 + j, (*chip, c), me).wait_recv()
                p = cp(a, 4 + j, (*chip, c), sib)
                p.start()
                passed.append(p)
        for a in range(n):
            cp(a, 0, sib, me).wait_recv()
            for j, chip in enumerate(chips):
                cp(a, 4 + j, (*chip, 1 - c), me).wait_recv()
        for f in first + passed:
            f.wait_send()
        for m in mine:
            m.wait()

    return start, finish


_gather_ops.out_shapes = lambda arrs: [jax.ShapeDtypeStruct((N_DEV,) + a.shape, a.dtype) for a in arrs]


def _a2a_ops(ins, outs, send, recv, loc):
    n = len(ins)
    me = _mesh_pos()
    flips = [(fx, fy, fc) for fx in (0, 1) for fy in (0, 1) for fc in (0, 1)][1:]
    peers = [tuple(1 - p if fl else p for p, fl in zip(me, fl3)) for fl3 in flips]

    def cp(a, k, peer, dst_slot):
        return pltpu.make_async_remote_copy(
            src_ref=ins[a].at[_slot(peer)], dst_ref=outs[a].at[dst_slot], send_sem=send.at[7 * a + k],
            recv_sem=recv.at[7 * a + k], device_id=peer, device_id_type=MESH)

    mine = [pltpu.make_async_copy(ins[a].at[_slot(me)], outs[a].at[_slot(me)], loc.at[a]) for a in range(n)]
    sends = [cp(a, k, peer, _slot(me)) for a in range(n) for k, peer in enumerate(peers)]

    def start():
        for d in mine + sends:
            d.start()

    def finish():
        for a in range(n):
            for k, peer in enumerate(peers):
                cp(a, k, peer, _slot(peer)).wait_recv()
        for s_ in sends:
            s_.wait_send()
        for m in mine:
            m.wait()

    return start, finish


_a2a_ops.out_shapes = lambda arrs: [jax.ShapeDtypeStruct(a.shape, a.dtype) for a in arrs]


def _exchange(name, ops, arrs):
    n = len(arrs)

    def body(*refs):
        start, finish = ops(refs[:n], refs[n:2 * n], *refs[2 * n:])
        start()
        finish()

    any_spec = pl.BlockSpec(memory_space=pl.ANY)
    return pl.pallas_call(
        body, name=name, in_specs=[any_spec] * n, out_specs=[any_spec] * n, out_shape=ops.out_shapes(arrs),
        scratch_shapes=_exchange_sems(n),
    )(*arrs)


def _all_gather(name, arrs):
    return _exchange(name, _gather_ops, arrs)


def _all_to_all(name, arrs):
    return _exchange(name, _a2a_ops, arrs)


def _sum8(parts_ref):
    g = _f(parts_ref[0])
    for j in range(1, N_DEV):
        g = g + _f(parts_ref[j])
    return g


def _adam_math(w, g, m, v):
    m = ADAM_B1 * m + (1.0 - ADAM_B1) * g
    v = ADAM_B2 * v + (1.0 - ADAM_B2) * (g * g)
    m_hat = m / (1.0 - ADAM_B1 ** ADAM_STEP)
    v_hat = v / (1.0 - ADAM_B2 ** ADAM_STEP)
    return -ADAM_LR * (m_hat / (jnp.sqrt(v_hat) + ADAM_EPS) + ADAM_WD * w), m, v


def _reduce_sum(name, parts):
    _, r, c = parts.shape

    def body(p_ref, o_ref):
        o_ref[...] = _sum8(p_ref)

    return pl.pallas_call(body, name=name, out_shape=jax.ShapeDtypeStruct((r, c), F32))(parts)


def _adam(name, w, g, m, v, parts=None):
    r, c = w.shape
    tr = _tile(r, (128, 64, 32, 16)) if r * c > 2 ** 18 else r
    spec = pl.BlockSpec((tr, c), lambda i: (i, 0))
    summed = parts is not None

    def body(*refs):
        if summed:
            p_ref, w_ref, m_ref, v_ref, g_out, d_out, m_out, v_out = refs
            gv = _sum8(p_ref)
            g_out[...] = gv
        else:
            g_ref, w_ref, m_ref, v_ref, d_out, m_out, v_out = refs
            gv = g_ref[...]
        d, mn, vn = _adam_math(w_ref[...], gv, m_ref[...], v_ref[...])
        d_out[...] = d
        m_out[...] = mn
        v_out[...] = vn

    first = [pl.BlockSpec((N_DEV, tr, c), lambda i: (0, i, 0))] if summed else [spec]
    n_out = 4 if summed else 3
    return pl.pallas_call(
        body, name=name, grid=(r // tr,), in_specs=first + [spec] * 3, out_specs=[spec] * n_out,
        out_shape=[jax.ShapeDtypeStruct((r, c), F32)] * n_out, compiler_params=_params(1),
    )(parts if summed else g, w, m, v)


def _adam_many(name, ws, gs, ms, vs):
    n = len(ws)

    def body(*refs):
        outs = refs[4 * n:]
        for k in range(n):
            d, mn, vn = _adam_math(refs[k][...], refs[n + k][...], refs[2 * n + k][...], refs[3 * n + k][...])
            outs[k][...] = d
            outs[n + k][...] = mn
            outs[2 * n + k][...] = vn

    res = pl.pallas_call(
        body, name=name, out_shape=[jax.ShapeDtypeStruct(w.shape, F32) for w in ws] * 3,
        compiler_params=pltpu.CompilerParams(vmem_limit_bytes=VMEM_LIMIT),
    )(*ws, *gs, *ms, *vs)
    return res[:n], res[n:2 * n], res[2 * n:]


def _slot_offsets(sizes):
    offs, o = [], 0
    for sz in sizes:
        offs.append(o)
        o += -(-sz // 128) * 128
    return offs, o + (-o % 1024)


def _pack_vectors(name, vecs):
    sizes = [v.shape[1] for v in vecs]
    offs, total = _slot_offsets(sizes)

    def body(*refs):
        out = refs[-1]
        out[...] = jnp.zeros_like(out)
        for r, o, sz in zip(refs[:-1], offs, sizes):
            out[:, o:o + sz] = r[...]

    return pl.pallas_call(body, name=name, out_shape=jax.ShapeDtypeStruct((1, total), F32),
                          compiler_params=pltpu.CompilerParams(vmem_limit_bytes=VMEM_LIMIT))(*vecs).reshape(-1, 128)


def _xbc_runs(n_groups, base=0):
    ds, nb = n_groups * GROUP_X, n_groups * D_STATE
    runs = []
    for g in range(n_groups):
        runs += [(base + GROUP_X * g, GROUP_X), (base + ds + D_STATE * g, D_STATE), (base + ds + nb + D_STATE * g, D_STATE)]
    return runs


def _interleave_runs(width, base=0):
    half = width // 2
    runs = []
    for j in range(half // COL_BLOCK):
        runs += [(base + COL_BLOCK * j, COL_BLOCK), (base + half + COL_BLOCK * j, COL_BLOCK)]
    return runs


def _w_in_runs(d, d_ssm, d_xbc, n_heads, n_groups):
    o_dt = d_ssm + d_xbc
    o_glu = o_dt + 2 * n_heads
    return ([(0, d_ssm)] + _xbc_runs(n_groups, d_ssm) + _interleave_runs(2 * d, o_glu) + _interleave_runs(2 * d, o_glu + 2 * d)
            + [(o_dt, 2 * n_heads)])


def _inverse_runs(runs):
    offs = np.concatenate([[0], np.cumsum([l for _, l in runs])])
    order = sorted(range(len(runs)), key=lambda k: runs[k][0])
    return [(int(offs[k]), runs[k][1]) for k in order]


def _permute_cols(name, srcs, runs, out_widths=None, src_sharded=False, dst_sharded=False):
    rows = srcs[0].shape[-2]
    dtype = srcs[0].dtype
    total = sum(l for _, l in runs)
    src_w = [srcs[0].shape[2]] * N_DEV if src_sharded else [a.shape[1] for a in srcs]
    dst_w = [total // N_DEV] * N_DEV if dst_sharded else (out_widths or [total])
    assert sum(dst_w) == total and sum(src_w) >= max(s_ + l for s_, l in runs)
    src_edges, dst_edges = np.cumsum([0] + src_w), np.cumsum([0] + dst_w)

    def locate(edges, col):
        k = int(np.searchsorted(edges, col, side="right")) - 1
        return k, col - int(edges[k])

    pieces, o = [], 0
    for s_, l in runs:
        while l:
            (ks, cs_), (kd, cd) = locate(src_edges, s_), locate(dst_edges, o)
            n = min(l, src_w[ks] - cs_, dst_w[kd] - cd)
            if pieces and pieces[-1][0] == ks and pieces[-1][2] == kd and pieces[-1][1] + pieces[-1][4] == cs_:
                pieces[-1] = pieces[-1][:4] + (pieces[-1][4] + n,)
            else:
                pieces.append((ks, cs_, kd, cd, n))
            s_, o, l = s_ + n, o + n, l - n
    tr = rows
    while tr % 16 == 0 and tr * max(sum(src_w), total) * dtype.itemsize > 4 * 2 ** 20:
        tr //= 2
    n_src = len(srcs)

    def body(*refs):
        ins, outs = refs[:n_src], refs[n_src:]
        for ks, cs_, kd, cd, n in pieces:
            val = ins[0][ks, :, cs_:cs_ + n] if src_sharded else ins[ks][:, cs_:cs_ + n]
            if dst_sharded:
                outs[0][kd, :, cd:cd + n] = val
            else:
                outs[kd][:, cd:cd + n] = val

    def spec(sharded, width):
        if sharded:
            return pl.BlockSpec((N_DEV, tr, width), lambda i: (0, i, 0))
        return pl.BlockSpec((tr, width), lambda i: (i, 0))

    in_specs = [spec(True, src_w[0])] if src_sharded else [spec(False, w) for w in src_w]
    if dst_sharded:
        out_specs, out_shape = [spec(True, dst_w[0])], [jax.ShapeDtypeStruct((N_DEV, rows, dst_w[0]), dtype)]
    else:
        out_specs, out_shape = [spec(False, w) for w in dst_w], [jax.ShapeDtypeStruct((rows, w), dtype) for w in dst_w]
    return pl.pallas_call(body, name=name, grid=(rows // tr,), in_specs=in_specs, out_specs=out_specs, out_shape=out_shape,
                          compiler_params=_params(1))(*srcs)


def _cols_from_shards(g):
    return g.transpose(1, 0, 2).reshape(g.shape[1], -1)


def _cols_to_shards(a):
    return a.reshape(a.shape[0], N_DEV, -1).transpose(1, 0, 2)


def _pad_rows(a, rows):
    return jnp.concatenate([a, jnp.zeros((rows - a.shape[0], a.shape[1]), a.dtype)], axis=0)


def _pack(vecs):
    a = jnp.concatenate(vecs, axis=1)
    return jnp.concatenate([a, jnp.zeros((1, -a.shape[1] % 1024), a.dtype)], axis=1).reshape(-1, 128)


def kernel(x, c, w_ada, b_ada, g_pre_mix, g_post_mix, w_in, w_conv_ssm, b_conv_ssm, dt_bias_fwd, dt_bias_bwd, a_log_fwd, a_log_bwd, d_skip, g_ssm_norm, w_ssm_out, b_glu, w_dw, b_dw, ln_g, ln_b, w_conv_out, b_conv_out, b_gate, w_mix_out, g_pre_ffn, g_post_ffn, w_gate_up, w_down, loss_target, m_w_ada, m_b_ada, m_g_pre_mix, m_g_post_mix, m_w_in, m_w_conv_ssm, m_b_conv_ssm, m_dt_bias_fwd, m_dt_bias_bwd, m_a_log_fwd, m_a_log_bwd, m_d_skip, m_g_ssm_norm, m_w_ssm_out, m_b_glu, m_w_dw, m_b_dw, m_ln_g, m_ln_b, m_w_conv_out, m_b_conv_out, m_b_gate, m_w_mix_out, m_g_pre_ffn, m_g_post_ffn, m_w_gate_up, m_w_down, v_w_ada, v_b_ada, v_g_pre_mix, v_g_post_mix, v_w_in, v_w_conv_ssm, v_b_conv_ssm, v_dt_bias_fwd, v_dt_bias_bwd, v_a_log_fwd, v_a_log_bwd, v_d_skip, v_g_ssm_norm, v_w_ssm_out, v_b_glu, v_w_dw, v_b_dw, v_ln_g, v_ln_b, v_w_conv_out, v_b_conv_out, v_b_gate, v_w_mix_out, v_g_pre_ffn, v_g_post_ffn, v_w_gate_up, v_w_down):
    weights = dict(w_ada=w_ada, b_ada=b_ada, g_pre_mix=g_pre_mix, g_post_mix=g_post_mix, w_in=w_in, w_conv_ssm=w_conv_ssm, b_conv_ssm=b_conv_ssm, dt_bias_fwd=dt_bias_fwd, dt_bias_bwd=dt_bias_bwd, a_log_fwd=a_log_fwd, a_log_bwd=a_log_bwd, d_skip=d_skip, g_ssm_norm=g_ssm_norm, w_ssm_out=w_ssm_out, b_glu=b_glu, w_dw=w_dw, b_dw=b_dw, ln_g=ln_g, ln_b=ln_b, w_conv_out=w_conv_out, b_conv_out=b_conv_out, b_gate=b_gate, w_mix_out=w_mix_out, g_pre_ffn=g_pre_ffn, g_post_ffn=g_post_ffn, w_gate_up=w_gate_up, w_down=w_down)
    mom_m = dict(w_ada=m_w_ada, b_ada=m_b_ada, g_pre_mix=m_g_pre_mix, g_post_mix=m_g_post_mix, w_in=m_w_in, w_conv_ssm=m_w_conv_ssm, b_conv_ssm=m_b_conv_ssm, dt_bias_fwd=m_dt_bias_fwd, dt_bias_bwd=m_dt_bias_bwd, a_log_fwd=m_a_log_fwd, a_log_bwd=m_a_log_bwd, d_skip=m_d_skip, g_ssm_norm=m_g_ssm_norm, w_ssm_out=m_w_ssm_out, b_glu=m_b_glu, w_dw=m_w_dw, b_dw=m_b_dw, ln_g=m_ln_g, ln_b=m_ln_b, w_conv_out=m_w_conv_out, b_conv_out=m_b_conv_out, b_gate=m_b_gate, w_mix_out=m_w_mix_out, g_pre_ffn=m_g_pre_ffn, g_post_ffn=m_g_post_ffn, w_gate_up=m_w_gate_up, w_down=m_w_down)
    mom_v = dict(w_ada=v_w_ada, b_ada=v_b_ada, g_pre_mix=v_g_pre_mix, g_post_mix=v_g_post_mix, w_in=v_w_in, w_conv_ssm=v_w_conv_ssm, b_conv_ssm=v_b_conv_ssm, dt_bias_fwd=v_dt_bias_fwd, dt_bias_bwd=v_dt_bias_bwd, a_log_fwd=v_a_log_fwd, a_log_bwd=v_a_log_bwd, d_skip=v_d_skip, g_ssm_norm=v_g_ssm_norm, w_ssm_out=v_w_ssm_out, b_glu=v_b_glu, w_dw=v_w_dw, b_dw=v_b_dw, ln_g=v_ln_g, ln_b=v_ln_b, w_conv_out=v_w_conv_out, b_conv_out=v_b_conv_out, b_gate=v_b_gate, w_mix_out=v_w_mix_out, g_pre_ffn=v_g_pre_ffn, g_post_ffn=v_g_post_ffn, w_gate_up=v_w_gate_up, w_down=v_w_down)
    names = list(weights)

    s, d = x.shape[1], x.shape[2]
    d_ssm, n_heads, d_xbc = g_ssm_norm.shape[-1], d_skip.shape[-1], b_conv_ssm.shape[-1]
    n_groups = n_heads // HEADS_PER_GROUP
    d_ff = w_down.shape[1] * N_DEV
    k_ssm, k_dw = w_conv_ssm.shape[1], w_dw.shape[1]
    assert d_ssm == n_groups * GROUP_X and d_xbc == n_groups * GROUP_W and s % CHUNK == 0
    assert d % COL_BLOCK == 0 and d_ff % COL_BLOCK == 0
    mc = w_ada.shape[-1]
    me = _slot(_mesh_pos())
    wb = d_ssm + d_xbc + 4 * d
    p_x, p_glu, p_gate = d_ssm, d_ssm + d_xbc, d_ssm + d_xbc + 2 * d
    ncb, nfb = d // COL_BLOCK, d_ff // COL_BLOCK
    x2, tgt = x[0], loss_target[0]

    big = ["w_in", "w_ssm_out", "w_conv_out", "w_mix_out", "w_gate_up", "w_down"]
    (w_in_g,) = _all_gather("gather_w_in", [w_in[0].astype(BF16)])
    w_in_runs, xbc_runs, il_runs = _w_in_runs(d, d_ssm, d_xbc, n_heads, n_groups), _xbc_runs(n_groups), _interleave_runs(2 * d)
    w_big, w_dt = _permute_cols("order_w_in", [w_in_g], w_in_runs, out_widths=[wb, 2 * n_heads], src_sharded=True)

    n_cs, n_dw = k_ssm * (d_xbc // N_DEV), k_dw * (d // N_DEV)
    (small_all,) = _all_gather("gather_small", [_pack([c, w_conv_ssm.reshape(1, n_cs), w_dw.reshape(1, n_dw)])])
    small_all = small_all.reshape(N_DEV, -1)
    c_all = small_all[:, :d]
    wcs = _cols_from_shards(small_all[:, d:d + n_cs].reshape(N_DEV, k_ssm, -1))
    wdw = _cols_from_shards(small_all[:, d + n_cs:d + n_cs + n_dw].reshape(N_DEV, k_dw, -1))
    (wcs_p,) = _permute_cols("order_conv_taps", [wcs], xbc_runs)
    (bcs_p,) = _permute_cols("order_conv_bias", [b_conv_ssm], xbc_runs)
    (b_glu_il,), (b_gate_il,) = _permute_cols("order_b_glu", [b_glu], il_runs), _permute_cols("order_b_gate", [b_gate], il_runs)
    kp_ssm, kp_dw = -(-k_ssm // 8) * 8, -(-k_dw // 8) * 8
    zero_x, zero_d = jnp.zeros((1, d_xbc), F32), jnp.zeros((1, d), F32)

    (c_act,), _ = _rowcall("c_act", _fwd_fn(_silu), [(c_all, 0, d)], [], [(d, F32)], [])
    b_my = lax.dynamic_slice(b_ada, (0, me * mc), (1, mc))
    mod_cols = _matmul("mod", c_act, w_ada[0], "nn", F32, acc_in=jnp.broadcast_to(b_my, (N_DEV, mc)))
    (mod_rows,) = _all_to_all("mod_to_owner", [mod_cols.reshape(N_DEV, 1, mc)])
    mod = mod_rows.reshape(1, N_DEV * mc)
    sh1, sc1, g1, sh2, sc2, g2 = [mod[:, k * d:(k + 1) * d] for k in range(6)]

    (h1,), _ = _rowcall("prenorm_mix", _fwd_fn(_t_prenorm), [(x2, 0, d)], [g_pre_mix, sc1, sh1], [(d, BF16)], [])
    proj, rest = _matmul("in_proj", h1, w_big, "nn", BF16, carry=(_gather_ops, [weights[k][0].astype(BF16) for k in big[1:]]))
    gath = dict(zip(big[1:], rest))
    w_ssm, w_co, w_mo = gath["w_ssm_out"].reshape(d_ssm, d), gath["w_conv_out"].reshape(d, d), gath["w_mix_out"].reshape(d, d)
    gu_runs = _interleave_runs(2 * d_ff)
    (w_gu,) = _permute_cols("order_w_gate_up", [gath["w_gate_up"]], gu_runs, src_sharded=True)
    w_dn = gath["w_down"].reshape(d_ff, d)
    dt_raw = _matmul("in_proj_dt", h1, w_dt, "nn", F32)
    xbc_c = _dwconv("conv_ssm", proj, p_x, d_xbc, _pad_rows(wcs_p, kp_ssm), k_ssm, bcs_p, True, BF16)
    alog_col = jnp.concatenate([a_log_fwd, a_log_bwd], axis=1).reshape(2 * n_heads, 1)
    dt_t, a_t = _dtprep("dt_prep", dt_raw, jnp.concatenate([dt_bias_fwd, dt_bias_bwd], axis=1), alog_col)
    dsk = jnp.repeat(d_skip, HEAD_DIM, axis=1)
    y_f, hin_f, y_b, hin_b = _ssd_fwd("ssd", xbc_c, dt_t, a_t, dsk, n_groups)
    gn_rows = [(y_f, 0, d_ssm), (y_b, 0, d_ssm), (proj, 0, d_ssm)]
    (yn,), _ = _rowcall("ssm_norm", _fwd_fn(_t_gnorm), gn_rows, [g_ssm_norm], [(d_ssm, BF16)], [], ncol=n_groups)
    ya = _matmul("ssm_out", yn, w_ssm, "nn", BF16)
    (u0,), _ = _rowcall("glu", _glu_fwd, [(proj, p_glu, 2 * d)], [b_glu_il], [(d, BF16)], [], ncol=ncb)
    wdw_pad = _pad_rows(wdw, kp_dw)
    u1 = _dwconv("conv_dw", u0, 0, d, wdw_pad, k_dw, b_dw, False, BF16)
    (u2,), _ = _rowcall("ln_silu", _fwd_fn(_t_lnsilu), [(u1, 0, d)], [ln_g, ln_b], [(d, BF16)], [])
    yb = _matmul("conv_out", u2, w_co, "nn", BF16)
    gm_rows, gm_prm = [(ya, 0, d), (yb, 0, d), (proj, p_gate, 2 * d)], [b_conv_out, b_gate_il]
    (m_in,), _ = _rowcall("gate_mix", _gatemix_fwd, gm_rows, gm_prm, [(d, BF16)], [], ncol=ncb)
    mix = _matmul("mix_out", m_in, w_mo, "nn", BF16)
    pp_prm = [g_post_mix, g1, g_pre_ffn, sc2, sh2]
    (x1, h2), _ = _rowcall("post_mix", _fwd_fn(_t_postpre), [(x2, 0, d), (mix, 0, d)], pp_prm, [(d, F32), (d, BF16)], [])
    gu, act = _matmul("gate_up", h2, w_gu, "nn", BF16, tn=2 * COL_BLOCK,
                      epilogue=(lambda p: (p, _swiglu_fwd(p)[0][0]), [], [(2 * d_ff, BF16), (d_ff, BF16)]))
    f = _matmul("down", act, w_dn, "nn", BF16)

    def loss_fn(x1_t, f_t, tgt_t, gpost, g2_t):
        args = [_f(x1_t), _f(f_t), _f(tgt_t), _f(gpost), _f(g2_t)]
        val, vjp = jax.vjp(_t_loss, *args)
        gr = vjp(jnp.ones((), F32))
        return (gr[0], gr[1]), (jnp.zeros((1, 128), F32) + val, gr[3], gr[4])

    (dx1a, df), (loss_v, d_gpost2, d_g2) = _rowcall(
        "loss", loss_fn, [(x1, 0, d), (f, 0, d), (tgt, 0, d)], [g_post_ffn, g2], [(d, F32), (d, BF16)], [128, d, d])
    loss = lax.psum(loss_v[0, 0], ("x", "y", "c"))

    dgu = _matmul("down_dx", df, w_dn, "nt", BF16, tn=COL_BLOCK,
                  epilogue=(lambda p, gu_t: _swiglu_bwd(gu_t, p)[0], [gu], [(2 * d_ff, BF16)]))
    dw_dn = _matmul("down_dw", act, df, "tn", BF16)
    recv = {}
    dh2, (recv["w_down"],) = _matmul("gate_up_dx", dgu, w_gu, "nt", BF16, carry=(_a2a_ops, [dw_dn.reshape(N_DEV, -1, d)]))
    dw_gu = _matmul("gate_up_dw", h2, dgu, "tn", BF16)
    (dxa, dmix), (d_gpost1, d_g1, d_gpre2, d_sc2, d_sh2) = _rowcall(
        "post_mix_bwd", _vjp_fn(_t_postpre, 2, 2, (0, 1), (0, 1, 2, 3, 4)),
        [(x2, 0, d), (mix, 0, d), (dx1a, 0, d), (dh2, 0, d)], pp_prm, [(d, F32), (d, BF16)], [d] * 5)
    dm_in = _matmul("mix_out_dx", dmix, w_mo, "nt", BF16)
    dw_mo = _matmul("mix_out_dw", m_in, dmix, "tn", BF16)
    dproj = lax.empty((s, wb), BF16)
    (dya, dyb, dproj), (d_bco, d_bgate) = _rowcall(
        "gate_mix_bwd", _gatemix_bwd, gm_rows + [(dm_in, 0, d)], gm_prm,
        [(d, BF16), (d, BF16), (2 * d, BF16, dproj, p_gate)], [d, 2 * d], ncol=ncb)
    du2 = _matmul("conv_out_dx", dyb, w_co, "nt", BF16)
    dw_co = _matmul("conv_out_dw", u2, dyb, "tn", BF16)
    (du1,), (d_lng, d_lnb) = _rowcall("ln_silu_bwd", _vjp_fn(_t_lnsilu, 1, 1, (0,), (0, 1)), [(u1, 0, d), (du2, 0, d)],
                                      [ln_g, ln_b], [(d, BF16)], [d, d])
    dwdw_p, d_bdw = _dwconv_bwd_w("conv_dw_dw", u0, 0, d, wdw_pad, k_dw, b_dw, [du1], False)
    du0 = _dwconv("conv_dw_dx", du1, 0, d, _pad_rows(wdw[::-1], kp_dw), k_dw, zero_d, False, BF16)
    (dproj,), (d_bglu,) = _rowcall("glu_bwd", _glu_bwd, [(proj, p_glu, 2 * d), (du0, 0, d)], [b_glu_il],
                                   [(2 * d, BF16, dproj, p_glu)], [2 * d], ncol=ncb)
    dyn, (recv["w_mix_out"], recv["w_conv_out"]) = _matmul(
        "ssm_out_dx", dya, w_ssm, "nt", BF16, carry=(_a2a_ops, [dw_mo.reshape(N_DEV, -1, d), dw_co.reshape(N_DEV, -1, d)]))
    dw_ssm = _matmul("ssm_out_dw", yn, dya, "tn", BF16)
    (dy_ssd, dproj), (d_gn,) = _rowcall("ssm_norm_bwd", _vjp_fn(_t_gnorm, 3, 1, (0, 2), (0,)), gn_rows + [(dyn, 0, d_ssm)],
                                        [g_ssm_norm], [(d_ssm, BF16), (d_ssm, BF16, dproj, 0)], [d_ssm], ncol=n_groups)
    dxbc_f, ddt_f, da_f, dxbc_b, ddt_b, da_b, ddsk = _ssd_bwd("ssd_bwd", xbc_c, dt_t, a_t, dy_ssd, hin_f, hin_b, dsk, n_groups)
    ddt_raw, d_dtb, d_alog = _dtprep_bwd("dt_prep_bwd", ddt_f, ddt_b, da_f, da_b, dt_t, alog_col)
    dpre, dwcs_p, dbcs_p = _dwconv_bwd_w("conv_ssm_dw", proj, p_x, d_xbc, _pad_rows(wcs_p, kp_ssm), k_ssm, bcs_p,
                                          [dxbc_f, dxbc_b], True)
    dproj = _dwconv("conv_ssm_dx", dpre, 0, d_xbc, _pad_rows(wcs_p[::-1], kp_ssm), k_ssm, zero_x, False, BF16, into=(dproj, p_x))
    dw_big, (recv["w_gate_up"], recv["w_ssm_out"]) = _matmul(
        "in_proj_dw", h1, dproj, "tn", BF16,
        carry=(_a2a_ops, [_permute_cols("restore_dw_gate_up", [dw_gu], _inverse_runs(gu_runs), dst_sharded=True)[0],
                          dw_ssm.reshape(N_DEV, -1, d)]))
    dw_dt = _matmul("in_proj_dt_dw", h1, ddt_raw, "tn", BF16)
    (dw_in,) = _permute_cols("restore_dw_in", [dw_big, dw_dt], _inverse_runs(w_in_runs), dst_sharded=True)
    dh1, (recv["w_in"],) = _matmul("in_proj_dx", dproj, w_big, "nt", F32, carry=(_a2a_ops, [dw_in]))
    dh1 = _matmul("in_proj_dt_dx", ddt_raw, w_dt, "nt", BF16, acc_in=dh1)

    def prenorm_bwd_fn(x_t, dh_t, dxa_t, g, sc, sh):
        rows, prm = _vjp_fn(_t_prenorm, 1, 1, (0,), (0, 1, 2))(x_t, dh_t, g, sc, sh)
        return (rows[0] + _f(dxa_t),), prm

    (grad_x,), (d_gpre1, d_sc1, d_sh1) = _rowcall(
        "prenorm_mix_bwd", prenorm_bwd_fn, [(x2, 0, d), (dh1, 0, d), (dxa, 0, d)], [g_pre_mix, sc1, sh1], [(d, F32)], [d] * 3)

    dmod = jnp.concatenate([d_sh1, d_sc1, d_g1, d_sh2, d_sc2, d_g2], axis=1)
    small_g = {
        "b_ada": dmod, "g_pre_mix": d_gpre1, "g_post_mix": d_gpost1,
        "b_conv_ssm": _permute_cols("restore_db_conv", [dbcs_p], _inverse_runs(xbc_runs))[0],
        "dt_bias_fwd": d_dtb[:n_heads].reshape(1, n_heads), "dt_bias_bwd": d_dtb[n_heads:].reshape(1, n_heads),
        "a_log_fwd": d_alog[:n_heads].reshape(1, n_heads), "a_log_bwd": d_alog[n_heads:].reshape(1, n_heads),
        "d_skip": ddsk.reshape(n_heads, HEAD_DIM).sum(axis=1).reshape(1, n_heads), "g_ssm_norm": d_gn,
        "b_glu": _permute_cols("restore_db_glu", [d_bglu], _inverse_runs(il_runs))[0], "b_dw": d_bdw, "ln_g": d_lng, "ln_b": d_lnb,
        "b_conv_out": d_bco, "b_gate": _permute_cols("restore_db_gate", [d_bgate], _inverse_runs(il_runs))[0],
        "g_pre_ffn": d_gpre2, "g_post_ffn": d_gpost2,
        "w_conv_ssm": _permute_cols("restore_dw_conv", [dwcs_p], _inverse_runs(xbc_runs))[0][:k_ssm].reshape(1, k_ssm * d_xbc),
        "w_dw": dwdw_p[:k_dw].reshape(1, k_dw * d),
    }
    small_names = list(small_g)
    sizes = [small_g[k].shape[1] for k in small_names]
    offs, _ = _slot_offsets(sizes)
    (small_parts,) = _all_gather("gather_small_grads", [_pack_vectors("pack_small_grads", [small_g[k] for k in small_names])])
    dmod_all = small_parts.reshape(N_DEV, -1)[:, :6 * d]
    small_tot = _reduce_sum("sum_small_grads", small_parts).reshape(1, -1)
    grads = {k: small_tot[:, offs[i]:offs[i] + sizes[i]] for i, k in enumerate(small_names)}
    grads["w_conv_ssm"] = lax.dynamic_slice(grads["w_conv_ssm"].reshape(k_ssm, d_xbc), (0, me * (d_xbc // N_DEV)), (k_ssm, d_xbc // N_DEV))
    grads["w_dw"] = lax.dynamic_slice(grads["w_dw"].reshape(k_dw, d), (0, me * (d // N_DEV)), (k_dw, d // N_DEV))
    grads["w_ada"] = _matmul("mod_dw", c_act, lax.dynamic_slice(dmod_all, (0, me * mc), (N_DEV, mc)), "tn", F32, highest=True)

    delta, new_m, new_v = {}, {}, {}
    for k in big:
        grads[k], delta[k], new_m[k], new_v[k] = _adam("adam_" + k, weights[k][0], None, mom_m[k][0], mom_v[k][0], parts=recv[k])
    for k in ["w_ada", "w_conv_ssm", "w_dw"]:
        delta[k], new_m[k], new_v[k] = _adam("adam_" + k, weights[k][0], grads[k], mom_m[k][0], mom_v[k][0])
    rep = [k for k in names if k not in big and k not in ("w_ada", "w_conv_ssm", "w_dw")]
    r_delta, r_m, r_v = _adam_many("adam_small", *[[t[k] for k in rep] for t in (weights, grads, mom_m, mom_v)])
    for i, k in enumerate(rep):
        delta[k], new_m[k], new_v[k] = r_delta[i], r_m[i], r_v[i]

    def shaped(t, k):
        return t[k].reshape(weights[k].shape)

    return (loss, grad_x.reshape(x.shape), *[shaped(grads, k) for k in names], *[shaped(delta, k) for k in names],
            *[shaped(new_m, k) for k in names], *[shaped(new_v, k) for k in names])
```

```python
import functools

import numpy as np
import jax
import jax.numpy as jnp
from jax import lax
from jax.experimental import pallas as pl
from jax.experimental.pallas import tpu as pltpu

F32 = jnp.float32
BF16 = jnp.bfloat16
HIGHEST = lax.Precision.HIGHEST
MESH = pl.DeviceIdType.MESH
N_DEV = 8
EPS = 1e-6
CHUNK = 128
HEAD_DIM = 64
D_STATE = 128
HEADS_PER_GROUP = 8
GROUP_X = HEADS_PER_GROUP * HEAD_DIM
GROUP_W = GROUP_X + 2 * D_STATE
COL_BLOCK = 512
HALO = 16
VMEM_LIMIT = 56 * 1024 * 1024
ADAM_LR, ADAM_B1, ADAM_B2, ADAM_EPS, ADAM_WD, ADAM_STEP = 0.001, 0.9, 0.999, 1e-08, 0.01, 10

NN = (((1,), (0,)), ((), ()))
NT = (((1,), (1,)), ((), ()))
TN = (((0,), (0,)), ((), ()))


def _tile(dim, prefs):
    for p in prefs:
        if p <= dim and dim % p == 0:
            return p
    return dim


def _params(n_grid):
    return pltpu.CompilerParams(dimension_semantics=("arbitrary",) * n_grid, vmem_limit_bytes=VMEM_LIMIT)


def _f(v):
    return v.astype(F32)


def _matmul(name, a, b, mode, out_dtype, acc_in=None, highest=False, carry=None, tn=None, epilogue=None):
    if mode == "nn":
        (m, k), (k2, n) = a.shape, b.shape
    elif mode == "nt":
        (m, k), (n, k2) = a.shape, b.shape
    else:
        (k, m), (k2, n) = a.shape, b.shape
    assert k == k2, (name, a.shape, b.shape, mode)
    tm = _tile(m, (1024, 512, 256, 128))
    tn = _tile(n, (1024, 512, 256, 128)) if tn is None else tn
    tk = _tile(k, (2816, 2048, 1024, 512, 256, 128))
    nk, nj = k // tk, n // tn
    assert n % tn == 0
    dims = {"nn": NN, "nt": NT, "tn": TN}[mode]
    a_spec = pl.BlockSpec((tk, tm), lambda i, j, kk: (kk, i)) if mode == "tn" else pl.BlockSpec((tm, tk), lambda i, j, kk: (i, kk))
    b_spec = pl.BlockSpec((tn, tk), lambda i, j, kk: (j, kk)) if mode == "nt" else pl.BlockSpec((tk, tn), lambda i, j, kk: (kk, j))
    o_spec = pl.BlockSpec((tm, tn), lambda i, j, kk: (i, j))
    has_acc = acc_in is not None
    epi_fn, e_arrs, e_outs = epilogue if epilogue is not None else (None, [], [])
    ne = len(e_arrs)
    n_in = (3 if has_acc else 2) + ne
    if epilogue is None:
        o_specs, o_shapes = [o_spec], [jax.ShapeDtypeStruct((m, n), out_dtype)]
    else:
        o_specs = [pl.BlockSpec((tm, w // nj), lambda i, j, kk: (i, j)) for w, _ in e_outs]
        o_shapes = [jax.ShapeDtypeStruct((m, w), dt) for w, dt in e_outs]
    e_specs = [pl.BlockSpec((tm, arr.shape[1] // nj), lambda i, j, kk: (i, j)) for arr in e_arrs]
    n_out = len(o_specs)
    ops, c_arrs = carry if carry is not None else (None, [])
    nc_ = len(c_arrs)
    grid = (m // tm, nj, nk)

    def body(*refs):
        a_ref, b_ref = refs[0], refs[1]
        c_ref = refs[2] if has_acc else None
        e_refs = refs[n_in - ne:n_in]
        c_ins = refs[n_in:n_in + nc_]
        o_refs = refs[n_in + nc_:n_in + nc_ + n_out]
        c_outs = refs[n_in + nc_ + n_out:n_in + 2 * nc_ + n_out]
        acc = refs[n_in + 2 * nc_ + n_out]
        i, j, kk = pl.program_id(0), pl.program_id(1), pl.program_id(2)
        if nc_:
            start, finish = ops(c_ins, c_outs, *refs[n_in + 2 * nc_ + n_out + 1:])
            pl.when((i == 0) & (j == 0) & (kk == 0))(start)

        def write(val):
            if epi_fn is None:
                o_refs[0][...] = val.astype(out_dtype)
            else:
                for ref, v in zip(o_refs, epi_fn(val, *[r[...] for r in e_refs])):
                    ref[...] = v.astype(ref.dtype)

        def product(first):
            if highest:
                p = lax.dot_general(_f(a_ref[...]), _f(b_ref[...]), dims, precision=HIGHEST, preferred_element_type=F32)
            else:
                p = lax.dot_general(a_ref[...].astype(BF16), b_ref[...].astype(BF16), dims, preferred_element_type=F32)
            return p + _f(c_ref[...]) if (first and has_acc) else p

        if nk == 1:
            write(product(True))
        else:
            @pl.when(kk == 0)
            def _():
                acc[...] = product(True)

            if nk > 2:
                @pl.when((kk > 0) & (kk < nk - 1))
                def _():
                    acc[...] += product(False)

            @pl.when(kk == nk - 1)
            def _():
                write(acc[...] + product(False))

        if nc_:
            pl.when((i == grid[0] - 1) & (j == grid[1] - 1) & (kk == nk - 1))(finish)

    any_spec = pl.BlockSpec(memory_space=pl.ANY)
    c_shapes = ops.out_shapes(c_arrs) if nc_ else []
    res = pl.pallas_call(
        body, name=name, grid=grid,
        in_specs=[a_spec, b_spec] + ([o_spec] if has_acc else []) + e_specs + [any_spec] * nc_,
        out_specs=o_specs + [any_spec] * nc_, out_shape=o_shapes + c_shapes,
        scratch_shapes=[pltpu.VMEM((tm, tn) if nk > 1 else (8, 128), F32)] + (_exchange_sems(nc_) if nc_ else []),
        compiler_params=_params(3),
    )(*([a, b] + ([acc_in] if has_acc else []) + list(e_arrs) + list(c_arrs)))
    outs = res[0] if n_out == 1 else res[:n_out]
    return (outs, res[n_out:]) if nc_ else outs


def _rowcall(name, fn, rows, params, out_rows, out_accs, ncol=1):
    s = rows[0][0].shape[0]
    tm = _tile(s, (256 if ncol == 1 else 1024, 128, 64, 32, 16, 8))
    args, in_specs = [], []
    for arr, off, w in rows:
        bw = w // ncol
        assert w % ncol == 0 and off % bw == 0, (name, off, w, ncol)
        in_specs.append(pl.BlockSpec((tm, bw), functools.partial(lambda j, i, ob: (i, ob + j), ob=off // bw)))
        args.append(arr)
    for p in params:
        assert p.shape[1] % ncol == 0, (name, p.shape)
        in_specs.append(pl.BlockSpec((p.shape[0], p.shape[1] // ncol), lambda j, i: (0, j)))
        args.append(p)
    n_in, n_ro = len(args), len(out_rows)
    out_shape, out_specs, aliases = [], [], {}
    for k, spec in enumerate(out_rows):
        w, dt = spec[0], spec[1]
        bw = w // ncol
        if len(spec) == 4:
            buf, off = spec[2], spec[3]
            assert off % bw == 0 and buf.dtype == dt, (name, off, bw)
            out_shape.append(jax.ShapeDtypeStruct(buf.shape, dt))
            out_specs.append(pl.BlockSpec((tm, bw), functools.partial(lambda j, i, ob: (i, ob + j), ob=off // bw)))
            aliases[len(args)] = k
            in_specs.append(pl.BlockSpec(memory_space=pl.ANY))
            args.append(buf)
        else:
            out_shape.append(jax.ShapeDtypeStruct((s, w), dt))
            out_specs.append(pl.BlockSpec((tm, bw), lambda j, i: (i, j)))
    for w in out_accs:
        out_shape.append(jax.ShapeDtypeStruct((1, w), F32))
        out_specs.append(pl.BlockSpec((1, w // ncol), lambda j, i: (0, j)))
    n_args = len(args)

    def body(*refs):
        i = pl.program_id(1)
        ro, ao = fn(*[r[...] for r in refs[:n_in]])
        outs = refs[n_args:]
        assert len(ro) == n_ro and len(ao) == len(out_accs), name
        for ref, v in zip(outs[:n_ro], ro):
            ref[...] = v.astype(ref.dtype)
        for ref, v in zip(outs[n_ro:], ao):
            @pl.when(i == 0)
            def _(ref=ref, v=v):
                ref[...] = v

            @pl.when(i > 0)
            def _(ref=ref, v=v):
                ref[...] += v

    res = pl.pallas_call(
        body, name=name, grid=(ncol, s // tm), in_specs=in_specs, out_specs=out_specs,
        out_shape=out_shape, input_output_aliases=aliases, compiler_params=_params(2),
    )(*args)
    return res[:n_ro], res[n_ro:]


def _fwd_fn(f):
    def fn(*vals):
        out = f(*[_f(v) for v in vals])
        return (out if isinstance(out, tuple) else (out,)), ()
    return fn


def _vjp_fn(f, n_prim, n_ct, want_rows, want_params):
    def fn(*vals):
        prims = [_f(v) for v in vals[:n_prim]]
        cts = [_f(v) for v in vals[n_prim:n_prim + n_ct]]
        prms = [_f(v) for v in vals[n_prim + n_ct:]]
        out, vjp = jax.vjp(f, *prims, *prms)
        g = vjp(tuple(cts) if isinstance(out, tuple) else cts[0])
        return tuple(g[k] for k in want_rows), tuple(g[n_prim + k] for k in want_params)
    return fn


def _rms(v):
    return v * lax.rsqrt(jnp.mean(v * v, axis=-1, keepdims=True) + EPS)


def _silu(v):
    return v * jax.nn.sigmoid(v)


def _t_prenorm(x, g, sc, sh):
    return _rms(x) * g * (1.0 + sc) + sh


def _t_gnorm(yf, yb, z, gn):
    return _rms((yf + yb) * _silu(z)) * gn


def _t_glu(a, b, ba, bb):
    return (a + ba) * jax.nn.sigmoid(b + bb)


def _t_lnsilu(u, g, b):
    uc = u - jnp.mean(u, axis=-1, keepdims=True)
    return _silu(uc * lax.rsqrt(jnp.mean(uc * uc, axis=-1, keepdims=True) + EPS) * g + b)


def _t_gatemix(ya, yb, la, lb, bco, bga, bgb):
    return jax.nn.sigmoid(la + bga) * ya + jax.nn.sigmoid(lb + bgb) * (yb + bco)


def _t_postpre(x, mix, gpost, g1, gpre, sc, sh):
    x1 = x + g1 * (_rms(mix) * gpost)
    return x1, _rms(x1) * gpre * (1.0 + sc) + sh


def _t_swiglu(gt, up):
    return _silu(gt) * up


def _halves(v):
    return _f(v[:, :COL_BLOCK]), _f(v[:, COL_BLOCK:])


def _glu_fwd(ab, bias):
    return (_t_glu(*_halves(ab), *_halves(bias)),), ()


def _glu_bwd(ab, du, bias):
    _, vjp = jax.vjp(_t_glu, *_halves(ab), *_halves(bias))
    da, db, dba, dbb = vjp(_f(du))
    return (jnp.concatenate([da, db], axis=1),), (jnp.concatenate([dba, dbb], axis=1),)


def _gatemix_fwd(ya, yb, logits, bco, bg):
    return (_t_gatemix(_f(ya), _f(yb), *_halves(logits), _f(bco), *_halves(bg)),), ()


def _gatemix_bwd(ya, yb, logits, dm, bco, bg):
    _, vjp = jax.vjp(_t_gatemix, _f(ya), _f(yb), *_halves(logits), _f(bco), *_halves(bg))
    dya, dyb, dla, dlb, dbco, dbga, dbgb = vjp(_f(dm))
    return (dya, dyb, jnp.concatenate([dla, dlb], axis=1)), (dbco, jnp.concatenate([dbga, dbgb], axis=1))


def _swiglu_fwd(gu):
    return (_t_swiglu(*_halves(gu)),), ()


def _swiglu_bwd(gu, dact):
    _, vjp = jax.vjp(_t_swiglu, *_halves(gu))
    return (jnp.concatenate(vjp(_f(dact)), axis=1),), ()


def _t_loss(x1, f, tgt, gpost, g2):
    e = x1 + g2 * (_rms(f) * gpost) - tgt
    return 0.5 * jnp.sum(jnp.mean(e * e, axis=-1))


def _conv_specs(s, tm, cw, ob, w_rows):
    nh, last_h = tm // HALO, s // HALO - 1
    return [
        pl.BlockSpec((tm, cw), lambda j, i: (i, ob + j)),
        pl.BlockSpec((HALO, cw), lambda j, i: (jnp.maximum(i * nh - 1, 0), ob + j)),
        pl.BlockSpec((HALO, cw), lambda j, i: (jnp.minimum((i + 1) * nh, last_h), ob + j)),
        pl.BlockSpec((w_rows, cw), lambda j, i: (0, j)),
        pl.BlockSpec((1, cw), lambda j, i: (0, j)),
    ]


def _window(x_ref, p_ref, n_ref, i, nrow):
    prev = _f(p_ref[...]) * jnp.where(i > 0, 1.0, 0.0).astype(F32)
    nxt = _f(n_ref[...]) * jnp.where(i < nrow - 1, 1.0, 0.0).astype(F32)
    return jnp.concatenate([prev, _f(x_ref[...]), nxt], axis=0)


def _shifted(win, k_taps, tm):
    pad, moved, out = (k_taps - 1) // 2, {}, []
    for k in range(k_taps):
        q, r = divmod(HALO + k - pad, 8)
        if r not in moved:
            moved[r] = pltpu.roll(win, win.shape[0] - r, 0) if r else win
        out.append(moved[r][8 * q:8 * q + tm, :])
    return out


def _taps(win, w_ref, k_taps, tm):
    sh = _shifted(win, k_taps, tm)
    acc = w_ref[0:1, :] * sh[0]
    for k in range(1, k_taps):
        acc = acc + w_ref[k:k + 1, :] * sh[k]
    return acc


def _conv_rows(k_taps):
    return 1024 if k_taps <= 8 else 512


def _dwconv(name, x, xoff, c, w_pad, k_taps, b, act, out_dtype, cw=256, into=None):
    s = x.shape[0]
    tm, cw = _tile(s, (_conv_rows(k_taps), 256, 128, 64, 32, 16)), _tile(c, (cw, 128))
    assert xoff % cw == 0 and tm % HALO == 0 and (k_taps - 1) // 2 <= HALO
    nrow = s // tm

    def body(x_ref, p_ref, n_ref, w_ref, b_ref, *rest):
        win = _window(x_ref, p_ref, n_ref, pl.program_id(1), nrow)
        acc = _taps(win, w_ref, k_taps, tm) + b_ref[...]
        rest[-1][...] = (_silu(acc) if act else acc).astype(out_dtype)

    in_specs, args = _conv_specs(s, tm, cw, xoff // cw, w_pad.shape[0]), [x, x, x, w_pad, b]
    if into is None:
        ob, out_shape, aliases = 0, jax.ShapeDtypeStruct((s, c), out_dtype), {}
    else:
        buf, first = into
        assert first % cw == 0 and buf.dtype == out_dtype
        ob, out_shape, aliases = first // cw, jax.ShapeDtypeStruct(buf.shape, out_dtype), {len(args): 0}
        in_specs, args = in_specs + [pl.BlockSpec(memory_space=pl.ANY)], args + [buf]
    return pl.pallas_call(
        body, name=name, grid=(c // cw, nrow), in_specs=in_specs,
        out_specs=pl.BlockSpec((tm, cw), lambda j, i: (i, ob + j)), out_shape=out_shape,
        input_output_aliases=aliases, compiler_params=_params(2),
    )(*args)


def _dwconv_bwd_w(name, x, xoff, c, w_pad, k_taps, b, douts, act, cw=256):
    s = x.shape[0]
    tm, cw = _tile(s, (_conv_rows(k_taps), 256, 128, 64, 32, 16)), _tile(c, (cw, 128))
    assert xoff % cw == 0 and tm % HALO == 0
    nrow, pad, n_d = s // tm, (k_taps - 1) // 2, len(douts)

    def body(x_ref, p_ref, n_ref, w_ref, b_ref, *rest):
        d_refs, outs = rest[:n_d], rest[n_d:]
        i = pl.program_id(1)
        win = _window(x_ref, p_ref, n_ref, i, nrow)
        d = _f(d_refs[0][...])
        for r in d_refs[1:]:
            d = d + _f(r[...])
        if act:
            shifted = _shifted(win, k_taps, tm)
        else:
            shifted = [win[HALO + k - pad:HALO + k - pad + tm, :] for k in range(k_taps)]
        if act:
            pre = b_ref[...] + w_ref[0:1, :] * shifted[0]
            for k in range(1, k_taps):
                pre = pre + w_ref[k:k + 1, :] * shifted[k]
            sg = jax.nn.sigmoid(pre)
            d = d * (sg * (1.0 + pre * (1.0 - sg)))
            outs[0][...] = d.astype(outs[0].dtype)
        dw_ref, db_ref = outs[-2], outs[-1]

        @pl.when(i == 0)
        def _():
            dw_ref[...] = jnp.zeros_like(dw_ref)
            db_ref[...] = jnp.zeros_like(db_ref)

        for k in range(k_taps):
            dw_ref[k:k + 1, :] += jnp.sum(d * shifted[k], axis=0, keepdims=True)
        db_ref[...] += jnp.sum(d, axis=0, keepdims=True)

    row_spec = pl.BlockSpec((tm, cw), lambda j, i: (i, j))
    out_shape = [jax.ShapeDtypeStruct((w_pad.shape[0], c), F32), jax.ShapeDtypeStruct((1, c), F32)]
    out_specs = [pl.BlockSpec((w_pad.shape[0], cw), lambda j, i: (0, j)), pl.BlockSpec((1, cw), lambda j, i: (0, j))]
    if act:
        out_shape, out_specs = [jax.ShapeDtypeStruct((s, c), BF16)] + out_shape, [row_spec] + out_specs
    return pl.pallas_call(
        body, name=name, grid=(c // cw, nrow),
        in_specs=_conv_specs(s, tm, cw, xoff // cw, w_pad.shape[0]) + [row_spec] * n_d,
        out_specs=out_specs, out_shape=out_shape, compiler_params=_params(2),
    )(x, x, x, w_pad, b, *douts)


def _dtprep(name, dt_raw, bias, alog_col):
    s, h2 = dt_raw.shape
    ts = _tile(s, (512, 256, 128))

    def body(r_ref, b_ref, al_ref, dt_ref, a_ref):
        v = r_ref[...] + b_ref[...]
        dt = (jnp.maximum(v, 0.0) + jnp.log(1.0 + jnp.exp(-jnp.abs(v)))).T
        dt_ref[...] = dt
        a_ref[...] = dt * (-jnp.exp(al_ref[...]))

    t_spec = pl.BlockSpec((h2, ts), lambda i: (0, i))
    return pl.pallas_call(
        body, name=name, grid=(s // ts,),
        in_specs=[pl.BlockSpec((ts, h2), lambda i: (i, 0)), pl.BlockSpec((1, h2), lambda i: (0, 0)), pl.BlockSpec((h2, 1), lambda i: (0, 0))],
        out_specs=[t_spec, t_spec], out_shape=[jax.ShapeDtypeStruct((h2, s), F32)] * 2, compiler_params=_params(1),
    )(dt_raw, bias, alog_col)


def _dtprep_bwd(name, ddt_f, ddt_b, da_f, da_b, dt_t, alog_col):
    h2, s = dt_t.shape
    h = h2 // 2
    ts = _tile(s, (512, 256, 128))

    def body(ddf, ddb, daf, dab, dt_ref, al_ref, raw_ref, db_ref, dal_ref):
        i = pl.program_id(0)
        a_col = -jnp.exp(al_ref[...])
        da = jnp.concatenate([daf[...], dab[...]], axis=0)
        dt = dt_ref[...]
        d_raw = (jnp.concatenate([ddf[...], ddb[...]], axis=0) + da * a_col) * (1.0 - jnp.exp(-dt))
        raw_ref[...] = d_raw.T

        @pl.when(i == 0)
        def _():
            db_ref[...] = jnp.zeros_like(db_ref)
            dal_ref[...] = jnp.zeros_like(dal_ref)

        db_ref[...] += jnp.sum(d_raw, axis=1, keepdims=True)
        dal_ref[...] += jnp.sum(da * dt, axis=1, keepdims=True) * a_col

    half = pl.BlockSpec((h, ts), lambda i: (0, i))
    col = pl.BlockSpec((h2, 1), lambda i: (0, 0))
    return pl.pallas_call(
        body, name=name, grid=(s // ts,),
        in_specs=[half, half, half, half, pl.BlockSpec((h2, ts), lambda i: (0, i)), col],
        out_specs=[pl.BlockSpec((ts, h2), lambda i: (i, 0)), col, col],
        out_shape=[jax.ShapeDtypeStruct((s, h2), F32), jax.ShapeDtypeStruct((h2, 1), F32), jax.ShapeDtypeStruct((h2, 1), F32)],
        compiler_params=_params(1),
    )(ddt_f, ddt_b, da_f, da_b, dt_t, alog_col)


def _dot_exact(a, b, dims, exact_lhs):
    other = b if exact_lhs else a
    out = None
    for _ in range(3):
        part = other.astype(BF16)
        other = other - _f(part)
        p = lax.dot_general(a if exact_lhs else part, part if exact_lhs else b, dims, preferred_element_type=F32)
        out = p if out is None else out + p
    return out


def _chunk_consts(a_r, dt_r, rev):
    ii = lax.broadcasted_iota(jnp.int32, (CHUNK, CHUNK), 0)
    jj = lax.broadcasted_iota(jnp.int32, (CHUNK, CHUNK), 1)
    w = (jj >= ii) if rev else (jj <= ii)
    wt = (ii >= jj) if rev else (ii <= jj)
    eye = (ii == jj).astype(F32)
    wb_ = w.astype(BF16)
    cs_col = _dot_exact(wb_, a_r, NT, exact_lhs=True)
    cs_row = _dot_exact(a_r, wb_, NT, exact_lhs=False)
    tot = jnp.sum(a_r, axis=1, keepdims=True)
    lo_half = lax.broadcasted_iota(jnp.int32, (CHUNK, 128), 1) < HEAD_DIM

    def lanes(col):
        return [jnp.broadcast_to(col[:, j:j + 1], (CHUNK, 128)) for j in range(HEADS_PER_GROUP)]

    def spread(bs):
        return jnp.concatenate([jnp.where(lo_half, bs[2 * k], bs[2 * k + 1]) for k in range(HEADS_PER_GROUP // 2)], axis=1)

    csc = lanes(cs_col)
    cs_x = spread(csc)
    hh = lax.broadcasted_iota(jnp.int32, (HEADS_PER_GROUP, GROUP_X), 0)
    ll = lax.broadcasted_iota(jnp.int32, (HEADS_PER_GROUP, GROUP_X), 1)
    expand = (lax.shift_right_logical(ll, 6) == hh).astype(BF16)
    dt_hi = dt_r.astype(BF16)
    dt_x = (lax.dot_general(dt_hi, expand, TN, preferred_element_type=F32)
            + lax.dot_general((dt_r - _f(dt_hi)).astype(BF16), expand, TN, preferred_element_type=F32))
    tot_x = cs_x[0:1, :] if rev else cs_x[CHUNK - 1:CHUNK, :]
    return w, wt, eye, csc, cs_row, tot, cs_x, dt_x, tot_x


def _hi_lo(v):
    hi = v.astype(BF16)
    return jnp.concatenate([hi, (v - _f(hi)).astype(BF16)], axis=1)


def _head_sums(v, terms=3):
    rr = lax.broadcasted_iota(jnp.int32, (GROUP_X, HEADS_PER_GROUP), 0)
    cc = lax.broadcasted_iota(jnp.int32, (GROUP_X, HEADS_PER_GROUP), 1)
    et = (lax.shift_right_logical(rr, 6) == cc).astype(BF16)
    out, rest = None, v
    for t in range(terms):
        part = rest.astype(BF16)
        if t < terms - 1:
            rest = rest - _f(part)
        p = jnp.dot(part, et, preferred_element_type=F32)
        out = p if out is None else out + p
    return out


def _chunks_per_step(nc):
    return 4 if nc % 4 == 0 else 2 if nc % 2 == 0 else 1


def _ssd_fwd(name, xbc, dt_t, a_t, dsk, n_groups):
    s = xbc.shape[0]
    nc = s // CHUNK
    d_ssm = n_groups * GROUP_X
    cps = _chunks_per_step(nc)
    nblk, rows = nc // cps, cps * CHUNK

    def one(x_ref, dt_ref, a_ref, dsk_ref, y_ref, hin_ref, h_scr, rev, c):
        skip = dsk_ref is not None
        r0 = c * CHUNK
        xs = _f(x_ref[r0:r0 + CHUNK, 0:GROUP_X])
        bm = x_ref[r0:r0 + CHUNK, GROUP_X:GROUP_X + D_STATE]
        cm = x_ref[r0:r0 + CHUNK, GROUP_X + D_STATE:GROUP_W]
        w, _, _, csc, cs_row, tot, cs_x, dt_x, tot_x = _chunk_consts(a_ref[:, r0:r0 + CHUNK], dt_ref[:, r0:r0 + CHUNK], rev)
        cb = lax.dot_general(cm, bm, NT, preferred_element_type=F32)
        h = h_scr[...]
        h_b = h.astype(BF16)
        hin_ref[0, c] = h_b
        xdt = xs * dt_x
        xdt_b = xdt.astype(BF16)
        yo = lax.dot_general(cm, h_b, NT, preferred_element_type=F32) * jnp.exp(cs_x)
        st = lax.dot_general((xdt * jnp.exp(tot_x - cs_x)).astype(BF16), bm, TN, preferred_element_type=F32)
        yds = []
        for j in range(HEADS_PER_GROUP):
            lo = HEAD_DIM * j
            h_scr[lo:lo + HEAD_DIM, :] = h[lo:lo + HEAD_DIM, :] * jnp.exp(tot[j:j + 1, 0:1]) + st[lo:lo + HEAD_DIM, :]
            lm = jnp.exp(jnp.where(w, csc[j] - cs_row[j:j + 1, :], -jnp.inf))
            yds.append(jnp.dot((cb * lm).astype(BF16), xdt_b[:, lo:lo + HEAD_DIM], preferred_element_type=F32))
        y = jnp.concatenate(yds, axis=1) + yo
        if skip:
            y = y + dsk_ref[...] * xs
        y_ref[r0:r0 + CHUNK, :] = y.astype(y_ref.dtype)

    def body(xf_ref, dtf_ref, af_ref, xr_ref, dtr_ref, ar_ref, dsk_ref, yf_ref, hf_ref, yr_ref, hr_ref, hf_scr, hr_scr):
        @pl.when(pl.program_id(1) == 0)
        def _():
            hf_scr[...] = jnp.zeros_like(hf_scr)
            hr_scr[...] = jnp.zeros_like(hr_scr)

        for c in range(cps):
            one(xf_ref, dtf_ref, af_ref, dsk_ref, yf_ref, hf_ref, hf_scr, False, c)
            one(xr_ref, dtr_ref, ar_ref, None, yr_ref, hr_ref, hr_scr, True, cps - 1 - c)

    def specs(rev):
        def z(zi):
            return nblk - 1 - zi if rev else zi
        t_spec = pl.BlockSpec((HEADS_PER_GROUP, rows), lambda g, zi: ((n_groups if rev else 0) + g, z(zi)))
        ins = [pl.BlockSpec((rows, GROUP_W), lambda g, zi: (z(zi), g)), t_spec, t_spec]
        outs = [pl.BlockSpec((rows, GROUP_X), lambda g, zi: (z(zi), g)),
                pl.BlockSpec((1, cps, GROUP_X, D_STATE), lambda g, zi: (g, z(zi), 0, 0))]
        return ins, outs

    (in_f, out_f), (in_r, out_r) = specs(False), specs(True)
    shapes = [jax.ShapeDtypeStruct((s, d_ssm), BF16), jax.ShapeDtypeStruct((n_groups, nc, GROUP_X, D_STATE), BF16)]
    return pl.pallas_call(
        body, name=name, grid=(n_groups, nblk), in_specs=in_f + in_r + [pl.BlockSpec((1, GROUP_X), lambda g, zi: (0, g))],
        out_specs=out_f + out_r, out_shape=shapes + shapes,
        scratch_shapes=[pltpu.VMEM((GROUP_X, D_STATE), F32)] * 2, compiler_params=_params(2),
    )(xbc, dt_t, a_t, xbc, dt_t, a_t, dsk)


def _ssd_bwd(name, xbc, dt_t, a_t, dy, hin_f, hin_r, dsk, n_groups):
    s = xbc.shape[0]
    nc = s // CHUNK
    n_heads = n_groups * HEADS_PER_GROUP
    cps = _chunks_per_step(nc)
    nblk, rows = nc // cps, cps * CHUNK

    def one(x_ref, dt_ref, a_ref, dy_ref, hin_ref, dsk_ref, dx_ref, ddt_ref, da_ref, ddsk_ref, g_scr, rev, c):
        skip = dsk_ref is not None
        r0 = c * CHUNK
        xs = _f(x_ref[r0:r0 + CHUNK, 0:GROUP_X])
        bm = x_ref[r0:r0 + CHUNK, GROUP_X:GROUP_X + D_STATE]
        cm = x_ref[r0:r0 + CHUNK, GROUP_X + D_STATE:GROUP_W]
        dyv = _f(dy_ref[r0:r0 + CHUNK, :])
        w, wt, eye, csc, cs_row, tot, cs_x, dt_x, tot_x = _chunk_consts(a_ref[:, r0:r0 + CHUNK], dt_ref[:, r0:r0 + CHUNK], rev)
        cb = lax.dot_general(cm, bm, NT, preferred_element_type=F32)
        cbt = lax.dot_general(bm, cm, NT, preferred_element_type=F32)
        onehot = lax.broadcasted_iota(jnp.int32, (1, HEADS_PER_GROUP), 1)
        hin = hin_ref[0, c]
        g = g_scr[...]
        g_b = g.astype(BF16)
        din, dte = jnp.exp(cs_x), jnp.exp(tot_x - cs_x)
        xdt = xs * dt_x
        xdt_b, dy_b = xdt.astype(BF16), dyv.astype(BF16)
        dyo_b = (dyv * din).astype(BF16)
        yo = lax.dot_general(cm, hin, NT, preferred_element_type=F32) * din
        dxdt_s = lax.dot_general(bm, g_b, NT, preferred_element_type=F32) * dte
        d_c = jnp.dot(dyo_b, hin, preferred_element_type=F32)
        d_b = jnp.dot((xdt * dte).astype(BF16), g_b, preferred_element_type=F32)
        dhin = lax.dot_general(dyo_b, cm, TN, preferred_element_type=F32)
        dcb = jnp.zeros((CHUNK, CHUNK), F32)
        dtot = jnp.zeros((1, HEADS_PER_GROUP), F32)
        yds, dxds = [], []
        for j in range(HEADS_PER_GROUP):
            lo = HEAD_DIM * j
            cd = jnp.exp(tot[j:j + 1, 0:1])
            gj = g[lo:lo + HEAD_DIM, :]
            dcd = jnp.sum(jnp.sum(gj * _f(hin[lo:lo + HEAD_DIM, :]), axis=1, keepdims=True), axis=0, keepdims=True)
            dtot = dtot + (dcd * cd) * (onehot == j).astype(F32)
            g_scr[lo:lo + HEAD_DIM, :] = dhin[lo:lo + HEAD_DIM, :] + gj * cd
            dd = csc[j] - cs_row[j:j + 1, :]
            lm = jnp.exp(jnp.where(w, dd, -jnp.inf))
            lmt = jnp.exp(jnp.where(wt, -dd, -jnp.inf))
            xj_b, dyj_b = xdt_b[:, lo:lo + HEAD_DIM], dy_b[:, lo:lo + HEAD_DIM]
            yds.append(jnp.dot(_hi_lo(cb * lm), jnp.concatenate([xj_b, xj_b], axis=0), preferred_element_type=F32))
            dxds.append(jnp.dot(_hi_lo(cbt * lmt), jnp.concatenate([dyj_b, dyj_b], axis=0), preferred_element_type=F32))
            dcb = dcb + lax.dot_general(dyj_b, xj_b, NT, preferred_element_type=F32) * lm
        dxdt_d = jnp.concatenate(dxds, axis=1)
        z_state = xdt * dxdt_s
        yd = jnp.concatenate(yds, axis=1)
        dcs = _head_sums(dyv * yo + _f(dy_b) * yd - _f(xdt_b) * dxdt_d - z_state)
        dtot = dtot + _head_sums(jnp.broadcast_to(jnp.sum(z_state, axis=0, keepdims=True), (8, GROUP_X)))[0:1, :]
        dxdt = dxdt_s + dxdt_d
        ddt_c = _head_sums(dxdt * xs, terms=2)
        dx = dxdt * dt_x
        if skip:
            dx = dx + dsk_ref[...] * dyv
        dcb_b = dcb.astype(BF16)
        d_c = d_c + jnp.dot(dcb_b, bm, preferred_element_type=F32)
        d_b = d_b + lax.dot_general(dcb_b, cm, TN, preferred_element_type=F32)
        eye_b = eye.astype(BF16)
        da_c = _dot_exact(wt.astype(BF16), dcs, NN, exact_lhs=True) + dtot
        ddt_ref[:, r0:r0 + CHUNK] = _dot_exact(ddt_c, eye_b, TN, exact_lhs=False)
        da_ref[:, r0:r0 + CHUNK] = _dot_exact(da_c, eye_b, TN, exact_lhs=False)
        dx_ref[r0:r0 + CHUNK, :] = jnp.concatenate([dx, d_b, d_c], axis=1).astype(dx_ref.dtype)
        if skip:
            ddsk_ref[...] += jnp.sum(dyv * xs, axis=0, keepdims=True)

    def body(xf, dtf, af, dyf, hf, xr, dtr, ar, dyr, hr, dsk_ref, dxf, ddtf, daf, dxr, ddtr, dar, ddsk_ref, gf_scr, gr_scr):
        @pl.when(pl.program_id(1) == 0)
        def _():
            gf_scr[...] = jnp.zeros_like(gf_scr)
            gr_scr[...] = jnp.zeros_like(gr_scr)
            ddsk_ref[...] = jnp.zeros_like(ddsk_ref)

        for c in range(cps):
            one(xf, dtf, af, dyf, hf, dsk_ref, dxf, ddtf, daf, ddsk_ref, gf_scr, False, cps - 1 - c)
            one(xr, dtr, ar, dyr, hr, None, dxr, ddtr, dar, None, gr_scr, True, c)

    def specs(rev):
        def z(zi):
            return zi if rev else nblk - 1 - zi
        t_in = pl.BlockSpec((HEADS_PER_GROUP, rows), lambda g, zi: ((n_groups if rev else 0) + g, z(zi)))
        t_out = pl.BlockSpec((HEADS_PER_GROUP, rows), lambda g, zi: (g, z(zi)))
        x_spec = pl.BlockSpec((rows, GROUP_W), lambda g, zi: (z(zi), g))
        ins = [x_spec, t_in, t_in, pl.BlockSpec((rows, GROUP_X), lambda g, zi: (z(zi), g)),
               pl.BlockSpec((1, cps, GROUP_X, D_STATE), lambda g, zi: (g, z(zi), 0, 0))]
        return ins, [x_spec, t_out, t_out]

    (in_f, out_f), (in_r, out_r) = specs(False), specs(True)
    lane_spec = pl.BlockSpec((1, GROUP_X), lambda g, zi: (0, g))
    shapes = [jax.ShapeDtypeStruct((s, n_groups * GROUP_W), BF16), jax.ShapeDtypeStruct((n_heads, s), F32),
              jax.ShapeDtypeStruct((n_heads, s), F32)]
    return pl.pallas_call(
        body, name=name, grid=(n_groups, nblk), in_specs=in_f + in_r + [lane_spec], out_specs=out_f + out_r + [lane_spec],
        out_shape=shapes + shapes + [jax.ShapeDtypeStruct((1, n_groups * GROUP_X), F32)],
        scratch_shapes=[pltpu.VMEM((GROUP_X, D_STATE), F32)] * 2, compiler_params=_params(2),
    )(xbc, dt_t, a_t, dy, hin_f, xbc, dt_t, a_t, dy, hin_r, dsk)


def _mesh_pos():
    return lax.axis_index("x"), lax.axis_index("y"), lax.axis_index("c")


def _slot(p):
    return 4 * p[0] + 2 * p[1] + p[2]


def _exchange_sems(n):
    return [pltpu.SemaphoreType.DMA((7 * n,)), pltpu.SemaphoreType.DMA((7 * n,)), pltpu.SemaphoreType.DMA((n,))]


def _gather_ops(ins, outs, send, recv, loc):
    n = len(ins)
    x, y, c = _mesh_pos()
    me, sib = (x, y, c), (x, y, 1 - c)
    chips = [(1 - x, y), (x, 1 - y), (1 - x, 1 - y)]

    def cp(a, k, block, to, src=None):
        dst = outs[a].at[_slot(block)]
        return pltpu.make_async_remote_copy(
            src_ref=dst if src is None else src, dst_ref=dst, send_sem=send.at[7 * a + k], recv_sem=recv.at[7 * a + k],
            device_id=to, device_id_type=MESH)

    mine = [pltpu.make_async_copy(ins[a], outs[a].at[_slot(me)], loc.at[a]) for a in range(n)]
    first = []
    for a in range(n):
        first.append(cp(a, 0, me, sib, src=ins[a]))
        first += [cp(a, 1 + j, me, (*chip, c), src=ins[a]) for j, chip in enumerate(chips)]

    def start():
        for d in mine + first:
            d.start()

    def finish():
        passed = []
        for a in range(n):
            for j, chip in enumerate(chips):
                cp(a, 1 + j, (*chip, c), me).wait_recv()
                p = cp(a, 4 + j, (*chip, c), sib)
                p.start()
                passed.append(p)
        for a in range(n):
            cp(a, 0, sib, me).wait_recv()
            for j, chip in enumerate(chips):
                cp(a, 4 + j, (*chip, 1 - c), me).wait_recv()
        for f in first + passed:
            f.wait_send()
        for m in mine:
            m.wait()

    return start, finish


_gather_ops.out_shapes = lambda arrs: [jax.ShapeDtypeStruct((N_DEV,) + a.shape, a.dtype) for a in arrs]


def _a2a_ops(ins, outs, send, recv, loc):
    n = len(ins)
    me = _mesh_pos()
    flips = [(fx, fy, fc) for fx in (0, 1) for fy in (0, 1) for fc in (0, 1)][1:]
    peers = [tuple(1 - p if fl else p for p, fl in zip(me, fl3)) for fl3 in flips]

    def cp(a, k, peer, dst_slot):
        return pltpu.make_async_remote_copy(
            src_ref=ins[a].at[_slot(peer)], dst_ref=outs[a].at[dst_slot], send_sem=send.at[7 * a + k],
            recv_sem=recv.at[7 * a + k], device_id=peer, device_id_type=MESH)

    mine = [pltpu.make_async_copy(ins[a].at[_slot(me)], outs[a].at[_slot(me)], loc.at[a]) for a in range(n)]
    sends = [cp(a, k, peer, _slot(me)) for a in range(n) for k, peer in enumerate(peers)]

    def start():
        for d in mine + sends:
            d.start()

    def finish():
        for a in range(n):
            for k, peer in enumerate(peers):
                cp(a, k, peer, _slot(peer)).wait_recv()
        for s_ in sends:
            s_.wait_send()
        for m in mine:
            m.wait()

    return start, finish


_a2a_ops.out_shapes = lambda arrs: [jax.ShapeDtypeStruct(a.shape, a.dtype) for a in arrs]


def _exchange(name, ops, arrs):
    n = len(arrs)

    def body(*refs):
        start, finish = ops(refs[:n], refs[n:2 * n], *refs[2 * n:])
        start()
        finish()

    any_spec = pl.BlockSpec(memory_space=pl.ANY)
    return pl.pallas_call(
        body, name=name, in_specs=[any_spec] * n, out_specs=[any_spec] * n, out_shape=ops.out_shapes(arrs),
        scratch_shapes=_exchange_sems(n),
    )(*arrs)


def _all_gather(name, arrs):
    return _exchange(name, _gather_ops, arrs)


def _all_to_all(name, arrs):
    return _exchange(name, _a2a_ops, arrs)


def _sum8(parts_ref):
    g = _f(parts_ref[0])
    for j in range(1, N_DEV):
        g = g + _f(parts_ref[j])
    return g


def _adam_math(w, g, m, v):
    m = ADAM_B1 * m + (1.0 - ADAM_B1) * g
    v = ADAM_B2 * v + (1.0 - ADAM_B2) * (g * g)
    m_hat = m / (1.0 - ADAM_B1 ** ADAM_STEP)
    v_hat = v / (1.0 - ADAM_B2 ** ADAM_STEP)
    return -ADAM_LR * (m_hat / (jnp.sqrt(v_hat) + ADAM_EPS) + ADAM_WD * w), m, v


def _reduce_sum(name, parts):
    _, r, c = parts.shape

    def body(p_ref, o_ref):
        o_ref[...] = _sum8(p_ref)

    return pl.pallas_call(body, name=name, out_shape=jax.ShapeDtypeStruct((r, c), F32))(parts)


def _adam(name, w, g, m, v, parts=None):
    r, c = w.shape
    tr = _tile(r, (128, 64, 32, 16)) if r * c > 2 ** 18 else r
    spec = pl.BlockSpec((tr, c), lambda i: (i, 0))
    summed = parts is not None

    def body(*refs):
        if summed:
            p_ref, w_ref, m_ref, v_ref, g_out, d_out, m_out, v_out = refs
            gv = _sum8(p_ref)
            g_out[...] = gv
        else:
            g_ref, w_ref, m_ref, v_ref, d_out, m_out, v_out = refs
            gv = g_ref[...]
        d, mn, vn = _adam_math(w_ref[...], gv, m_ref[...], v_ref[...])
        d_out[...] = d
        m_out[...] = mn
        v_out[...] = vn

    first = [pl.BlockSpec((N_DEV, tr, c), lambda i: (0, i, 0))] if summed else [spec]
    n_out = 4 if summed else 3
    return pl.pallas_call(
        body, name=name, grid=(r // tr,), in_specs=first + [spec] * 3, out_specs=[spec] * n_out,
        out_shape=[jax.ShapeDtypeStruct((r, c), F32)] * n_out, compiler_params=_params(1),
    )(parts if summed else g, w, m, v)


def _adam_many(name, ws, gs, ms, vs):
    n = len(ws)

    def body(*refs):
        outs = refs[4 * n:]
        for k in range(n):
            d, mn, vn = _adam_math(refs[k][...], refs[n + k][...], refs[2 * n + k][...], refs[3 * n + k][...])
            outs[k][...] = d
            outs[n + k][...] = mn
            outs[2 * n + k][...] = vn

    res = pl.pallas_call(
        body, name=name, out_shape=[jax.ShapeDtypeStruct(w.shape, F32) for w in ws] * 3,
        compiler_params=pltpu.CompilerParams(vmem_limit_bytes=VMEM_LIMIT),
    )(*ws, *gs, *ms, *vs)
    return res[:n], res[n:2 * n], res[2 * n:]


def _slot_offsets(sizes):
    offs, o = [], 0
    for sz in sizes:
        offs.append(o)
        o += -(-sz // 128) * 128
    return offs, o + (-o % 1024)


def _pack_vectors(name, vecs):
    sizes = [v.shape[1] for v in vecs]
    offs, total = _slot_offsets(sizes)

    def body(*refs):
        out = refs[-1]
        out[...] = jnp.zeros_like(out)
        for r, o, sz in zip(refs[:-1], offs, sizes):
            out[:, o:o + sz] = r[...]

    return pl.pallas_call(body, name=name, out_shape=jax.ShapeDtypeStruct((1, total), F32),
                          compiler_params=pltpu.CompilerParams(vmem_limit_bytes=VMEM_LIMIT))(*vecs).reshape(-1, 128)


def _xbc_runs(n_groups, base=0):
    ds, nb = n_groups * GROUP_X, n_groups * D_STATE
    runs = []
    for g in range(n_groups):
        runs += [(base + GROUP_X * g, GROUP_X), (base + ds + D_STATE * g, D_STATE), (base + ds + nb + D_STATE * g, D_STATE)]
    return runs


def _interleave_runs(width, base=0):
    half = width // 2
    runs = []
    for j in range(half // COL_BLOCK):
        runs += [(base + COL_BLOCK * j, COL_BLOCK), (base + half + COL_BLOCK * j, COL_BLOCK)]
    return runs


def _w_in_runs(d, d_ssm, d_xbc, n_heads, n_groups):
    o_dt = d_ssm + d_xbc
    o_glu = o_dt + 2 * n_heads
    return ([(0, d_ssm)] + _xbc_runs(n_groups, d_ssm) + _interleave_runs(2 * d, o_glu) + _interleave_runs(2 * d, o_glu + 2 * d)
            + [(o_dt, 2 * n_heads)])


def _inverse_runs(runs):
    offs = np.concatenate([[0], np.cumsum([l for _, l in runs])])
    order = sorted(range(len(runs)), key=lambda k: runs[k][0])
    return [(int(offs[k]), runs[k][1]) for k in order]


def _permute_cols(name, srcs, runs, out_widths=None, src_sharded=False, dst_sharded=False):
    rows = srcs[0].shape[-2]
    dtype = srcs[0].dtype
    total = sum(l for _, l in runs)
    src_w = [srcs[0].shape[2]] * N_DEV if src_sharded else [a.shape[1] for a in srcs]
    dst_w = [total // N_DEV] * N_DEV if dst_sharded else (out_widths or [total])
    assert sum(dst_w) == total and sum(src_w) >= max(s_ + l for s_, l in runs)
    src_edges, dst_edges = np.cumsum([0] + src_w), np.cumsum([0] + dst_w)

    def locate(edges, col):
        k = int(np.searchsorted(edges, col, side="right")) - 1
        return k, col - int(edges[k])

    pieces, o = [], 0
    for s_, l in runs:
        while l:
            (ks, cs_), (kd, cd) = locate(src_edges, s_), locate(dst_edges, o)
            n = min(l, src_w[ks] - cs_, dst_w[kd] - cd)
            if pieces and pieces[-1][0] == ks and pieces[-1][2] == kd and pieces[-1][1] + pieces[-1][4] == cs_:
                pieces[-1] = pieces[-1][:4] + (pieces[-1][4] + n,)
            else:
                pieces.append((ks, cs_, kd, cd, n))
            s_, o, l = s_ + n, o + n, l - n
    tr = rows
    while tr % 16 == 0 and tr * max(sum(src_w), total) * dtype.itemsize > 4 * 2 ** 20:
        tr //= 2
    n_src = len(srcs)

    def body(*refs):
        ins, outs = refs[:n_src], refs[n_src:]
        for ks, cs_, kd, cd, n in pieces:
            val = ins[0][ks, :, cs_:cs_ + n] if src_sharded else ins[ks][:, cs_:cs_ + n]
            if dst_sharded:
                outs[0][kd, :, cd:cd + n] = val
            else:
                outs[kd][:, cd:cd + n] = val

    def spec(sharded, width):
        if sharded:
            return pl.BlockSpec((N_DEV, tr, width), lambda i: (0, i, 0))
        return pl.BlockSpec((tr, width), lambda i: (i, 0))

    in_specs = [spec(True, src_w[0])] if src_sharded else [spec(False, w) for w in src_w]
    if dst_sharded:
        out_specs, out_shape = [spec(True, dst_w[0])], [jax.ShapeDtypeStruct((N_DEV, rows, dst_w[0]), dtype)]
    else:
        out_specs, out_shape = [spec(False, w) for w in dst_w], [jax.ShapeDtypeStruct((rows, w), dtype) for w in dst_w]
    return pl.pallas_call(body, name=name, grid=(rows // tr,), in_specs=in_specs, out_specs=out_specs, out_shape=out_shape,
                          compiler_params=_params(1))(*srcs)


def _cols_from_shards(g):
    return g.transpose(1, 0, 2).reshape(g.shape[1], -1)


def _cols_to_shards(a):
    return a.reshape(a.shape[0], N_DEV, -1).transpose(1, 0, 2)


def _pad_rows(a, rows):
    return jnp.concatenate([a, jnp.zeros((rows - a.shape[0], a.shape[1]), a.dtype)], axis=0)


def _pack(vecs):
    a = jnp.concatenate(vecs, axis=1)
    return jnp.concatenate([a, jnp.zeros((1, -a.shape[1] % 1024), a.dtype)], axis=1).reshape(-1, 128)


def kernel(x, c, w_ada, b_ada, g_pre_mix, g_post_mix, w_in, w_conv_ssm, b_conv_ssm, dt_bias_fwd, dt_bias_bwd, a_log_fwd, a_log_bwd, d_skip, g_ssm_norm, w_ssm_out, b_glu, w_dw, b_dw, ln_g, ln_b, w_conv_out, b_conv_out, b_gate, w_mix_out, g_pre_ffn, g_post_ffn, w_gate_up, w_down, loss_target, m_w_ada, m_b_ada, m_g_pre_mix, m_g_post_mix, m_w_in, m_w_conv_ssm, m_b_conv_ssm, m_dt_bias_fwd, m_dt_bias_bwd, m_a_log_fwd, m_a_log_bwd, m_d_skip, m_g_ssm_norm, m_w_ssm_out, m_b_glu, m_w_dw, m_b_dw, m_ln_g, m_ln_b, m_w_conv_out, m_b_conv_out, m_b_gate, m_w_mix_out, m_g_pre_ffn, m_g_post_ffn, m_w_gate_up, m_w_down, v_w_ada, v_b_ada, v_g_pre_mix, v_g_post_mix, v_w_in, v_w_conv_ssm, v_b_conv_ssm, v_dt_bias_fwd, v_dt_bias_bwd, v_a_log_fwd, v_a_log_bwd, v_d_skip, v_g_ssm_norm, v_w_ssm_out, v_b_glu, v_w_dw, v_b_dw, v_ln_g, v_ln_b, v_w_conv_out, v_b_conv_out, v_b_gate, v_w_mix_out, v_g_pre_ffn, v_g_post_ffn, v_w_gate_up, v_w_down):
    weights = dict(w_ada=w_ada, b_ada=b_ada, g_pre_mix=g_pre_mix, g_post_mix=g_post_mix, w_in=w_in, w_conv_ssm=w_conv_ssm, b_conv_ssm=b_conv_ssm, dt_bias_fwd=dt_bias_fwd, dt_bias_bwd=dt_bias_bwd, a_log_fwd=a_log_fwd, a_log_bwd=a_log_bwd, d_skip=d_skip, g_ssm_norm=g_ssm_norm, w_ssm_out=w_ssm_out, b_glu=b_glu, w_dw=w_dw, b_dw=b_dw, ln_g=ln_g, ln_b=ln_b, w_conv_out=w_conv_out, b_conv_out=b_conv_out, b_gate=b_gate, w_mix_out=w_mix_out, g_pre_ffn=g_pre_ffn, g_post_ffn=g_post_ffn, w_gate_up=w_gate_up, w_down=w_down)
    mom_m = dict(w_ada=m_w_ada, b_ada=m_b_ada, g_pre_mix=m_g_pre_mix, g_post_mix=m_g_post_mix, w_in=m_w_in, w_conv_ssm=m_w_conv_ssm, b_conv_ssm=m_b_conv_ssm, dt_bias_fwd=m_dt_bias_fwd, dt_bias_bwd=m_dt_bias_bwd, a_log_fwd=m_a_log_fwd, a_log_bwd=m_a_log_bwd, d_skip=m_d_skip, g_ssm_norm=m_g_ssm_norm, w_ssm_out=m_w_ssm_out, b_glu=m_b_glu, w_dw=m_w_dw, b_dw=m_b_dw, ln_g=m_ln_g, ln_b=m_ln_b, w_conv_out=m_w_conv_out, b_conv_out=m_b_conv_out, b_gate=m_b_gate, w_mix_out=m_w_mix_out, g_pre_ffn=m_g_pre_ffn, g_post_ffn=m_g_post_ffn, w_gate_up=m_w_gate_up, w_down=m_w_down)
    mom_v = dict(w_ada=v_w_ada, b_ada=v_b_ada, g_pre_mix=v_g_pre_mix, g_post_mix=v_g_post_mix, w_in=v_w_in, w_conv_ssm=v_w_conv_ssm, b_conv_ssm=v_b_conv_ssm, dt_bias_fwd=v_dt_bias_fwd, dt_bias_bwd=v_dt_bias_bwd, a_log_fwd=v_a_log_fwd, a_log_bwd=v_a_log_bwd, d_skip=v_d_skip, g_ssm_norm=v_g_ssm_norm, w_ssm_out=v_w_ssm_out, b_glu=v_b_glu, w_dw=v_w_dw, b_dw=v_b_dw, ln_g=v_ln_g, ln_b=v_ln_b, w_conv_out=v_w_conv_out, b_conv_out=v_b_conv_out, b_gate=v_b_gate, w_mix_out=v_w_mix_out, g_pre_ffn=v_g_pre_ffn, g_post_ffn=v_g_post_ffn, w_gate_up=v_w_gate_up, w_down=v_w_down)
    names = list(weights)

    s, d = x.shape[1], x.shape[2]
    d_ssm, n_heads, d_xbc = g_ssm_norm.shape[-1], d_skip.shape[-1], b_conv_ssm.shape[-1]
    n_groups = n_heads // HEADS_PER_GROUP
    d_ff = w_down.shape[1] * N_DEV
    k_ssm, k_dw = w_conv_ssm.shape[1], w_dw.shape[1]
    assert d_ssm == n_groups * GROUP_X and d_xbc == n_groups * GROUP_W and s % CHUNK == 0
    assert d % COL_BLOCK == 0 and d_ff % COL_BLOCK == 0
    mc = w_ada.shape[-1]
    me = _slot(_mesh_pos())
    wb = d_ssm + d_xbc + 4 * d
    p_x, p_glu, p_gate = d_ssm, d_ssm + d_xbc, d_ssm + d_xbc + 2 * d
    ncb, nfb = d // COL_BLOCK, d_ff // COL_BLOCK
    x2, tgt = x[0], loss_target[0]

    big = ["w_in", "w_ssm_out", "w_conv_out", "w_mix_out", "w_gate_up", "w_down"]
    (w_in_g,) = _all_gather("gather_w_in", [w_in[0].astype(BF16)])
    w_in_runs, xbc_runs, il_runs = _w_in_runs(d, d_ssm, d_xbc, n_heads, n_groups), _xbc_runs(n_groups), _interleave_runs(2 * d)
    w_big, w_dt = _permute_cols("order_w_in", [w_in_g], w_in_runs, out_widths=[wb, 2 * n_heads], src_sharded=True)

    n_cs, n_dw = k_ssm * (d_xbc // N_DEV), k_dw * (d // N_DEV)
    (small_all,) = _all_gather("gather_small", [_pack([c, w_conv_ssm.reshape(1, n_cs), w_dw.reshape(1, n_dw)])])
    small_all = small_all.reshape(N_DEV, -1)
    c_all = small_all[:, :d]
    wcs = _cols_from_shards(small_all[:, d:d + n_cs].reshape(N_DEV, k_ssm, -1))
    wdw = _cols_from_shards(small_all[:, d + n_cs:d + n_cs + n_dw].reshape(N_DEV, k_dw, -1))
    (wcs_p,) = _permute_cols("order_conv_taps", [wcs], xbc_runs)
    (bcs_p,) = _permute_cols("order_conv_bias", [b_conv_ssm], xbc_runs)
    (b_glu_il,), (b_gate_il,) = _permute_cols("order_b_glu", [b_glu], il_runs), _permute_cols("order_b_gate", [b_gate], il_runs)
    kp_ssm, kp_dw = -(-k_ssm // 8) * 8, -(-k_dw // 8) * 8
    zero_x, zero_d = jnp.zeros((1, d_xbc), F32), jnp.zeros((1, d), F32)

    (c_act,), _ = _rowcall("c_act", _fwd_fn(_silu), [(c_all, 0, d)], [], [(d, F32)], [])
    b_my = lax.dynamic_slice(b_ada, (0, me * mc), (1, mc))
    mod_cols = _matmul("mod", c_act, w_ada[0], "nn", F32, acc_in=jnp.broadcast_to(b_my, (N_DEV, mc)))
    (mod_rows,) = _all_to_all("mod_to_owner", [mod_cols.reshape(N_DEV, 1, mc)])
    mod = mod_rows.reshape(1, N_DEV * mc)
    sh1, sc1, g1, sh2, sc2, g2 = [mod[:, k * d:(k + 1) * d] for k in range(6)]

    (h1,), _ = _rowcall("prenorm_mix", _fwd_fn(_t_prenorm), [(x2, 0, d)], [g_pre_mix, sc1, sh1], [(d, BF16)], [])
    proj, rest = _matmul("in_proj", h1, w_big, "nn", BF16, carry=(_gather_ops, [weights[k][0].astype(BF16) for k in big[1:]]))
    gath = dict(zip(big[1:], rest))
    w_ssm, w_co, w_mo = gath["w_ssm_out"].reshape(d_ssm, d), gath["w_conv_out"].reshape(d, d), gath["w_mix_out"].reshape(d, d)
    gu_runs = _interleave_runs(2 * d_ff)
    (w_gu,) = _permute_cols("order_w_gate_up", [gath["w_gate_up"]], gu_runs, src_sharded=True)
    w_dn = gath["w_down"].reshape(d_ff, d)
    dt_raw = _matmul("in_proj_dt", h1, w_dt, "nn", F32)
    xbc_c = _dwconv("conv_ssm", proj, p_x, d_xbc, _pad_rows(wcs_p, kp_ssm), k_ssm, bcs_p, True, BF16)
    alog_col = jnp.concatenate([a_log_fwd, a_log_bwd], axis=1).reshape(2 * n_heads, 1)
    dt_t, a_t = _dtprep("dt_prep", dt_raw, jnp.concatenate([dt_bias_fwd, dt_bias_bwd], axis=1), alog_col)
    dsk = jnp.repeat(d_skip, HEAD_DIM, axis=1)
    y_f, hin_f, y_b, hin_b = _ssd_fwd("ssd", xbc_c, dt_t, a_t, dsk, n_groups)
    gn_rows = [(y_f, 0, d_ssm), (y_b, 0, d_ssm), (proj, 0, d_ssm)]
    (yn,), _ = _rowcall("ssm_norm", _fwd_fn(_t_gnorm), gn_rows, [g_ssm_norm], [(d_ssm, BF16)], [], ncol=n_groups)
    ya = _matmul("ssm_out", yn, w_ssm, "nn", BF16)
    (u0,), _ = _rowcall("glu", _glu_fwd, [(proj, p_glu, 2 * d)], [b_glu_il], [(d, BF16)], [], ncol=ncb)
    wdw_pad = _pad_rows(wdw, kp_dw)
    u1 = _dwconv("conv_dw", u0, 0, d, wdw_pad, k_dw, b_dw, False, BF16)
    (u2,), _ = _rowcall("ln_silu", _fwd_fn(_t_lnsilu), [(u1, 0, d)], [ln_g, ln_b], [(d, BF16)], [])
    yb = _matmul("conv_out", u2, w_co, "nn", BF16)
    gm_rows, gm_prm = [(ya, 0, d), (yb, 0, d), (proj, p_gate, 2 * d)], [b_conv_out, b_gate_il]
    (m_in,), _ = _rowcall("gate_mix", _gatemix_fwd, gm_rows, gm_prm, [(d, BF16)], [], ncol=ncb)
    mix = _matmul("mix_out", m_in, w_mo, "nn", BF16)
    pp_prm = [g_post_mix, g1, g_pre_ffn, sc2, sh2]
    (x1, h2), _ = _rowcall("post_mix", _fwd_fn(_t_postpre), [(x2, 0, d), (mix, 0, d)], pp_prm, [(d, F32), (d, BF16)], [])
    gu, act = _matmul("gate_up", h2, w_gu, "nn", BF16, tn=2 * COL_BLOCK,
                      epilogue=(lambda p: (p, _swiglu_fwd(p)[0][0]), [], [(2 * d_ff, BF16), (d_ff, BF16)]))
    f = _matmul("down", act, w_dn, "nn", BF16)

    def loss_fn(x1_t, f_t, tgt_t, gpost, g2_t):
        args = [_f(x1_t), _f(f_t), _f(tgt_t), _f(gpost), _f(g2_t)]
        val, vjp = jax.vjp(_t_loss, *args)
        gr = vjp(jnp.ones((), F32))
        return (gr[0], gr[1]), (jnp.zeros((1, 128), F32) + val, gr[3], gr[4])

    (dx1a, df), (loss_v, d_gpost2, d_g2) = _rowcall(
        "loss", loss_fn, [(x1, 0, d), (f, 0, d), (tgt, 0, d)], [g_post_ffn, g2], [(d, F32), (d, BF16)], [128, d, d])
    loss = lax.psum(loss_v[0, 0], ("x", "y", "c"))

    dgu = _matmul("down_dx", df, w_dn, "nt", BF16, tn=COL_BLOCK,
                  epilogue=(lambda p, gu_t: _swiglu_bwd(gu_t, p)[0], [gu], [(2 * d_ff, BF16)]))
    dw_dn = _matmul("down_dw", act, df, "tn", BF16)
    recv = {}
    dh2, (recv["w_down"],) = _matmul("gate_up_dx", dgu, w_gu, "nt", BF16, carry=(_a2a_ops, [dw_dn.reshape(N_DEV, -1, d)]))
    dw_gu = _matmul("gate_up_dw", h2, dgu, "tn", BF16)
    (dxa, dmix), (d_gpost1, d_g1, d_gpre2, d_sc2, d_sh2) = _rowcall(
        "post_mix_bwd", _vjp_fn(_t_postpre, 2, 2, (0, 1), (0, 1, 2, 3, 4)),
        [(x2, 0, d), (mix, 0, d), (dx1a, 0, d), (dh2, 0, d)], pp_prm, [(d, F32), (d, BF16)], [d] * 5)
    dm_in = _matmul("mix_out_dx", dmix, w_mo, "nt", BF16)
    dw_mo = _matmul("mix_out_dw", m_in, dmix, "tn", BF16)
    dproj = lax.empty((s, wb), BF16)
    (dya, dyb, dproj), (d_bco, d_bgate) = _rowcall(
        "gate_mix_bwd", _gatemix_bwd, gm_rows + [(dm_in, 0, d)], gm_prm,
        [(d, BF16), (d, BF16), (2 * d, BF16, dproj, p_gate)], [d, 2 * d], ncol=ncb)
    du2 = _matmul("conv_out_dx", dyb, w_co, "nt", BF16)
    dw_co = _matmul("conv_out_dw", u2, dyb, "tn", BF16)
    (du1,), (d_lng, d_lnb) = _rowcall("ln_silu_bwd", _vjp_fn(_t_lnsilu, 1, 1, (0,), (0, 1)), [(u1, 0, d), (du2, 0, d)],
                                      [ln_g, ln_b], [(d, BF16)], [d, d])
    dwdw_p, d_bdw = _dwconv_bwd_w("conv_dw_dw", u0, 0, d, wdw_pad, k_dw, b_dw, [du1], False)
    du0 = _dwconv("conv_dw_dx", du1, 0, d, _pad_rows(wdw[::-1], kp_dw), k_dw, zero_d, False, BF16)
    (dproj,), (d_bglu,) = _rowcall("glu_bwd", _glu_bwd, [(proj, p_glu, 2 * d), (du0, 0, d)], [b_glu_il],
                                   [(2 * d, BF16, dproj, p_glu)], [2 * d], ncol=ncb)
    dyn, (recv["w_mix_out"], recv["w_conv_out"]) = _matmul(
        "ssm_out_dx", dya, w_ssm, "nt", BF16, carry=(_a2a_ops, [dw_mo.reshape(N_DEV, -1, d), dw_co.reshape(N_DEV, -1, d)]))
    dw_ssm = _matmul("ssm_out_dw", yn, dya, "tn", BF16)
    (dy_ssd, dproj), (d_gn,) = _rowcall("ssm_norm_bwd", _vjp_fn(_t_gnorm, 3, 1, (0, 2), (0,)), gn_rows + [(dyn, 0, d_ssm)],
                                        [g_ssm_norm], [(d_ssm, BF16), (d_ssm, BF16, dproj, 0)], [d_ssm], ncol=n_groups)
    dxbc_f, ddt_f, da_f, dxbc_b, ddt_b, da_b, ddsk = _ssd_bwd("ssd_bwd", xbc_c, dt_t, a_t, dy_ssd, hin_f, hin_b, dsk, n_groups)
    ddt_raw, d_dtb, d_alog = _dtprep_bwd("dt_prep_bwd", ddt_f, ddt_b, da_f, da_b, dt_t, alog_col)
    dpre, dwcs_p, dbcs_p = _dwconv_bwd_w("conv_ssm_dw", proj, p_x, d_xbc, _pad_rows(wcs_p, kp_ssm), k_ssm, bcs_p,
                                          [dxbc_f, dxbc_b], True)
    dproj = _dwconv("conv_ssm_dx", dpre, 0, d_xbc, _pad_rows(wcs_p[::-1], kp_ssm), k_ssm, zero_x, False, BF16, into=(dproj, p_x))
    dw_big, (recv["w_gate_up"], recv["w_ssm_out"]) = _matmul(
        "in_proj_dw", h1, dproj, "tn", BF16,
        carry=(_a2a_ops, [_permute_cols("restore_dw_gate_up", [dw_gu], _inverse_runs(gu_runs), dst_sharded=True)[0],
                          dw_ssm.reshape(N_DEV, -1, d)]))
    dw_dt = _matmul("in_proj_dt_dw", h1, ddt_raw, "tn", BF16)
    (dw_in,) = _permute_cols("restore_dw_in", [dw_big, dw_dt], _inverse_runs(w_in_runs), dst_sharded=True)
    dh1, (recv["w_in"],) = _matmul("in_proj_dx", dproj, w_big, "nt", F32, carry=(_a2a_ops, [dw_in]))
    dh1 = _matmul("in_proj_dt_dx", ddt_raw, w_dt, "nt", BF16, acc_in=dh1)

    def prenorm_bwd_fn(x_t, dh_t, dxa_t, g, sc, sh):
        rows, prm = _vjp_fn(_t_prenorm, 1, 1, (0,), (0, 1, 2))(x_t, dh_t, g, sc, sh)
        return (rows[0] + _f(dxa_t),), prm

    (grad_x,), (d_gpre1, d_sc1, d_sh1) = _rowcall(
        "prenorm_mix_bwd", prenorm_bwd_fn, [(x2, 0, d), (dh1, 0, d), (dxa, 0, d)], [g_pre_mix, sc1, sh1], [(d, F32)], [d] * 3)

    dmod = jnp.concatenate([d_sh1, d_sc1, d_g1, d_sh2, d_sc2, d_g2], axis=1)
    small_g = {
        "b_ada": dmod, "g_pre_mix": d_gpre1, "g_post_mix": d_gpost1,
        "b_conv_ssm": _permute_cols("restore_db_conv", [dbcs_p], _inverse_runs(xbc_runs))[0],
        "dt_bias_fwd": d_dtb[:n_heads].reshape(1, n_heads), "dt_bias_bwd": d_dtb[n_heads:].reshape(1, n_heads),
        "a_log_fwd": d_alog[:n_heads].reshape(1, n_heads), "a_log_bwd": d_alog[n_heads:].reshape(1, n_heads),
        "d_skip": ddsk.reshape(n_heads, HEAD_DIM).sum(axis=1).reshape(1, n_heads), "g_ssm_norm": d_gn,
        "b_glu": _permute_cols("restore_db_glu", [d_bglu], _inverse_runs(il_runs))[0], "b_dw": d_bdw, "ln_g": d_lng, "ln_b": d_lnb,
        "b_conv_out": d_bco, "b_gate": _permute_cols("restore_db_gate", [d_bgate], _inverse_runs(il_runs))[0],
        "g_pre_ffn": d_gpre2, "g_post_ffn": d_gpost2,
        "w_conv_ssm": _permute_cols("restore_dw_conv", [dwcs_p], _inverse_runs(xbc_runs))[0][:k_ssm].reshape(1, k_ssm * d_xbc),
        "w_dw": dwdw_p[:k_dw].reshape(1, k_dw * d),
    }
    small_names = list(small_g)
    sizes = [small_g[k].shape[1] for k in small_names]
    offs, _ = _slot_offsets(sizes)
    (small_parts,) = _all_gather("gather_small_grads", [_pack_vectors("pack_small_grads", [small_g[k] for k in small_names])])
    dmod_all = small_parts.reshape(N_DEV, -1)[:, :6 * d]
    small_tot = _reduce_sum("sum_small_grads", small_parts).reshape(1, -1)
    grads = {k: small_tot[:, offs[i]:offs[i] + sizes[i]] for i, k in enumerate(small_names)}
    grads["w_conv_ssm"] = lax.dynamic_slice(grads["w_conv_ssm"].reshape(k_ssm, d_xbc), (0, me * (d_xbc // N_DEV)), (k_ssm, d_xbc // N_DEV))
    grads["w_dw"] = lax.dynamic_slice(grads["w_dw"].reshape(k_dw, d), (0, me * (d // N_DEV)), (k_dw, d // N_DEV))
    grads["w_ada"] = _matmul("mod_dw", c_act, lax.dynamic_slice(dmod_all, (0, me * mc), (N_DEV, mc)), "tn", F32, highest=True)

    delta, new_m, new_v = {}, {}, {}
    for k in big:
        grads[k], delta[k], new_m[k], new_v[k] = _adam("adam_" + k, weights[k][0], None, mom_m[k][0], mom_v[k][0], parts=recv[k])
    for k in ["w_ada", "w_conv_ssm", "w_dw"]:
        delta[k], new_m[k], new_v[k] = _adam("adam_" + k, weights[k][0], grads[k], mom_m[k][0], mom_v[k][0])
    rep = [k for k in names if k not in big and k not in ("w_ada", "w_conv_ssm", "w_dw")]
    r_delta, r_m, r_v = _adam_many("adam_small", *[[t[k] for k in rep] for t in (weights, grads, mom_m, mom_v)])
    for i, k in enumerate(rep):
        delta[k], new_m[k], new_v[k] = r_delta[i], r_m[i], r_v[i]

    def shaped(t, k):
        return t[k].reshape(weights[k].shape)

    return (loss, grad_x.reshape(x.shape), *[shaped(grads, k) for k in names], *[shaped(delta, k) for k in names],
            *[shaped(new_m, k) for k in names], *[shaped(new_v, k) for k in names])
```

```python
import functools

import numpy as np
import jax
import jax.numpy as jnp
from jax import lax
from jax.experimental import pallas as pl
from jax.experimental.pallas import tpu as pltpu

F32 = jnp.float32
BF16 = jnp.bfloat16
HIGHEST = lax.Precision.HIGHEST
MESH = pl.DeviceIdType.MESH
N_DEV = 8
EPS = 1e-6
CHUNK = 128
HEAD_DIM = 64
D_STATE = 128
HEADS_PER_GROUP = 8
GROUP_X = HEADS_PER_GROUP * HEAD_DIM
GROUP_W = GROUP_X + 2 * D_STATE
COL_BLOCK = 512
HALO = 16
VMEM_LIMIT = 56 * 1024 * 1024
ADAM_LR, ADAM_B1, ADAM_B2, ADAM_EPS, ADAM_WD, ADAM_STEP = 0.001, 0.9, 0.999, 1e-08, 0.01, 10

NN = (((1,), (0,)), ((), ()))
NT = (((1,), (1,)), ((), ()))
TN = (((0,), (0,)), ((), ()))


def _tile(dim, prefs):
    for p in prefs:
        if p <= dim and dim % p == 0:
            return p
    return dim


def _params(n_grid):
    return pltpu.CompilerParams(dimension_semantics=("arbitrary",) * n_grid, vmem_limit_bytes=VMEM_LIMIT)


def _f(v):
    return v.astype(F32)


def _matmul(name, a, b, mode, out_dtype, acc_in=None, highest=False, carry=None, tn=None, epilogue=None):
    if mode == "nn":
        (m, k), (k2, n) = a.shape, b.shape
    elif mode == "nt":
        (m, k), (n, k2) = a.shape, b.shape
    else:
        (k, m), (k2, n) = a.shape, b.shape
    assert k == k2, (name, a.shape, b.shape, mode)
    tm = _tile(m, (1024, 512, 256, 128))
    tn = _tile(n, (1024, 512, 256, 128)) if tn is None else tn
    tk = _tile(k, (2816, 2048, 1024, 512, 256, 128))
    nk, nj = k // tk, n // tn
    assert n % tn == 0
    dims = {"nn": NN, "nt": NT, "tn": TN}[mode]
    a_spec = pl.BlockSpec((tk, tm), lambda i, j, kk: (kk, i)) if mode == "tn" else pl.BlockSpec((tm, tk), lambda i, j, kk: (i, kk))
    b_spec = pl.BlockSpec((tn, tk), lambda i, j, kk: (j, kk)) if mode == "nt" else pl.BlockSpec((tk, tn), lambda i, j, kk: (kk, j))
    o_spec = pl.BlockSpec((tm, tn), lambda i, j, kk: (i, j))
    has_acc = acc_in is not None
    epi_fn, e_arrs, e_outs = epilogue if epilogue is not None else (None, [], [])
    ne = len(e_arrs)
    n_in = (3 if has_acc else 2) + ne
    if epilogue is None:
        o_specs, o_shapes = [o_spec], [jax.ShapeDtypeStruct((m, n), out_dtype)]
    else:
        o_specs = [pl.BlockSpec((tm, w // nj), lambda i, j, kk: (i, j)) for w, _ in e_outs]
        o_shapes = [jax.ShapeDtypeStruct((m, w), dt) for w, dt in e_outs]
    e_specs = [pl.BlockSpec((tm, arr.shape[1] // nj), lambda i, j, kk: (i, j)) for arr in e_arrs]
    n_out = len(o_specs)
    ops, c_arrs = carry if carry is not None else (None, [])
    nc_ = len(c_arrs)
    grid = (m // tm, nj, nk)

    def body(*refs):
        a_ref, b_ref = refs[0], refs[1]
        c_ref = refs[2] if has_acc else None
        e_refs = refs[n_in - ne:n_in]
        c_ins = refs[n_in:n_in + nc_]
        o_refs = refs[n_in + nc_:n_in + nc_ + n_out]
        c_outs = refs[n_in + nc_ + n_out:n_in + 2 * nc_ + n_out]
        acc = refs[n_in + 2 * nc_ + n_out]
        i, j, kk = pl.program_id(0), pl.program_id(1), pl.program_id(2)
        if nc_:
            start, finish = ops(c_ins, c_outs, *refs[n_in + 2 * nc_ + n_out + 1:])
            pl.when((i == 0) & (j == 0) & (kk == 0))(start)

        def write(val):
            if epi_fn is None:
                o_refs[0][...] = val.astype(out_dtype)
            else:
                for ref, v in zip(o_refs, epi_fn(val, *[r[...] for r in e_refs])):
                    ref[...] = v.astype(ref.dtype)

        def product(first):
            if highest:
                p = lax.dot_general(_f(a_ref[...]), _f(b_ref[...]), dims, precision=HIGHEST, preferred_element_type=F32)
            else:
                p = lax.dot_general(a_ref[...].astype(BF16), b_ref[...].astype(BF16), dims, preferred_element_type=F32)
            return p + _f(c_ref[...]) if (first and has_acc) else p

        if nk == 1:
            write(product(True))
        else:
            @pl.when(kk == 0)
            def _():
                acc[...] = product(True)

            if nk > 2:
                @pl.when((kk > 0) & (kk < nk - 1))
                def _():
                    acc[...] += product(False)

            @pl.when(kk == nk - 1)
            def _():
                write(acc[...] + product(False))

        if nc_:
            pl.when((i == grid[0] - 1) & (j == grid[1] - 1) & (kk == nk - 1))(finish)

    any_spec = pl.BlockSpec(memory_space=pl.ANY)
    c_shapes = ops.out_shapes(c_arrs) if nc_ else []
    res = pl.pallas_call(
        body, name=name, grid=grid,
        in_specs=[a_spec, b_spec] + ([o_spec] if has_acc else []) + e_specs + [any_spec] * nc_,
        out_specs=o_specs + [any_spec] * nc_, out_shape=o_shapes + c_shapes,
        scratch_shapes=[pltpu.VMEM((tm, tn) if nk > 1 else (8, 128), F32)] + (_exchange_sems(nc_) if nc_ else []),
        compiler_params=_params(3),
    )(*([a, b] + ([acc_in] if has_acc else []) + list(e_arrs) + list(c_arrs)))
    outs = res[0] if n_out == 1 else res[:n_out]
    return (outs, res[n_out:]) if nc_ else outs


def _rowcall(name, fn, rows, params, out_rows, out_accs, ncol=1):
    s = rows[0][0].shape[0]
    tm = _tile(s, (256 if ncol == 1 else 1024, 128, 64, 32, 16, 8))
    args, in_specs = [], []
    for arr, off, w in rows:
        bw = w // ncol
        assert w % ncol == 0 and off % bw == 0, (name, off, w, ncol)
        in_specs.append(pl.BlockSpec((tm, bw), functools.partial(lambda j, i, ob: (i, ob + j), ob=off // bw)))
        args.append(arr)
    for p in params:
        assert p.shape[1] % ncol == 0, (name, p.shape)
        in_specs.append(pl.BlockSpec((p.shape[0], p.shape[1] // ncol), lambda j, i: (0, j)))
        args.append(p)
    n_in, n_ro = len(args), len(out_rows)
    out_shape, out_specs, aliases = [], [], {}
    for k, spec in enumerate(out_rows):
        w, dt = spec[0], spec[1]
        bw = w // ncol
        if len(spec) == 4:
            buf, off = spec[2], spec[3]
            assert off % bw == 0 and buf.dtype == dt, (name, off, bw)
            out_shape.append(jax.ShapeDtypeStruct(buf.shape, dt))
            out_specs.append(pl.BlockSpec((tm, bw), functools.partial(lambda j, i, ob: (i, ob + j), ob=off // bw)))
            aliases[len(args)] = k
            in_specs.append(pl.BlockSpec(memory_space=pl.ANY))
            args.append(buf)
        else:
            out_shape.append(jax.ShapeDtypeStruct((s, w), dt))
            out_specs.append(pl.BlockSpec((tm, bw), lambda j, i: (i, j)))
    for w in out_accs:
        out_shape.append(jax.ShapeDtypeStruct((1, w), F32))
        out_specs.append(pl.BlockSpec((1, w // ncol), lambda j, i: (0, j)))
    n_args = len(args)

    def body(*refs):
        i = pl.program_id(1)
        ro, ao = fn(*[r[...] for r in refs[:n_in]])
        outs = refs[n_args:]
        assert len(ro) == n_ro and len(ao) == len(out_accs), name
        for ref, v in zip(outs[:n_ro], ro):
            ref[...] = v.astype(ref.dtype)
        for ref, v in zip(outs[n_ro:], ao):
            @pl.when(i == 0)
            def _(ref=ref, v=v):
                ref[...] = v

            @pl.when(i > 0)
            def _(ref=ref, v=v):
                ref[...] += v

    res = pl.pallas_call(
        body, name=name, grid=(ncol, s // tm), in_specs=in_specs, out_specs=out_specs,
        out_shape=out_shape, input_output_aliases=aliases, compiler_params=_params(2),
    )(*args)
    return res[:n_ro], res[n_ro:]


def _fwd_fn(f):
    def fn(*vals):
        out = f(*[_f(v) for v in vals])
        return (out if isinstance(out, tuple) else (out,)), ()
    return fn


def _vjp_fn(f, n_prim, n_ct, want_rows, want_params):
    def fn(*vals):
        prims = [_f(v) for v in vals[:n_prim]]
        cts = [_f(v) for v in vals[n_prim:n_prim + n_ct]]
        prms = [_f(v) for v in vals[n_prim + n_ct:]]
        out, vjp = jax.vjp(f, *prims, *prms)
        g = vjp(tuple(cts) if isinstance(out, tuple) else cts[0])
        return tuple(g[k] for k in want_rows), tuple(g[n_prim + k] for k in want_params)
    return fn


def _rms(v):
    return v * lax.rsqrt(jnp.mean(v * v, axis=-1, keepdims=True) + EPS)


def _silu(v):
    return v * jax.nn.sigmoid(v)


def _t_prenorm(x, g, sc, sh):
    return _rms(x) * g * (1.0 + sc) + sh


def _t_gnorm(yf, yb, z, gn):
    return _rms((yf + yb) * _silu(z)) * gn


def _t_glu(a, b, ba, bb):
    return (a + ba) * jax.nn.sigmoid(b + bb)


def _t_lnsilu(u, g, b):
    uc = u - jnp.mean(u, axis=-1, keepdims=True)
    return _silu(uc * lax.rsqrt(jnp.mean(uc * uc, axis=-1, keepdims=True) + EPS) * g + b)


def _t_gatemix(ya, yb, la, lb, bco, bga, bgb):
    return jax.nn.sigmoid(la + bga) * ya + jax.nn.sigmoid(lb + bgb) * (yb + bco)


def _t_postpre(x, mix, gpost, g1, gpre, sc, sh):
    x1 = x + g1 * (_rms(mix) * gpost)
    return x1, _rms(x1) * gpre * (1.0 + sc) + sh


def _t_swiglu(gt, up):
    return _silu(gt) * up


def _halves(v):
    return _f(v[:, :COL_BLOCK]), _f(v[:, COL_BLOCK:])


def _glu_fwd(ab, bias):
    return (_t_glu(*_halves(ab), *_halves(bias)),), ()


def _glu_bwd(ab, du, bias):
    _, vjp = jax.vjp(_t_glu, *_halves(ab), *_halves(bias))
    da, db, dba, dbb = vjp(_f(du))
    return (jnp.concatenate([da, db], axis=1),), (jnp.concatenate([dba, dbb], axis=1),)


def _gatemix_fwd(ya, yb, logits, bco, bg):
    return (_t_gatemix(_f(ya), _f(yb), *_halves(logits), _f(bco), *_halves(bg)),), ()


def _gatemix_bwd(ya, yb, logits, dm, bco, bg):
    _, vjp = jax.vjp(_t_gatemix, _f(ya), _f(yb), *_halves(logits), _f(bco), *_halves(bg))
    dya, dyb, dla, dlb, dbco, dbga, dbgb = vjp(_f(dm))
    return (dya, dyb, jnp.concatenate([dla, dlb], axis=1)), (dbco, jnp.concatenate([dbga, dbgb], axis=1))


def _swiglu_fwd(gu):
    return (_t_swiglu(*_halves(gu)),), ()


def _swiglu_bwd(gu, dact):
    _, vjp = jax.vjp(_t_swiglu, *_halves(gu))
    return (jnp.concatenate(vjp(_f(dact)), axis=1),), ()


def _t_loss(x1, f, tgt, gpost, g2):
    e = x1 + g2 * (_rms(f) * gpost) - tgt
    return 0.5 * jnp.sum(jnp.mean(e * e, axis=-1))


def _conv_specs(s, tm, cw, ob, w_rows):
    nh, last_h = tm // HALO, s // HALO - 1
    return [
        pl.BlockSpec((tm, cw), lambda j, i: (i, ob + j)),
        pl.BlockSpec((HALO, cw), lambda j, i: (jnp.maximum(i * nh - 1, 0), ob + j)),
        pl.BlockSpec((HALO, cw), lambda j, i: (jnp.minimum((i + 1) * nh, last_h), ob + j)),
        pl.BlockSpec((w_rows, cw), lambda j, i: (0, j)),
        pl.BlockSpec((1, cw), lambda j, i: (0, j)),
    ]


def _window(x_ref, p_ref, n_ref, i, nrow):
    prev = _f(p_ref[...]) * jnp.where(i > 0, 1.0, 0.0).astype(F32)
    nxt = _f(n_ref[...]) * jnp.where(i < nrow - 1, 1.0, 0.0).astype(F32)
    return jnp.concatenate([prev, _f(x_ref[...]), nxt], axis=0)


def _shifted(win, k_taps, tm):
    pad, moved, out = (k_taps - 1) // 2, {}, []
    for k in range(k_taps):
        q, r = divmod(HALO + k - pad, 8)
        if r not in moved:
            moved[r] = pltpu.roll(win, win.shape[0] - r, 0) if r else win
        out.append(moved[r][8 * q:8 * q + tm, :])
    return out


def _taps(win, w_ref, k_taps, tm):
    sh = _shifted(win, k_taps, tm)
    acc = w_ref[0:1, :] * sh[0]
    for k in range(1, k_taps):
        acc = acc + w_ref[k:k + 1, :] * sh[k]
    return acc


def _conv_rows(k_taps):
    return 1024 if k_taps <= 8 else 512


def _dwconv(name, x, xoff, c, w_pad, k_taps, b, act, out_dtype, cw=256, into=None):
    s = x.shape[0]
    tm, cw = _tile(s, (_conv_rows(k_taps), 256, 128, 64, 32, 16)), _tile(c, (cw, 128))
    assert xoff % cw == 0 and tm % HALO == 0 and (k_taps - 1) // 2 <= HALO
    nrow = s // tm

    def body(x_ref, p_ref, n_ref, w_ref, b_ref, *rest):
        win = _window(x_ref, p_ref, n_ref, pl.program_id(1), nrow)
        acc = _taps(win, w_ref, k_taps, tm) + b_ref[...]
        rest[-1][...] = (_silu(acc) if act else acc).astype(out_dtype)

    in_specs, args = _conv_specs(s, tm, cw, xoff // cw, w_pad.shape[0]), [x, x, x, w_pad, b]
    if into is None:
        ob, out_shape, aliases = 0, jax.ShapeDtypeStruct((s, c), out_dtype), {}
    else:
        buf, first = into
        assert first % cw == 0 and buf.dtype == out_dtype
        ob, out_shape, aliases = first // cw, jax.ShapeDtypeStruct(buf.shape, out_dtype), {len(args): 0}
        in_specs, args = in_specs + [pl.BlockSpec(memory_space=pl.ANY)], args + [buf]
    return pl.pallas_call(
        body, name=name, grid=(c // cw, nrow), in_specs=in_specs,
        out_specs=pl.BlockSpec((tm, cw), lambda j, i: (i, ob + j)), out_shape=out_shape,
        input_output_aliases=aliases, compiler_params=_params(2),
    )(*args)


def _dwconv_bwd_w(name, x, xoff, c, w_pad, k_taps, b, douts, act, cw=256):
    s = x.shape[0]
    tm, cw = _tile(s, (_conv_rows(k_taps), 256, 128, 64, 32, 16)), _tile(c, (cw, 128))
    assert xoff % cw == 0 and tm % HALO == 0
    nrow, pad, n_d = s // tm, (k_taps - 1) // 2, len(douts)

    def body(x_ref, p_ref, n_ref, w_ref, b_ref, *rest):
        d_refs, outs = rest[:n_d], rest[n_d:]
        i = pl.program_id(1)
        win = _window(x_ref, p_ref, n_ref, i, nrow)
        d = _f(d_refs[0][...])
        for r in d_refs[1:]:
            d = d + _f(r[...])
        if act:
            shifted = _shifted(win, k_taps, tm)
        else:
            shifted = [win[HALO + k - pad:HALO + k - pad + tm, :] for k in range(k_taps)]
        if act:
            pre = b_ref[...] + w_ref[0:1, :] * shifted[0]
            for k in range(1, k_taps):
                pre = pre + w_ref[k:k + 1, :] * shifted[k]
            sg = jax.nn.sigmoid(pre)
            d = d * (sg * (1.0 + pre * (1.0 - sg)))
            outs[0][...] = d.astype(outs[0].dtype)
        dw_ref, db_ref = outs[-2], outs[-1]

        @pl.when(i == 0)
        def _():
            dw_ref[...] = jnp.zeros_like(dw_ref)
            db_ref[...] = jnp.zeros_like(db_ref)

        for k in range(k_taps):
            dw_ref[k:k + 1, :] += jnp.sum(d * shifted[k], axis=0, keepdims=True)
        db_ref[...] += jnp.sum(d, axis=0, keepdims=True)

    row_spec = pl.BlockSpec((tm, cw), lambda j, i: (i, j))
    out_shape = [jax.ShapeDtypeStruct((w_pad.shape[0], c), F32), jax.ShapeDtypeStruct((1, c), F32)]
    out_specs = [pl.BlockSpec((w_pad.shape[0], cw), lambda j, i: (0, j)), pl.BlockSpec((1, cw), lambda j, i: (0, j))]
    if act:
        out_shape, out_specs = [jax.ShapeDtypeStruct((s, c), BF16)] + out_shape, [row_spec] + out_specs
    return pl.pallas_call(
        body, name=name, grid=(c // cw, nrow),
        in_specs=_conv_specs(s, tm, cw, xoff // cw, w_pad.shape[0]) + [row_spec] * n_d,
        out_specs=out_specs, out_shape=out_shape, compiler_params=_params(2),
    )(x, x, x, w_pad, b, *douts)


def _dtprep(name, dt_raw, bias, alog_col):
    s, h2 = dt_raw.shape
    ts = _tile(s, (512, 256, 128))

    def body(r_ref, b_ref, al_ref, dt_ref, a_ref):
        v = r_ref[...] + b_ref[...]
        dt = (jnp.maximum(v, 0.0) + jnp.log(1.0 + jnp.exp(-jnp.abs(v)))).T
        dt_ref[...] = dt
        a_ref[...] = dt * (-jnp.exp(al_ref[...]))

    t_spec = pl.BlockSpec((h2, ts), lambda i: (0, i))
    return pl.pallas_call(
        body, name=name, grid=(s // ts,),
        in_specs=[pl.BlockSpec((ts, h2), lambda i: (i, 0)), pl.BlockSpec((1, h2), lambda i: (0, 0)), pl.BlockSpec((h2, 1), lambda i: (0, 0))],
        out_specs=[t_spec, t_spec], out_shape=[jax.ShapeDtypeStruct((h2, s), F32)] * 2, compiler_params=_params(1),
    )(dt_raw, bias, alog_col)


def _dtprep_bwd(name, ddt_f, ddt_b, da_f, da_b, dt_t, alog_col):
    h2, s = dt_t.shape
    h = h2 // 2
    ts = _tile(s, (512, 256, 128))

    def body(ddf, ddb, daf, dab, dt_ref, al_ref, raw_ref, db_ref, dal_ref):
        i = pl.program_id(0)
        a_col = -jnp.exp(al_ref[...])
        da = jnp.concatenate([daf[...], dab[...]], axis=0)
        dt = dt_ref[...]
        d_raw = (jnp.concatenate([ddf[...], ddb[...]], axis=0) + da * a_col) * (1.0 - jnp.exp(-dt))
        raw_ref[...] = d_raw.T

        @pl.when(i == 0)
        def _():
            db_ref[...] = jnp.zeros_like(db_ref)
            dal_ref[...] = jnp.zeros_like(dal_ref)

        db_ref[...] += jnp.sum(d_raw, axis=1, keepdims=True)
        dal_ref[...] += jnp.sum(da * dt, axis=1, keepdims=True) * a_col

    half = pl.BlockSpec((h, ts), lambda i: (0, i))
    col = pl.BlockSpec((h2, 1), lambda i: (0, 0))
    return pl.pallas_call(
        body, name=name, grid=(s // ts,),
        in_specs=[half, half, half, half, pl.BlockSpec((h2, ts), lambda i: (0, i)), col],
        out_specs=[pl.BlockSpec((ts, h2), lambda i: (i, 0)), col, col],
        out_shape=[jax.ShapeDtypeStruct((s, h2), F32), jax.ShapeDtypeStruct((h2, 1), F32), jax.ShapeDtypeStruct((h2, 1), F32)],
        compiler_params=_params(1),
    )(ddt_f, ddt_b, da_f, da_b, dt_t, alog_col)


def _dot_exact(a, b, dims, exact_lhs):
    other = b if exact_lhs else a
    out = None
    for _ in range(3):
        part = other.astype(BF16)
        other = other - _f(part)
        p = lax.dot_general(a if exact_lhs else part, part if exact_lhs else b, dims, preferred_element_type=F32)
        out = p if out is None else out + p
    return out


def _chunk_consts(a_r, dt_r, rev):
    ii = lax.broadcasted_iota(jnp.int32, (CHUNK, CHUNK), 0)
    jj = lax.broadcasted_iota(jnp.int32, (CHUNK, CHUNK), 1)
    w = (jj >= ii) if rev else (jj <= ii)
    wt = (ii >= jj) if rev else (ii <= jj)
    eye = (ii == jj).astype(F32)
    wb_ = w.astype(BF16)
    cs_col = _dot_exact(wb_, a_r, NT, exact_lhs=True)
    cs_row = _dot_exact(a_r, wb_, NT, exact_lhs=False)
    tot = jnp.sum(a_r, axis=1, keepdims=True)
    lo_half = lax.broadcasted_iota(jnp.int32, (CHUNK, 128), 1) < HEAD_DIM

    def lanes(col):
        return [jnp.broadcast_to(col[:, j:j + 1], (CHUNK, 128)) for j in range(HEADS_PER_GROUP)]

    def spread(bs):
        return jnp.concatenate([jnp.where(lo_half, bs[2 * k], bs[2 * k + 1]) for k in range(HEADS_PER_GROUP // 2)], axis=1)

    csc = lanes(cs_col)
    cs_x = spread(csc)
    hh = lax.broadcasted_iota(jnp.int32, (HEADS_PER_GROUP, GROUP_X), 0)
    ll = lax.broadcasted_iota(jnp.int32, (HEADS_PER_GROUP, GROUP_X), 1)
    expand = (lax.shift_right_logical(ll, 6) == hh).astype(BF16)
    dt_hi = dt_r.astype(BF16)
    dt_x = (lax.dot_general(dt_hi, expand, TN, preferred_element_type=F32)
            + lax.dot_general((dt_r - _f(dt_hi)).astype(BF16), expand, TN, preferred_element_type=F32))
    tot_x = cs_x[0:1, :] if rev else cs_x[CHUNK - 1:CHUNK, :]
    return w, wt, eye, csc, cs_row, tot, cs_x, dt_x, tot_x


def _hi_lo(v):
    hi = v.astype(BF16)
    return jnp.concatenate([hi, (v - _f(hi)).astype(BF16)], axis=1)


def _head_sums(v, terms=3):
    rr = lax.broadcasted_iota(jnp.int32, (GROUP_X, HEADS_PER_GROUP), 0)
    cc = lax.broadcasted_iota(jnp.int32, (GROUP_X, HEADS_PER_GROUP), 1)
    et = (lax.shift_right_logical(rr, 6) == cc).astype(BF16)
    out, rest = None, v
    for t in range(terms):
        part = rest.astype(BF16)
        if t < terms - 1:
            rest = rest - _f(part)
        p = jnp.dot(part, et, preferred_element_type=F32)
        out = p if out is None else out + p
    return out


def _chunks_per_step(nc):
    return 8 if nc % 8 == 0 else 4 if nc % 4 == 0 else 2 if nc % 2 == 0 else 1


def _ssd_fwd(name, xbc, dt_t, a_t, dsk, n_groups):
    s = xbc.shape[0]
    nc = s // CHUNK
    d_ssm = n_groups * GROUP_X
    cps = _chunks_per_step(nc)
    nblk, rows = nc // cps, cps * CHUNK

    def one(x_ref, dt_ref, a_ref, dsk_ref, y_ref, hin_ref, h_scr, rev, c):
        skip = dsk_ref is not None
        r0 = c * CHUNK
        xs = _f(x_ref[r0:r0 + CHUNK, 0:GROUP_X])
        bm = x_ref[r0:r0 + CHUNK, GROUP_X:GROUP_X + D_STATE]
        cm = x_ref[r0:r0 + CHUNK, GROUP_X + D_STATE:GROUP_W]
        w, _, _, csc, cs_row, tot, cs_x, dt_x, tot_x = _chunk_consts(a_ref[:, r0:r0 + CHUNK], dt_ref[:, r0:r0 + CHUNK], rev)
        cb = lax.dot_general(cm, bm, NT, preferred_element_type=F32)
        h = h_scr[...]
        h_b = h.astype(BF16)
        hin_ref[0, c] = h_b
        xdt = xs * dt_x
        xdt_b = xdt.astype(BF16)
        yo = lax.dot_general(cm, h_b, NT, preferred_element_type=F32) * jnp.exp(cs_x)
        st = lax.dot_general((xdt * jnp.exp(tot_x - cs_x)).astype(BF16), bm, TN, preferred_element_type=F32)
        yds = []
        for j in range(HEADS_PER_GROUP):
            lo = HEAD_DIM * j
            h_scr[lo:lo + HEAD_DIM, :] = h[lo:lo + HEAD_DIM, :] * jnp.exp(tot[j:j + 1, 0:1]) + st[lo:lo + HEAD_DIM, :]
            lm = jnp.exp(jnp.where(w, csc[j] - cs_row[j:j + 1, :], -jnp.inf))
            yds.append(jnp.dot((cb * lm).astype(BF16), xdt_b[:, lo:lo + HEAD_DIM], preferred_element_type=F32))
        y = jnp.concatenate(yds, axis=1) + yo
        if skip:
            y = y + dsk_ref[...] * xs
        y_ref[r0:r0 + CHUNK, :] = y.astype(y_ref.dtype)

    def body(xf_ref, dtf_ref, af_ref, xr_ref, dtr_ref, ar_ref, dsk_ref, yf_ref, hf_ref, yr_ref, hr_ref, hf_scr, hr_scr):
        @pl.when(pl.program_id(1) == 0)
        def _():
            hf_scr[...] = jnp.zeros_like(hf_scr)
            hr_scr[...] = jnp.zeros_like(hr_scr)

        for c in range(cps):
            one(xf_ref, dtf_ref, af_ref, dsk_ref, yf_ref, hf_ref, hf_scr, False, c)
            one(xr_ref, dtr_ref, ar_ref, None, yr_ref, hr_ref, hr_scr, True, cps - 1 - c)

    def specs(rev):
        def z(zi):
            return nblk - 1 - zi if rev else zi
        t_spec = pl.BlockSpec((HEADS_PER_GROUP, rows), lambda g, zi: ((n_groups if rev else 0) + g, z(zi)))
        ins = [pl.BlockSpec((rows, GROUP_W), lambda g, zi: (z(zi), g)), t_spec, t_spec]
        outs = [pl.BlockSpec((rows, GROUP_X), lambda g, zi: (z(zi), g)),
                pl.BlockSpec((1, cps, GROUP_X, D_STATE), lambda g, zi: (g, z(zi), 0, 0))]
        return ins, outs

    (in_f, out_f), (in_r, out_r) = specs(False), specs(True)
    shapes = [jax.ShapeDtypeStruct((s, d_ssm), BF16), jax.ShapeDtypeStruct((n_groups, nc, GROUP_X, D_STATE), BF16)]
    return pl.pallas_call(
        body, name=name, grid=(n_groups, nblk), in_specs=in_f + in_r + [pl.BlockSpec((1, GROUP_X), lambda g, zi: (0, g))],
        out_specs=out_f + out_r, out_shape=shapes + shapes,
        scratch_shapes=[pltpu.VMEM((GROUP_X, D_STATE), F32)] * 2, compiler_params=_params(2),
    )(xbc, dt_t, a_t, xbc, dt_t, a_t, dsk)


def _ssd_bwd(name, xbc, dt_t, a_t, dy, hin_f, hin_r, dsk, n_groups):
    s = xbc.shape[0]
    nc = s // CHUNK
    n_heads = n_groups * HEADS_PER_GROUP
    cps = _chunks_per_step(nc)
    nblk, rows = nc // cps, cps * CHUNK

    def one(x_ref, dt_ref, a_ref, dy_ref, hin_ref, dsk_ref, dx_ref, ddt_ref, da_ref, ddsk_ref, g_scr, rev, c):
        skip = dsk_ref is not None
        r0 = c * CHUNK
        xs = _f(x_ref[r0:r0 + CHUNK, 0:GROUP_X])
        bm = x_ref[r0:r0 + CHUNK, GROUP_X:GROUP_X + D_STATE]
        cm = x_ref[r0:r0 + CHUNK, GROUP_X + D_STATE:GROUP_W]
        dyv = _f(dy_ref[r0:r0 + CHUNK, :])
        w, wt, eye, csc, cs_row, tot, cs_x, dt_x, tot_x = _chunk_consts(a_ref[:, r0:r0 + CHUNK], dt_ref[:, r0:r0 + CHUNK], rev)
        cb = lax.dot_general(cm, bm, NT, preferred_element_type=F32)
        cbt = lax.dot_general(bm, cm, NT, preferred_element_type=F32)
        onehot = lax.broadcasted_iota(jnp.int32, (1, HEADS_PER_GROUP), 1)
        hin = hin_ref[0, c]
        g = g_scr[...]
        g_b = g.astype(BF16)
        din, dte = jnp.exp(cs_x), jnp.exp(tot_x - cs_x)
        xdt = xs * dt_x
        xdt_b, dy_b = xdt.astype(BF16), dyv.astype(BF16)
        dyo_b = (dyv * din).astype(BF16)
        yo = lax.dot_general(cm, hin, NT, preferred_element_type=F32) * din
        dxdt_s = lax.dot_general(bm, g_b, NT, preferred_element_type=F32) * dte
        d_c = jnp.dot(dyo_b, hin, preferred_element_type=F32)
        d_b = jnp.dot((xdt * dte).astype(BF16), g_b, preferred_element_type=F32)
        dhin = lax.dot_general(dyo_b, cm, TN, preferred_element_type=F32)
        dcb = jnp.zeros((CHUNK, CHUNK), F32)
        dtot = jnp.zeros((1, HEADS_PER_GROUP), F32)
        yds, dxds = [], []
        for j in range(HEADS_PER_GROUP):
            lo = HEAD_DIM * j
            cd = jnp.exp(tot[j:j + 1, 0:1])
            gj = g[lo:lo + HEAD_DIM, :]
            dcd = jnp.sum(jnp.sum(gj * _f(hin[lo:lo + HEAD_DIM, :]), axis=1, keepdims=True), axis=0, keepdims=True)
            dtot = dtot + (dcd * cd) * (onehot == j).astype(F32)
            g_scr[lo:lo + HEAD_DIM, :] = dhin[lo:lo + HEAD_DIM, :] + gj * cd
            dd = csc[j] - cs_row[j:j + 1, :]
            lm = jnp.exp(jnp.where(w, dd, -jnp.inf))
            lmt = jnp.exp(jnp.where(wt, -dd, -jnp.inf))
            xj_b, dyj_b = xdt_b[:, lo:lo + HEAD_DIM], dy_b[:, lo:lo + HEAD_DIM]
            yds.append(jnp.dot(_hi_lo(cb * lm), jnp.concatenate([xj_b, xj_b], axis=0), preferred_element_type=F32))
            dxds.append(jnp.dot(_hi_lo(cbt * lmt), jnp.concatenate([dyj_b, dyj_b], axis=0), preferred_element_type=F32))
            dcb = dcb + lax.dot_general(dyj_b, xj_b, NT, preferred_element_type=F32) * lm
        dxdt_d = jnp.concatenate(dxds, axis=1)
        z_state = xdt * dxdt_s
        yd = jnp.concatenate(yds, axis=1)
        dcs = _head_sums(dyv * yo + _f(dy_b) * yd - _f(xdt_b) * dxdt_d - z_state)
        dtot = dtot + _head_sums(jnp.broadcast_to(jnp.sum(z_state, axis=0, keepdims=True), (8, GROUP_X)))[0:1, :]
        dxdt = dxdt_s + dxdt_d
        ddt_c = _head_sums(dxdt * xs, terms=2)
        dx = dxdt * dt_x
        if skip:
            dx = dx + dsk_ref[...] * dyv
        dcb_b = dcb.astype(BF16)
        d_c = d_c + jnp.dot(dcb_b, bm, preferred_element_type=F32)
        d_b = d_b + lax.dot_general(dcb_b, cm, TN, preferred_element_type=F32)
        eye_b = eye.astype(BF16)
        da_c = _dot_exact(wt.astype(BF16), dcs, NN, exact_lhs=True) + dtot
        ddt_ref[:, r0:r0 + CHUNK] = _dot_exact(ddt_c, eye_b, TN, exact_lhs=False)
        da_ref[:, r0:r0 + CHUNK] = _dot_exact(da_c, eye_b, TN, exact_lhs=False)
        dx_ref[r0:r0 + CHUNK, :] = jnp.concatenate([dx, d_b, d_c], axis=1).astype(dx_ref.dtype)
        if skip:
            ddsk_ref[...] += jnp.sum(dyv * xs, axis=0, keepdims=True)

    def body(xf, dtf, af, dyf, hf, xr, dtr, ar, dyr, hr, dsk_ref, dxf, ddtf, daf, dxr, ddtr, dar, ddsk_ref, gf_scr, gr_scr):
        @pl.when(pl.program_id(1) == 0)
        def _():
            gf_scr[...] = jnp.zeros_like(gf_scr)
            gr_scr[...] = jnp.zeros_like(gr_scr)
            ddsk_ref[...] = jnp.zeros_like(ddsk_ref)

        for c in range(cps):
            one(xf, dtf, af, dyf, hf, dsk_ref, dxf, ddtf, daf, ddsk_ref, gf_scr, False, cps - 1 - c)
            one(xr, dtr, ar, dyr, hr, None, dxr, ddtr, dar, None, gr_scr, True, c)

    def specs(rev):
        def z(zi):
            return zi if rev else nblk - 1 - zi
        t_in = pl.BlockSpec((HEADS_PER_GROUP, rows), lambda g, zi: ((n_groups if rev else 0) + g, z(zi)))
        t_out = pl.BlockSpec((HEADS_PER_GROUP, rows), lambda g, zi: (g, z(zi)))
        x_spec = pl.BlockSpec((rows, GROUP_W), lambda g, zi: (z(zi), g))
        ins = [x_spec, t_in, t_in, pl.BlockSpec((rows, GROUP_X), lambda g, zi: (z(zi), g)),
               pl.BlockSpec((1, cps, GROUP_X, D_STATE), lambda g, zi: (g, z(zi), 0, 0))]
        return ins, [x_spec, t_out, t_out]

    (in_f, out_f), (in_r, out_r) = specs(False), specs(True)
    lane_spec = pl.BlockSpec((1, GROUP_X), lambda g, zi: (0, g))
    shapes = [jax.ShapeDtypeStruct((s, n_groups * GROUP_W), BF16), jax.ShapeDtypeStruct((n_heads, s), F32),
              jax.ShapeDtypeStruct((n_heads, s), F32)]
    return pl.pallas_call(
        body, name=name, grid=(n_groups, nblk), in_specs=in_f + in_r + [lane_spec], out_specs=out_f + out_r + [lane_spec],
        out_shape=shapes + shapes + [jax.ShapeDtypeStruct((1, n_groups * GROUP_X), F32)],
        scratch_shapes=[pltpu.VMEM((GROUP_X, D_STATE), F32)] * 2, compiler_params=_params(2),
    )(xbc, dt_t, a_t, dy, hin_f, xbc, dt_t, a_t, dy, hin_r, dsk)


def _mesh_pos():
    return lax.axis_index("x"), lax.axis_index("y"), lax.axis_index("c")


def _slot(p):
    return 4 * p[0] + 2 * p[1] + p[2]


def _exchange_sems(n):
    return [pltpu.SemaphoreType.DMA((7 * n,)), pltpu.SemaphoreType.DMA((7 * n,)), pltpu.SemaphoreType.DMA((n,))]


def _gather_ops(ins, outs, send, recv, loc):
    n = len(ins)
    x, y, c = _mesh_pos()
    me, sib = (x, y, c), (x, y, 1 - c)
    chips = [(1 - x, y), (x, 1 - y), (1 - x, 1 - y)]

    def cp(a, k, block, to, src=None):
        dst = outs[a].at[_slot(block)]
        return pltpu.make_async_remote_copy(
            src_ref=dst if src is None else src, dst_ref=dst, send_sem=send.at[7 * a + k], recv_sem=recv.at[7 * a + k],
            device_id=to, device_id_type=MESH)

    mine = [pltpu.make_async_copy(ins[a], outs[a].at[_slot(me)], loc.at[a]) for a in range(n)]
    first = []
    for a in range(n):
        first.append(cp(a, 0, me, sib, src=ins[a]))
        first += [cp(a, 1 + j, me, (*chip, c), src=ins[a]) for j, chip in enumerate(chips)]

    def start():
        for d in mine + first:
            d.start()

    def finish():
        passed = []
        for a in range(n):
            for j, chip in enumerate(chips):
                cp(a, 1 + j, (*chip, c), me).wait_recv()
                p = cp(a, 4 + j, (*chip, c), sib)
                p.start()
                passed.append(p)
        for a in range(n):
            cp(a, 0, sib, me).wait_recv()
            for j, chip in enumerate(chips):
                cp(a, 4 + j, (*chip, 1 - c), me).wait_recv()
        for f in first + passed:
            f.wait_send()
        for m in mine:
            m.wait()

    return start, finish


_gather_ops.out_shapes = lambda arrs: [jax.ShapeDtypeStruct((N_DEV,) + a.shape, a.dtype) for a in arrs]


def _a2a_ops(ins, outs, send, recv, loc):
    n = len(ins)
    me = _mesh_pos()
    flips = [(fx, fy, fc) for fx in (0, 1) for fy in (0, 1) for fc in (0, 1)][1:]
    peers = [tuple(1 - p if fl else p for p, fl in zip(me, fl3)) for fl3 in flips]

    def cp(a, k, peer, dst_slot):
        return pltpu.make_async_remote_copy(
            src_ref=ins[a].at[_slot(peer)], dst_ref=outs[a].at[dst_slot], send_sem=send.at[7 * a + k],
            recv_sem=recv.at[7 * a + k], device_id=peer, device_id_type=MESH)

    mine = [pltpu.make_async_copy(ins[a].at[_slot(me)], outs[a].at[_slot(me)], loc.at[a]) for a in range(n)]
    sends = [cp(a, k, peer, _slot(me)) for a in range(n) for k, peer in enumerate(peers)]

    def start():
        for d in mine + sends:
            d.start()

    def finish():
        for a in range(n):
            for k, peer in enumerate(peers):
                cp(a, k, peer, _slot(peer)).wait_recv()
        for s_ in sends:
            s_.wait_send()
        for m in mine:
            m.wait()

    return start, finish


_a2a_ops.out_shapes = lambda arrs: [jax.ShapeDtypeStruct(a.shape, a.dtype) for a in arrs]


def _exchange(name, ops, arrs):
    n = len(arrs)

    def body(*refs):
        start, finish = ops(refs[:n], refs[n:2 * n], *refs[2 * n:])
        start()
        finish()

    any_spec = pl.BlockSpec(memory_space=pl.ANY)
    return pl.pallas_call(
        body, name=name, in_specs=[any_spec] * n, out_specs=[any_spec] * n, out_shape=ops.out_shapes(arrs),
        scratch_shapes=_exchange_sems(n),
    )(*arrs)


def _all_gather(name, arrs):
    return _exchange(name, _gather_ops, arrs)


def _all_to_all(name, arrs):
    return _exchange(name, _a2a_ops, arrs)


def _sum8(parts_ref):
    g = _f(parts_ref[0])
    for j in range(1, N_DEV):
        g = g + _f(parts_ref[j])
    return g


def _adam_math(w, g, m, v):
    m = ADAM_B1 * m + (1.0 - ADAM_B1) * g
    v = ADAM_B2 * v + (1.0 - ADAM_B2) * (g * g)
    m_hat = m / (1.0 - ADAM_B1 ** ADAM_STEP)
    v_hat = v / (1.0 - ADAM_B2 ** ADAM_STEP)
    return -ADAM_LR * (m_hat / (jnp.sqrt(v_hat) + ADAM_EPS) + ADAM_WD * w), m, v


def _reduce_sum(name, parts):
    _, r, c = parts.shape

    def body(p_ref, o_ref):
        o_ref[...] = _sum8(p_ref)

    return pl.pallas_call(body, name=name, out_shape=jax.ShapeDtypeStruct((r, c), F32))(parts)


def _adam(name, w, g, m, v, parts=None):
    r, c = w.shape
    tr = _tile(r, (128, 64, 32, 16)) if r * c > 2 ** 18 else r
    spec = pl.BlockSpec((tr, c), lambda i: (i, 0))
    summed = parts is not None

    def body(*refs):
        if summed:
            p_ref, w_ref, m_ref, v_ref, g_out, d_out, m_out, v_out = refs
            gv = _sum8(p_ref)
            g_out[...] = gv
        else:
            g_ref, w_ref, m_ref, v_ref, d_out, m_out, v_out = refs
            gv = g_ref[...]
        d, mn, vn = _adam_math(w_ref[...], gv, m_ref[...], v_ref[...])
        d_out[...] = d
        m_out[...] = mn
        v_out[...] = vn

    first = [pl.BlockSpec((N_DEV, tr, c), lambda i: (0, i, 0))] if summed else [spec]
    n_out = 4 if summed else 3
    return pl.pallas_call(
        body, name=name, grid=(r // tr,), in_specs=first + [spec] * 3, out_specs=[spec] * n_out,
        out_shape=[jax.ShapeDtypeStruct((r, c), F32)] * n_out, compiler_params=_params(1),
    )(parts if summed else g, w, m, v)


def _adam_many(name, ws, gs, ms, vs):
    n = len(ws)

    def body(*refs):
        outs = refs[4 * n:]
        for k in range(n):
            d, mn, vn = _adam_math(refs[k][...], refs[n + k][...], refs[2 * n + k][...], refs[3 * n + k][...])
            outs[k][...] = d
            outs[n + k][...] = mn
            outs[2 * n + k][...] = vn

    res = pl.pallas_call(
        body, name=name, out_shape=[jax.ShapeDtypeStruct(w.shape, F32) for w in ws] * 3,
        compiler_params=pltpu.CompilerParams(vmem_limit_bytes=VMEM_LIMIT),
    )(*ws, *gs, *ms, *vs)
    return res[:n], res[n:2 * n], res[2 * n:]


def _slot_offsets(sizes):
    offs, o = [], 0
    for sz in sizes:
        offs.append(o)
        o += -(-sz // 128) * 128
    return offs, o + (-o % 1024)


def _pack_vectors(name, vecs):
    sizes = [v.shape[1] for v in vecs]
    offs, total = _slot_offsets(sizes)

    def body(*refs):
        out = refs[-1]
        out[...] = jnp.zeros_like(out)
        for r, o, sz in zip(refs[:-1], offs, sizes):
            out[:, o:o + sz] = r[...]

    return pl.pallas_call(body, name=name, out_shape=jax.ShapeDtypeStruct((1, total), F32),
                          compiler_params=pltpu.CompilerParams(vmem_limit_bytes=VMEM_LIMIT))(*vecs).reshape(-1, 128)


def _xbc_runs(n_groups, base=0):
    ds, nb = n_groups * GROUP_X, n_groups * D_STATE
    runs = []
    for g in range(n_groups):
        runs += [(base + GROUP_X * g, GROUP_X), (base + ds + D_STATE * g, D_STATE), (base + ds + nb + D_STATE * g, D_STATE)]
    return runs


def _interleave_runs(width, base=0):
    half = width // 2
    runs = []
    for j in range(half // COL_BLOCK):
        runs += [(base + COL_BLOCK * j, COL_BLOCK), (base + half + COL_BLOCK * j, COL_BLOCK)]
    return runs


def _w_in_runs(d, d_ssm, d_xbc, n_heads, n_groups):
    o_dt = d_ssm + d_xbc
    o_glu = o_dt + 2 * n_heads
    return ([(0, d_ssm)] + _xbc_runs(n_groups, d_ssm) + _interleave_runs(2 * d, o_glu) + _interleave_runs(2 * d, o_glu + 2 * d)
            + [(o_dt, 2 * n_heads)])


def _inverse_runs(runs):
    offs = np.concatenate([[0], np.cumsum([l for _, l in runs])])
    order = sorted(range(len(runs)), key=lambda k: runs[k][0])
    return [(int(offs[k]), runs[k][1]) for k in order]


def _permute_cols(name, srcs, runs, out_widths=None, src_sharded=False, dst_sharded=False):
    rows = srcs[0].shape[-2]
    dtype = srcs[0].dtype
    total = sum(l for _, l in runs)
    src_w = [srcs[0].shape[2]] * N_DEV if src_sharded else [a.shape[1] for a in srcs]
    dst_w = [total // N_DEV] * N_DEV if dst_sharded else (out_widths or [total])
    assert sum(dst_w) == total and sum(src_w) >= max(s_ + l for s_, l in runs)
    src_edges, dst_edges = np.cumsum([0] + src_w), np.cumsum([0] + dst_w)

    def locate(edges, col):
        k = int(np.searchsorted(edges, col, side="right")) - 1
        return k, col - int(edges[k])

    pieces, o = [], 0
    for s_, l in runs:
        while l:
            (ks, cs_), (kd, cd) = locate(src_edges, s_), locate(dst_edges, o)
            n = min(l, src_w[ks] - cs_, dst_w[kd] - cd)
            if pieces and pieces[-1][0] == ks and pieces[-1][2] == kd and pieces[-1][1] + pieces[-1][4] == cs_:
                pieces[-1] = pieces[-1][:4] + (pieces[-1][4] + n,)
            else:
                pieces.append((ks, cs_, kd, cd, n))
            s_, o, l = s_ + n, o + n, l - n
    tr = rows
    while tr % 16 == 0 and tr * max(sum(src_w), total) * dtype.itemsize > 4 * 2 ** 20:
        tr //= 2
    n_src = len(srcs)

    def body(*refs):
        ins, outs = refs[:n_src], refs[n_src:]
        for ks, cs_, kd, cd, n in pieces:
            val = ins[0][ks, :, cs_:cs_ + n] if src_sharded else ins[ks][:, cs_:cs_ + n]
            if dst_sharded:
                outs[0][kd, :, cd:cd + n] = val
            else:
                outs[kd][:, cd:cd + n] = val

    def spec(sharded, width):
        if sharded:
            return pl.BlockSpec((N_DEV, tr, width), lambda i: (0, i, 0))
        return pl.BlockSpec((tr, width), lambda i: (i, 0))

    in_specs = [spec(True, src_w[0])] if src_sharded else [spec(False, w) for w in src_w]
    if dst_sharded:
        out_specs, out_shape = [spec(True, dst_w[0])], [jax.ShapeDtypeStruct((N_DEV, rows, dst_w[0]), dtype)]
    else:
        out_specs, out_shape = [spec(False, w) for w in dst_w], [jax.ShapeDtypeStruct((rows, w), dtype) for w in dst_w]
    return pl.pallas_call(body, name=name, grid=(rows // tr,), in_specs=in_specs, out_specs=out_specs, out_shape=out_shape,
                          compiler_params=_params(1))(*srcs)


def _cols_from_shards(g):
    return g.transpose(1, 0, 2).reshape(g.shape[1], -1)


def _cols_to_shards(a):
    return a.reshape(a.shape[0], N_DEV, -1).transpose(1, 0, 2)


def _pad_rows(a, rows):
    return jnp.concatenate([a, jnp.zeros((rows - a.shape[0], a.shape[1]), a.dtype)], axis=0)


def _pack(vecs):
    a = jnp.concatenate(vecs, axis=1)
    return jnp.concatenate([a, jnp.zeros((1, -a.shape[1] % 1024), a.dtype)], axis=1).reshape(-1, 128)


def kernel(x, c, w_ada, b_ada, g_pre_mix, g_post_mix, w_in, w_conv_ssm, b_conv_ssm, dt_bias_fwd, dt_bias_bwd, a_log_fwd, a_log_bwd, d_skip, g_ssm_norm, w_ssm_out, b_glu, w_dw, b_dw, ln_g, ln_b, w_conv_out, b_conv_out, b_gate, w_mix_out, g_pre_ffn, g_post_ffn, w_gate_up, w_down, loss_target, m_w_ada, m_b_ada, m_g_pre_mix, m_g_post_mix, m_w_in, m_w_conv_ssm, m_b_conv_ssm, m_dt_bias_fwd, m_dt_bias_bwd, m_a_log_fwd, m_a_log_bwd, m_d_skip, m_g_ssm_norm, m_w_ssm_out, m_b_glu, m_w_dw, m_b_dw, m_ln_g, m_ln_b, m_w_conv_out, m_b_conv_out, m_b_gate, m_w_mix_out, m_g_pre_ffn, m_g_post_ffn, m_w_gate_up, m_w_down, v_w_ada, v_b_ada, v_g_pre_mix, v_g_post_mix, v_w_in, v_w_conv_ssm, v_b_conv_ssm, v_dt_bias_fwd, v_dt_bias_bwd, v_a_log_fwd, v_a_log_bwd, v_d_skip, v_g_ssm_norm, v_w_ssm_out, v_b_glu, v_w_dw, v_b_dw, v_ln_g, v_ln_b, v_w_conv_out, v_b_conv_out, v_b_gate, v_w_mix_out, v_g_pre_ffn, v_g_post_ffn, v_w_gate_up, v_w_down):
    weights = dict(w_ada=w_ada, b_ada=b_ada, g_pre_mix=g_pre_mix, g_post_mix=g_post_mix, w_in=w_in, w_conv_ssm=w_conv_ssm, b_conv_ssm=b_conv_ssm, dt_bias_fwd=dt_bias_fwd, dt_bias_bwd=dt_bias_bwd, a_log_fwd=a_log_fwd, a_log_bwd=a_log_bwd, d_skip=d_skip, g_ssm_norm=g_ssm_norm, w_ssm_out=w_ssm_out, b_glu=b_glu, w_dw=w_dw, b_dw=b_dw, ln_g=ln_g, ln_b=ln_b, w_conv_out=w_conv_out, b_conv_out=b_conv_out, b_gate=b_gate, w_mix_out=w_mix_out, g_pre_ffn=g_pre_ffn, g_post_ffn=g_post_ffn, w_gate_up=w_gate_up, w_down=w_down)
    mom_m = dict(w_ada=m_w_ada, b_ada=m_b_ada, g_pre_mix=m_g_pre_mix, g_post_mix=m_g_post_mix, w_in=m_w_in, w_conv_ssm=m_w_conv_ssm, b_conv_ssm=m_b_conv_ssm, dt_bias_fwd=m_dt_bias_fwd, dt_bias_bwd=m_dt_bias_bwd, a_log_fwd=m_a_log_fwd, a_log_bwd=m_a_log_bwd, d_skip=m_d_skip, g_ssm_norm=m_g_ssm_norm, w_ssm_out=m_w_ssm_out, b_glu=m_b_glu, w_dw=m_w_dw, b_dw=m_b_dw, ln_g=m_ln_g, ln_b=m_ln_b, w_conv_out=m_w_conv_out, b_conv_out=m_b_conv_out, b_gate=m_b_gate, w_mix_out=m_w_mix_out, g_pre_ffn=m_g_pre_ffn, g_post_ffn=m_g_post_ffn, w_gate_up=m_w_gate_up, w_down=m_w_down)
    mom_v = dict(w_ada=v_w_ada, b_ada=v_b_ada, g_pre_mix=v_g_pre_mix, g_post_mix=v_g_post_mix, w_in=v_w_in, w_conv_ssm=v_w_conv_ssm, b_conv_ssm=v_b_conv_ssm, dt_bias_fwd=v_dt_bias_fwd, dt_bias_bwd=v_dt_bias_bwd, a_log_fwd=v_a_log_fwd, a_log_bwd=v_a_log_bwd, d_skip=v_d_skip, g_ssm_norm=v_g_ssm_norm, w_ssm_out=v_w_ssm_out, b_glu=v_b_glu, w_dw=v_w_dw, b_dw=v_b_dw, ln_g=v_ln_g, ln_b=v_ln_b, w_conv_out=v_w_conv_out, b_conv_out=v_b_conv_out, b_gate=v_b_gate, w_mix_out=v_w_mix_out, g_pre_ffn=v_g_pre_ffn, g_post_ffn=v_g_post_ffn, w_gate_up=v_w_gate_up, w_down=v_w_down)
    names = list(weights)

    s, d = x.shape[1], x.shape[2]
    d_ssm, n_heads, d_xbc = g_ssm_norm.shape[-1], d_skip.shape[-1], b_conv_ssm.shape[-1]
    n_groups = n_heads // HEADS_PER_GROUP
    d_ff = w_down.shape[1] * N_DEV
    k_ssm, k_dw = w_conv_ssm.shape[1], w_dw.shape[1]
    assert d_ssm == n_groups * GROUP_X and d_xbc == n_groups * GROUP_W and s % CHUNK == 0
    assert d % COL_BLOCK == 0 and d_ff % COL_BLOCK == 0
    mc = w_ada.shape[-1]
    me = _slot(_mesh_pos())
    wb = d_ssm + d_xbc + 4 * d
    p_x, p_glu, p_gate = d_ssm, d_ssm + d_xbc, d_ssm + d_xbc + 2 * d
    ncb, nfb = d // COL_BLOCK, d_ff // COL_BLOCK
    x2, tgt = x[0], loss_target[0]

    big = ["w_in", "w_ssm_out", "w_conv_out", "w_mix_out", "w_gate_up", "w_down"]
    (w_in_g,) = _all_gather("gather_w_in", [w_in[0].astype(BF16)])
    w_in_runs, xbc_runs, il_runs = _w_in_runs(d, d_ssm, d_xbc, n_heads, n_groups), _xbc_runs(n_groups), _interleave_runs(2 * d)
    w_big, w_dt = _permute_cols("order_w_in", [w_in_g], w_in_runs, out_widths=[wb, 2 * n_heads], src_sharded=True)

    n_cs, n_dw = k_ssm * (d_xbc // N_DEV), k_dw * (d // N_DEV)
    (small_all,) = _all_gather("gather_small", [_pack([c, w_conv_ssm.reshape(1, n_cs), w_dw.reshape(1, n_dw)])])
    small_all = small_all.reshape(N_DEV, -1)
    c_all = small_all[:, :d]
    wcs = _cols_from_shards(small_all[:, d:d + n_cs].reshape(N_DEV, k_ssm, -1))
    wdw = _cols_from_shards(small_all[:, d + n_cs:d + n_cs + n_dw].reshape(N_DEV, k_dw, -1))
    (wcs_p,) = _permute_cols("order_conv_taps", [wcs], xbc_runs)
    (bcs_p,) = _permute_cols("order_conv_bias", [b_conv_ssm], xbc_runs)
    (b_glu_il,), (b_gate_il,) = _permute_cols("order_b_glu", [b_glu], il_runs), _permute_cols("order_b_gate", [b_gate], il_runs)
    kp_ssm, kp_dw = -(-k_ssm // 8) * 8, -(-k_dw // 8) * 8
    zero_x, zero_d = jnp.zeros((1, d_xbc), F32), jnp.zeros((1, d), F32)

    (c_act,), _ = _rowcall("c_act", _fwd_fn(_silu), [(c_all, 0, d)], [], [(d, F32)], [])
    b_my = lax.dynamic_slice(b_ada, (0, me * mc), (1, mc))
    mod_cols = _matmul("mod", c_act, w_ada[0], "nn", F32, acc_in=jnp.broadcast_to(b_my, (N_DEV, mc)))
    (mod_rows,) = _all_to_all("mod_to_owner", [mod_cols.reshape(N_DEV, 1, mc)])
    mod = mod_rows.reshape(1, N_DEV * mc)
    sh1, sc1, g1, sh2, sc2, g2 = [mod[:, k * d:(k + 1) * d] for k in range(6)]

    (h1,), _ = _rowcall("prenorm_mix", _fwd_fn(_t_prenorm), [(x2, 0, d)], [g_pre_mix, sc1, sh1], [(d, BF16)], [])
    proj, rest = _matmul("in_proj", h1, w_big, "nn", BF16, carry=(_gather_ops, [weights[k][0].astype(BF16) for k in big[1:]]))
    gath = dict(zip(big[1:], rest))
    w_ssm, w_co, w_mo = gath["w_ssm_out"].reshape(d_ssm, d), gath["w_conv_out"].reshape(d, d), gath["w_mix_out"].reshape(d, d)
    gu_runs = _interleave_runs(2 * d_ff)
    (w_gu,) = _permute_cols("order_w_gate_up", [gath["w_gate_up"]], gu_runs, src_sharded=True)
    w_dn = gath["w_down"].reshape(d_ff, d)
    dt_raw = _matmul("in_proj_dt", h1, w_dt, "nn", F32)
    xbc_c = _dwconv("conv_ssm", proj, p_x, d_xbc, _pad_rows(wcs_p, kp_ssm), k_ssm, bcs_p, True, BF16)
    alog_col = jnp.concatenate([a_log_fwd, a_log_bwd], axis=1).reshape(2 * n_heads, 1)
    dt_t, a_t = _dtprep("dt_prep", dt_raw, jnp.concatenate([dt_bias_fwd, dt_bias_bwd], axis=1), alog_col)
    dsk = jnp.repeat(d_skip, HEAD_DIM, axis=1)
    y_f, hin_f, y_b, hin_b = _ssd_fwd("ssd", xbc_c, dt_t, a_t, dsk, n_groups)
    gn_rows = [(y_f, 0, d_ssm), (y_b, 0, d_ssm), (proj, 0, d_ssm)]
    (yn,), _ = _rowcall("ssm_norm", _fwd_fn(_t_gnorm), gn_rows, [g_ssm_norm], [(d_ssm, BF16)], [], ncol=n_groups)
    ya = _matmul("ssm_out", yn, w_ssm, "nn", BF16)
    (u0,), _ = _rowcall("glu", _glu_fwd, [(proj, p_glu, 2 * d)], [b_glu_il], [(d, BF16)], [], ncol=ncb)
    wdw_pad = _pad_rows(wdw, kp_dw)
    u1 = _dwconv("conv_dw", u0, 0, d, wdw_pad, k_dw, b_dw, False, BF16)
    (u2,), _ = _rowcall("ln_silu", _fwd_fn(_t_lnsilu), [(u1, 0, d)], [ln_g, ln_b], [(d, BF16)], [])
    yb = _matmul("conv_out", u2, w_co, "nn", BF16)
    gm_rows, gm_prm = [(ya, 0, d), (yb, 0, d), (proj, p_gate, 2 * d)], [b_conv_out, b_gate_il]
    (m_in,), _ = _rowcall("gate_mix", _gatemix_fwd, gm_rows, gm_prm, [(d, BF16)], [], ncol=ncb)
    mix = _matmul("mix_out", m_in, w_mo, "nn", BF16)
    pp_prm = [g_post_mix, g1, g_pre_ffn, sc2, sh2]
    (x1, h2), _ = _rowcall("post_mix", _fwd_fn(_t_postpre), [(x2, 0, d), (mix, 0, d)], pp_prm, [(d, F32), (d, BF16)], [])
    gu, act = _matmul("gate_up", h2, w_gu, "nn", BF16, tn=2 * COL_BLOCK,
                      epilogue=(lambda p: (p, _swiglu_fwd(p)[0][0]), [], [(2 * d_ff, BF16), (d_ff, BF16)]))
    f = _matmul("down", act, w_dn, "nn", BF16)

    def loss_fn(x1_t, f_t, tgt_t, gpost, g2_t):
        args = [_f(x1_t), _f(f_t), _f(tgt_t), _f(gpost), _f(g2_t)]
        val, vjp = jax.vjp(_t_loss, *args)
        gr = vjp(jnp.ones((), F32))
        return (gr[0], gr[1]), (jnp.zeros((1, 128), F32) + val, gr[3], gr[4])

    (dx1a, df), (loss_v, d_gpost2, d_g2) = _rowcall(
        "loss", loss_fn, [(x1, 0, d), (f, 0, d), (tgt, 0, d)], [g_post_ffn, g2], [(d, F32), (d, BF16)], [128, d, d])
    loss = lax.psum(loss_v[0, 0], ("x", "y", "c"))

    dgu = _matmul("down_dx", df, w_dn, "nt", BF16, tn=COL_BLOCK,
                  epilogue=(lambda p, gu_t: _swiglu_bwd(gu_t, p)[0], [gu], [(2 * d_ff, BF16)]))
    dw_dn = _matmul("down_dw", act, df, "tn", BF16)
    recv = {}
    dh2, (recv["w_down"],) = _matmul("gate_up_dx", dgu, w_gu, "nt", BF16, carry=(_a2a_ops, [dw_dn.reshape(N_DEV, -1, d)]))
    dw_gu = _matmul("gate_up_dw", h2, dgu, "tn", BF16)
    (dxa, dmix), (d_gpost1, d_g1, d_gpre2, d_sc2, d_sh2) = _rowcall(
        "post_mix_bwd", _vjp_fn(_t_postpre, 2, 2, (0, 1), (0, 1, 2, 3, 4)),
        [(x2, 0, d), (mix, 0, d), (dx1a, 0, d), (dh2, 0, d)], pp_prm, [(d, F32), (d, BF16)], [d] * 5)
    dm_in = _matmul("mix_out_dx", dmix, w_mo, "nt", BF16)
    dw_mo = _matmul("mix_out_dw", m_in, dmix, "tn", BF16)
    dproj = lax.empty((s, wb), BF16)
    (dya, dyb, dproj), (d_bco, d_bgate) = _rowcall(
        "gate_mix_bwd", _gatemix_bwd, gm_rows + [(dm_in, 0, d)], gm_prm,
        [(d, BF16), (d, BF16), (2 * d, BF16, dproj, p_gate)], [d, 2 * d], ncol=ncb)
    du2 = _matmul("conv_out_dx", dyb, w_co, "nt", BF16)
    dw_co = _matmul("conv_out_dw", u2, dyb, "tn", BF16)
    (du1,), (d_lng, d_lnb) = _rowcall("ln_silu_bwd", _vjp_fn(_t_lnsilu, 1, 1, (0,), (0, 1)), [(u1, 0, d), (du2, 0, d)],
                                      [ln_g, ln_b], [(d, BF16)], [d, d])
    dwdw_p, d_bdw = _dwconv_bwd_w("conv_dw_dw", u0, 0, d, wdw_pad, k_dw, b_dw, [du1], False)
    du0 = _dwconv("conv_dw_dx", du1, 0, d, _pad_rows(wdw[::-1], kp_dw), k_dw, zero_d, False, BF16)
    (dproj,), (d_bglu,) = _rowcall("glu_bwd", _glu_bwd, [(proj, p_glu, 2 * d), (du0, 0, d)], [b_glu_il],
                                   [(2 * d, BF16, dproj, p_glu)], [2 * d], ncol=ncb)
    dyn, (recv["w_mix_out"], recv["w_conv_out"]) = _matmul(
        "ssm_out_dx", dya, w_ssm, "nt", BF16, carry=(_a2a_ops, [dw_mo.reshape(N_DEV, -1, d), dw_co.reshape(N_DEV, -1, d)]))
    dw_ssm = _matmul("ssm_out_dw", yn, dya, "tn", BF16)
    (dy_ssd, dproj), (d_gn,) = _rowcall("ssm_norm_bwd", _vjp_fn(_t_gnorm, 3, 1, (0, 2), (0,)), gn_rows + [(dyn, 0, d_ssm)],
                                        [g_ssm_norm], [(d_ssm, BF16), (d_ssm, BF16, dproj, 0)], [d_ssm], ncol=n_groups)
    dxbc_f, ddt_f, da_f, dxbc_b, ddt_b, da_b, ddsk = _ssd_bwd("ssd_bwd", xbc_c, dt_t, a_t, dy_ssd, hin_f, hin_b, dsk, n_groups)
    ddt_raw, d_dtb, d_alog = _dtprep_bwd("dt_prep_bwd", ddt_f, ddt_b, da_f, da_b, dt_t, alog_col)
    dpre, dwcs_p, dbcs_p = _dwconv_bwd_w("conv_ssm_dw", proj, p_x, d_xbc, _pad_rows(wcs_p, kp_ssm), k_ssm, bcs_p,
                                          [dxbc_f, dxbc_b], True)
    dproj = _dwconv("conv_ssm_dx", dpre, 0, d_xbc, _pad_rows(wcs_p[::-1], kp_ssm), k_ssm, zero_x, False, BF16, into=(dproj, p_x))
    dw_big, (recv["w_gate_up"], recv["w_ssm_out"]) = _matmul(
        "in_proj_dw", h1, dproj, "tn", BF16,
        carry=(_a2a_ops, [_permute_cols("restore_dw_gate_up", [dw_gu], _inverse_runs(gu_runs), dst_sharded=True)[0],
                          dw_ssm.reshape(N_DEV, -1, d)]))
    dw_dt = _matmul("in_proj_dt_dw", h1, ddt_raw, "tn", BF16)
    (dw_in,) = _permute_cols("restore_dw_in", [dw_big, dw_dt], _inverse_runs(w_in_runs), dst_sharded=True)
    dh1, (recv["w_in"],) = _matmul("in_proj_dx", dproj, w_big, "nt", F32, carry=(_a2a_ops, [dw_in]))
    dh1 = _matmul("in_proj_dt_dx", ddt_raw, w_dt, "nt", BF16, acc_in=dh1)

    def prenorm_bwd_fn(x_t, dh_t, dxa_t, g, sc, sh):
        rows, prm = _vjp_fn(_t_prenorm, 1, 1, (0,), (0, 1, 2))(x_t, dh_t, g, sc, sh)
        return (rows[0] + _f(dxa_t),), prm

    (grad_x,), (d_gpre1, d_sc1, d_sh1) = _rowcall(
        "prenorm_mix_bwd", prenorm_bwd_fn, [(x2, 0, d), (dh1, 0, d), (dxa, 0, d)], [g_pre_mix, sc1, sh1], [(d, F32)], [d] * 3)

    dmod = jnp.concatenate([d_sh1, d_sc1, d_g1, d_sh2, d_sc2, d_g2], axis=1)
    small_g = {
        "b_ada": dmod, "g_pre_mix": d_gpre1, "g_post_mix": d_gpost1,
        "b_conv_ssm": _permute_cols("restore_db_conv", [dbcs_p], _inverse_runs(xbc_runs))[0],
        "dt_bias_fwd": d_dtb[:n_heads].reshape(1, n_heads), "dt_bias_bwd": d_dtb[n_heads:].reshape(1, n_heads),
        "a_log_fwd": d_alog[:n_heads].reshape(1, n_heads), "a_log_bwd": d_alog[n_heads:].reshape(1, n_heads),
        "d_skip": ddsk.reshape(n_heads, HEAD_DIM).sum(axis=1).reshape(1, n_heads), "g_ssm_norm": d_gn,
        "b_glu": _permute_cols("restore_db_glu", [d_bglu], _inverse_runs(il_runs))[0], "b_dw": d_bdw, "ln_g": d_lng, "ln_b": d_lnb,
        "b_conv_out": d_bco, "b_gate": _permute_cols("restore_db_gate", [d_bgate], _inverse_runs(il_runs))[0],
        "g_pre_ffn": d_gpre2, "g_post_ffn": d_gpost2,
        "w_conv_ssm": _permute_cols("restore_dw_conv", [dwcs_p], _inverse_runs(xbc_runs))[0][:k_ssm].reshape(1, k_ssm * d_xbc),
        "w_dw": dwdw_p[:k_dw].reshape(1, k_dw * d),
    }
    small_names = list(small_g)
    sizes = [small_g[k].shape[1] for k in small_names]
    offs, _ = _slot_offsets(sizes)
    (small_parts,) = _all_gather("gather_small_grads", [_pack_vectors("pack_small_grads", [small_g[k] for k in small_names])])
    dmod_all = small_parts.reshape(N_DEV, -1)[:, :6 * d]
    small_tot = _reduce_sum("sum_small_grads", small_parts).reshape(1, -1)
    grads = {k: small_tot[:, offs[i]:offs[i] + sizes[i]] for i, k in enumerate(small_names)}
    grads["w_conv_ssm"] = lax.dynamic_slice(grads["w_conv_ssm"].reshape(k_ssm, d_xbc), (0, me * (d_xbc // N_DEV)), (k_ssm, d_xbc // N_DEV))
    grads["w_dw"] = lax.dynamic_slice(grads["w_dw"].reshape(k_dw, d), (0, me * (d // N_DEV)), (k_dw, d // N_DEV))
    grads["w_ada"] = _matmul("mod_dw", c_act, lax.dynamic_slice(dmod_all, (0, me * mc), (N_DEV, mc)), "tn", F32, highest=True)

    delta, new_m, new_v = {}, {}, {}
    for k in big:
        grads[k], delta[k], new_m[k], new_v[k] = _adam("adam_" + k, weights[k][0], None, mom_m[k][0], mom_v[k][0], parts=recv[k])
    for k in ["w_ada", "w_conv_ssm", "w_dw"]:
        delta[k], new_m[k], new_v[k] = _adam("adam_" + k, weights[k][0], grads[k], mom_m[k][0], mom_v[k][0])
    rep = [k for k in names if k not in big and k not in ("w_ada", "w_conv_ssm", "w_dw")]
    r_delta, r_m, r_v = _adam_many("adam_small", *[[t[k] for k in rep] for t in (weights, grads, mom_m, mom_v)])
    for i, k in enumerate(rep):
        delta[k], new_m[k], new_v[k] = r_delta[i], r_m[i], r_v[i]

    def shaped(t, k):
        return t[k].reshape(weights[k].shape)

    return (loss, grad_x.reshape(x.shape), *[shaped(grads, k) for k in names], *[shaped(delta, k) for k in names],
            *[shaped(new_m, k) for k in names], *[shaped(new_v, k) for k in names])
```

```python
import functools

import numpy as np
import jax
import jax.numpy as jnp
from jax import lax
from jax.experimental import pallas as pl
from jax.experimental.pallas import tpu as pltpu

F32 = jnp.float32
BF16 = jnp.bfloat16
HIGHEST = lax.Precision.HIGHEST
MESH = pl.DeviceIdType.MESH
N_DEV = 8
EPS = 1e-6
CHUNK = 128
HEAD_DIM = 64
D_STATE = 128
HEADS_PER_GROUP = 8
GROUP_X = HEADS_PER_GROUP * HEAD_DIM
GROUP_W = GROUP_X + 2 * D_STATE
COL_BLOCK = 512
HALO = 16
VMEM_LIMIT = 56 * 1024 * 1024
ADAM_LR, ADAM_B1, ADAM_B2, ADAM_EPS, ADAM_WD, ADAM_STEP = 0.001, 0.9, 0.999, 1e-08, 0.01, 10

NN = (((1,), (0,)), ((), ()))
NT = (((1,), (1,)), ((), ()))
TN = (((0,), (0,)), ((), ()))


def _tile(dim, prefs):
    for p in prefs:
        if p <= dim and dim % p == 0:
            return p
    return dim


def _params(n_grid):
    return pltpu.CompilerParams(dimension_semantics=("arbitrary",) * n_grid, vmem_limit_bytes=VMEM_LIMIT)


def _f(v):
    return v.astype(F32)


def _matmul(name, a, b, mode, out_dtype, acc_in=None, highest=False, carry=None, tn=None, epilogue=None):
    if mode == "nn":
        (m, k), (k2, n) = a.shape, b.shape
    elif mode == "nt":
        (m, k), (n, k2) = a.shape, b.shape
    else:
        (k, m), (k2, n) = a.shape, b.shape
    assert k == k2, (name, a.shape, b.shape, mode)
    tm = _tile(m, (1024, 512, 256, 128))
    tn = _tile(n, (1024, 512, 256, 128)) if tn is None else tn
    tk = _tile(k, (2816, 2048, 1024, 512, 256, 128))
    nk, nj = k // tk, n // tn
    assert n % tn == 0
    dims = {"nn": NN, "nt": NT, "tn": TN}[mode]
    a_spec = pl.BlockSpec((tk, tm), lambda i, j, kk: (kk, i)) if mode == "tn" else pl.BlockSpec((tm, tk), lambda i, j, kk: (i, kk))
    b_spec = pl.BlockSpec((tn, tk), lambda i, j, kk: (j, kk)) if mode == "nt" else pl.BlockSpec((tk, tn), lambda i, j, kk: (kk, j))
    o_spec = pl.BlockSpec((tm, tn), lambda i, j, kk: (i, j))
    has_acc = acc_in is not None
    epi_fn, e_arrs, e_outs = epilogue if epilogue is not None else (None, [], [])
    ne = len(e_arrs)
    n_in = (3 if has_acc else 2) + ne
    if epilogue is None:
        o_specs, o_shapes = [o_spec], [jax.ShapeDtypeStruct((m, n), out_dtype)]
    else:
        o_specs = [pl.BlockSpec((tm, w // nj), lambda i, j, kk: (i, j)) for w, _ in e_outs]
        o_shapes = [jax.ShapeDtypeStruct((m, w), dt) for w, dt in e_outs]
    e_specs = [pl.BlockSpec((tm, arr.shape[1] // nj), lambda i, j, kk: (i, j)) for arr in e_arrs]
    n_out = len(o_specs)
    ops, c_arrs = carry if carry is not None else (None, [])
    nc_ = len(c_arrs)
    grid = (m // tm, nj, nk)

    def body(*refs):
        a_ref, b_ref = refs[0], refs[1]
        c_ref = refs[2] if has_acc else None
        e_refs = refs[n_in - ne:n_in]
        c_ins = refs[n_in:n_in + nc_]
        o_refs = refs[n_in + nc_:n_in + nc_ + n_out]
        c_outs = refs[n_in + nc_ + n_out:n_in + 2 * nc_ + n_out]
        acc = refs[n_in + 2 * nc_ + n_out]
        i, j, kk = pl.program_id(0), pl.program_id(1), pl.program_id(2)
        if nc_:
            start, finish = ops(c_ins, c_outs, *refs[n_in + 2 * nc_ + n_out + 1:])
            pl.when((i == 0) & (j == 0) & (kk == 0))(start)

        def write(val):
            if epi_fn is None:
                o_refs[0][...] = val.astype(out_dtype)
            else:
                for ref, v in zip(o_refs, epi_fn(val, *[r[...] for r in e_refs])):
                    ref[...] = v.astype(ref.dtype)

        def product(first):
            if highest:
                p = lax.dot_general(_f(a_ref[...]), _f(b_ref[...]), dims, precision=HIGHEST, preferred_element_type=F32)
            else:
                p = lax.dot_general(a_ref[...].astype(BF16), b_ref[...].astype(BF16), dims, preferred_element_type=F32)
            return p + _f(c_ref[...]) if (first and has_acc) else p

        if nk == 1:
            write(product(True))
        else:
            @pl.when(kk == 0)
            def _():
                acc[...] = product(True)

            if nk > 2:
                @pl.when((kk > 0) & (kk < nk - 1))
                def _():
                    acc[...] += product(False)

            @pl.when(kk == nk - 1)
            def _():
                write(acc[...] + product(False))

        if nc_:
            pl.when((i == grid[0] - 1) & (j == grid[1] - 1) & (kk == nk - 1))(finish)

    any_spec = pl.BlockSpec(memory_space=pl.ANY)
    c_shapes = ops.out_shapes(c_arrs) if nc_ else []
    res = pl.pallas_call(
        body, name=name, grid=grid,
        in_specs=[a_spec, b_spec] + ([o_spec] if has_acc else []) + e_specs + [any_spec] * nc_,
        out_specs=o_specs + [any_spec] * nc_, out_shape=o_shapes + c_shapes,
        scratch_shapes=[pltpu.VMEM((tm, tn) if nk > 1 else (8, 128), F32)] + (_exchange_sems(nc_) if nc_ else []),
        compiler_params=_params(3),
    )(*([a, b] + ([acc_in] if has_acc else []) + list(e_arrs) + list(c_arrs)))
    outs = res[0] if n_out == 1 else res[:n_out]
    return (outs, res[n_out:]) if nc_ else outs


def _rowcall(name, fn, rows, params, out_rows, out_accs, ncol=1):
    s = rows[0][0].shape[0]
    tm = _tile(s, (256 if ncol == 1 else 1024, 128, 64, 32, 16, 8))
    args, in_specs = [], []
    for arr, off, w in rows:
        bw = w // ncol
        assert w % ncol == 0 and off % bw == 0, (name, off, w, ncol)
        in_specs.append(pl.BlockSpec((tm, bw), functools.partial(lambda j, i, ob: (i, ob + j), ob=off // bw)))
        args.append(arr)
    for p in params:
        assert p.shape[1] % ncol == 0, (name, p.shape)
        in_specs.append(pl.BlockSpec((p.shape[0], p.shape[1] // ncol), lambda j, i: (0, j)))
        args.append(p)
    n_in, n_ro = len(args), len(out_rows)
    out_shape, out_specs, aliases = [], [], {}
    for k, spec in enumerate(out_rows):
        w, dt = spec[0], spec[1]
        bw = w // ncol
        if len(spec) == 4:
            buf, off = spec[2], spec[3]
            assert off % bw == 0 and buf.dtype == dt, (name, off, bw)
            out_shape.append(jax.ShapeDtypeStruct(buf.shape, dt))
            out_specs.append(pl.BlockSpec((tm, bw), functools.partial(lambda j, i, ob: (i, ob + j), ob=off // bw)))
            aliases[len(args)] = k
            in_specs.append(pl.BlockSpec(memory_space=pl.ANY))
            args.append(buf)
        else:
            out_shape.append(jax.ShapeDtypeStruct((s, w), dt))
            out_specs.append(pl.BlockSpec((tm, bw), lambda j, i: (i, j)))
    for w in out_accs:
        out_shape.append(jax.ShapeDtypeStruct((1, w), F32))
        out_specs.append(pl.BlockSpec((1, w // ncol), lambda j, i: (0, j)))
    n_args = len(args)

    def body(*refs):
        i = pl.program_id(1)
        ro, ao = fn(*[r[...] for r in refs[:n_in]])
        outs = refs[n_args:]
        assert len(ro) == n_ro and len(ao) == len(out_accs), name
        for ref, v in zip(outs[:n_ro], ro):
            ref[...] = v.astype(ref.dtype)
        for ref, v in zip(outs[n_ro:], ao):
            @pl.when(i == 0)
            def _(ref=ref, v=v):
                ref[...] = v

            @pl.when(i > 0)
            def _(ref=ref, v=v):
                ref[...] += v

    res = pl.pallas_call(
        body, name=name, grid=(ncol, s // tm), in_specs=in_specs, out_specs=out_specs,
        out_shape=out_shape, input_output_aliases=aliases, compiler_params=_params(2),
    )(*args)
    return res[:n_ro], res[n_ro:]


def _fwd_fn(f):
    def fn(*vals):
        out = f(*[_f(v) for v in vals])
        return (out if isinstance(out, tuple) else (out,)), ()
    return fn


def _vjp_fn(f, n_prim, n_ct, want_rows, want_params):
    def fn(*vals):
        prims = [_f(v) for v in vals[:n_prim]]
        cts = [_f(v) for v in vals[n_prim:n_prim + n_ct]]
        prms = [_f(v) for v in vals[n_prim + n_ct:]]
        out, vjp = jax.vjp(f, *prims, *prms)
        g = vjp(tuple(cts) if isinstance(out, tuple) else cts[0])
        return tuple(g[k] for k in want_rows), tuple(g[n_prim + k] for k in want_params)
    return fn


def _rms(v):
    return v * lax.rsqrt(jnp.mean(v * v, axis=-1, keepdims=True) + EPS)


def _silu(v):
    return v * jax.nn.sigmoid(v)


def _t_prenorm(x, g, sc, sh):
    return _rms(x) * g * (1.0 + sc) + sh


def _t_gnorm(yf, yb, z, gn):
    return _rms((yf + yb) * _silu(z)) * gn


def _t_glu(a, b, ba, bb):
    return (a + ba) * jax.nn.sigmoid(b + bb)


def _t_lnsilu(u, g, b):
    uc = u - jnp.mean(u, axis=-1, keepdims=True)
    return _silu(uc * lax.rsqrt(jnp.mean(uc * uc, axis=-1, keepdims=True) + EPS) * g + b)


def _t_gatemix(ya, yb, la, lb, bco, bga, bgb):
    return jax.nn.sigmoid(la + bga) * ya + jax.nn.sigmoid(lb + bgb) * (yb + bco)


def _t_postpre(x, mix, gpost, g1, gpre, sc, sh):
    x1 = x + g1 * (_rms(mix) * gpost)
    return x1, _rms(x1) * gpre * (1.0 + sc) + sh


def _t_swiglu(gt, up):
    return _silu(gt) * up


def _halves(v):
    return _f(v[:, :COL_BLOCK]), _f(v[:, COL_BLOCK:])


def _glu_fwd(ab, bias):
    return (_t_glu(*_halves(ab), *_halves(bias)),), ()


def _glu_bwd(ab, du, bias):
    _, vjp = jax.vjp(_t_glu, *_halves(ab), *_halves(bias))
    da, db, dba, dbb = vjp(_f(du))
    return (jnp.concatenate([da, db], axis=1),), (jnp.concatenate([dba, dbb], axis=1),)


def _gatemix_fwd(ya, yb, logits, bco, bg):
    return (_t_gatemix(_f(ya), _f(yb), *_halves(logits), _f(bco), *_halves(bg)),), ()


def _gatemix_bwd(ya, yb, logits, dm, bco, bg):
    _, vjp = jax.vjp(_t_gatemix, _f(ya), _f(yb), *_halves(logits), _f(bco), *_halves(bg))
    dya, dyb, dla, dlb, dbco, dbga, dbgb = vjp(_f(dm))
    return (dya, dyb, jnp.concatenate([dla, dlb], axis=1)), (dbco, jnp.concatenate([dbga, dbgb], axis=1))


def _swiglu_fwd(gu):
    return (_t_swiglu(*_halves(gu)),), ()


def _swiglu_bwd(gu, dact):
    _, vjp = jax.vjp(_t_swiglu, *_halves(gu))
    return (jnp.concatenate(vjp(_f(dact)), axis=1),), ()


def _t_loss(x1, f, tgt, gpost, g2):
    e = x1 + g2 * (_rms(f) * gpost) - tgt
    return 0.5 * jnp.sum(jnp.mean(e * e, axis=-1))


def _conv_specs(s, tm, cw, ob, w_rows):
    nh, last_h = tm // HALO, s // HALO - 1
    return [
        pl.BlockSpec((tm, cw), lambda j, i: (i, ob + j)),
        pl.BlockSpec((HALO, cw), lambda j, i: (jnp.maximum(i * nh - 1, 0), ob + j)),
        pl.BlockSpec((HALO, cw), lambda j, i: (jnp.minimum((i + 1) * nh, last_h), ob + j)),
        pl.BlockSpec((w_rows, cw), lambda j, i: (0, j)),
        pl.BlockSpec((1, cw), lambda j, i: (0, j)),
    ]


def _window(x_ref, p_ref, n_ref, i, nrow):
    prev = _f(p_ref[...]) * jnp.where(i > 0, 1.0, 0.0).astype(F32)
    nxt = _f(n_ref[...]) * jnp.where(i < nrow - 1, 1.0, 0.0).astype(F32)
    return jnp.concatenate([prev, _f(x_ref[...]), nxt], axis=0)


def _shifted(win, k_taps, tm):
    pad, moved, out = (k_taps - 1) // 2, {}, []
    for k in range(k_taps):
        q, r = divmod(HALO + k - pad, 8)
        if r not in moved:
            moved[r] = pltpu.roll(win, win.shape[0] - r, 0) if r else win
        out.append(moved[r][8 * q:8 * q + tm, :])
    return out


def _taps(win, w_ref, k_taps, tm):
    sh = _shifted(win, k_taps, tm)
    acc = w_ref[0:1, :] * sh[0]
    for k in range(1, k_taps):
        acc = acc + w_ref[k:k + 1, :] * sh[k]
    return acc


def _conv_rows(k_taps):
    return 1024 if k_taps <= 8 else 512


def _dwconv(name, x, xoff, c, w_pad, k_taps, b, act, out_dtype, cw=256, into=None):
    s = x.shape[0]
    tm, cw = _tile(s, (_conv_rows(k_taps), 256, 128, 64, 32, 16)), _tile(c, (cw, 128))
    assert xoff % cw == 0 and tm % HALO == 0 and (k_taps - 1) // 2 <= HALO
    nrow = s // tm

    def body(x_ref, p_ref, n_ref, w_ref, b_ref, *rest):
        win = _window(x_ref, p_ref, n_ref, pl.program_id(1), nrow)
        acc = _taps(win, w_ref, k_taps, tm) + b_ref[...]
        rest[-1][...] = (_silu(acc) if act else acc).astype(out_dtype)

    in_specs, args = _conv_specs(s, tm, cw, xoff // cw, w_pad.shape[0]), [x, x, x, w_pad, b]
    if into is None:
        ob, out_shape, aliases = 0, jax.ShapeDtypeStruct((s, c), out_dtype), {}
    else:
        buf, first = into
        assert first % cw == 0 and buf.dtype == out_dtype
        ob, out_shape, aliases = first // cw, jax.ShapeDtypeStruct(buf.shape, out_dtype), {len(args): 0}
        in_specs, args = in_specs + [pl.BlockSpec(memory_space=pl.ANY)], args + [buf]
    return pl.pallas_call(
        body, name=name, grid=(c // cw, nrow), in_specs=in_specs,
        out_specs=pl.BlockSpec((tm, cw), lambda j, i: (i, ob + j)), out_shape=out_shape,
        input_output_aliases=aliases, compiler_params=_params(2),
    )(*args)


def _dwconv_bwd_w(name, x, xoff, c, w_pad, k_taps, b, douts, act, cw=256):
    s = x.shape[0]
    tm, cw = _tile(s, (_conv_rows(k_taps), 256, 128, 64, 32, 16)), _tile(c, (cw, 128))
    assert xoff % cw == 0 and tm % HALO == 0
    nrow, pad, n_d = s // tm, (k_taps - 1) // 2, len(douts)

    def body(x_ref, p_ref, n_ref, w_ref, b_ref, *rest):
        d_refs, outs = rest[:n_d], rest[n_d:]
        i = pl.program_id(1)
        win = _window(x_ref, p_ref, n_ref, i, nrow)
        d = _f(d_refs[0][...])
        for r in d_refs[1:]:
            d = d + _f(r[...])
        if act:
            shifted = _shifted(win, k_taps, tm)
        else:
            shifted = [win[HALO + k - pad:HALO + k - pad + tm, :] for k in range(k_taps)]
        if act:
            pre = b_ref[...] + w_ref[0:1, :] * shifted[0]
            for k in range(1, k_taps):
                pre = pre + w_ref[k:k + 1, :] * shifted[k]
            sg = jax.nn.sigmoid(pre)
            d = d * (sg * (1.0 + pre * (1.0 - sg)))
            outs[0][...] = d.astype(outs[0].dtype)
        dw_ref, db_ref = outs[-2], outs[-1]

        @pl.when(i == 0)
        def _():
            dw_ref[...] = jnp.zeros_like(dw_ref)
            db_ref[...] = jnp.zeros_like(db_ref)

        for k in range(k_taps):
            dw_ref[k:k + 1, :] += jnp.sum(d * shifted[k], axis=0, keepdims=True)
        db_ref[...] += jnp.sum(d, axis=0, keepdims=True)

    row_spec = pl.BlockSpec((tm, cw), lambda j, i: (i, j))
    out_shape = [jax.ShapeDtypeStruct((w_pad.shape[0], c), F32), jax.ShapeDtypeStruct((1, c), F32)]
    out_specs = [pl.BlockSpec((w_pad.shape[0], cw), lambda j, i: (0, j)), pl.BlockSpec((1, cw), lambda j, i: (0, j))]
    if act:
        out_shape, out_specs = [jax.ShapeDtypeStruct((s, c), BF16)] + out_shape, [row_spec] + out_specs
    return pl.pallas_call(
        body, name=name, grid=(c // cw, nrow),
        in_specs=_conv_specs(s, tm, cw, xoff // cw, w_pad.shape[0]) + [row_spec] * n_d,
        out_specs=out_specs, out_shape=out_shape, compiler_params=_params(2),
    )(x, x, x, w_pad, b, *douts)


def _dtprep(name, dt_raw, bias, alog_col):
    s, h2 = dt_raw.shape
    ts = _tile(s, (512, 256, 128))

    def body(r_ref, b_ref, al_ref, dt_ref, a_ref):
        v = r_ref[...] + b_ref[...]
        dt = (jnp.maximum(v, 0.0) + jnp.log(1.0 + jnp.exp(-jnp.abs(v)))).T
        dt_ref[...] = dt
        a_ref[...] = dt * (-jnp.exp(al_ref[...]))

    t_spec = pl.BlockSpec((h2, ts), lambda i: (0, i))
    return pl.pallas_call(
        body, name=name, grid=(s // ts,),
        in_specs=[pl.BlockSpec((ts, h2), lambda i: (i, 0)), pl.BlockSpec((1, h2), lambda i: (0, 0)), pl.BlockSpec((h2, 1), lambda i: (0, 0))],
        out_specs=[t_spec, t_spec], out_shape=[jax.ShapeDtypeStruct((h2, s), F32)] * 2, compiler_params=_params(1),
    )(dt_raw, bias, alog_col)


def _dtprep_bwd(name, ddt_f, ddt_b, da_f, da_b, dt_t, alog_col):
    h2, s = dt_t.shape
    h = h2 // 2
    ts = _tile(s, (512, 256, 128))

    def body(ddf, ddb, daf, dab, dt_ref, al_ref, raw_ref, db_ref, dal_ref):
        i = pl.program_id(0)
        a_col = -jnp.exp(al_ref[...])
        da = jnp.concatenate([daf[...], dab[...]], axis=0)
        dt = dt_ref[...]
        d_raw = (jnp.concatenate([ddf[...], ddb[...]], axis=0) + da * a_col) * (1.0 - jnp.exp(-dt))
        raw_ref[...] = d_raw.T

        @pl.when(i == 0)
        def _():
            db_ref[...] = jnp.zeros_like(db_ref)
            dal_ref[...] = jnp.zeros_like(dal_ref)

        db_ref[...] += jnp.sum(d_raw, axis=1, keepdims=True)
        dal_ref[...] += jnp.sum(da * dt, axis=1, keepdims=True) * a_col

    half = pl.BlockSpec((h, ts), lambda i: (0, i))
    col = pl.BlockSpec((h2, 1), lambda i: (0, 0))
    return pl.pallas_call(
        body, name=name, grid=(s // ts,),
        in_specs=[half, half, half, half, pl.BlockSpec((h2, ts), lambda i: (0, i)), col],
        out_specs=[pl.BlockSpec((ts, h2), lambda i: (i, 0)), col, col],
        out_shape=[jax.ShapeDtypeStruct((s, h2), F32), jax.ShapeDtypeStruct((h2, 1), F32), jax.ShapeDtypeStruct((h2, 1), F32)],
        compiler_params=_params(1),
    )(ddt_f, ddt_b, da_f, da_b, dt_t, alog_col)


def _dot_exact(a, b, dims, exact_lhs):
    other = b if exact_lhs else a
    out = None
    for _ in range(3):
        part = other.astype(BF16)
        other = other - _f(part)
        p = lax.dot_general(a if exact_lhs else part, part if exact_lhs else b, dims, preferred_element_type=F32)
        out = p if out is None else out + p
    return out


def _chunk_consts(a_r, dt_r, rev):
    ii = lax.broadcasted_iota(jnp.int32, (CHUNK, CHUNK), 0)
    jj = lax.broadcasted_iota(jnp.int32, (CHUNK, CHUNK), 1)
    w = (jj >= ii) if rev else (jj <= ii)
    wt = (ii >= jj) if rev else (ii <= jj)
    eye = (ii == jj).astype(F32)
    wb_ = w.astype(BF16)
    cs_col = _dot_exact(wb_, a_r, NT, exact_lhs=True)
    cs_row = _dot_exact(a_r, wb_, NT, exact_lhs=False)
    tot = jnp.sum(a_r, axis=1, keepdims=True)
    lo_half = lax.broadcasted_iota(jnp.int32, (CHUNK, 128), 1) < HEAD_DIM

    def lanes(col):
        return [jnp.broadcast_to(col[:, j:j + 1], (CHUNK, 128)) for j in range(HEADS_PER_GROUP)]

    def spread(bs):
        return jnp.concatenate([jnp.where(lo_half, bs[2 * k], bs[2 * k + 1]) for k in range(HEADS_PER_GROUP // 2)], axis=1)

    csc = lanes(cs_col)
    cs_x = spread(csc)
    hh = lax.broadcasted_iota(jnp.int32, (HEADS_PER_GROUP, GROUP_X), 0)
    ll = lax.broadcasted_iota(jnp.int32, (HEADS_PER_GROUP, GROUP_X), 1)
    expand = (lax.shift_right_logical(ll, 6) == hh).astype(BF16)
    dt_hi = dt_r.astype(BF16)
    dt_x = (lax.dot_general(dt_hi, expand, TN, preferred_element_type=F32)
            + lax.dot_general((dt_r - _f(dt_hi)).astype(BF16), expand, TN, preferred_element_type=F32))
    tot_x = cs_x[0:1, :] if rev else cs_x[CHUNK - 1:CHUNK, :]
    return w, wt, eye, csc, cs_row, tot, cs_x, dt_x, tot_x


def _hi_lo(v):
    hi = v.astype(BF16)
    return jnp.concatenate([hi, (v - _f(hi)).astype(BF16)], axis=1)


def _head_sums(v, terms=3):
    rr = lax.broadcasted_iota(jnp.int32, (GROUP_X, HEADS_PER_GROUP), 0)
    cc = lax.broadcasted_iota(jnp.int32, (GROUP_X, HEADS_PER_GROUP), 1)
    et = (lax.shift_right_logical(rr, 6) == cc).astype(BF16)
    out, rest = None, v
    for t in range(terms):
        part = rest.astype(BF16)
        if t < terms - 1:
            rest = rest - _f(part)
        p = jnp.dot(part, et, preferred_element_type=F32)
        out = p if out is None else out + p
    return out


def _chunks_per_step(nc):
    return 8 if nc % 8 == 0 else 4 if nc % 4 == 0 else 2 if nc % 2 == 0 else 1


def _ssd_fwd(name, xbc, dt_t, a_t, dsk, n_groups):
    s = xbc.shape[0]
    nc = s // CHUNK
    d_ssm = n_groups * GROUP_X
    cps = _chunks_per_step(nc)
    nblk, rows = nc // cps, cps * CHUNK

    def one(x_ref, dt_ref, a_ref, dsk_ref, y_ref, hin_ref, h_scr, rev, c):
        skip = dsk_ref is not None
        r0 = c * CHUNK
        xs = _f(x_ref[r0:r0 + CHUNK, 0:GROUP_X])
        bm = x_ref[r0:r0 + CHUNK, GROUP_X:GROUP_X + D_STATE]
        cm = x_ref[r0:r0 + CHUNK, GROUP_X + D_STATE:GROUP_W]
        w, _, _, csc, cs_row, tot, cs_x, dt_x, tot_x = _chunk_consts(a_ref[:, r0:r0 + CHUNK], dt_ref[:, r0:r0 + CHUNK], rev)
        cb = lax.dot_general(cm, bm, NT, preferred_element_type=F32)
        h = h_scr[...]
        h_b = h.astype(BF16)
        hin_ref[0, c] = h_b
        xdt = xs * dt_x
        xdt_b = xdt.astype(BF16)
        yo = lax.dot_general(cm, h_b, NT, preferred_element_type=F32) * jnp.exp(cs_x)
        st = lax.dot_general((xdt * jnp.exp(tot_x - cs_x)).astype(BF16), bm, TN, preferred_element_type=F32)
        yds = []
        for j in range(HEADS_PER_GROUP):
            lo = HEAD_DIM * j
            h_scr[lo:lo + HEAD_DIM, :] = h[lo:lo + HEAD_DIM, :] * jnp.exp(tot[j:j + 1, 0:1]) + st[lo:lo + HEAD_DIM, :]
            lm = jnp.exp(jnp.where(w, csc[j] - cs_row[j:j + 1, :], -jnp.inf))
            yds.append(jnp.dot((cb * lm).astype(BF16), xdt_b[:, lo:lo + HEAD_DIM], preferred_element_type=F32))
        y = jnp.concatenate(yds, axis=1) + yo
        if skip:
            y = y + dsk_ref[...] * xs
        y_ref[r0:r0 + CHUNK, :] = y.astype(y_ref.dtype)

    def body(xf_ref, dtf_ref, af_ref, xr_ref, dtr_ref, ar_ref, dsk_ref, yf_ref, hf_ref, yr_ref, hr_ref, hf_scr, hr_scr):
        @pl.when(pl.program_id(1) == 0)
        def _():
            hf_scr[...] = jnp.zeros_like(hf_scr)
            hr_scr[...] = jnp.zeros_like(hr_scr)

        for c in range(cps):
            one(xf_ref, dtf_ref, af_ref, dsk_ref, yf_ref, hf_ref, hf_scr, False, c)
            one(xr_ref, dtr_ref, ar_ref, None, yr_ref, hr_ref, hr_scr, True, cps - 1 - c)

    def specs(rev):
        def z(zi):
            return nblk - 1 - zi if rev else zi
        t_spec = pl.BlockSpec((HEADS_PER_GROUP, rows), lambda g, zi: ((n_groups if rev else 0) + g, z(zi)))
        ins = [pl.BlockSpec((rows, GROUP_W), lambda g, zi: (z(zi), g)), t_spec, t_spec]
        outs = [pl.BlockSpec((rows, GROUP_X), lambda g, zi: (z(zi), g)),
                pl.BlockSpec((1, cps, GROUP_X, D_STATE), lambda g, zi: (g, z(zi), 0, 0))]
        return ins, outs

    (in_f, out_f), (in_r, out_r) = specs(False), specs(True)
    shapes = [jax.ShapeDtypeStruct((s, d_ssm), BF16), jax.ShapeDtypeStruct((n_groups, nc, GROUP_X, D_STATE), BF16)]
    return pl.pallas_call(
        body, name=name, grid=(n_groups, nblk), in_specs=in_f + in_r + [pl.BlockSpec((1, GROUP_X), lambda g, zi: (0, g))],
        out_specs=out_f + out_r, out_shape=shapes + shapes,
        scratch_shapes=[pltpu.VMEM((GROUP_X, D_STATE), F32)] * 2, compiler_params=_params(2),
    )(xbc, dt_t, a_t, xbc, dt_t, a_t, dsk)


def _ssd_bwd(name, xbc, dt_t, a_t, dy, hin_f, hin_r, dsk, n_groups, carry=None):
    s = xbc.shape[0]
    nc = s // CHUNK
    n_heads = n_groups * HEADS_PER_GROUP
    cps = _chunks_per_step(nc)
    nblk, rows = nc // cps, cps * CHUNK

    def one(x_ref, dt_ref, a_ref, dy_ref, hin_ref, dsk_ref, dx_ref, ddt_ref, da_ref, ddsk_ref, g_scr, rev, c):
        skip = dsk_ref is not None
        r0 = c * CHUNK
        xs = _f(x_ref[r0:r0 + CHUNK, 0:GROUP_X])
        bm = x_ref[r0:r0 + CHUNK, GROUP_X:GROUP_X + D_STATE]
        cm = x_ref[r0:r0 + CHUNK, GROUP_X + D_STATE:GROUP_W]
        dyv = _f(dy_ref[r0:r0 + CHUNK, :])
        w, wt, eye, csc, cs_row, tot, cs_x, dt_x, tot_x = _chunk_consts(a_ref[:, r0:r0 + CHUNK], dt_ref[:, r0:r0 + CHUNK], rev)
        cb = lax.dot_general(cm, bm, NT, preferred_element_type=F32)
        cbt = lax.dot_general(bm, cm, NT, preferred_element_type=F32)
        onehot = lax.broadcasted_iota(jnp.int32, (1, HEADS_PER_GROUP), 1)
        hin = hin_ref[0, c]
        g = g_scr[...]
        g_b = g.astype(BF16)
        din, dte = jnp.exp(cs_x), jnp.exp(tot_x - cs_x)
        xdt = xs * dt_x
        xdt_b, dy_b = xdt.astype(BF16), dyv.astype(BF16)
        dyo_b = (dyv * din).astype(BF16)
        yo = lax.dot_general(cm, hin, NT, preferred_element_type=F32) * din
        dxdt_s = lax.dot_general(bm, g_b, NT, preferred_element_type=F32) * dte
        d_c = jnp.dot(dyo_b, hin, preferred_element_type=F32)
        d_b = jnp.dot((xdt * dte).astype(BF16), g_b, preferred_element_type=F32)
        dhin = lax.dot_general(dyo_b, cm, TN, preferred_element_type=F32)
        dcb = jnp.zeros((CHUNK, CHUNK), F32)
        dtot = jnp.zeros((1, HEADS_PER_GROUP), F32)
        yds, dxds = [], []
        for j in range(HEADS_PER_GROUP):
            lo = HEAD_DIM * j
            cd = jnp.exp(tot[j:j + 1, 0:1])
            gj = g[lo:lo + HEAD_DIM, :]
            dcd = jnp.sum(jnp.sum(gj * _f(hin[lo:lo + HEAD_DIM, :]), axis=1, keepdims=True), axis=0, keepdims=True)
            dtot = dtot + (dcd * cd) * (onehot == j).astype(F32)
            g_scr[lo:lo + HEAD_DIM, :] = dhin[lo:lo + HEAD_DIM, :] + gj * cd
            dd = csc[j] - cs_row[j:j + 1, :]
            lm = jnp.exp(jnp.where(w, dd, -jnp.inf))
            lmt = jnp.exp(jnp.where(wt, -dd, -jnp.inf))
            xj_b, dyj_b = xdt_b[:, lo:lo + HEAD_DIM], dy_b[:, lo:lo + HEAD_DIM]
            yds.append(jnp.dot(_hi_lo(cb * lm), jnp.concatenate([xj_b, xj_b], axis=0), preferred_element_type=F32))
            dxds.append(jnp.dot(_hi_lo(cbt * lmt), jnp.concatenate([dyj_b, dyj_b], axis=0), preferred_element_type=F32))
            dcb = dcb + lax.dot_general(dyj_b, xj_b, NT, preferred_element_type=F32) * lm
        dxdt_d = jnp.concatenate(dxds, axis=1)
        z_state = xdt * dxdt_s
        yd = jnp.concatenate(yds, axis=1)
        dcs = _head_sums(dyv * yo + _f(dy_b) * yd - _f(xdt_b) * dxdt_d - z_state)
        dtot = dtot + _head_sums(jnp.broadcast_to(jnp.sum(z_state, axis=0, keepdims=True), (8, GROUP_X)))[0:1, :]
        dxdt = dxdt_s + dxdt_d
        ddt_c = _head_sums(dxdt * xs, terms=2)
        dx = dxdt * dt_x
        if skip:
            dx = dx + dsk_ref[...] * dyv
        dcb_b = dcb.astype(BF16)
        d_c = d_c + jnp.dot(dcb_b, bm, preferred_element_type=F32)
        d_b = d_b + lax.dot_general(dcb_b, cm, TN, preferred_element_type=F32)
        eye_b = eye.astype(BF16)
        da_c = _dot_exact(wt.astype(BF16), dcs, NN, exact_lhs=True) + dtot
        ddt_ref[:, r0:r0 + CHUNK] = _dot_exact(ddt_c, eye_b, TN, exact_lhs=False)
        da_ref[:, r0:r0 + CHUNK] = _dot_exact(da_c, eye_b, TN, exact_lhs=False)
        dx_ref[r0:r0 + CHUNK, :] = jnp.concatenate([dx, d_b, d_c], axis=1).astype(dx_ref.dtype)
        if skip:
            ddsk_ref[...] += jnp.sum(dyv * xs, axis=0, keepdims=True)

    ops, c_arrs = carry if carry is not None else (None, [])
    nc_ = len(c_arrs)

    def body(xf, dtf, af, dyf, hf, xr, dtr, ar, dyr, hr, dsk_ref, *rest):
        c_ins, (dxf, ddtf, daf, dxr, ddtr, dar, ddsk_ref) = rest[:nc_], rest[nc_:nc_ + 7]
        c_outs, (gf_scr, gr_scr) = rest[nc_ + 7:2 * nc_ + 7], rest[2 * nc_ + 7:2 * nc_ + 9]
        if nc_:
            start, finish = ops(c_ins, c_outs, *rest[2 * nc_ + 9:])
            pl.when((pl.program_id(0) == 0) & (pl.program_id(1) == 0))(start)

        @pl.when(pl.program_id(1) == 0)
        def _():
            gf_scr[...] = jnp.zeros_like(gf_scr)
            gr_scr[...] = jnp.zeros_like(gr_scr)
            ddsk_ref[...] = jnp.zeros_like(ddsk_ref)

        for c in range(cps):
            one(xf, dtf, af, dyf, hf, dsk_ref, dxf, ddtf, daf, ddsk_ref, gf_scr, False, cps - 1 - c)
            one(xr, dtr, ar, dyr, hr, None, dxr, ddtr, dar, None, gr_scr, True, c)
        if nc_:
            pl.when((pl.program_id(0) == n_groups - 1) & (pl.program_id(1) == nblk - 1))(finish)

    def specs(rev):
        def z(zi):
            return zi if rev else nblk - 1 - zi
        t_in = pl.BlockSpec((HEADS_PER_GROUP, rows), lambda g, zi: ((n_groups if rev else 0) + g, z(zi)))
        t_out = pl.BlockSpec((HEADS_PER_GROUP, rows), lambda g, zi: (g, z(zi)))
        x_spec = pl.BlockSpec((rows, GROUP_W), lambda g, zi: (z(zi), g))
        ins = [x_spec, t_in, t_in, pl.BlockSpec((rows, GROUP_X), lambda g, zi: (z(zi), g)),
               pl.BlockSpec((1, cps, GROUP_X, D_STATE), lambda g, zi: (g, z(zi), 0, 0))]
        return ins, [x_spec, t_out, t_out]

    (in_f, out_f), (in_r, out_r) = specs(False), specs(True)
    lane_spec = pl.BlockSpec((1, GROUP_X), lambda g, zi: (0, g))
    shapes = [jax.ShapeDtypeStruct((s, n_groups * GROUP_W), BF16), jax.ShapeDtypeStruct((n_heads, s), F32),
              jax.ShapeDtypeStruct((n_heads, s), F32)]
    any_spec = pl.BlockSpec(memory_space=pl.ANY)
    res = pl.pallas_call(
        body, name=name, grid=(n_groups, nblk), in_specs=in_f + in_r + [lane_spec] + [any_spec] * nc_,
        out_specs=out_f + out_r + [lane_spec] + [any_spec] * nc_,
        out_shape=shapes + shapes + [jax.ShapeDtypeStruct((1, n_groups * GROUP_X), F32)] + (ops.out_shapes(c_arrs) if nc_ else []),
        scratch_shapes=[pltpu.VMEM((GROUP_X, D_STATE), F32)] * 2 + (_exchange_sems(nc_) if nc_ else []),
        compiler_params=_params(2),
    )(xbc, dt_t, a_t, dy, hin_f, xbc, dt_t, a_t, dy, hin_r, dsk, *c_arrs)
    return (res[:7], res[7:]) if nc_ else res


def _mesh_pos():
    return lax.axis_index("x"), lax.axis_index("y"), lax.axis_index("c")


def _slot(p):
    return 4 * p[0] + 2 * p[1] + p[2]


def _exchange_sems(n):
    return [pltpu.SemaphoreType.DMA((7 * n,)), pltpu.SemaphoreType.DMA((7 * n,)), pltpu.SemaphoreType.DMA((n,))]


def _gather_ops(ins, outs, send, recv, loc):
    n = len(ins)
    x, y, c = _mesh_pos()
    me, sib = (x, y, c), (x, y, 1 - c)
    chips = [(1 - x, y), (x, 1 - y), (1 - x, 1 - y)]

    def cp(a, k, block, to, src=None):
        dst = outs[a].at[_slot(block)]
        return pltpu.make_async_remote_copy(
            src_ref=dst if src is None else src, dst_ref=dst, send_sem=send.at[7 * a + k], recv_sem=recv.at[7 * a + k],
            device_id=to, device_id_type=MESH)

    mine = [pltpu.make_async_copy(ins[a], outs[a].at[_slot(me)], loc.at[a]) for a in range(n)]
    first = []
    for a in range(n):
        first.append(cp(a, 0, me, sib, src=ins[a]))
        first += [cp(a, 1 + j, me, (*chip, c), src=ins[a]) for j, chip in enumerate(chips)]

    def start():
        for d in mine + first:
            d.start()

    def finish():
        passed = []
        for a in range(n):
            for j, chip in enumerate(chips):
                cp(a, 1 + j, (*chip, c), me).wait_recv()
                p = cp(a, 4 + j, (*chip, c), sib)
                p.start()
                passed.append(p)
        for a in range(n):
            cp(a, 0, sib, me).wait_recv()
            for j, chip in enumerate(chips):
                cp(a, 4 + j, (*chip, 1 - c), me).wait_recv()
        for f in first + passed:
            f.wait_send()
        for m in mine:
            m.wait()

    return start, finish


_gather_ops.out_shapes = lambda arrs: [jax.ShapeDtypeStruct((N_DEV,) + a.shape, a.dtype) for a in arrs]


def _a2a_ops(ins, outs, send, recv, loc):
    n = len(ins)
    me = _mesh_pos()
    flips = [(fx, fy, fc) for fx in (0, 1) for fy in (0, 1) for fc in (0, 1)][1:]
    peers = [tuple(1 - p if fl else p for p, fl in zip(me, fl3)) for fl3 in flips]

    def cp(a, k, peer, dst_slot):
        return pltpu.make_async_remote_copy(
            src_ref=ins[a].at[_slot(peer)], dst_ref=outs[a].at[dst_slot], send_sem=send.at[7 * a + k],
            recv_sem=recv.at[7 * a + k], device_id=peer, device_id_type=MESH)

    mine = [pltpu.make_async_copy(ins[a].at[_slot(me)], outs[a].at[_slot(me)], loc.at[a]) for a in range(n)]
    sends = [cp(a, k, peer, _slot(me)) for a in range(n) for k, peer in enumerate(peers)]

    def start():
        for d in mine + sends:
            d.start()

    def finish():
        for a in range(n):
            for k, peer in enumerate(peers):
                cp(a, k, peer, _slot(peer)).wait_recv()
        for s_ in sends:
            s_.wait_send()
        for m in mine:
            m.wait()

    return start, finish


_a2a_ops.out_shapes = lambda arrs: [jax.ShapeDtypeStruct(a.shape, a.dtype) for a in arrs]


def _exchange(name, ops, arrs):
    n = len(arrs)

    def body(*refs):
        start, finish = ops(refs[:n], refs[n:2 * n], *refs[2 * n:])
        start()
        finish()

    any_spec = pl.BlockSpec(memory_space=pl.ANY)
    return pl.pallas_call(
        body, name=name, in_specs=[any_spec] * n, out_specs=[any_spec] * n, out_shape=ops.out_shapes(arrs),
        scratch_shapes=_exchange_sems(n),
    )(*arrs)


def _all_gather(name, arrs):
    return _exchange(name, _gather_ops, arrs)


def _all_to_all(name, arrs):
    return _exchange(name, _a2a_ops, arrs)


def _sum8(parts_ref):
    g = _f(parts_ref[0])
    for j in range(1, N_DEV):
        g = g + _f(parts_ref[j])
    return g


def _adam_math(w, g, m, v):
    m = ADAM_B1 * m + (1.0 - ADAM_B1) * g
    v = ADAM_B2 * v + (1.0 - ADAM_B2) * (g * g)
    m_hat = m / (1.0 - ADAM_B1 ** ADAM_STEP)
    v_hat = v / (1.0 - ADAM_B2 ** ADAM_STEP)
    return -ADAM_LR * (m_hat / (jnp.sqrt(v_hat) + ADAM_EPS) + ADAM_WD * w), m, v


def _reduce_sum(name, parts):
    _, r, c = parts.shape

    def body(p_ref, o_ref):
        o_ref[...] = _sum8(p_ref)

    return pl.pallas_call(body, name=name, out_shape=jax.ShapeDtypeStruct((r, c), F32))(parts)


def _adam(name, w, g, m, v, parts=None):
    r, c = w.shape
    tr = _tile(r, (128, 64, 32, 16)) if r * c > 2 ** 18 else r
    spec = pl.BlockSpec((tr, c), lambda i: (i, 0))
    summed = parts is not None

    def body(*refs):
        if summed:
            p_ref, w_ref, m_ref, v_ref, g_out, d_out, m_out, v_out = refs
            gv = _sum8(p_ref)
            g_out[...] = gv
        else:
            g_ref, w_ref, m_ref, v_ref, d_out, m_out, v_out = refs
            gv = g_ref[...]
        d, mn, vn = _adam_math(w_ref[...], gv, m_ref[...], v_ref[...])
        d_out[...] = d
        m_out[...] = mn
        v_out[...] = vn

    first = [pl.BlockSpec((N_DEV, tr, c), lambda i: (0, i, 0))] if summed else [spec]
    n_out = 4 if summed else 3
    return pl.pallas_call(
        body, name=name, grid=(r // tr,), in_specs=first + [spec] * 3, out_specs=[spec] * n_out,
        out_shape=[jax.ShapeDtypeStruct((r, c), F32)] * n_out, compiler_params=_params(1),
    )(parts if summed else g, w, m, v)


def _adam_many(name, ws, gs, ms, vs):
    n = len(ws)

    def body(*refs):
        outs = refs[4 * n:]
        for k in range(n):
            d, mn, vn = _adam_math(refs[k][...], refs[n + k][...], refs[2 * n + k][...], refs[3 * n + k][...])
            outs[k][...] = d
            outs[n + k][...] = mn
            outs[2 * n + k][...] = vn

    res = pl.pallas_call(
        body, name=name, out_shape=[jax.ShapeDtypeStruct(w.shape, F32) for w in ws] * 3,
        compiler_params=pltpu.CompilerParams(vmem_limit_bytes=VMEM_LIMIT),
    )(*ws, *gs, *ms, *vs)
    return res[:n], res[n:2 * n], res[2 * n:]


def _slot_offsets(sizes):
    offs, o = [], 0
    for sz in sizes:
        offs.append(o)
        o += -(-sz // 128) * 128
    return offs, o + (-o % 1024)


def _pack_vectors(name, vecs):
    sizes = [v.shape[1] for v in vecs]
    offs, total = _slot_offsets(sizes)

    def body(*refs):
        out = refs[-1]
        out[...] = jnp.zeros_like(out)
        for r, o, sz in zip(refs[:-1], offs, sizes):
            out[:, o:o + sz] = r[...]

    return pl.pallas_call(body, name=name, out_shape=jax.ShapeDtypeStruct((1, total), F32),
                          compiler_params=pltpu.CompilerParams(vmem_limit_bytes=VMEM_LIMIT))(*vecs).reshape(-1, 128)


def _xbc_runs(n_groups, base=0):
    ds, nb = n_groups * GROUP_X, n_groups * D_STATE
    runs = []
    for g in range(n_groups):
        runs += [(base + GROUP_X * g, GROUP_X), (base + ds + D_STATE * g, D_STATE), (base + ds + nb + D_STATE * g, D_STATE)]
    return runs


def _interleave_runs(width, base=0):
    half = width // 2
    runs = []
    for j in range(half // COL_BLOCK):
        runs += [(base + COL_BLOCK * j, COL_BLOCK), (base + half + COL_BLOCK * j, COL_BLOCK)]
    return runs


def _w_in_runs(d, d_ssm, d_xbc, n_heads, n_groups):
    o_dt = d_ssm + d_xbc
    o_glu = o_dt + 2 * n_heads
    return ([(0, d_ssm)] + _xbc_runs(n_groups, d_ssm) + _interleave_runs(2 * d, o_glu) + _interleave_runs(2 * d, o_glu + 2 * d)
            + [(o_dt, 2 * n_heads)])


def _inverse_runs(runs):
    offs = np.concatenate([[0], np.cumsum([l for _, l in runs])])
    order = sorted(range(len(runs)), key=lambda k: runs[k][0])
    return [(int(offs[k]), runs[k][1]) for k in order]


def _permute_cols(name, srcs, runs, out_widths=None, src_sharded=False, dst_sharded=False):
    rows = srcs[0].shape[-2]
    dtype = srcs[0].dtype
    total = sum(l for _, l in runs)
    src_w = [srcs[0].shape[2]] * N_DEV if src_sharded else [a.shape[1] for a in srcs]
    dst_w = [total // N_DEV] * N_DEV if dst_sharded else (out_widths or [total])
    assert sum(dst_w) == total and sum(src_w) >= max(s_ + l for s_, l in runs)
    src_edges, dst_edges = np.cumsum([0] + src_w), np.cumsum([0] + dst_w)

    def locate(edges, col):
        k = int(np.searchsorted(edges, col, side="right")) - 1
        return k, col - int(edges[k])

    pieces, o = [], 0
    for s_, l in runs:
        while l:
            (ks, cs_), (kd, cd) = locate(src_edges, s_), locate(dst_edges, o)
            n = min(l, src_w[ks] - cs_, dst_w[kd] - cd)
            if pieces and pieces[-1][0] == ks and pieces[-1][2] == kd and pieces[-1][1] + pieces[-1][4] == cs_:
                pieces[-1] = pieces[-1][:4] + (pieces[-1][4] + n,)
            else:
                pieces.append((ks, cs_, kd, cd, n))
            s_, o, l = s_ + n, o + n, l - n
    tr = rows
    while tr % 16 == 0 and tr * max(sum(src_w), total) * dtype.itemsize > 4 * 2 ** 20:
        tr //= 2
    n_src = len(srcs)

    def body(*refs):
        ins, outs = refs[:n_src], refs[n_src:]
        for ks, cs_, kd, cd, n in pieces:
            val = ins[0][ks, :, cs_:cs_ + n] if src_sharded else ins[ks][:, cs_:cs_ + n]
            if dst_sharded:
                outs[0][kd, :, cd:cd + n] = val
            else:
                outs[kd][:, cd:cd + n] = val

    def spec(sharded, width):
        if sharded:
            return pl.BlockSpec((N_DEV, tr, width), lambda i: (0, i, 0))
        return pl.BlockSpec((tr, width), lambda i: (i, 0))

    in_specs = [spec(True, src_w[0])] if src_sharded else [spec(False, w) for w in src_w]
    if dst_sharded:
        out_specs, out_shape = [spec(True, dst_w[0])], [jax.ShapeDtypeStruct((N_DEV, rows, dst_w[0]), dtype)]
    else:
        out_specs, out_shape = [spec(False, w) for w in dst_w], [jax.ShapeDtypeStruct((rows, w), dtype) for w in dst_w]
    return pl.pallas_call(body, name=name, grid=(rows // tr,), in_specs=in_specs, out_specs=out_specs, out_shape=out_shape,
                          compiler_params=_params(1))(*srcs)


def _cols_from_shards(g):
    return g.transpose(1, 0, 2).reshape(g.shape[1], -1)


def _cols_to_shards(a):
    return a.reshape(a.shape[0], N_DEV, -1).transpose(1, 0, 2)


def _pad_rows(a, rows):
    return jnp.concatenate([a, jnp.zeros((rows - a.shape[0], a.shape[1]), a.dtype)], axis=0)


def _pack(vecs):
    a = jnp.concatenate(vecs, axis=1)
    return jnp.concatenate([a, jnp.zeros((1, -a.shape[1] % 1024), a.dtype)], axis=1).reshape(-1, 128)


def kernel(x, c, w_ada, b_ada, g_pre_mix, g_post_mix, w_in, w_conv_ssm, b_conv_ssm, dt_bias_fwd, dt_bias_bwd, a_log_fwd, a_log_bwd, d_skip, g_ssm_norm, w_ssm_out, b_glu, w_dw, b_dw, ln_g, ln_b, w_conv_out, b_conv_out, b_gate, w_mix_out, g_pre_ffn, g_post_ffn, w_gate_up, w_down, loss_target, m_w_ada, m_b_ada, m_g_pre_mix, m_g_post_mix, m_w_in, m_w_conv_ssm, m_b_conv_ssm, m_dt_bias_fwd, m_dt_bias_bwd, m_a_log_fwd, m_a_log_bwd, m_d_skip, m_g_ssm_norm, m_w_ssm_out, m_b_glu, m_w_dw, m_b_dw, m_ln_g, m_ln_b, m_w_conv_out, m_b_conv_out, m_b_gate, m_w_mix_out, m_g_pre_ffn, m_g_post_ffn, m_w_gate_up, m_w_down, v_w_ada, v_b_ada, v_g_pre_mix, v_g_post_mix, v_w_in, v_w_conv_ssm, v_b_conv_ssm, v_dt_bias_fwd, v_dt_bias_bwd, v_a_log_fwd, v_a_log_bwd, v_d_skip, v_g_ssm_norm, v_w_ssm_out, v_b_glu, v_w_dw, v_b_dw, v_ln_g, v_ln_b, v_w_conv_out, v_b_conv_out, v_b_gate, v_w_mix_out, v_g_pre_ffn, v_g_post_ffn, v_w_gate_up, v_w_down):
    weights = dict(w_ada=w_ada, b_ada=b_ada, g_pre_mix=g_pre_mix, g_post_mix=g_post_mix, w_in=w_in, w_conv_ssm=w_conv_ssm, b_conv_ssm=b_conv_ssm, dt_bias_fwd=dt_bias_fwd, dt_bias_bwd=dt_bias_bwd, a_log_fwd=a_log_fwd, a_log_bwd=a_log_bwd, d_skip=d_skip, g_ssm_norm=g_ssm_norm, w_ssm_out=w_ssm_out, b_glu=b_glu, w_dw=w_dw, b_dw=b_dw, ln_g=ln_g, ln_b=ln_b, w_conv_out=w_conv_out, b_conv_out=b_conv_out, b_gate=b_gate, w_mix_out=w_mix_out, g_pre_ffn=g_pre_ffn, g_post_ffn=g_post_ffn, w_gate_up=w_gate_up, w_down=w_down)
    mom_m = dict(w_ada=m_w_ada, b_ada=m_b_ada, g_pre_mix=m_g_pre_mix, g_post_mix=m_g_post_mix, w_in=m_w_in, w_conv_ssm=m_w_conv_ssm, b_conv_ssm=m_b_conv_ssm, dt_bias_fwd=m_dt_bias_fwd, dt_bias_bwd=m_dt_bias_bwd, a_log_fwd=m_a_log_fwd, a_log_bwd=m_a_log_bwd, d_skip=m_d_skip, g_ssm_norm=m_g_ssm_norm, w_ssm_out=m_w_ssm_out, b_glu=m_b_glu, w_dw=m_w_dw, b_dw=m_b_dw, ln_g=m_ln_g, ln_b=m_ln_b, w_conv_out=m_w_conv_out, b_conv_out=m_b_conv_out, b_gate=m_b_gate, w_mix_out=m_w_mix_out, g_pre_ffn=m_g_pre_ffn, g_post_ffn=m_g_post_ffn, w_gate_up=m_w_gate_up, w_down=m_w_down)
    mom_v = dict(w_ada=v_w_ada, b_ada=v_b_ada, g_pre_mix=v_g_pre_mix, g_post_mix=v_g_post_mix, w_in=v_w_in, w_conv_ssm=v_w_conv_ssm, b_conv_ssm=v_b_conv_ssm, dt_bias_fwd=v_dt_bias_fwd, dt_bias_bwd=v_dt_bias_bwd, a_log_fwd=v_a_log_fwd, a_log_bwd=v_a_log_bwd, d_skip=v_d_skip, g_ssm_norm=v_g_ssm_norm, w_ssm_out=v_w_ssm_out, b_glu=v_b_glu, w_dw=v_w_dw, b_dw=v_b_dw, ln_g=v_ln_g, ln_b=v_ln_b, w_conv_out=v_w_conv_out, b_conv_out=v_b_conv_out, b_gate=v_b_gate, w_mix_out=v_w_mix_out, g_pre_ffn=v_g_pre_ffn, g_post_ffn=v_g_post_ffn, w_gate_up=v_w_gate_up, w_down=v_w_down)
    names = list(weights)

    s, d = x.shape[1], x.shape[2]
    d_ssm, n_heads, d_xbc = g_ssm_norm.shape[-1], d_skip.shape[-1], b_conv_ssm.shape[-1]
    n_groups = n_heads // HEADS_PER_GROUP
    d_ff = w_down.shape[1] * N_DEV
    k_ssm, k_dw = w_conv_ssm.shape[1], w_dw.shape[1]
    assert d_ssm == n_groups * GROUP_X and d_xbc == n_groups * GROUP_W and s % CHUNK == 0
    assert d % COL_BLOCK == 0 and d_ff % COL_BLOCK == 0
    mc = w_ada.shape[-1]
    me = _slot(_mesh_pos())
    wb = d_ssm + d_xbc + 4 * d
    p_x, p_glu, p_gate = d_ssm, d_ssm + d_xbc, d_ssm + d_xbc + 2 * d
    ncb, nfb = d // COL_BLOCK, d_ff // COL_BLOCK
    x2, tgt = x[0], loss_target[0]

    big = ["w_in", "w_ssm_out", "w_conv_out", "w_mix_out", "w_gate_up", "w_down"]
    (w_in_g,) = _all_gather("gather_w_in", [w_in[0].astype(BF16)])
    w_in_runs, xbc_runs, il_runs = _w_in_runs(d, d_ssm, d_xbc, n_heads, n_groups), _xbc_runs(n_groups), _interleave_runs(2 * d)
    w_big, w_dt = _permute_cols("order_w_in", [w_in_g], w_in_runs, out_widths=[wb, 2 * n_heads], src_sharded=True)

    n_cs, n_dw = k_ssm * (d_xbc // N_DEV), k_dw * (d // N_DEV)
    (small_all,) = _all_gather("gather_small", [_pack([c, w_conv_ssm.reshape(1, n_cs), w_dw.reshape(1, n_dw)])])
    small_all = small_all.reshape(N_DEV, -1)
    c_all = small_all[:, :d]
    wcs = _cols_from_shards(small_all[:, d:d + n_cs].reshape(N_DEV, k_ssm, -1))
    wdw = _cols_from_shards(small_all[:, d + n_cs:d + n_cs + n_dw].reshape(N_DEV, k_dw, -1))
    (wcs_p,) = _permute_cols("order_conv_taps", [wcs], xbc_runs)
    (bcs_p,) = _permute_cols("order_conv_bias", [b_conv_ssm], xbc_runs)
    (b_glu_il,), (b_gate_il,) = _permute_cols("order_b_glu", [b_glu], il_runs), _permute_cols("order_b_gate", [b_gate], il_runs)
    kp_ssm, kp_dw = -(-k_ssm // 8) * 8, -(-k_dw // 8) * 8
    zero_x, zero_d = jnp.zeros((1, d_xbc), F32), jnp.zeros((1, d), F32)

    (c_act,), _ = _rowcall("c_act", _fwd_fn(_silu), [(c_all, 0, d)], [], [(d, F32)], [])
    b_my = lax.dynamic_slice(b_ada, (0, me * mc), (1, mc))
    mod_cols = _matmul("mod", c_act, w_ada[0], "nn", F32, acc_in=jnp.broadcast_to(b_my, (N_DEV, mc)))
    (mod_rows,) = _all_to_all("mod_to_owner", [mod_cols.reshape(N_DEV, 1, mc)])
    mod = mod_rows.reshape(1, N_DEV * mc)
    sh1, sc1, g1, sh2, sc2, g2 = [mod[:, k * d:(k + 1) * d] for k in range(6)]

    (h1,), _ = _rowcall("prenorm_mix", _fwd_fn(_t_prenorm), [(x2, 0, d)], [g_pre_mix, sc1, sh1], [(d, BF16)], [])
    proj, rest = _matmul("in_proj", h1, w_big, "nn", BF16, carry=(_gather_ops, [weights[k][0].astype(BF16) for k in big[1:]]))
    gath = dict(zip(big[1:], rest))
    w_ssm, w_co, w_mo = gath["w_ssm_out"].reshape(d_ssm, d), gath["w_conv_out"].reshape(d, d), gath["w_mix_out"].reshape(d, d)
    gu_runs = _interleave_runs(2 * d_ff)
    (w_gu,) = _permute_cols("order_w_gate_up", [gath["w_gate_up"]], gu_runs, src_sharded=True)
    w_dn = gath["w_down"].reshape(d_ff, d)
    dt_raw = _matmul("in_proj_dt", h1, w_dt, "nn", F32)
    xbc_c = _dwconv("conv_ssm", proj, p_x, d_xbc, _pad_rows(wcs_p, kp_ssm), k_ssm, bcs_p, True, BF16)
    alog_col = jnp.concatenate([a_log_fwd, a_log_bwd], axis=1).reshape(2 * n_heads, 1)
    dt_t, a_t = _dtprep("dt_prep", dt_raw, jnp.concatenate([dt_bias_fwd, dt_bias_bwd], axis=1), alog_col)
    dsk = jnp.repeat(d_skip, HEAD_DIM, axis=1)
    y_f, hin_f, y_b, hin_b = _ssd_fwd("ssd", xbc_c, dt_t, a_t, dsk, n_groups)
    gn_rows = [(y_f, 0, d_ssm), (y_b, 0, d_ssm), (proj, 0, d_ssm)]
    (yn,), _ = _rowcall("ssm_norm", _fwd_fn(_t_gnorm), gn_rows, [g_ssm_norm], [(d_ssm, BF16)], [], ncol=n_groups)
    ya = _matmul("ssm_out", yn, w_ssm, "nn", BF16)
    (u0,), _ = _rowcall("glu", _glu_fwd, [(proj, p_glu, 2 * d)], [b_glu_il], [(d, BF16)], [], ncol=ncb)
    wdw_pad = _pad_rows(wdw, kp_dw)
    u1 = _dwconv("conv_dw", u0, 0, d, wdw_pad, k_dw, b_dw, False, BF16)
    (u2,), _ = _rowcall("ln_silu", _fwd_fn(_t_lnsilu), [(u1, 0, d)], [ln_g, ln_b], [(d, BF16)], [])
    yb = _matmul("conv_out", u2, w_co, "nn", BF16)
    gm_rows, gm_prm = [(ya, 0, d), (yb, 0, d), (proj, p_gate, 2 * d)], [b_conv_out, b_gate_il]
    (m_in,), _ = _rowcall("gate_mix", _gatemix_fwd, gm_rows, gm_prm, [(d, BF16)], [], ncol=ncb)
    mix = _matmul("mix_out", m_in, w_mo, "nn", BF16)
    pp_prm = [g_post_mix, g1, g_pre_ffn, sc2, sh2]
    (x1, h2), _ = _rowcall("post_mix", _fwd_fn(_t_postpre), [(x2, 0, d), (mix, 0, d)], pp_prm, [(d, F32), (d, BF16)], [])
    gu, act = _matmul("gate_up", h2, w_gu, "nn", BF16, tn=2 * COL_BLOCK,
                      epilogue=(lambda p: (p, _swiglu_fwd(p)[0][0]), [], [(2 * d_ff, BF16), (d_ff, BF16)]))
    f = _matmul("down", act, w_dn, "nn", BF16)

    def loss_fn(x1_t, f_t, tgt_t, gpost, g2_t):
        args = [_f(x1_t), _f(f_t), _f(tgt_t), _f(gpost), _f(g2_t)]
        val, vjp = jax.vjp(_t_loss, *args)
        gr = vjp(jnp.ones((), F32))
        return (gr[0], gr[1]), (jnp.zeros((1, 128), F32) + val, gr[3], gr[4])

    (dx1a, df), (loss_v, d_gpost2, d_g2) = _rowcall(
        "loss", loss_fn, [(x1, 0, d), (f, 0, d), (tgt, 0, d)], [g_post_ffn, g2], [(d, F32), (d, BF16)], [128, d, d])
    loss = lax.psum(loss_v[0, 0], ("x", "y", "c"))

    dgu = _matmul("down_dx", df, w_dn, "nt", BF16, tn=COL_BLOCK,
                  epilogue=(lambda p, gu_t: _swiglu_bwd(gu_t, p)[0], [gu], [(2 * d_ff, BF16)]))
    dw_dn = _matmul("down_dw", act, df, "tn", BF16)
    recv = {}
    dh2, (recv["w_down"],) = _matmul("gate_up_dx", dgu, w_gu, "nt", BF16, carry=(_a2a_ops, [dw_dn.reshape(N_DEV, -1, d)]))
    dw_gu = _matmul("gate_up_dw", h2, dgu, "tn", BF16)
    (dxa, dmix), (d_gpost1, d_g1, d_gpre2, d_sc2, d_sh2) = _rowcall(
        "post_mix_bwd", _vjp_fn(_t_postpre, 2, 2, (0, 1), (0, 1, 2, 3, 4)),
        [(x2, 0, d), (mix, 0, d), (dx1a, 0, d), (dh2, 0, d)], pp_prm, [(d, F32), (d, BF16)], [d] * 5)
    dm_in = _matmul("mix_out_dx", dmix, w_mo, "nt", BF16)
    dw_mo = _matmul("mix_out_dw", m_in, dmix, "tn", BF16)
    dproj = lax.empty((s, wb), BF16)
    (dya, dyb, dproj), (d_bco, d_bgate) = _rowcall(
        "gate_mix_bwd", _gatemix_bwd, gm_rows + [(dm_in, 0, d)], gm_prm,
        [(d, BF16), (d, BF16), (2 * d, BF16, dproj, p_gate)], [d, 2 * d], ncol=ncb)
    du2 = _matmul("conv_out_dx", dyb, w_co, "nt", BF16)
    dw_co = _matmul("conv_out_dw", u2, dyb, "tn", BF16)
    (du1,), (d_lng, d_lnb) = _rowcall("ln_silu_bwd", _vjp_fn(_t_lnsilu, 1, 1, (0,), (0, 1)), [(u1, 0, d), (du2, 0, d)],
                                      [ln_g, ln_b], [(d, BF16)], [d, d])
    dwdw_p, d_bdw = _dwconv_bwd_w("conv_dw_dw", u0, 0, d, wdw_pad, k_dw, b_dw, [du1], False)
    du0 = _dwconv("conv_dw_dx", du1, 0, d, _pad_rows(wdw[::-1], kp_dw), k_dw, zero_d, False, BF16)
    (dproj,), (d_bglu,) = _rowcall("glu_bwd", _glu_bwd, [(proj, p_glu, 2 * d), (du0, 0, d)], [b_glu_il],
                                   [(2 * d, BF16, dproj, p_glu)], [2 * d], ncol=ncb)
    dyn = _matmul("ssm_out_dx", dya, w_ssm, "nt", BF16)
    dw_ssm = _matmul("ssm_out_dw", yn, dya, "tn", BF16)
    (dy_ssd, dproj), (d_gn,) = _rowcall("ssm_norm_bwd", _vjp_fn(_t_gnorm, 3, 1, (0, 2), (0,)), gn_rows + [(dyn, 0, d_ssm)],
                                        [g_ssm_norm], [(d_ssm, BF16), (d_ssm, BF16, dproj, 0)], [d_ssm], ncol=n_groups)
    ready = [dw_mo.reshape(N_DEV, -1, d), dw_co.reshape(N_DEV, -1, d), dw_ssm.reshape(N_DEV, -1, d),
             _permute_cols("restore_dw_gate_up", [dw_gu], _inverse_runs(gu_runs), dst_sharded=True)[0]]
    (dxbc_f, ddt_f, da_f, dxbc_b, ddt_b, da_b, ddsk), got = _ssd_bwd(
        "ssd_bwd", xbc_c, dt_t, a_t, dy_ssd, hin_f, hin_b, dsk, n_groups, carry=(_a2a_ops, ready))
    recv["w_mix_out"], recv["w_conv_out"], recv["w_ssm_out"], recv["w_gate_up"] = got
    ddt_raw, d_dtb, d_alog = _dtprep_bwd("dt_prep_bwd", ddt_f, ddt_b, da_f, da_b, dt_t, alog_col)
    dpre, dwcs_p, dbcs_p = _dwconv_bwd_w("conv_ssm_dw", proj, p_x, d_xbc, _pad_rows(wcs_p, kp_ssm), k_ssm, bcs_p,
                                          [dxbc_f, dxbc_b], True)
    dproj = _dwconv("conv_ssm_dx", dpre, 0, d_xbc, _pad_rows(wcs_p[::-1], kp_ssm), k_ssm, zero_x, False, BF16, into=(dproj, p_x))
    dw_big = _matmul("in_proj_dw", h1, dproj, "tn", BF16)
    dw_dt = _matmul("in_proj_dt_dw", h1, ddt_raw, "tn", BF16)
    (dw_in,) = _permute_cols("restore_dw_in", [dw_big, dw_dt], _inverse_runs(w_in_runs), dst_sharded=True)
    dh1, (recv["w_in"],) = _matmul("in_proj_dx", dproj, w_big, "nt", F32, carry=(_a2a_ops, [dw_in]))
    dh1 = _matmul("in_proj_dt_dx", ddt_raw, w_dt, "nt", BF16, acc_in=dh1)

    def prenorm_bwd_fn(x_t, dh_t, dxa_t, g, sc, sh):
        rows, prm = _vjp_fn(_t_prenorm, 1, 1, (0,), (0, 1, 2))(x_t, dh_t, g, sc, sh)
        return (rows[0] + _f(dxa_t),), prm

    (grad_x,), (d_gpre1, d_sc1, d_sh1) = _rowcall(
        "prenorm_mix_bwd", prenorm_bwd_fn, [(x2, 0, d), (dh1, 0, d), (dxa, 0, d)], [g_pre_mix, sc1, sh1], [(d, F32)], [d] * 3)

    dmod = jnp.concatenate([d_sh1, d_sc1, d_g1, d_sh2, d_sc2, d_g2], axis=1)
    small_g = {
        "b_ada": dmod, "g_pre_mix": d_gpre1, "g_post_mix": d_gpost1,
        "b_conv_ssm": _permute_cols("restore_db_conv", [dbcs_p], _inverse_runs(xbc_runs))[0],
        "dt_bias_fwd": d_dtb[:n_heads].reshape(1, n_heads), "dt_bias_bwd": d_dtb[n_heads:].reshape(1, n_heads),
        "a_log_fwd": d_alog[:n_heads].reshape(1, n_heads), "a_log_bwd": d_alog[n_heads:].reshape(1, n_heads),
        "d_skip": ddsk.reshape(n_heads, HEAD_DIM).sum(axis=1).reshape(1, n_heads), "g_ssm_norm": d_gn,
        "b_glu": _permute_cols("restore_db_glu", [d_bglu], _inverse_runs(il_runs))[0], "b_dw": d_bdw, "ln_g": d_lng, "ln_b": d_lnb,
        "b_conv_out": d_bco, "b_gate": _permute_cols("restore_db_gate", [d_bgate], _inverse_runs(il_runs))[0],
        "g_pre_ffn": d_gpre2, "g_post_ffn": d_gpost2,
        "w_conv_ssm": _permute_cols("restore_dw_conv", [dwcs_p], _inverse_runs(xbc_runs))[0][:k_ssm].reshape(1, k_ssm * d_xbc),
        "w_dw": dwdw_p[:k_dw].reshape(1, k_dw * d),
    }
    small_names = list(small_g)
    sizes = [small_g[k].shape[1] for k in small_names]
    offs, _ = _slot_offsets(sizes)
    (small_parts,) = _all_gather("gather_small_grads", [_pack_vectors("pack_small_grads", [small_g[k] for k in small_names])])
    dmod_all = small_parts.reshape(N_DEV, -1)[:, :6 * d]
    small_tot = _reduce_sum("sum_small_grads", small_parts).reshape(1, -1)
    grads = {k: small_tot[:, offs[i]:offs[i] + sizes[i]] for i, k in enumerate(small_names)}
    grads["w_conv_ssm"] = lax.dynamic_slice(grads["w_conv_ssm"].reshape(k_ssm, d_xbc), (0, me * (d_xbc // N_DEV)), (k_ssm, d_xbc // N_DEV))
    grads["w_dw"] = lax.dynamic_slice(grads["w_dw"].reshape(k_dw, d), (0, me * (d // N_DEV)), (k_dw, d // N_DEV))
    grads["w_ada"] = _matmul("mod_dw", c_act, lax.dynamic_slice(dmod_all, (0, me * mc), (N_DEV, mc)), "tn", F32, highest=True)

    delta, new_m, new_v = {}, {}, {}
    for k in big:
        grads[k], delta[k], new_m[k], new_v[k] = _adam("adam_" + k, weights[k][0], None, mom_m[k][0], mom_v[k][0], parts=recv[k])
    for k in ["w_ada", "w_conv_ssm", "w_dw"]:
        delta[k], new_m[k], new_v[k] = _adam("adam_" + k, weights[k][0], grads[k], mom_m[k][0], mom_v[k][0])
    rep = [k for k in names if k not in big and k not in ("w_ada", "w_conv_ssm", "w_dw")]
    r_delta, r_m, r_v = _adam_many("adam_small", *[[t[k] for k in rep] for t in (weights, grads, mom_m, mom_v)])
    for i, k in enumerate(rep):
        delta[k], new_m[k], new_v[k] = r_delta[i], r_m[i], r_v[i]

    def shaped(t, k):
        return t[k].reshape(weights[k].shape)

    return (loss, grad_x.reshape(x.shape), *[shaped(grads, k) for k in names], *[shaped(delta, k) for k in names],
            *[shaped(new_m, k) for k in names], *[shaped(new_v, k) for k in names])
```

```python
import functools

import numpy as np
import jax
import jax.numpy as jnp
from jax import lax
from jax.experimental import pallas as pl
from jax.experimental.pallas import tpu as pltpu

F32 = jnp.float32
BF16 = jnp.bfloat16
HIGHEST = lax.Precision.HIGHEST
MESH = pl.DeviceIdType.MESH
N_DEV = 8
EPS = 1e-6
CHUNK = 128
HEAD_DIM = 64
D_STATE = 128
HEADS_PER_GROUP = 8
GROUP_X = HEADS_PER_GROUP * HEAD_DIM
GROUP_W = GROUP_X + 2 * D_STATE
COL_BLOCK = 512
HALO = 16
VMEM_LIMIT = 56 * 1024 * 1024
ADAM_LR, ADAM_B1, ADAM_B2, ADAM_EPS, ADAM_WD, ADAM_STEP = 0.001, 0.9, 0.999, 1e-08, 0.01, 10

NN = (((1,), (0,)), ((), ()))
NT = (((1,), (1,)), ((), ()))
TN = (((0,), (0,)), ((), ()))


def _tile(dim, prefs):
    for p in prefs:
        if p <= dim and dim % p == 0:
            return p
    return dim


def _params(n_grid):
    return pltpu.CompilerParams(dimension_semantics=("arbitrary",) * n_grid, vmem_limit_bytes=VMEM_LIMIT)


def _f(v):
    return v.astype(F32)


def _matmul(name, a, b, mode, out_dtype, acc_in=None, highest=False, carry=None, tn=None, epilogue=None):
    if mode == "nn":
        (m, k), (k2, n) = a.shape, b.shape
    elif mode == "nt":
        (m, k), (n, k2) = a.shape, b.shape
    else:
        (k, m), (k2, n) = a.shape, b.shape
    assert k == k2, (name, a.shape, b.shape, mode)
    tm = _tile(m, (1024, 512, 256, 128))
    tn = _tile(n, (1024, 512, 256, 128)) if tn is None else tn
    tk = _tile(k, (2816, 2048, 1024, 512, 256, 128))
    nk, nj = k // tk, n // tn
    assert n % tn == 0
    dims = {"nn": NN, "nt": NT, "tn": TN}[mode]
    a_spec = pl.BlockSpec((tk, tm), lambda i, j, kk: (kk, i)) if mode == "tn" else pl.BlockSpec((tm, tk), lambda i, j, kk: (i, kk))
    b_spec = pl.BlockSpec((tn, tk), lambda i, j, kk: (j, kk)) if mode == "nt" else pl.BlockSpec((tk, tn), lambda i, j, kk: (kk, j))
    o_spec = pl.BlockSpec((tm, tn), lambda i, j, kk: (i, j))
    has_acc = acc_in is not None
    epi_fn, e_arrs, e_outs = epilogue if epilogue is not None else (None, [], [])
    ne = len(e_arrs)
    n_in = (3 if has_acc else 2) + ne
    if epilogue is None:
        o_specs, o_shapes = [o_spec], [jax.ShapeDtypeStruct((m, n), out_dtype)]
    else:
        o_specs = [pl.BlockSpec((tm, w // nj), lambda i, j, kk: (i, j)) for w, _ in e_outs]
        o_shapes = [jax.ShapeDtypeStruct((m, w), dt) for w, dt in e_outs]
    e_specs = [pl.BlockSpec((tm, arr.shape[1] // nj), lambda i, j, kk: (i, j)) for arr in e_arrs]
    n_out = len(o_specs)
    ops, c_arrs = carry if carry is not None else (None, [])
    nc_ = len(c_arrs)
    grid = (m // tm, nj, nk)

    def body(*refs):
        a_ref, b_ref = refs[0], refs[1]
        c_ref = refs[2] if has_acc else None
        e_refs = refs[n_in - ne:n_in]
        c_ins = refs[n_in:n_in + nc_]
        o_refs = refs[n_in + nc_:n_in + nc_ + n_out]
        c_outs = refs[n_in + nc_ + n_out:n_in + 2 * nc_ + n_out]
        acc = refs[n_in + 2 * nc_ + n_out]
        i, j, kk = pl.program_id(0), pl.program_id(1), pl.program_id(2)
        if nc_:
            start, finish = ops(c_ins, c_outs, *refs[n_in + 2 * nc_ + n_out + 1:])
            pl.when((i == 0) & (j == 0) & (kk == 0))(start)

        def write(val):
            if epi_fn is None:
                o_refs[0][...] = val.astype(out_dtype)
            else:
                for ref, v in zip(o_refs, epi_fn(val, *[r[...] for r in e_refs])):
                    ref[...] = v.astype(ref.dtype)

        def product(first):
            if highest:
                p = lax.dot_general(_f(a_ref[...]), _f(b_ref[...]), dims, precision=HIGHEST, preferred_element_type=F32)
            else:
                p = lax.dot_general(a_ref[...].astype(BF16), b_ref[...].astype(BF16), dims, preferred_element_type=F32)
            return p + _f(c_ref[...]) if (first and has_acc) else p

        if nk == 1:
            write(product(True))
        else:
            @pl.when(kk == 0)
            def _():
                acc[...] = product(True)

            if nk > 2:
                @pl.when((kk > 0) & (kk < nk - 1))
                def _():
                    acc[...] += product(False)

            @pl.when(kk == nk - 1)
            def _():
                write(acc[...] + product(False))

        if nc_:
            pl.when((i == grid[0] - 1) & (j == grid[1] - 1) & (kk == nk - 1))(finish)

    any_spec = pl.BlockSpec(memory_space=pl.ANY)
    c_shapes = ops.out_shapes(c_arrs) if nc_ else []
    res = pl.pallas_call(
        body, name=name, grid=grid,
        in_specs=[a_spec, b_spec] + ([o_spec] if has_acc else []) + e_specs + [any_spec] * nc_,
        out_specs=o_specs + [any_spec] * nc_, out_shape=o_shapes + c_shapes,
        scratch_shapes=[pltpu.VMEM((tm, tn) if nk > 1 else (8, 128), F32)] + (_exchange_sems(nc_) if nc_ else []),
        compiler_params=_params(3),
    )(*([a, b] + ([acc_in] if has_acc else []) + list(e_arrs) + list(c_arrs)))
    outs = res[0] if n_out == 1 else res[:n_out]
    return (outs, res[n_out:]) if nc_ else outs


def _rowcall(name, fn, rows, params, out_rows, out_accs, ncol=1):
    s = rows[0][0].shape[0]
    tm = _tile(s, (256 if ncol == 1 else 1024, 128, 64, 32, 16, 8))
    args, in_specs = [], []
    for arr, off, w in rows:
        bw = w // ncol
        assert w % ncol == 0 and off % bw == 0, (name, off, w, ncol)
        in_specs.append(pl.BlockSpec((tm, bw), functools.partial(lambda j, i, ob: (i, ob + j), ob=off // bw)))
        args.append(arr)
    for p in params:
        assert p.shape[1] % ncol == 0, (name, p.shape)
        in_specs.append(pl.BlockSpec((p.shape[0], p.shape[1] // ncol), lambda j, i: (0, j)))
        args.append(p)
    n_in, n_ro = len(args), len(out_rows)
    out_shape, out_specs, aliases = [], [], {}
    for k, spec in enumerate(out_rows):
        w, dt = spec[0], spec[1]
        bw = w // ncol
        if len(spec) == 4:
            buf, off = spec[2], spec[3]
            assert off % bw == 0 and buf.dtype == dt, (name, off, bw)
            out_shape.append(jax.ShapeDtypeStruct(buf.shape, dt))
            out_specs.append(pl.BlockSpec((tm, bw), functools.partial(lambda j, i, ob: (i, ob + j), ob=off // bw)))
            aliases[len(args)] = k
            in_specs.append(pl.BlockSpec(memory_space=pl.ANY))
            args.append(buf)
        else:
            out_shape.append(jax.ShapeDtypeStruct((s, w), dt))
            out_specs.append(pl.BlockSpec((tm, bw), lambda j, i: (i, j)))
    for w in out_accs:
        out_shape.append(jax.ShapeDtypeStruct((1, w), F32))
        out_specs.append(pl.BlockSpec((1, w // ncol), lambda j, i: (0, j)))
    n_args = len(args)

    def body(*refs):
        i = pl.program_id(1)
        ro, ao = fn(*[r[...] for r in refs[:n_in]])
        outs = refs[n_args:]
        assert len(ro) == n_ro and len(ao) == len(out_accs), name
        for ref, v in zip(outs[:n_ro], ro):
            ref[...] = v.astype(ref.dtype)
        for ref, v in zip(outs[n_ro:], ao):
            @pl.when(i == 0)
            def _(ref=ref, v=v):
                ref[...] = v

            @pl.when(i > 0)
            def _(ref=ref, v=v):
                ref[...] += v

    res = pl.pallas_call(
        body, name=name, grid=(ncol, s // tm), in_specs=in_specs, out_specs=out_specs,
        out_shape=out_shape, input_output_aliases=aliases, compiler_params=_params(2),
    )(*args)
    return res[:n_ro], res[n_ro:]


def _fwd_fn(f):
    def fn(*vals):
        out = f(*[_f(v) for v in vals])
        return (out if isinstance(out, tuple) else (out,)), ()
    return fn


def _vjp_fn(f, n_prim, n_ct, want_rows, want_params):
    def fn(*vals):
        prims = [_f(v) for v in vals[:n_prim]]
        cts = [_f(v) for v in vals[n_prim:n_prim + n_ct]]
        prms = [_f(v) for v in vals[n_prim + n_ct:]]
        out, vjp = jax.vjp(f, *prims, *prms)
        g = vjp(tuple(cts) if isinstance(out, tuple) else cts[0])
        return tuple(g[k] for k in want_rows), tuple(g[n_prim + k] for k in want_params)
    return fn


def _rms(v):
    return v * lax.rsqrt(jnp.mean(v * v, axis=-1, keepdims=True) + EPS)


def _silu(v):
    return v * jax.nn.sigmoid(v)


def _t_prenorm(x, g, sc, sh):
    return _rms(x) * g * (1.0 + sc) + sh


def _t_gnorm(yf, yb, z, gn):
    return _rms((yf + yb) * _silu(z)) * gn


def _t_glu(a, b, ba, bb):
    return (a + ba) * jax.nn.sigmoid(b + bb)


def _t_lnsilu(u, g, b):
    uc = u - jnp.mean(u, axis=-1, keepdims=True)
    return _silu(uc * lax.rsqrt(jnp.mean(uc * uc, axis=-1, keepdims=True) + EPS) * g + b)


def _t_gatemix(ya, yb, la, lb, bco, bga, bgb):
    return jax.nn.sigmoid(la + bga) * ya + jax.nn.sigmoid(lb + bgb) * (yb + bco)


def _t_postpre(x, mix, gpost, g1, gpre, sc, sh):
    x1 = x + g1 * (_rms(mix) * gpost)
    return x1, _rms(x1) * gpre * (1.0 + sc) + sh


def _t_swiglu(gt, up):
    return _silu(gt) * up


def _halves(v):
    return _f(v[:, :COL_BLOCK]), _f(v[:, COL_BLOCK:])


def _glu_fwd(ab, bias):
    return (_t_glu(*_halves(ab), *_halves(bias)),), ()


def _glu_bwd(ab, du, bias):
    _, vjp = jax.vjp(_t_glu, *_halves(ab), *_halves(bias))
    da, db, dba, dbb = vjp(_f(du))
    return (jnp.concatenate([da, db], axis=1),), (jnp.concatenate([dba, dbb], axis=1),)


def _gatemix_fwd(ya, yb, logits, bco, bg):
    return (_t_gatemix(_f(ya), _f(yb), *_halves(logits), _f(bco), *_halves(bg)),), ()


def _gatemix_bwd(ya, yb, logits, dm, bco, bg):
    _, vjp = jax.vjp(_t_gatemix, _f(ya), _f(yb), *_halves(logits), _f(bco), *_halves(bg))
    dya, dyb, dla, dlb, dbco, dbga, dbgb = vjp(_f(dm))
    return (dya, dyb, jnp.concatenate([dla, dlb], axis=1)), (dbco, jnp.concatenate([dbga, dbgb], axis=1))


def _swiglu_fwd(gu):
    return (_t_swiglu(*_halves(gu)),), ()


def _swiglu_bwd(gu, dact):
    _, vjp = jax.vjp(_t_swiglu, *_halves(gu))
    return (jnp.concatenate(vjp(_f(dact)), axis=1),), ()


def _t_loss(x1, f, tgt, gpost, g2):
    e = x1 + g2 * (_rms(f) * gpost) - tgt
    return 0.5 * jnp.sum(jnp.mean(e * e, axis=-1))


def _conv_specs(s, tm, cw, ob, w_rows):
    nh, last_h = tm // HALO, s // HALO - 1
    return [
        pl.BlockSpec((tm, cw), lambda j, i: (i, ob + j)),
        pl.BlockSpec((HALO, cw), lambda j, i: (jnp.maximum(i * nh - 1, 0), ob + j)),
        pl.BlockSpec((HALO, cw), lambda j, i: (jnp.minimum((i + 1) * nh, last_h), ob + j)),
        pl.BlockSpec((w_rows, cw), lambda j, i: (0, j)),
        pl.BlockSpec((1, cw), lambda j, i: (0, j)),
    ]


def _window(x_ref, p_ref, n_ref, i, nrow):
    prev = _f(p_ref[...]) * jnp.where(i > 0, 1.0, 0.0).astype(F32)
    nxt = _f(n_ref[...]) * jnp.where(i < nrow - 1, 1.0, 0.0).astype(F32)
    return jnp.concatenate([prev, _f(x_ref[...]), nxt], axis=0)


def _shifted(win, k_taps, tm):
    pad, moved, out = (k_taps - 1) // 2, {}, []
    for k in range(k_taps):
        q, r = divmod(HALO + k - pad, 8)
        if r not in moved:
            moved[r] = pltpu.roll(win, win.shape[0] - r, 0) if r else win
        out.append(moved[r][8 * q:8 * q + tm, :])
    return out


def _taps(win, w_ref, k_taps, tm):
    sh = _shifted(win, k_taps, tm)
    acc = w_ref[0:1, :] * sh[0]
    for k in range(1, k_taps):
        acc = acc + w_ref[k:k + 1, :] * sh[k]
    return acc


def _conv_rows(k_taps):
    return 1024 if k_taps <= 8 else 512


def _dwconv(name, x, xoff, c, w_pad, k_taps, b, act, out_dtype, cw=256, into=None):
    s = x.shape[0]
    tm, cw = _tile(s, (_conv_rows(k_taps), 256, 128, 64, 32, 16)), _tile(c, (cw, 128))
    assert xoff % cw == 0 and tm % HALO == 0 and (k_taps - 1) // 2 <= HALO
    nrow = s // tm

    def body(x_ref, p_ref, n_ref, w_ref, b_ref, *rest):
        win = _window(x_ref, p_ref, n_ref, pl.program_id(1), nrow)
        acc = _taps(win, w_ref, k_taps, tm) + b_ref[...]
        rest[-1][...] = (_silu(acc) if act else acc).astype(out_dtype)

    in_specs, args = _conv_specs(s, tm, cw, xoff // cw, w_pad.shape[0]), [x, x, x, w_pad, b]
    if into is None:
        ob, out_shape, aliases = 0, jax.ShapeDtypeStruct((s, c), out_dtype), {}
    else:
        buf, first = into
        assert first % cw == 0 and buf.dtype == out_dtype
        ob, out_shape, aliases = first // cw, jax.ShapeDtypeStruct(buf.shape, out_dtype), {len(args): 0}
        in_specs, args = in_specs + [pl.BlockSpec(memory_space=pl.ANY)], args + [buf]
    return pl.pallas_call(
        body, name=name, grid=(c // cw, nrow), in_specs=in_specs,
        out_specs=pl.BlockSpec((tm, cw), lambda j, i: (i, ob + j)), out_shape=out_shape,
        input_output_aliases=aliases, compiler_params=_params(2),
    )(*args)


def _dwconv_bwd_w(name, x, xoff, c, w_pad, k_taps, b, douts, act, cw=256):
    s = x.shape[0]
    tm, cw = _tile(s, (_conv_rows(k_taps), 256, 128, 64, 32, 16)), _tile(c, (cw, 128))
    assert xoff % cw == 0 and tm % HALO == 0
    nrow, pad, n_d = s // tm, (k_taps - 1) // 2, len(douts)

    def body(x_ref, p_ref, n_ref, w_ref, b_ref, *rest):
        d_refs, outs = rest[:n_d], rest[n_d:]
        i = pl.program_id(1)
        win = _window(x_ref, p_ref, n_ref, i, nrow)
        d = _f(d_refs[0][...])
        for r in d_refs[1:]:
            d = d + _f(r[...])
        if act:
            shifted = _shifted(win, k_taps, tm)
        else:
            shifted = [win[HALO + k - pad:HALO + k - pad + tm, :] for k in range(k_taps)]
        if act:
            pre = b_ref[...] + w_ref[0:1, :] * shifted[0]
            for k in range(1, k_taps):
                pre = pre + w_ref[k:k + 1, :] * shifted[k]
            sg = jax.nn.sigmoid(pre)
            d = d * (sg * (1.0 + pre * (1.0 - sg)))
            outs[0][...] = d.astype(outs[0].dtype)
        dw_ref, db_ref = outs[-2], outs[-1]

        @pl.when(i == 0)
        def _():
            dw_ref[...] = jnp.zeros_like(dw_ref)
            db_ref[...] = jnp.zeros_like(db_ref)

        for k in range(k_taps):
            dw_ref[k:k + 1, :] += jnp.sum(d * shifted[k], axis=0, keepdims=True)
        db_ref[...] += jnp.sum(d, axis=0, keepdims=True)

    row_spec = pl.BlockSpec((tm, cw), lambda j, i: (i, j))
    out_shape = [jax.ShapeDtypeStruct((w_pad.shape[0], c), F32), jax.ShapeDtypeStruct((1, c), F32)]
    out_specs = [pl.BlockSpec((w_pad.shape[0], cw), lambda j, i: (0, j)), pl.BlockSpec((1, cw), lambda j, i: (0, j))]
    if act:
        out_shape, out_specs = [jax.ShapeDtypeStruct((s, c), BF16)] + out_shape, [row_spec] + out_specs
    return pl.pallas_call(
        body, name=name, grid=(c // cw, nrow),
        in_specs=_conv_specs(s, tm, cw, xoff // cw, w_pad.shape[0]) + [row_spec] * n_d,
        out_specs=out_specs, out_shape=out_shape, compiler_params=_params(2),
    )(x, x, x, w_pad, b, *douts)


def _dtprep(name, dt_raw, bias, alog_col):
    s, h2 = dt_raw.shape
    ts = _tile(s, (512, 256, 128))

    def body(r_ref, b_ref, al_ref, dt_ref, a_ref):
        v = r_ref[...] + b_ref[...]
        dt = (jnp.maximum(v, 0.0) + jnp.log(1.0 + jnp.exp(-jnp.abs(v)))).T
        dt_ref[...] = dt
        a_ref[...] = dt * (-jnp.exp(al_ref[...]))

    t_spec = pl.BlockSpec((h2, ts), lambda i: (0, i))
    return pl.pallas_call(
        body, name=name, grid=(s // ts,),
        in_specs=[pl.BlockSpec((ts, h2), lambda i: (i, 0)), pl.BlockSpec((1, h2), lambda i: (0, 0)), pl.BlockSpec((h2, 1), lambda i: (0, 0))],
        out_specs=[t_spec, t_spec], out_shape=[jax.ShapeDtypeStruct((h2, s), F32)] * 2, compiler_params=_params(1),
    )(dt_raw, bias, alog_col)


def _dtprep_bwd(name, ddt_f, ddt_b, da_f, da_b, dt_t, alog_col):
    h2, s = dt_t.shape
    h = h2 // 2
    ts = _tile(s, (512, 256, 128))

    def body(ddf, ddb, daf, dab, dt_ref, al_ref, raw_ref, db_ref, dal_ref):
        i = pl.program_id(0)
        a_col = -jnp.exp(al_ref[...])
        da = jnp.concatenate([daf[...], dab[...]], axis=0)
        dt = dt_ref[...]
        d_raw = (jnp.concatenate([ddf[...], ddb[...]], axis=0) + da * a_col) * (1.0 - jnp.exp(-dt))
        raw_ref[...] = d_raw.T

        @pl.when(i == 0)
        def _():
            db_ref[...] = jnp.zeros_like(db_ref)
            dal_ref[...] = jnp.zeros_like(dal_ref)

        db_ref[...] += jnp.sum(d_raw, axis=1, keepdims=True)
        dal_ref[...] += jnp.sum(da * dt, axis=1, keepdims=True) * a_col

    half = pl.BlockSpec((h, ts), lambda i: (0, i))
    col = pl.BlockSpec((h2, 1), lambda i: (0, 0))
    return pl.pallas_call(
        body, name=name, grid=(s // ts,),
        in_specs=[half, half, half, half, pl.BlockSpec((h2, ts), lambda i: (0, i)), col],
        out_specs=[pl.BlockSpec((ts, h2), lambda i: (i, 0)), col, col],
        out_shape=[jax.ShapeDtypeStruct((s, h2), F32), jax.ShapeDtypeStruct((h2, 1), F32), jax.ShapeDtypeStruct((h2, 1), F32)],
        compiler_params=_params(1),
    )(ddt_f, ddt_b, da_f, da_b, dt_t, alog_col)


def _dot_exact(a, b, dims, exact_lhs):
    other = b if exact_lhs else a
    out = None
    for _ in range(3):
        part = other.astype(BF16)
        other = other - _f(part)
        p = lax.dot_general(a if exact_lhs else part, part if exact_lhs else b, dims, preferred_element_type=F32)
        out = p if out is None else out + p
    return out


def _chunk_consts(a_r, dt_r, rev):
    ii = lax.broadcasted_iota(jnp.int32, (CHUNK, CHUNK), 0)
    jj = lax.broadcasted_iota(jnp.int32, (CHUNK, CHUNK), 1)
    w = (jj >= ii) if rev else (jj <= ii)
    wt = (ii >= jj) if rev else (ii <= jj)
    eye = (ii == jj).astype(F32)
    wb_ = w.astype(BF16)
    cs_col = _dot_exact(wb_, a_r, NT, exact_lhs=True)
    cs_row = _dot_exact(a_r, wb_, NT, exact_lhs=False)
    tot = jnp.sum(a_r, axis=1, keepdims=True)
    lo_half = lax.broadcasted_iota(jnp.int32, (CHUNK, 128), 1) < HEAD_DIM

    def lanes(col):
        return [jnp.broadcast_to(col[:, j:j + 1], (CHUNK, 128)) for j in range(HEADS_PER_GROUP)]

    def spread(bs):
        return jnp.concatenate([jnp.where(lo_half, bs[2 * k], bs[2 * k + 1]) for k in range(HEADS_PER_GROUP // 2)], axis=1)

    csc = lanes(cs_col)
    cs_x = spread(csc)
    hh = lax.broadcasted_iota(jnp.int32, (HEADS_PER_GROUP, GROUP_X), 0)
    ll = lax.broadcasted_iota(jnp.int32, (HEADS_PER_GROUP, GROUP_X), 1)
    expand = (lax.shift_right_logical(ll, 6) == hh).astype(BF16)
    dt_hi = dt_r.astype(BF16)
    dt_x = (lax.dot_general(dt_hi, expand, TN, preferred_element_type=F32)
            + lax.dot_general((dt_r - _f(dt_hi)).astype(BF16), expand, TN, preferred_element_type=F32))
    tot_x = cs_x[0:1, :] if rev else cs_x[CHUNK - 1:CHUNK, :]
    return w, wt, eye, csc, cs_row, tot, cs_x, dt_x, tot_x


def _hi_lo(v):
    hi = v.astype(BF16)
    return jnp.concatenate([hi, (v - _f(hi)).astype(BF16)], axis=1)


def _head_sums(v, terms=3):
    rr = lax.broadcasted_iota(jnp.int32, (GROUP_X, HEADS_PER_GROUP), 0)
    cc = lax.broadcasted_iota(jnp.int32, (GROUP_X, HEADS_PER_GROUP), 1)
    et = (lax.shift_right_logical(rr, 6) == cc).astype(BF16)
    out, rest = None, v
    for t in range(terms):
        part = rest.astype(BF16)
        if t < terms - 1:
            rest = rest - _f(part)
        p = jnp.dot(part, et, preferred_element_type=F32)
        out = p if out is None else out + p
    return out


def _chunks_per_step(nc):
    return 8 if nc % 8 == 0 else 4 if nc % 4 == 0 else 2 if nc % 2 == 0 else 1


def _ssd_fwd(name, xbc, dt_t, a_t, dsk, n_groups):
    s = xbc.shape[0]
    nc = s // CHUNK
    d_ssm = n_groups * GROUP_X
    cps = _chunks_per_step(nc)
    nblk, rows = nc // cps, cps * CHUNK

    def one(x_ref, dt_ref, a_ref, dsk_ref, y_ref, hin_ref, h_scr, rev, c):
        skip = dsk_ref is not None
        r0 = c * CHUNK
        xs = _f(x_ref[r0:r0 + CHUNK, 0:GROUP_X])
        bm = x_ref[r0:r0 + CHUNK, GROUP_X:GROUP_X + D_STATE]
        cm = x_ref[r0:r0 + CHUNK, GROUP_X + D_STATE:GROUP_W]
        w, _, _, csc, cs_row, tot, cs_x, dt_x, tot_x = _chunk_consts(a_ref[:, r0:r0 + CHUNK], dt_ref[:, r0:r0 + CHUNK], rev)
        cb = lax.dot_general(cm, bm, NT, preferred_element_type=F32)
        h = h_scr[...]
        h_b = h.astype(BF16)
        hin_ref[0, c] = h_b
        xdt = xs * dt_x
        xdt_b = xdt.astype(BF16)
        yo = lax.dot_general(cm, h_b, NT, preferred_element_type=F32) * jnp.exp(cs_x)
        st = lax.dot_general((xdt * jnp.exp(tot_x - cs_x)).astype(BF16), bm, TN, preferred_element_type=F32)
        yds = []
        for j in range(HEADS_PER_GROUP):
            lo = HEAD_DIM * j
            h_scr[lo:lo + HEAD_DIM, :] = h[lo:lo + HEAD_DIM, :] * jnp.exp(tot[j:j + 1, 0:1]) + st[lo:lo + HEAD_DIM, :]
            lm = jnp.exp(jnp.where(w, csc[j] - cs_row[j:j + 1, :], -jnp.inf))
            yds.append(jnp.dot((cb * lm).astype(BF16), xdt_b[:, lo:lo + HEAD_DIM], preferred_element_type=F32))
        y = jnp.concatenate(yds, axis=1) + yo
        if skip:
            y = y + dsk_ref[...] * xs
        y_ref[r0:r0 + CHUNK, :] = y.astype(y_ref.dtype)

    def body(xf_ref, dtf_ref, af_ref, xr_ref, dtr_ref, ar_ref, dsk_ref, yf_ref, hf_ref, yr_ref, hr_ref, hf_scr, hr_scr):
        @pl.when(pl.program_id(1) == 0)
        def _():
            hf_scr[...] = jnp.zeros_like(hf_scr)
            hr_scr[...] = jnp.zeros_like(hr_scr)

        for c in range(cps):
            one(xf_ref, dtf_ref, af_ref, dsk_ref, yf_ref, hf_ref, hf_scr, False, c)
            one(xr_ref, dtr_ref, ar_ref, None, yr_ref, hr_ref, hr_scr, True, cps - 1 - c)

    def specs(rev):
        def z(zi):
            return nblk - 1 - zi if rev else zi
        t_spec = pl.BlockSpec((HEADS_PER_GROUP, rows), lambda g, zi: ((n_groups if rev else 0) + g, z(zi)))
        ins = [pl.BlockSpec((rows, GROUP_W), lambda g, zi: (z(zi), g)), t_spec, t_spec]
        outs = [pl.BlockSpec((rows, GROUP_X), lambda g, zi: (z(zi), g)),
                pl.BlockSpec((1, cps, GROUP_X, D_STATE), lambda g, zi: (g, z(zi), 0, 0))]
        return ins, outs

    (in_f, out_f), (in_r, out_r) = specs(False), specs(True)
    shapes = [jax.ShapeDtypeStruct((s, d_ssm), BF16), jax.ShapeDtypeStruct((n_groups, nc, GROUP_X, D_STATE), BF16)]
    return pl.pallas_call(
        body, name=name, grid=(n_groups, nblk), in_specs=in_f + in_r + [pl.BlockSpec((1, GROUP_X), lambda g, zi: (0, g))],
        out_specs=out_f + out_r, out_shape=shapes + shapes,
        scratch_shapes=[pltpu.VMEM((GROUP_X, D_STATE), F32)] * 2, compiler_params=_params(2),
    )(xbc, dt_t, a_t, xbc, dt_t, a_t, dsk)


def _ssd_bwd(name, xbc, dt_t, a_t, dy, hin_f, hin_r, dsk, n_groups, carry=None):
    s = xbc.shape[0]
    nc = s // CHUNK
    n_heads = n_groups * HEADS_PER_GROUP
    cps = _chunks_per_step(nc)
    nblk, rows = nc // cps, cps * CHUNK

    def one(x_ref, dt_ref, a_ref, dy_ref, hin_ref, dsk_ref, dx_ref, ddt_ref, da_ref, ddsk_ref, g_scr, rev, c):
        skip = dsk_ref is not None
        r0 = c * CHUNK
        xs = _f(x_ref[r0:r0 + CHUNK, 0:GROUP_X])
        bm = x_ref[r0:r0 + CHUNK, GROUP_X:GROUP_X + D_STATE]
        cm = x_ref[r0:r0 + CHUNK, GROUP_X + D_STATE:GROUP_W]
        dyv = _f(dy_ref[r0:r0 + CHUNK, :])
        w, wt, eye, csc, cs_row, tot, cs_x, dt_x, tot_x = _chunk_consts(a_ref[:, r0:r0 + CHUNK], dt_ref[:, r0:r0 + CHUNK], rev)
        cb = lax.dot_general(cm, bm, NT, preferred_element_type=F32)
        cbt = lax.dot_general(bm, cm, NT, preferred_element_type=F32)
        onehot = lax.broadcasted_iota(jnp.int32, (1, HEADS_PER_GROUP), 1)
        hin = hin_ref[0, c]
        g = g_scr[...]
        g_b = g.astype(BF16)
        din, dte = jnp.exp(cs_x), jnp.exp(tot_x - cs_x)
        xdt = xs * dt_x
        xdt_b, dy_b = xdt.astype(BF16), dyv.astype(BF16)
        dyo_b = (dyv * din).astype(BF16)
        yo = lax.dot_general(cm, hin, NT, preferred_element_type=F32) * din
        dxdt_s = lax.dot_general(bm, g_b, NT, preferred_element_type=F32) * dte
        d_c = jnp.dot(dyo_b, hin, preferred_element_type=F32)
        d_b = jnp.dot((xdt * dte).astype(BF16), g_b, preferred_element_type=F32)
        dhin = lax.dot_general(dyo_b, cm, TN, preferred_element_type=F32)
        dcb = jnp.zeros((CHUNK, CHUNK), F32)
        dtot = jnp.zeros((1, HEADS_PER_GROUP), F32)
        yds, dxds = [], []
        for j in range(HEADS_PER_GROUP):
            lo = HEAD_DIM * j
            cd = jnp.exp(tot[j:j + 1, 0:1])
            gj = g[lo:lo + HEAD_DIM, :]
            dcd = jnp.sum(jnp.sum(gj * _f(hin[lo:lo + HEAD_DIM, :]), axis=1, keepdims=True), axis=0, keepdims=True)
            dtot = dtot + (dcd * cd) * (onehot == j).astype(F32)
            g_scr[lo:lo + HEAD_DIM, :] = dhin[lo:lo + HEAD_DIM, :] + gj * cd
            dd = csc[j] - cs_row[j:j + 1, :]
            lm = jnp.exp(jnp.where(w, dd, -jnp.inf))
            lmt = jnp.exp(jnp.where(wt, -dd, -jnp.inf))
            xj_b, dyj_b = xdt_b[:, lo:lo + HEAD_DIM], dy_b[:, lo:lo + HEAD_DIM]
            yds.append(jnp.dot(_hi_lo(cb * lm), jnp.concatenate([xj_b, xj_b], axis=0), preferred_element_type=F32))
            dxds.append(jnp.dot(_hi_lo(cbt * lmt), jnp.concatenate([dyj_b, dyj_b], axis=0), preferred_element_type=F32))
            dcb = dcb + lax.dot_general(dyj_b, xj_b, NT, preferred_element_type=F32) * lm
        dxdt_d = jnp.concatenate(dxds, axis=1)
        z_state = xdt * dxdt_s
        yd = jnp.concatenate(yds, axis=1)
        dcs = _head_sums(dyv * yo + _f(dy_b) * yd - _f(xdt_b) * dxdt_d - z_state)
        dtot = dtot + _head_sums(jnp.broadcast_to(jnp.sum(z_state, axis=0, keepdims=True), (8, GROUP_X)))[0:1, :]
        dxdt = dxdt_s + dxdt_d
        ddt_c = _head_sums(dxdt * xs, terms=2)
        dx = dxdt * dt_x
        if skip:
            dx = dx + dsk_ref[...] * dyv
        dcb_b = dcb.astype(BF16)
        d_c = d_c + jnp.dot(dcb_b, bm, preferred_element_type=F32)
        d_b = d_b + lax.dot_general(dcb_b, cm, TN, preferred_element_type=F32)
        eye_b = eye.astype(BF16)
        da_c = _dot_exact(wt.astype(BF16), dcs, NN, exact_lhs=True) + dtot
        ddt_ref[:, r0:r0 + CHUNK] = _dot_exact(ddt_c, eye_b, TN, exact_lhs=False)
        da_ref[:, r0:r0 + CHUNK] = _dot_exact(da_c, eye_b, TN, exact_lhs=False)
        dx_ref[r0:r0 + CHUNK, :] = jnp.concatenate([dx, d_b, d_c], axis=1).astype(dx_ref.dtype)
        if skip:
            ddsk_ref[...] += jnp.sum(dyv * xs, axis=0, keepdims=True)

    ops, c_arrs = carry if carry is not None else (None, [])
    nc_ = len(c_arrs)

    def body(xf, dtf, af, dyf, hf, xr, dtr, ar, dyr, hr, dsk_ref, *rest):
        c_ins, (dxf, ddtf, daf, dxr, ddtr, dar, ddsk_ref) = rest[:nc_], rest[nc_:nc_ + 7]
        c_outs, (gf_scr, gr_scr) = rest[nc_ + 7:2 * nc_ + 7], rest[2 * nc_ + 7:2 * nc_ + 9]
        if nc_:
            start, finish = ops(c_ins, c_outs, *rest[2 * nc_ + 9:])
            pl.when((pl.program_id(0) == 0) & (pl.program_id(1) == 0))(start)

        @pl.when(pl.program_id(1) == 0)
        def _():
            gf_scr[...] = jnp.zeros_like(gf_scr)
            gr_scr[...] = jnp.zeros_like(gr_scr)
            ddsk_ref[...] = jnp.zeros_like(ddsk_ref)

        for c in range(cps):
            one(xf, dtf, af, dyf, hf, dsk_ref, dxf, ddtf, daf, ddsk_ref, gf_scr, False, cps - 1 - c)
            one(xr, dtr, ar, dyr, hr, None, dxr, ddtr, dar, None, gr_scr, True, c)
        if nc_:
            pl.when((pl.program_id(0) == n_groups - 1) & (pl.program_id(1) == nblk - 1))(finish)

    def specs(rev):
        def z(zi):
            return zi if rev else nblk - 1 - zi
        t_in = pl.BlockSpec((HEADS_PER_GROUP, rows), lambda g, zi: ((n_groups if rev else 0) + g, z(zi)))
        t_out = pl.BlockSpec((HEADS_PER_GROUP, rows), lambda g, zi: (g, z(zi)))
        x_spec = pl.BlockSpec((rows, GROUP_W), lambda g, zi: (z(zi), g))
        ins = [x_spec, t_in, t_in, pl.BlockSpec((rows, GROUP_X), lambda g, zi: (z(zi), g)),
               pl.BlockSpec((1, cps, GROUP_X, D_STATE), lambda g, zi: (g, z(zi), 0, 0))]
        return ins, [x_spec, t_out, t_out]

    (in_f, out_f), (in_r, out_r) = specs(False), specs(True)
    lane_spec = pl.BlockSpec((1, GROUP_X), lambda g, zi: (0, g))
    shapes = [jax.ShapeDtypeStruct((s, n_groups * GROUP_W), BF16), jax.ShapeDtypeStruct((n_heads, s), F32),
              jax.ShapeDtypeStruct((n_heads, s), F32)]
    any_spec = pl.BlockSpec(memory_space=pl.ANY)
    res = pl.pallas_call(
        body, name=name, grid=(n_groups, nblk), in_specs=in_f + in_r + [lane_spec] + [any_spec] * nc_,
        out_specs=out_f + out_r + [lane_spec] + [any_spec] * nc_,
        out_shape=shapes + shapes + [jax.ShapeDtypeStruct((1, n_groups * GROUP_X), F32)] + (ops.out_shapes(c_arrs) if nc_ else []),
        scratch_shapes=[pltpu.VMEM((GROUP_X, D_STATE), F32)] * 2 + (_exchange_sems(nc_) if nc_ else []),
        compiler_params=_params(2),
    )(xbc, dt_t, a_t, dy, hin_f, xbc, dt_t, a_t, dy, hin_r, dsk, *c_arrs)
    return (res[:7], res[7:]) if nc_ else res


def _mesh_pos():
    return lax.axis_index("x"), lax.axis_index("y"), lax.axis_index("c")


def _slot(p):
    return 4 * p[0] + 2 * p[1] + p[2]


def _exchange_sems(n):
    return [pltpu.SemaphoreType.DMA((7 * n,)), pltpu.SemaphoreType.DMA((7 * n,)), pltpu.SemaphoreType.DMA((n,))]


def _gather_ops(ins, outs, send, recv, loc):
    n = len(ins)
    x, y, c = _mesh_pos()
    me, sib = (x, y, c), (x, y, 1 - c)
    chips = [(1 - x, y), (x, 1 - y), (1 - x, 1 - y)]

    def cp(a, k, block, to, src=None):
        dst = outs[a].at[_slot(block)]
        return pltpu.make_async_remote_copy(
            src_ref=dst if src is None else src, dst_ref=dst, send_sem=send.at[7 * a + k], recv_sem=recv.at[7 * a + k],
            device_id=to, device_id_type=MESH)

    mine = [pltpu.make_async_copy(ins[a], outs[a].at[_slot(me)], loc.at[a]) for a in range(n)]
    first = []
    for a in range(n):
        first.append(cp(a, 0, me, sib, src=ins[a]))
        first += [cp(a, 1 + j, me, (*chip, c), src=ins[a]) for j, chip in enumerate(chips)]

    def start():
        for d in mine + first:
            d.start()

    def finish():
        passed = []
        for a in range(n):
            for j, chip in enumerate(chips):
                cp(a, 1 + j, (*chip, c), me).wait_recv()
                p = cp(a, 4 + j, (*chip, c), sib)
                p.start()
                passed.append(p)
        for a in range(n):
            cp(a, 0, sib, me).wait_recv()
            for j, chip in enumerate(chips):
                cp(a, 4 + j, (*chip, 1 - c), me).wait_recv()
        for f in first + passed:
            f.wait_send()
        for m in mine:
            m.wait()

    return start, finish


_gather_ops.out_shapes = lambda arrs: [jax.ShapeDtypeStruct((N_DEV,) + a.shape, a.dtype) for a in arrs]


def _a2a_ops(ins, outs, send, recv, loc):
    n = len(ins)
    me = _mesh_pos()
    flips = [(fx, fy, fc) for fx in (0, 1) for fy in (0, 1) for fc in (0, 1)][1:]
    peers = [tuple(1 - p if fl else p for p, fl in zip(me, fl3)) for fl3 in flips]

    def cp(a, k, peer, dst_slot):
        return pltpu.make_async_remote_copy(
            src_ref=ins[a].at[_slot(peer)], dst_ref=outs[a].at[dst_slot], send_sem=send.at[7 * a + k],
            recv_sem=recv.at[7 * a + k], device_id=peer, device_id_type=MESH)

    mine = [pltpu.make_async_copy(ins[a].at[_slot(me)], outs[a].at[_slot(me)], loc.at[a]) for a in range(n)]
    sends = [cp(a, k, peer, _slot(me)) for a in range(n) for k, peer in enumerate(peers)]

    def start():
        for d in mine + sends:
            d.start()

    def finish():
        for a in range(n):
            for k, peer in enumerate(peers):
                cp(a, k, peer, _slot(peer)).wait_recv()
        for s_ in sends:
            s_.wait_send()
        for m in mine:
            m.wait()

    return start, finish


_a2a_ops.out_shapes = lambda arrs: [jax.ShapeDtypeStruct(a.shape, a.dtype) for a in arrs]


def _exchange(name, ops, arrs):
    n = len(arrs)

    def body(*refs):
        start, finish = ops(refs[:n], refs[n:2 * n], *refs[2 * n:])
        start()
        finish()

    any_spec = pl.BlockSpec(memory_space=pl.ANY)
    return pl.pallas_call(
        body, name=name, in_specs=[any_spec] * n, out_specs=[any_spec] * n, out_shape=ops.out_shapes(arrs),
        scratch_shapes=_exchange_sems(n),
    )(*arrs)


def _all_gather(name, arrs):
    return _exchange(name, _gather_ops, arrs)


def _all_to_all(name, arrs):
    return _exchange(name, _a2a_ops, arrs)


def _sum8(parts_ref):
    g = _f(parts_ref[0])
    for j in range(1, N_DEV):
        g = g + _f(parts_ref[j])
    return g


def _adam_math(w, g, m, v):
    m = ADAM_B1 * m + (1.0 - ADAM_B1) * g
    v = ADAM_B2 * v + (1.0 - ADAM_B2) * (g * g)
    m_hat = m / (1.0 - ADAM_B1 ** ADAM_STEP)
    v_hat = v / (1.0 - ADAM_B2 ** ADAM_STEP)
    return -ADAM_LR * (m_hat / (jnp.sqrt(v_hat) + ADAM_EPS) + ADAM_WD * w), m, v


def _reduce_sum(name, parts):
    _, r, c = parts.shape

    def body(p_ref, o_ref):
        o_ref[...] = _sum8(p_ref)

    return pl.pallas_call(body, name=name, out_shape=jax.ShapeDtypeStruct((r, c), F32))(parts)


def _adam(name, w, g, m, v, parts=None):
    r, c = w.shape
    tr = _tile(r, (128, 64, 32, 16)) if r * c > 2 ** 18 else r
    spec = pl.BlockSpec((tr, c), lambda i: (i, 0))
    summed = parts is not None
    halves = summed and isinstance(parts, (list, tuple))
    if halves:
        half = r // (2 * tr)
        assert len(parts) == 2 and r % (2 * tr) == 0
        first = [pl.BlockSpec((N_DEV, tr, c), lambda i: (0, jnp.minimum(i, half - 1), 0)),
                 pl.BlockSpec((N_DEV, tr, c), lambda i: (0, jnp.maximum(i - half, 0), 0))]
        lead = list(parts)
    else:
        first = [pl.BlockSpec((N_DEV, tr, c), lambda i: (0, i, 0))] if summed else [spec]
        lead = [parts if summed else g]
    n_lead = len(lead)

    def body(*refs):
        w_ref, m_ref, v_ref = refs[n_lead:n_lead + 3]
        outs = refs[n_lead + 3:]

        def update(gv):
            if summed:
                outs[0][...] = gv
            d, mn, vn = _adam_math(w_ref[...], gv, m_ref[...], v_ref[...])
            outs[-3][...] = d
            outs[-2][...] = mn
            outs[-1][...] = vn

        if halves:
            i = pl.program_id(0)
            pl.when(i < half)(lambda: update(_sum8(refs[0])))
            pl.when(i >= half)(lambda: update(_sum8(refs[1])))
        else:
            update(_sum8(refs[0]) if summed else refs[0][...])

    n_out = 4 if summed else 3
    return pl.pallas_call(
        body, name=name, grid=(r // tr,), in_specs=first + [spec] * 3, out_specs=[spec] * n_out,
        out_shape=[jax.ShapeDtypeStruct((r, c), F32)] * n_out, compiler_params=_params(1),
    )(*lead, w, m, v)


def _adam_many(name, ws, gs, ms, vs):
    n = len(ws)

    def body(*refs):
        outs = refs[4 * n:]
        for k in range(n):
            d, mn, vn = _adam_math(refs[k][...], refs[n + k][...], refs[2 * n + k][...], refs[3 * n + k][...])
            outs[k][...] = d
            outs[n + k][...] = mn
            outs[2 * n + k][...] = vn

    res = pl.pallas_call(
        body, name=name, out_shape=[jax.ShapeDtypeStruct(w.shape, F32) for w in ws] * 3,
        compiler_params=pltpu.CompilerParams(vmem_limit_bytes=VMEM_LIMIT),
    )(*ws, *gs, *ms, *vs)
    return res[:n], res[n:2 * n], res[2 * n:]


def _slot_offsets(sizes):
    offs, o = [], 0
    for sz in sizes:
        offs.append(o)
        o += -(-sz // 128) * 128
    return offs, o + (-o % 1024)


def _pack_vectors(name, vecs):
    sizes = [v.shape[1] for v in vecs]
    offs, total = _slot_offsets(sizes)

    def body(*refs):
        out = refs[-1]
        out[...] = jnp.zeros_like(out)
        for r, o, sz in zip(refs[:-1], offs, sizes):
            out[:, o:o + sz] = r[...]

    return pl.pallas_call(body, name=name, out_shape=jax.ShapeDtypeStruct((1, total), F32),
                          compiler_params=pltpu.CompilerParams(vmem_limit_bytes=VMEM_LIMIT))(*vecs).reshape(-1, 128)


def _xbc_runs(n_groups, base=0):
    ds, nb = n_groups * GROUP_X, n_groups * D_STATE
    runs = []
    for g in range(n_groups):
        runs += [(base + GROUP_X * g, GROUP_X), (base + ds + D_STATE * g, D_STATE), (base + ds + nb + D_STATE * g, D_STATE)]
    return runs


def _interleave_runs(width, base=0):
    half = width // 2
    runs = []
    for j in range(half // COL_BLOCK):
        runs += [(base + COL_BLOCK * j, COL_BLOCK), (base + half + COL_BLOCK * j, COL_BLOCK)]
    return runs


def _w_in_runs(d, d_ssm, d_xbc, n_heads, n_groups):
    o_dt = d_ssm + d_xbc
    o_glu = o_dt + 2 * n_heads
    return ([(0, d_ssm)] + _xbc_runs(n_groups, d_ssm) + _interleave_runs(2 * d, o_glu) + _interleave_runs(2 * d, o_glu + 2 * d)
            + [(o_dt, 2 * n_heads)])


def _inverse_runs(runs):
    offs = np.concatenate([[0], np.cumsum([l for _, l in runs])])
    order = sorted(range(len(runs)), key=lambda k: runs[k][0])
    return [(int(offs[k]), runs[k][1]) for k in order]


def _permute_cols(name, srcs, runs, out_widths=None, src_sharded=False, dst_sharded=False):
    rows = srcs[0].shape[-2]
    dtype = srcs[0].dtype
    total = sum(l for _, l in runs)
    src_w = [srcs[0].shape[2]] * N_DEV if src_sharded else [a.shape[1] for a in srcs]
    dst_w = [total // N_DEV] * N_DEV if dst_sharded else (out_widths or [total])
    assert sum(dst_w) == total and sum(src_w) >= max(s_ + l for s_, l in runs)
    src_edges, dst_edges = np.cumsum([0] + src_w), np.cumsum([0] + dst_w)

    def locate(edges, col):
        k = int(np.searchsorted(edges, col, side="right")) - 1
        return k, col - int(edges[k])

    pieces, o = [], 0
    for s_, l in runs:
        while l:
            (ks, cs_), (kd, cd) = locate(src_edges, s_), locate(dst_edges, o)
            n = min(l, src_w[ks] - cs_, dst_w[kd] - cd)
            if pieces and pieces[-1][0] == ks and pieces[-1][2] == kd and pieces[-1][1] + pieces[-1][4] == cs_:
                pieces[-1] = pieces[-1][:4] + (pieces[-1][4] + n,)
            else:
                pieces.append((ks, cs_, kd, cd, n))
            s_, o, l = s_ + n, o + n, l - n
    tr = rows
    while tr % 16 == 0 and tr * max(sum(src_w), total) * dtype.itemsize > 4 * 2 ** 20:
        tr //= 2
    n_src = len(srcs)

    def body(*refs):
        ins, outs = refs[:n_src], refs[n_src:]
        for ks, cs_, kd, cd, n in pieces:
            val = ins[0][ks, :, cs_:cs_ + n] if src_sharded else ins[ks][:, cs_:cs_ + n]
            if dst_sharded:
                outs[0][kd, :, cd:cd + n] = val
            else:
                outs[kd][:, cd:cd + n] = val

    def spec(sharded, width):
        if sharded:
            return pl.BlockSpec((N_DEV, tr, width), lambda i: (0, i, 0))
        return pl.BlockSpec((tr, width), lambda i: (i, 0))

    in_specs = [spec(True, src_w[0])] if src_sharded else [spec(False, w) for w in src_w]
    if dst_sharded:
        out_specs, out_shape = [spec(True, dst_w[0])], [jax.ShapeDtypeStruct((N_DEV, rows, dst_w[0]), dtype)]
    else:
        out_specs, out_shape = [spec(False, w) for w in dst_w], [jax.ShapeDtypeStruct((rows, w), dtype) for w in dst_w]
    return pl.pallas_call(body, name=name, grid=(rows // tr,), in_specs=in_specs, out_specs=out_specs, out_shape=out_shape,
                          compiler_params=_params(1))(*srcs)


def _cols_from_shards(g):
    return g.transpose(1, 0, 2).reshape(g.shape[1], -1)


def _cols_to_shards(a):
    return a.reshape(a.shape[0], N_DEV, -1).transpose(1, 0, 2)


def _pad_rows(a, rows):
    return jnp.concatenate([a, jnp.zeros((rows - a.shape[0], a.shape[1]), a.dtype)], axis=0)


def _pack(vecs):
    a = jnp.concatenate(vecs, axis=1)
    return jnp.concatenate([a, jnp.zeros((1, -a.shape[1] % 1024), a.dtype)], axis=1).reshape(-1, 128)


def kernel(x, c, w_ada, b_ada, g_pre_mix, g_post_mix, w_in, w_conv_ssm, b_conv_ssm, dt_bias_fwd, dt_bias_bwd, a_log_fwd, a_log_bwd, d_skip, g_ssm_norm, w_ssm_out, b_glu, w_dw, b_dw, ln_g, ln_b, w_conv_out, b_conv_out, b_gate, w_mix_out, g_pre_ffn, g_post_ffn, w_gate_up, w_down, loss_target, m_w_ada, m_b_ada, m_g_pre_mix, m_g_post_mix, m_w_in, m_w_conv_ssm, m_b_conv_ssm, m_dt_bias_fwd, m_dt_bias_bwd, m_a_log_fwd, m_a_log_bwd, m_d_skip, m_g_ssm_norm, m_w_ssm_out, m_b_glu, m_w_dw, m_b_dw, m_ln_g, m_ln_b, m_w_conv_out, m_b_conv_out, m_b_gate, m_w_mix_out, m_g_pre_ffn, m_g_post_ffn, m_w_gate_up, m_w_down, v_w_ada, v_b_ada, v_g_pre_mix, v_g_post_mix, v_w_in, v_w_conv_ssm, v_b_conv_ssm, v_dt_bias_fwd, v_dt_bias_bwd, v_a_log_fwd, v_a_log_bwd, v_d_skip, v_g_ssm_norm, v_w_ssm_out, v_b_glu, v_w_dw, v_b_dw, v_ln_g, v_ln_b, v_w_conv_out, v_b_conv_out, v_b_gate, v_w_mix_out, v_g_pre_ffn, v_g_post_ffn, v_w_gate_up, v_w_down):
    weights = dict(w_ada=w_ada, b_ada=b_ada, g_pre_mix=g_pre_mix, g_post_mix=g_post_mix, w_in=w_in, w_conv_ssm=w_conv_ssm, b_conv_ssm=b_conv_ssm, dt_bias_fwd=dt_bias_fwd, dt_bias_bwd=dt_bias_bwd, a_log_fwd=a_log_fwd, a_log_bwd=a_log_bwd, d_skip=d_skip, g_ssm_norm=g_ssm_norm, w_ssm_out=w_ssm_out, b_glu=b_glu, w_dw=w_dw, b_dw=b_dw, ln_g=ln_g, ln_b=ln_b, w_conv_out=w_conv_out, b_conv_out=b_conv_out, b_gate=b_gate, w_mix_out=w_mix_out, g_pre_ffn=g_pre_ffn, g_post_ffn=g_post_ffn, w_gate_up=w_gate_up, w_down=w_down)
    mom_m = dict(w_ada=m_w_ada, b_ada=m_b_ada, g_pre_mix=m_g_pre_mix, g_post_mix=m_g_post_mix, w_in=m_w_in, w_conv_ssm=m_w_conv_ssm, b_conv_ssm=m_b_conv_ssm, dt_bias_fwd=m_dt_bias_fwd, dt_bias_bwd=m_dt_bias_bwd, a_log_fwd=m_a_log_fwd, a_log_bwd=m_a_log_bwd, d_skip=m_d_skip, g_ssm_norm=m_g_ssm_norm, w_ssm_out=m_w_ssm_out, b_glu=m_b_glu, w_dw=m_w_dw, b_dw=m_b_dw, ln_g=m_ln_g, ln_b=m_ln_b, w_conv_out=m_w_conv_out, b_conv_out=m_b_conv_out, b_gate=m_b_gate, w_mix_out=m_w_mix_out, g_pre_ffn=m_g_pre_ffn, g_post_ffn=m_g_post_ffn, w_gate_up=m_w_gate_up, w_down=m_w_down)
    mom_v = dict(w_ada=v_w_ada, b_ada=v_b_ada, g_pre_mix=v_g_pre_mix, g_post_mix=v_g_post_mix, w_in=v_w_in, w_conv_ssm=v_w_conv_ssm, b_conv_ssm=v_b_conv_ssm, dt_bias_fwd=v_dt_bias_fwd, dt_bias_bwd=v_dt_bias_bwd, a_log_fwd=v_a_log_fwd, a_log_bwd=v_a_log_bwd, d_skip=v_d_skip, g_ssm_norm=v_g_ssm_norm, w_ssm_out=v_w_ssm_out, b_glu=v_b_glu, w_dw=v_w_dw, b_dw=v_b_dw, ln_g=v_ln_g, ln_b=v_ln_b, w_conv_out=v_w_conv_out, b_conv_out=v_b_conv_out, b_gate=v_b_gate, w_mix_out=v_w_mix_out, g_pre_ffn=v_g_pre_ffn, g_post_ffn=v_g_post_ffn, w_gate_up=v_w_gate_up, w_down=v_w_down)
    names = list(weights)

    s, d = x.shape[1], x.shape[2]
    d_ssm, n_heads, d_xbc = g_ssm_norm.shape[-1], d_skip.shape[-1], b_conv_ssm.shape[-1]
    n_groups = n_heads // HEADS_PER_GROUP
    d_ff = w_down.shape[1] * N_DEV
    k_ssm, k_dw = w_conv_ssm.shape[1], w_dw.shape[1]
    assert d_ssm == n_groups * GROUP_X and d_xbc == n_groups * GROUP_W and s % CHUNK == 0
    assert d % COL_BLOCK == 0 and d_ff % COL_BLOCK == 0
    mc = w_ada.shape[-1]
    me = _slot(_mesh_pos())
    wb = d_ssm + d_xbc + 4 * d
    p_x, p_glu, p_gate = d_ssm, d_ssm + d_xbc, d_ssm + d_xbc + 2 * d
    ncb, nfb = d // COL_BLOCK, d_ff // COL_BLOCK
    x2, tgt = x[0], loss_target[0]

    big = ["w_in", "w_ssm_out", "w_conv_out", "w_mix_out", "w_gate_up", "w_down"]
    (w_in_g,) = _all_gather("gather_w_in", [w_in[0].astype(BF16)])
    w_in_runs, xbc_runs, il_runs = _w_in_runs(d, d_ssm, d_xbc, n_heads, n_groups), _xbc_runs(n_groups), _interleave_runs(2 * d)
    w_big, w_dt = _permute_cols("order_w_in", [w_in_g], w_in_runs, out_widths=[wb, 2 * n_heads], src_sharded=True)

    n_cs, n_dw = k_ssm * (d_xbc // N_DEV), k_dw * (d // N_DEV)
    (small_all,) = _all_gather("gather_small", [_pack([c, w_conv_ssm.reshape(1, n_cs), w_dw.reshape(1, n_dw)])])
    small_all = small_all.reshape(N_DEV, -1)
    c_all = small_all[:, :d]
    wcs = _cols_from_shards(small_all[:, d:d + n_cs].reshape(N_DEV, k_ssm, -1))
    wdw = _cols_from_shards(small_all[:, d + n_cs:d + n_cs + n_dw].reshape(N_DEV, k_dw, -1))
    (wcs_p,) = _permute_cols("order_conv_taps", [wcs], xbc_runs)
    (bcs_p,) = _permute_cols("order_conv_bias", [b_conv_ssm], xbc_runs)
    (b_glu_il,), (b_gate_il,) = _permute_cols("order_b_glu", [b_glu], il_runs), _permute_cols("order_b_gate", [b_gate], il_runs)
    kp_ssm, kp_dw = -(-k_ssm // 8) * 8, -(-k_dw // 8) * 8
    zero_x, zero_d = jnp.zeros((1, d_xbc), F32), jnp.zeros((1, d), F32)

    (c_act,), _ = _rowcall("c_act", _fwd_fn(_silu), [(c_all, 0, d)], [], [(d, F32)], [])
    b_my = lax.dynamic_slice(b_ada, (0, me * mc), (1, mc))
    mod_cols = _matmul("mod", c_act, w_ada[0], "nn", F32, acc_in=jnp.broadcast_to(b_my, (N_DEV, mc)))
    (mod_rows,) = _all_to_all("mod_to_owner", [mod_cols.reshape(N_DEV, 1, mc)])
    mod = mod_rows.reshape(1, N_DEV * mc)
    sh1, sc1, g1, sh2, sc2, g2 = [mod[:, k * d:(k + 1) * d] for k in range(6)]

    (h1,), _ = _rowcall("prenorm_mix", _fwd_fn(_t_prenorm), [(x2, 0, d)], [g_pre_mix, sc1, sh1], [(d, BF16)], [])
    proj, rest = _matmul("in_proj", h1, w_big, "nn", BF16, carry=(_gather_ops, [weights[k][0].astype(BF16) for k in big[1:]]))
    gath = dict(zip(big[1:], rest))
    w_ssm, w_co, w_mo = gath["w_ssm_out"].reshape(d_ssm, d), gath["w_conv_out"].reshape(d, d), gath["w_mix_out"].reshape(d, d)
    gu_runs = _interleave_runs(2 * d_ff)
    (w_gu,) = _permute_cols("order_w_gate_up", [gath["w_gate_up"]], gu_runs, src_sharded=True)
    w_dn = gath["w_down"].reshape(d_ff, d)
    dt_raw = _matmul("in_proj_dt", h1, w_dt, "nn", F32)
    xbc_c = _dwconv("conv_ssm", proj, p_x, d_xbc, _pad_rows(wcs_p, kp_ssm), k_ssm, bcs_p, True, BF16)
    alog_col = jnp.concatenate([a_log_fwd, a_log_bwd], axis=1).reshape(2 * n_heads, 1)
    dt_t, a_t = _dtprep("dt_prep", dt_raw, jnp.concatenate([dt_bias_fwd, dt_bias_bwd], axis=1), alog_col)
    dsk = jnp.repeat(d_skip, HEAD_DIM, axis=1)
    y_f, hin_f, y_b, hin_b = _ssd_fwd("ssd", xbc_c, dt_t, a_t, dsk, n_groups)
    gn_rows = [(y_f, 0, d_ssm), (y_b, 0, d_ssm), (proj, 0, d_ssm)]
    (yn,), _ = _rowcall("ssm_norm", _fwd_fn(_t_gnorm), gn_rows, [g_ssm_norm], [(d_ssm, BF16)], [], ncol=n_groups)
    ya = _matmul("ssm_out", yn, w_ssm, "nn", BF16)
    (u0,), _ = _rowcall("glu", _glu_fwd, [(proj, p_glu, 2 * d)], [b_glu_il], [(d, BF16)], [], ncol=ncb)
    wdw_pad = _pad_rows(wdw, kp_dw)
    u1 = _dwconv("conv_dw", u0, 0, d, wdw_pad, k_dw, b_dw, False, BF16)
    (u2,), _ = _rowcall("ln_silu", _fwd_fn(_t_lnsilu), [(u1, 0, d)], [ln_g, ln_b], [(d, BF16)], [])
    yb = _matmul("conv_out", u2, w_co, "nn", BF16)
    gm_rows, gm_prm = [(ya, 0, d), (yb, 0, d), (proj, p_gate, 2 * d)], [b_conv_out, b_gate_il]
    (m_in,), _ = _rowcall("gate_mix", _gatemix_fwd, gm_rows, gm_prm, [(d, BF16)], [], ncol=ncb)
    mix = _matmul("mix_out", m_in, w_mo, "nn", BF16)
    pp_prm = [g_post_mix, g1, g_pre_ffn, sc2, sh2]
    (x1, h2), _ = _rowcall("post_mix", _fwd_fn(_t_postpre), [(x2, 0, d), (mix, 0, d)], pp_prm, [(d, F32), (d, BF16)], [])
    gu, act = _matmul("gate_up", h2, w_gu, "nn", BF16, tn=2 * COL_BLOCK,
                      epilogue=(lambda p: (p, _swiglu_fwd(p)[0][0]), [], [(2 * d_ff, BF16), (d_ff, BF16)]))
    f = _matmul("down", act, w_dn, "nn", BF16)

    def loss_fn(x1_t, f_t, tgt_t, gpost, g2_t):
        args = [_f(x1_t), _f(f_t), _f(tgt_t), _f(gpost), _f(g2_t)]
        val, vjp = jax.vjp(_t_loss, *args)
        gr = vjp(jnp.ones((), F32))
        return (gr[0], gr[1]), (jnp.zeros((1, 128), F32) + val, gr[3], gr[4])

    (dx1a, df), (loss_v, d_gpost2, d_g2) = _rowcall(
        "loss", loss_fn, [(x1, 0, d), (f, 0, d), (tgt, 0, d)], [g_post_ffn, g2], [(d, F32), (d, BF16)], [128, d, d])
    loss = lax.psum(loss_v[0, 0], ("x", "y", "c"))

    dgu = _matmul("down_dx", df, w_dn, "nt", BF16, tn=COL_BLOCK,
                  epilogue=(lambda p, gu_t: _swiglu_bwd(gu_t, p)[0], [gu], [(2 * d_ff, BF16)]))
    dw_dn = _matmul("down_dw", act, df, "tn", BF16)
    recv = {}
    dh2, (recv["w_down"],) = _matmul("gate_up_dx", dgu, w_gu, "nt", BF16, carry=(_a2a_ops, [dw_dn.reshape(N_DEV, -1, d)]))
    dw_gu = _matmul("gate_up_dw", h2, dgu, "tn", BF16)
    (dxa, dmix), (d_gpost1, d_g1, d_gpre2, d_sc2, d_sh2) = _rowcall(
        "post_mix_bwd", _vjp_fn(_t_postpre, 2, 2, (0, 1), (0, 1, 2, 3, 4)),
        [(x2, 0, d), (mix, 0, d), (dx1a, 0, d), (dh2, 0, d)], pp_prm, [(d, F32), (d, BF16)], [d] * 5)
    dm_in = _matmul("mix_out_dx", dmix, w_mo, "nt", BF16)
    dw_mo = _matmul("mix_out_dw", m_in, dmix, "tn", BF16)
    dproj = lax.empty((s, wb), BF16)
    (dya, dyb, dproj), (d_bco, d_bgate) = _rowcall(
        "gate_mix_bwd", _gatemix_bwd, gm_rows + [(dm_in, 0, d)], gm_prm,
        [(d, BF16), (d, BF16), (2 * d, BF16, dproj, p_gate)], [d, 2 * d], ncol=ncb)
    du2 = _matmul("conv_out_dx", dyb, w_co, "nt", BF16)
    dw_co = _matmul("conv_out_dw", u2, dyb, "tn", BF16)
    (du1,), (d_lng, d_lnb) = _rowcall("ln_silu_bwd", _vjp_fn(_t_lnsilu, 1, 1, (0,), (0, 1)), [(u1, 0, d), (du2, 0, d)],
                                      [ln_g, ln_b], [(d, BF16)], [d, d])
    dwdw_p, d_bdw = _dwconv_bwd_w("conv_dw_dw", u0, 0, d, wdw_pad, k_dw, b_dw, [du1], False)
    du0 = _dwconv("conv_dw_dx", du1, 0, d, _pad_rows(wdw[::-1], kp_dw), k_dw, zero_d, False, BF16)
    (dproj,), (d_bglu,) = _rowcall("glu_bwd", _glu_bwd, [(proj, p_glu, 2 * d), (du0, 0, d)], [b_glu_il],
                                   [(2 * d, BF16, dproj, p_glu)], [2 * d], ncol=ncb)
    dyn = _matmul("ssm_out_dx", dya, w_ssm, "nt", BF16)
    dw_ssm = _matmul("ssm_out_dw", yn, dya, "tn", BF16)
    (dy_ssd, dproj), (d_gn,) = _rowcall("ssm_norm_bwd", _vjp_fn(_t_gnorm, 3, 1, (0, 2), (0,)), gn_rows + [(dyn, 0, d_ssm)],
                                        [g_ssm_norm], [(d_ssm, BF16), (d_ssm, BF16, dproj, 0)], [d_ssm], ncol=n_groups)
    ready = [dw_mo.reshape(N_DEV, -1, d), dw_co.reshape(N_DEV, -1, d), dw_ssm.reshape(N_DEV, -1, d),
             _permute_cols("restore_dw_gate_up", [dw_gu], _inverse_runs(gu_runs), dst_sharded=True)[0]]
    (dxbc_f, ddt_f, da_f, dxbc_b, ddt_b, da_b, ddsk), got = _ssd_bwd(
        "ssd_bwd", xbc_c, dt_t, a_t, dy_ssd, hin_f, hin_b, dsk, n_groups, carry=(_a2a_ops, ready))
    recv["w_mix_out"], recv["w_conv_out"], recv["w_ssm_out"], recv["w_gate_up"] = got
    ddt_raw, d_dtb, d_alog = _dtprep_bwd("dt_prep_bwd", ddt_f, ddt_b, da_f, da_b, dt_t, alog_col)
    dpre, dwcs_p, dbcs_p = _dwconv_bwd_w("conv_ssm_dw", proj, p_x, d_xbc, _pad_rows(wcs_p, kp_ssm), k_ssm, bcs_p,
                                          [dxbc_f, dxbc_b], True)
    dproj = _dwconv("conv_ssm_dx", dpre, 0, d_xbc, _pad_rows(wcs_p[::-1], kp_ssm), k_ssm, zero_x, False, BF16, into=(dproj, p_x))
    dw_dt = _matmul("in_proj_dt_dw", h1, ddt_raw, "tn", BF16)
    hd, inv_in = d // 2, _inverse_runs(w_in_runs)
    dw_top = _matmul("in_proj_dw_top", h1[:, :hd], dproj, "tn", BF16)
    (dw_in_top,) = _permute_cols("restore_dw_in_top", [dw_top, dw_dt[:hd]], inv_in, dst_sharded=True)
    dw_bot, (recv_top,) = _matmul("in_proj_dw_bot", h1[:, hd:], dproj, "tn", BF16, carry=(_a2a_ops, [dw_in_top]))
    (dw_in_bot,) = _permute_cols("restore_dw_in_bot", [dw_bot, dw_dt[hd:]], inv_in, dst_sharded=True)
    dh1, (recv_bot,) = _matmul("in_proj_dx", dproj, w_big, "nt", F32, carry=(_a2a_ops, [dw_in_bot]))
    recv["w_in"] = [recv_top, recv_bot]
    dh1 = _matmul("in_proj_dt_dx", ddt_raw, w_dt, "nt", BF16, acc_in=dh1)

    def prenorm_bwd_fn(x_t, dh_t, dxa_t, g, sc, sh):
        rows, prm = _vjp_fn(_t_prenorm, 1, 1, (0,), (0, 1, 2))(x_t, dh_t, g, sc, sh)
        return (rows[0] + _f(dxa_t),), prm

    (grad_x,), (d_gpre1, d_sc1, d_sh1) = _rowcall(
        "prenorm_mix_bwd", prenorm_bwd_fn, [(x2, 0, d), (dh1, 0, d), (dxa, 0, d)], [g_pre_mix, sc1, sh1], [(d, F32)], [d] * 3)

    dmod = jnp.concatenate([d_sh1, d_sc1, d_g1, d_sh2, d_sc2, d_g2], axis=1)
    small_g = {
        "b_ada": dmod, "g_pre_mix": d_gpre1, "g_post_mix": d_gpost1,
        "b_conv_ssm": _permute_cols("restore_db_conv", [dbcs_p], _inverse_runs(xbc_runs))[0],
        "dt_bias_fwd": d_dtb[:n_heads].reshape(1, n_heads), "dt_bias_bwd": d_dtb[n_heads:].reshape(1, n_heads),
        "a_log_fwd": d_alog[:n_heads].reshape(1, n_heads), "a_log_bwd": d_alog[n_heads:].reshape(1, n_heads),
        "d_skip": ddsk.reshape(n_heads, HEAD_DIM).sum(axis=1).reshape(1, n_heads), "g_ssm_norm": d_gn,
        "b_glu": _permute_cols("restore_db_glu", [d_bglu], _inverse_runs(il_runs))[0], "b_dw": d_bdw, "ln_g": d_lng, "ln_b": d_lnb,
        "b_conv_out": d_bco, "b_gate": _permute_cols("restore_db_gate", [d_bgate], _inverse_runs(il_runs))[0],
        "g_pre_ffn": d_gpre2, "g_post_ffn": d_gpost2,
        "w_conv_ssm": _permute_cols("restore_dw_conv", [dwcs_p], _inverse_runs(xbc_runs))[0][:k_ssm].reshape(1, k_ssm * d_xbc),
        "w_dw": dwdw_p[:k_dw].reshape(1, k_dw * d),
    }
    small_names = list(small_g)
    sizes = [small_g[k].shape[1] for k in small_names]
    offs, _ = _slot_offsets(sizes)
    (small_parts,) = _all_gather("gather_small_grads", [_pack_vectors("pack_small_grads", [small_g[k] for k in small_names])])
    dmod_all = small_parts.reshape(N_DEV, -1)[:, :6 * d]
    small_tot = _reduce_sum("sum_small_grads", small_parts).reshape(1, -1)
    grads = {k: small_tot[:, offs[i]:offs[i] + sizes[i]] for i, k in enumerate(small_names)}
    grads["w_conv_ssm"] = lax.dynamic_slice(grads["w_conv_ssm"].reshape(k_ssm, d_xbc), (0, me * (d_xbc // N_DEV)), (k_ssm, d_xbc // N_DEV))
    grads["w_dw"] = lax.dynamic_slice(grads["w_dw"].reshape(k_dw, d), (0, me * (d // N_DEV)), (k_dw, d // N_DEV))
    grads["w_ada"] = _matmul("mod_dw", c_act, lax.dynamic_slice(dmod_all, (0, me * mc), (N_DEV, mc)), "tn", F32, highest=True)

    delta, new_m, new_v = {}, {}, {}
    for k in big:
        grads[k], delta[k], new_m[k], new_v[k] = _adam("adam_" + k, weights[k][0], None, mom_m[k][0], mom_v[k][0], parts=recv[k])
    for k in ["w_ada", "w_conv_ssm", "w_dw"]:
        delta[k], new_m[k], new_v[k] = _adam("adam_" + k, weights[k][0], grads[k], mom_m[k][0], mom_v[k][0])
    rep = [k for k in names if k not in big and k not in ("w_ada", "w_conv_ssm", "w_dw")]
    r_delta, r_m, r_v = _adam_many("adam_small", *[[t[k] for k in rep] for t in (weights, grads, mom_m, mom_v)])
    for i, k in enumerate(rep):
        delta[k], new_m[k], new_v[k] = r_delta[i], r_m[i], r_v[i]

    def shaped(t, k):
        return t[k].reshape(weights[k].shape)

    return (loss, grad_x.reshape(x.shape), *[shaped(grads, k) for k in names], *[shaped(delta, k) for k in names],
            *[shaped(new_m, k) for k in names], *[shaped(new_v, k) for k in names])
```
